```python
import jax, jax.numpy as jnp
from jax import lax
import numpy as np

D_MODEL = 1024
BATCH = 8
SEQ = 2048
DEPTH = 2
DEC_BATCH = 128
DEC_SEQ = 1
PAST_LEN = 16384
PAGE_SIZE = 128

N_HEADS_A = 4
DK_A = 128
DV_A = 256
MLSTM_CHUNK = 128
D_B = 1024
N_GROUPS_B = 4
DG_B = D_B // N_GROUPS_B
CHUNK_B = 128
D_FF = 2816
CONV_W = 3
EPS = 1e-6
P_IN = 2 * N_HEADS_A * DK_A + 2 * N_HEADS_A * DV_A + 2 * N_HEADS_A + 2 * D_B + 2 * D_MODEL

kernel_name = "hybrid_mlstm_gmlp_convffn_step"


def rmsnorm(x, g):
    xf = x.astype(jnp.float32)
    r = lax.rsqrt(jnp.mean(xf * xf, axis=-1, keepdims=True) + EPS)
    return (xf * r * g.astype(jnp.float32)).astype(x.dtype)


def split_in(z):
    sizes = (N_HEADS_A * DK_A, N_HEADS_A * DK_A, N_HEADS_A * DV_A, N_HEADS_A * DV_A,
             N_HEADS_A, N_HEADS_A, D_B, D_B, D_MODEL, D_MODEL)
    idx = np.cumsum(sizes)[:-1].tolist()
    return jnp.split(z, idx, axis=-1)


def mlstm_chunk(carry, q, k, v, ig, lf):
    C, n, m = carry
    L = q.shape[-2]
    causal = jnp.tril(jnp.ones((L, L), dtype=bool))
    b = jnp.cumsum(lf, axis=-1)
    logD = b[..., :, None] - b[..., None, :] + ig[..., None, :]
    logD = jnp.where(causal, logD, -jnp.inf)
    inter = b + m[..., None]
    m_t = jnp.maximum(inter, jnp.max(logD, axis=-1))
    Dmat = jnp.exp(logD - m_t[..., None])
    w_inter = jnp.exp(inter - m_t)
    s = jnp.einsum('nhtk,nhsk->nhts', q, k) * Dmat
    num = jnp.einsum('nhts,nhsv->nhtv', s, v) + w_inter[..., None] * jnp.einsum('nhvk,nhtk->nhtv', C, q)
    den = jnp.sum(s, axis=-1) + w_inter * jnp.einsum('nhk,nhtk->nht', n, q)
    h = num / jnp.maximum(jnp.abs(den), jnp.exp(-m_t))[..., None]
    m_new = m_t[..., -1]
    w_last = Dmat[..., -1, :]
    decay = w_inter[..., -1]
    C_new = decay[..., None, None] * C + jnp.einsum('nhs,nhsv,nhsk->nhvk', w_last, v, k)
    n_new = decay[..., None] * n + jnp.einsum('nhs,nhsk->nhk', w_last, k)
    return (C_new, n_new, m_new), h


def mlstm_branch(q, k, v, ig, lf, C, n, m):
    N, S = q.shape[0], q.shape[1]
    L = min(S, MLSTM_CHUNK)
    nc = S // L

    def to_chunks(t):
        t = t.astype(jnp.float32).reshape((N, nc, L) + t.shape[2:])
        t = jnp.moveaxis(t, 1, 0)
        return jnp.swapaxes(t, 2, 3)

    xs = (to_chunks(q), to_chunks(k), to_chunks(v), to_chunks(ig), to_chunks(lf))

    def step(carry, xc):
        return mlstm_chunk(carry, *xc)

    init = (C.astype(jnp.float32), n.astype(jnp.float32), m.astype(jnp.float32))
    (C, n, m), h = lax.scan(step, init, xs)
    h = jnp.swapaxes(jnp.moveaxis(h, 0, 1), 2, 3).reshape(N, S, N_HEADS_A * DV_A)
    return h, C, n, m


def spatial_gate(u, vn, w_s, b_s):
    N, S, _ = vn.shape
    L = min(S, CHUNK_B)
    nc = S // L
    vg = vn.reshape(N, nc, L, N_GROUPS_B, DG_B)
    causal = jnp.tril(jnp.ones((L, L), dtype=bool))
    w = jnp.where(causal, w_s[:, :L, :L], 0.0).astype(vn.dtype)
    mixed = jnp.einsum('gts,ncsgd->nctgd', w, vg) + b_s[:, :L].T[None, None, :, :, None].astype(vn.dtype)
    return u * mixed.reshape(N, S, D_B)


def layer(x, C, n, m, conv_buf, w_in, b_ig, b_fg, g1, g_v, w_s, b_s, w_a, w_b, w_o, g2, w_up, c_w, c_b, w_down):
    N, S, _ = x.shape
    xn = rmsnorm(x, g1)
    z = xn @ w_in
    q, k, v, o, ig, fg, u, vb, ga, gb = split_in(z)
    q = q.reshape(N, S, N_HEADS_A, DK_A)
    k = k.reshape(N, S, N_HEADS_A, DK_A) * (DK_A ** -0.5)
    v = v.reshape(N, S, N_HEADS_A, DV_A)
    ig = ig.astype(jnp.float32) + b_ig.astype(jnp.float32)
    lf = jax.nn.log_sigmoid(fg.astype(jnp.float32) + b_fg.astype(jnp.float32))
    h, C, n, m = mlstm_branch(q, k, v, ig, lf, C, n, m)
    y_a = (jax.nn.sigmoid(o) * h.astype(x.dtype)) @ w_a
    u = jax.nn.gelu(u)
    vn = rmsnorm(jax.nn.gelu(vb), g_v)
    y_b = spatial_gate(u, vn, w_s, b_s) @ w_b
    x = x + (jax.nn.sigmoid(ga) * y_a + jax.nn.sigmoid(gb) * y_b) @ w_o
    xn = rmsnorm(x, g2)
    up = xn @ w_up
    pad = jnp.concatenate([conv_buf.astype(up.dtype), up], axis=1)
    conv = c_b + c_w[0] * pad[:, 0:S]
    for j in range(1, CONV_W):
        conv = conv + c_w[j] * pad[:, j:j + S]
    a, gval = jnp.split(conv, 2, axis=-1)
    x = x + (jax.nn.silu(a) * gval) @ w_down
    return x, C, n, m, pad[:, -(CONV_W - 1):], vn


def setup_inputs(seed: int = 0) -> dict:
    key = jax.random.key(seed)
    ks = jax.random.split(key, 24)
    nrm = jax.random.normal
    f32 = jnp.float32
    inp = {}
    inp['x_prompt'] = nrm(ks[0], (BATCH, SEQ, D_MODEL), f32)
    inp['x_sample'] = nrm(ks[1], (DEC_BATCH, DEC_SEQ, D_MODEL), f32)
    inp['state_mlstm_C'] = 0.5 * nrm(ks[2], (DEPTH, DEC_BATCH, N_HEADS_A, DV_A, DK_A), f32)
    inp['state_mlstm_n'] = 0.5 * nrm(ks[3], (DEPTH, DEC_BATCH, N_HEADS_A, DK_A), f32)
    inp['state_mlstm_m'] = nrm(ks[4], (DEPTH, DEC_BATCH, N_HEADS_A), f32)
    inp['state_ffn_conv'] = nrm(ks[5], (DEPTH, DEC_BATCH, CONV_W - 1, 2 * D_FF), f32)
    inp['w_in'] = nrm(ks[6], (DEPTH, D_MODEL, P_IN), f32) * D_MODEL ** -0.5
    inp['b_igate'] = -1.0 + 0.1 * nrm(ks[7], (DEPTH, N_HEADS_A), f32)
    inp['b_fgate'] = 3.0 + 0.1 * nrm(ks[8], (DEPTH, N_HEADS_A), f32)
    inp['g_norm1'] = 1.0 + 0.05 * nrm(ks[9], (DEPTH, D_MODEL), f32)
    inp['g_vnorm'] = 1.0 + 0.05 * nrm(ks[10], (DEPTH, D_B), f32)
    inp['w_spatial'] = 0.5 * nrm(ks[11], (DEPTH, N_GROUPS_B, CHUNK_B, CHUNK_B), f32) * CHUNK_B ** -0.5
    inp['b_spatial'] = 1.0 + 0.1 * nrm(ks[12], (DEPTH, N_GROUPS_B, CHUNK_B), f32)
    inp['w_branch_a'] = nrm(ks[13], (DEPTH, N_HEADS_A * DV_A, D_MODEL), f32) * (N_HEADS_A * DV_A) ** -0.5
    inp['w_branch_b'] = nrm(ks[14], (DEPTH, D_B, D_MODEL), f32) * D_B ** -0.5
    inp['w_out'] = nrm(ks[15], (DEPTH, D_MODEL, D_MODEL), f32) * D_MODEL ** -0.5
    inp['g_norm2'] = 1.0 + 0.05 * nrm(ks[16], (DEPTH, D_MODEL), f32)
    inp['w_up'] = nrm(ks[17], (DEPTH, D_MODEL, 2 * D_FF), f32) * D_MODEL ** -0.5
    inp['conv_w'] = nrm(ks[18], (DEPTH, CONV_W, 2 * D_FF), f32) * CONV_W ** -0.5
    inp['conv_b'] = 0.02 * nrm(ks[19], (DEPTH, 2 * D_FF), f32)
    inp['w_down'] = nrm(ks[20], (DEPTH, D_FF, D_MODEL), f32) * D_FF ** -0.5
    inp['g_final'] = 1.0 + 0.05 * nrm(ks[21], (D_MODEL,), f32)
    return inp


def reference(x_prompt, x_sample, state_mlstm_C, state_mlstm_n, state_mlstm_m, state_ffn_conv,
              w_in, b_igate, b_fgate, g_norm1, g_vnorm, w_spatial, b_spatial, w_branch_a, w_branch_b,
              w_out, g_norm2, w_up, conv_w, conv_b, w_down, g_final):
    nb = x_prompt.shape[0]
    xp, xs = x_prompt, x_sample
    Cp_l, np_l, mp_l, cp_l = [], [], [], []
    Cs_l, ns_l, ms_l, cs_l, vs_l = [], [], [], [], []
    for l in range(DEPTH):
        params = (w_in[l], b_igate[l], b_fgate[l], g_norm1[l], g_vnorm[l], w_spatial[l], b_spatial[l],
                  w_branch_a[l], w_branch_b[l], w_out[l], g_norm2[l], w_up[l], conv_w[l], conv_b[l], w_down[l])
        C0 = jnp.zeros((nb, N_HEADS_A, DV_A, DK_A), jnp.float32)
        n0 = jnp.zeros((nb, N_HEADS_A, DK_A), jnp.float32)
        m0 = jnp.zeros((nb, N_HEADS_A), jnp.float32)
        cb0 = jnp.zeros((nb, CONV_W - 1, 2 * D_FF), xp.dtype)
        xp, Cp, npr, mp, cp, _ = layer(xp, C0, n0, m0, cb0, *params)
        xs, Cs, ns, ms, cs, vs = layer(xs, state_mlstm_C[l], state_mlstm_n[l], state_mlstm_m[l],
                                       state_ffn_conv[l], *params)
        Cp_l.append(Cp); np_l.append(npr); mp_l.append(mp); cp_l.append(cp)
        Cs_l.append(Cs); ns_l.append(ns); ms_l.append(ms); cs_l.append(cs); vs_l.append(vs)
    y_prompt = rmsnorm(xp, g_final)
    y_sample = rmsnorm(xs, g_final)
    return (y_prompt, y_sample,
            jnp.stack(Cp_l), jnp.stack(np_l), jnp.stack(mp_l), jnp.stack(cp_l),
            jnp.stack(Cs_l), jnp.stack(ns_l), jnp.stack(ms_l), jnp.stack(cs_l), jnp.stack(vs_l))
```

```python
import functools

import jax
import jax.numpy as jnp
from jax import lax
from jax.experimental import pallas as pl
from jax.experimental.pallas import tpu as pltpu

D_MODEL = 1024
N_HEADS = 4
DK = 128
DV = 256
CHUNK = 128
D_B = 1024
N_GROUPS = 4
DG = D_B // N_GROUPS
D_FF = 2816
CONV_W = 3
EPS = 1e-6
K_SCALE = DK ** -0.5

OFF_Q = 0
OFF_K = OFF_Q + N_HEADS * DK
OFF_V = OFF_K + N_HEADS * DK
OFF_O = OFF_V + N_HEADS * DV
OFF_U = OFF_O + N_HEADS * DV
OFF_VB = OFF_U + D_B
OFF_GA = OFF_VB + D_B
OFF_GB = OFF_GA + D_MODEL
P_MAIN = OFF_GB + D_MODEL
GATE_LO = 2 * N_HEADS * DK + 2 * N_HEADS * DV
GATE_HI = GATE_LO + 2 * N_HEADS

V7X_LANES = 128
V7X_SUBLANES = 8
GATE_ROWS = V7X_SUBLANES
V7X_VMEM_LIMIT = 56 * 1024 * 1024
FFN_COL_TILE = 256
PROMPT_BLOCK = 256
SAMPLE_BLOCK = 8

F32 = jnp.float32
BF16 = jnp.bfloat16


def _dot(a, b):
    return jnp.dot(a, b, preferred_element_type=F32)


def _dot_nt(a, b):
    return lax.dot_general(a, b, (((1,), (1,)), ((), ())), preferred_element_type=F32)


def _rmsnorm(x, g):
    r = lax.rsqrt(jnp.mean(x * x, axis=-1, keepdims=True) + EPS)
    return x * r * g


def _log_sigmoid(x):
    return jnp.minimum(x, 0.0) - jnp.log1p(jnp.exp(-jnp.abs(x)))


def _scan_lanes(x, op, fill):
    lane = lax.broadcasted_iota(jnp.int32, x.shape, 1)
    k = 1
    while k < x.shape[1]:
        shifted = pltpu.roll(x, k, 1)
        x = op(x, jnp.where(lane >= k, shifted, fill))
        k *= 2
    return x


def _mixer_kernel(x_ref, wm_ref, wgt_ref, gbias_ref, g1_ref, gv_ref, ws_ref, bst_ref,
                  wa_ref, wb_ref, wo_ref,
                  xo_ref, c_ref, n_ref, m_ref,
                  ct_s, n_s, m_s, h_s, um_s, *, block):
    s = pl.program_id(1)

    @pl.when(s == 0)
    def _():
        ct_s[...] = jnp.zeros_like(ct_s)
        n_s[...] = jnp.zeros_like(n_s)
        m_s[...] = jnp.zeros_like(m_s)

    x = x_ref[...]
    xn = _rmsnorm(x, g1_ref[...]).astype(BF16)

    gates = _dot_nt(wgt_ref[...], xn) + gbias_ref[...]

    q = _dot(xn, wm_ref[:, OFF_Q:OFF_K])
    k = _dot(xn, wm_ref[:, OFF_K:OFF_V]) * K_SCALE
    v = _dot(xn, wm_ref[:, OFF_V:OFF_O])

    row_i = lax.broadcasted_iota(jnp.int32, (CHUNK, CHUNK), 0)
    col_i = lax.broadcasted_iota(jnp.int32, (CHUNK, CHUNK), 1)
    causal = row_i >= col_i

    for c in range(block // CHUNK):
        r0 = c * CHUNK
        ig = gates[0:GATE_ROWS, r0:r0 + CHUNK]
        lf = _log_sigmoid(gates[GATE_ROWS:2 * GATE_ROWS, r0:r0 + CHUNK])
        b = _scan_lanes(lf, jnp.add, 0.0)
        a = ig - b
        m_prev = m_s[...]
        gmax = jnp.maximum(m_prev, _scan_lanes(a, jnp.maximum, -jnp.inf))
        m_t = b + gmax
        w_inter = jnp.exp(m_prev - gmax)
        g_last = gmax[:, CHUNK - 1:CHUNK]
        w_last = jnp.exp(a - g_last)
        floor = jnp.exp(-m_t)
        decay = w_inter[:, CHUNK - 1:CHUNK]
        m_s[...] = jnp.broadcast_to(m_t[:, CHUNK - 1:CHUNK], m_s.shape)

        rows = jnp.concatenate(
            [gmax, w_inter, w_last, floor,
             jnp.zeros((CHUNK - 4 * GATE_ROWS, CHUNK), F32)], axis=0)
        cols = rows.T

        for h in range(N_HEADS):
            qh = q[r0:r0 + CHUNK, h * DK:(h + 1) * DK]
            kh = k[r0:r0 + CHUNK, h * DK:(h + 1) * DK]
            vh = v[r0:r0 + CHUNK, h * DV:(h + 1) * DV]
            qb = qh.astype(BF16)
            kt = kh.T.astype(BF16)
            gmax_c = cols[:, h:h + 1]
            w_inter_c = cols[:, GATE_ROWS + h:GATE_ROWS + h + 1]
            w_last_c = cols[:, 2 * GATE_ROWS + h:2 * GATE_ROWS + h + 1]
            floor_c = cols[:, 3 * GATE_ROWS + h:3 * GATE_ROWS + h + 1]

            dmat = jnp.where(causal, jnp.exp(a[h:h + 1, :] - gmax_c), 0.0)
            sd = _dot(qb, kt) * dmat
            ct = ct_s[h]
            num = _dot(sd.astype(BF16), vh.astype(BF16)) + w_inter_c * _dot(qb, ct.astype(BF16))
            nh = n_s[h:h + 1, :]
            den = (jnp.sum(sd, axis=-1, keepdims=True)
                   + w_inter_c * jnp.sum(qh * nh, axis=-1, keepdims=True))
            hh = num * (1.0 / jnp.maximum(jnp.abs(den), floor_c))
            h_s[r0:r0 + CHUNK, h * DV:(h + 1) * DV] = hh

            dec = decay[h:h + 1, :]
            ct_s[h] = dec * ct + _dot(kt, (w_last_c * vh).astype(BF16))
            n_s[h:h + 1, :] = dec * nh + jnp.sum(w_last_c * kh, axis=0, keepdims=True)

    o = _dot(xn, wm_ref[:, OFF_O:OFF_U])
    y_a = _dot((jax.nn.sigmoid(o) * h_s[...]).astype(BF16), wa_ref[...])

    u = jax.nn.gelu(_dot(xn, wm_ref[:, OFF_U:OFF_VB]))
    vn = _rmsnorm(jax.nn.gelu(_dot(xn, wm_ref[:, OFF_VB:OFF_GA])), gv_ref[...])
    for g in range(N_GROUPS):
        w_tri = jnp.where(causal, ws_ref[g], 0.0).astype(BF16)
        bias_c = bst_ref[:, g:g + 1]
        for c in range(block // CHUNK):
            r0 = c * CHUNK
            mixed = _dot(w_tri, vn[r0:r0 + CHUNK, g * DG:(g + 1) * DG].astype(BF16)) + bias_c
            um_s[r0:r0 + CHUNK, g * DG:(g + 1) * DG] = u[r0:r0 + CHUNK, g * DG:(g + 1) * DG] * mixed
    y_b = _dot(um_s[...].astype(BF16), wb_ref[...])

    ga = _dot(xn, wm_ref[:, OFF_GA:OFF_GB])
    gb = _dot(xn, wm_ref[:, OFF_GB:P_MAIN])
    merged = jax.nn.sigmoid(ga) * y_a + jax.nn.sigmoid(gb) * y_b
    xo_ref[...] = x + _dot(merged.astype(BF16), wo_ref[...])

    @pl.when(s == pl.num_programs(1) - 1)
    def _():
        for h in range(N_HEADS):
            c_ref[h] = ct_s[h].T
        n_ref[...] = n_s[...]
        m_ref[...] = m_s[...]


def _resident(shape):
    nd = len(shape)
    return pl.BlockSpec(shape, lambda *_: (0,) * nd, pipeline_mode=pl.Buffered(1))


def _prompt_mixer(x, wm, wgt, gbias, g1, gv, ws, bst, wa, wb, wo):
    nb, seq, _ = x.shape
    block = PROMPT_BLOCK
    grid = (nb, seq // block)
    xspec = pl.BlockSpec((None, block, D_MODEL), lambda b, s: (b, s, 0))
    out_shape = (
        jax.ShapeDtypeStruct((nb, seq, D_MODEL), F32),
        jax.ShapeDtypeStruct((nb, N_HEADS, DV, DK), F32),
        jax.ShapeDtypeStruct((nb, N_HEADS, DK), F32),
        jax.ShapeDtypeStruct((nb, GATE_ROWS, V7X_LANES), F32),
    )
    out_specs = (
        xspec,
        pl.BlockSpec((None, N_HEADS, DV, DK), lambda b, s: (b, 0, 0, 0)),
        pl.BlockSpec((None, N_HEADS, DK), lambda b, s: (b, 0, 0)),
        pl.BlockSpec((None, GATE_ROWS, V7X_LANES), lambda b, s: (b, 0, 0)),
    )
    in_specs = [xspec] + [_resident(a.shape) for a in (wm, wgt, gbias, g1, gv, ws, bst, wa, wb, wo)]
    scratch = [
        pltpu.VMEM((N_HEADS, DK, DV), F32),
        pltpu.VMEM((N_HEADS, DK), F32),
        pltpu.VMEM((GATE_ROWS, V7X_LANES), F32),
        pltpu.VMEM((block, N_HEADS * DV), F32),
        pltpu.VMEM((block, D_B), F32),
    ]
    return pl.pallas_call(
        functools.partial(_mixer_kernel, block=block),
        grid=grid, in_specs=in_specs, out_specs=out_specs, out_shape=out_shape,
        scratch_shapes=scratch,
        compiler_params=pltpu.CompilerParams(
            dimension_semantics=("arbitrary", "arbitrary"), vmem_limit_bytes=V7X_VMEM_LIMIT),
        name="prompt_mixer",
    )(x, wm, wgt, gbias, g1, gv, ws, bst, wa, wb, wo)


def _conv_taps(up, prev, cw_ref, cb_ref, col0, width, row):
    cols = slice(col0, col0 + width)
    m1 = jnp.where(row == 0, prev[7:8], pltpu.roll(up, 1, 0))
    m2 = jnp.where(row == 0, prev[6:7], jnp.where(row == 1, prev[7:8], pltpu.roll(up, 2, 0)))
    return (cb_ref[:, cols] + cw_ref[0:1, cols] * m2 + cw_ref[1:2, cols] * m1
            + cw_ref[2:3, cols] * up)


def _ffn_kernel(x_ref, g2_ref, wup_ref, cw_ref, cb_ref, wdn_ref, gf_ref,
                xo_ref, conv_ref, carry_s, *, block, final):
    s = pl.program_id(1)

    @pl.when(s == 0)
    def _():
        carry_s[...] = jnp.zeros_like(carry_s)

    x = x_ref[...]
    xn = _rmsnorm(x, g2_ref[...]).astype(BF16)
    row = lax.broadcasted_iota(jnp.int32, (block, FFN_COL_TILE), 0)
    acc = jnp.zeros((block, D_MODEL), F32)
    for j in range(D_FF // FFN_COL_TILE):
        halves = []
        for half in range(2):
            col0 = half * D_FF + j * FFN_COL_TILE
            up = _dot(xn, wup_ref[:, col0:col0 + FFN_COL_TILE])
            prev = carry_s[:, col0:col0 + FFN_COL_TILE]
            carry_s[:, col0:col0 + FFN_COL_TILE] = up[block - V7X_SUBLANES:block]
            halves.append(_conv_taps(up, prev, cw_ref, cb_ref, col0, FFN_COL_TILE, row))
        act = (jax.nn.silu(halves[0]) * halves[1]).astype(BF16)
        acc = acc + _dot(act, wdn_ref[j * FFN_COL_TILE:(j + 1) * FFN_COL_TILE, :])
    y = x + acc
    if final:
        y = _rmsnorm(y, gf_ref[...])
    xo_ref[...] = y

    @pl.when(s == pl.num_programs(1) - 1)
    def _():
        conv_ref[...] = carry_s[V7X_SUBLANES - (CONV_W - 1):V7X_SUBLANES, :]


def _prompt_ffn(x, g2, wup, cw, cb, wdn, gf, final):
    nb, seq, _ = x.shape
    block = PROMPT_BLOCK
    grid = (nb, seq // block)
    xspec = pl.BlockSpec((None, block, D_MODEL), lambda b, s: (b, s, 0))
    out_shape = (
        jax.ShapeDtypeStruct((nb, seq, D_MODEL), F32),
        jax.ShapeDtypeStruct((nb, CONV_W - 1, 2 * D_FF), F32),
    )
    out_specs = (xspec, pl.BlockSpec((None, CONV_W - 1, 2 * D_FF), lambda b, s: (b, 0, 0)))
    in_specs = [xspec] + [_resident(a.shape) for a in (g2, wup, cw, cb, wdn, gf)]
    return pl.pallas_call(
        functools.partial(_ffn_kernel, block=block, final=final),
        grid=grid, in_specs=in_specs, out_specs=out_specs, out_shape=out_shape,
        scratch_shapes=[pltpu.VMEM((V7X_SUBLANES, 2 * D_FF), F32)],
        compiler_params=pltpu.CompilerParams(
            dimension_semantics=("arbitrary", "arbitrary"), vmem_limit_bytes=V7X_VMEM_LIMIT),
        name="prompt_ffn",
    )(x, g2, wup, cw, cb, wdn, gf)


def _sample_proj_kernel(x_ref, g1_ref, wm_ref, wgc_ref, z_ref, gates_ref):
    xn = _rmsnorm(x_ref[...], g1_ref[...]).astype(BF16)
    z_ref[...] = _dot(xn, wm_ref[...])
    gates_ref[...] = _dot(xn, wgc_ref[...])


def _sample_proj(x, g1, wm, wgc):
    n = x.shape[0]
    return pl.pallas_call(
        _sample_proj_kernel,
        out_shape=(jax.ShapeDtypeStruct((n, P_MAIN), F32),
                   jax.ShapeDtypeStruct((n, 2 * V7X_LANES), F32)),
        compiler_params=pltpu.CompilerParams(vmem_limit_bytes=V7X_VMEM_LIMIT),
        name="sample_proj",
    )(x, g1, wm, wgc)


def _sample_state_kernel(q_ref, k_ref, v_ref, gates_ref, gbias_ref, m_ref, n_ref, c_ref,
                         h_ref, co_ref, no_ref, mo_ref):
    tb = SAMPLE_BLOCK
    ig = gates_ref[:, 0:V7X_LANES] + gbias_ref[:, 0:V7X_LANES]
    lf = _log_sigmoid(gates_ref[:, V7X_LANES:] + gbias_ref[:, V7X_LANES:])
    inter = lf + m_ref[...]
    m_t = jnp.maximum(inter, ig)
    d_in = jnp.exp(ig - m_t)
    w_inter = jnp.exp(inter - m_t)
    floor = jnp.exp(-m_t)
    mo_ref[...] = m_t

    row8 = lax.broadcasted_iota(jnp.int32, (CHUNK, DK), 0)
    for h in range(N_HEADS):
        q8 = q_ref[:, h * DK:(h + 1) * DK]
        k8 = k_ref[:, h * DK:(h + 1) * DK] * K_SCALE
        v8 = v_ref[:, h * DV:(h + 1) * DV]
        d_h = d_in[:, h:h + 1]
        w_h = w_inter[:, h:h + 1]
        n8 = n_ref[:, h * DK:(h + 1) * DK]
        s = jnp.sum(q8 * k8, axis=-1, keepdims=True) * d_h
        den = s + w_h * jnp.sum(q8 * n8, axis=-1, keepdims=True)
        qb = q8.astype(BF16)
        inter_rows = [_dot_nt(qb, c_ref[j, h].astype(BF16))[j:j + 1] for j in range(tb)]
        num = s * v8 + w_h * jnp.concatenate(inter_rows, axis=0)
        h_ref[:, h * DV:(h + 1) * DV] = num / jnp.maximum(jnp.abs(den), floor[:, h:h + 1])
        no_ref[:, h * DK:(h + 1) * DK] = w_h * n8 + d_h * k8

        vt = jnp.concatenate([d_h * v8, jnp.zeros((CHUNK - tb, DV), F32)], axis=0).T.astype(BF16)
        kpad = jnp.concatenate([k8, jnp.zeros((CHUNK - tb, DK), F32)], axis=0)
        for j in range(tb):
            kj = jnp.where(row8 == j, kpad, 0.0).astype(BF16)
            co_ref[j, h] = w_inter[j:j + 1, h:h + 1] * c_ref[j, h] + _dot(vt, kj)


def _sample_state(z, gates, gbias, m_pad, n_state, c_state):
    n = z.shape[0]
    tb = SAMPLE_BLOCK
    in_specs = [
        pl.BlockSpec((tb, N_HEADS * DK), lambda i: (i, OFF_Q // (N_HEADS * DK))),
        pl.BlockSpec((tb, N_HEADS * DK), lambda i: (i, OFF_K // (N_HEADS * DK))),
        pl.BlockSpec((tb, N_HEADS * DV), lambda i: (i, OFF_V // (N_HEADS * DV))),
        pl.BlockSpec((tb, 2 * V7X_LANES), lambda i: (i, 0)),
        pl.BlockSpec((1, 2 * V7X_LANES), lambda i: (0, 0)),
        pl.BlockSpec((tb, V7X_LANES), lambda i: (i, 0)),
        pl.BlockSpec((tb, N_HEADS * DK), lambda i: (i, 0)),
        pl.BlockSpec((tb, N_HEADS, DV, DK), lambda i: (i, 0, 0, 0)),
    ]
    out_shape = (
        jax.ShapeDtypeStruct((n, N_HEADS * DV), F32),
        jax.ShapeDtypeStruct((n, N_HEADS, DV, DK), F32),
        jax.ShapeDtypeStruct((n, N_HEADS * DK), F32),
        jax.ShapeDtypeStruct((n, V7X_LANES), F32),
    )
    out_specs = (
        pl.BlockSpec((tb, N_HEADS * DV), lambda i: (i, 0)),
        pl.BlockSpec((tb, N_HEADS, DV, DK), lambda i: (i, 0, 0, 0)),
        pl.BlockSpec((tb, N_HEADS * DK), lambda i: (i, 0)),
        pl.BlockSpec((tb, V7X_LANES), lambda i: (i, 0)),
    )
    return pl.pallas_call(
        _sample_state_kernel,
        grid=(n // tb,), in_specs=in_specs, out_specs=out_specs, out_shape=out_shape,
        compiler_params=pltpu.CompilerParams(
            dimension_semantics=("arbitrary",), vmem_limit_bytes=V7X_VMEM_LIMIT),
        name="sample_state",
    )(z, z, z, gates, gbias, m_pad, n_state, c_state)


def _sample_mixer_kernel(x_ref, z_ref, h_ref, gv_ref, ws0_ref, bs0_ref, wa_ref, wb_ref, wo_ref,
                         xo_ref, vn_ref):
    y_a = _dot((jax.nn.sigmoid(z_ref[:, OFF_O:OFF_U]) * h_ref[...]).astype(BF16), wa_ref[...])
    u = jax.nn.gelu(z_ref[:, OFF_U:OFF_VB])
    vn = _rmsnorm(jax.nn.gelu(z_ref[:, OFF_VB:OFF_GA]), gv_ref[...])
    vn_ref[...] = vn
    mixed = ws0_ref[...] * vn + bs0_ref[...]
    y_b = _dot((u * mixed).astype(BF16), wb_ref[...])
    merged = (jax.nn.sigmoid(z_ref[:, OFF_GA:OFF_GB]) * y_a
              + jax.nn.sigmoid(z_ref[:, OFF_GB:P_MAIN]) * y_b)
    xo_ref[...] = x_ref[...] + _dot(merged.astype(BF16), wo_ref[...])


def _sample_mixer(x, z, h, gv, ws0, bs0, wa, wb, wo):
    n = x.shape[0]
    return pl.pallas_call(
        _sample_mixer_kernel,
        out_shape=(jax.ShapeDtypeStruct((n, D_MODEL), F32), jax.ShapeDtypeStruct((n, D_B), F32)),
        compiler_params=pltpu.CompilerParams(vmem_limit_bytes=V7X_VMEM_LIMIT),
        name="sample_mixer",
    )(x, z, h, gv, ws0, bs0, wa, wb, wo)


def _sample_ffn_kernel(x_ref, g2_ref, wup_ref, cw_ref, cb_ref, wdn_ref, gf_ref, buf_ref,
                       xo_ref, nbuf_ref, *, final):
    x = x_ref[...]
    xn = _rmsnorm(x, g2_ref[...]).astype(BF16)
    width = 2 * D_FF
    acc = jnp.zeros(x.shape, F32)
    for j in range(D_FF // FFN_COL_TILE):
        halves = []
        for half in range(2):
            col0 = half * D_FF + j * FFN_COL_TILE
            cols = slice(col0, col0 + FFN_COL_TILE)
            up = _dot(xn, wup_ref[:, cols])
            b0 = buf_ref[:, col0:col0 + FFN_COL_TILE]
            b1 = buf_ref[:, width + col0:width + col0 + FFN_COL_TILE]
            nbuf_ref[:, col0:col0 + FFN_COL_TILE] = b1
            nbuf_ref[:, width + col0:width + col0 + FFN_COL_TILE] = up
            halves.append(cb_ref[:, cols] + cw_ref[0:1, cols] * b0 + cw_ref[1:2, cols] * b1
                          + cw_ref[2:3, cols] * up)
        act = (jax.nn.silu(halves[0]) * halves[1]).astype(BF16)
        acc = acc + _dot(act, wdn_ref[j * FFN_COL_TILE:(j + 1) * FFN_COL_TILE, :])
    y = x + acc
    if final:
        y = _rmsnorm(y, gf_ref[...])
    xo_ref[...] = y


def _sample_ffn(x, g2, wup, cw, cb, wdn, gf, buf, final):
    n = x.shape[0]
    return pl.pallas_call(
        functools.partial(_sample_ffn_kernel, final=final),
        out_shape=(jax.ShapeDtypeStruct((n, D_MODEL), F32),
                   jax.ShapeDtypeStruct((n, (CONV_W - 1) * 2 * D_FF), F32)),
        compiler_params=pltpu.CompilerParams(vmem_limit_bytes=V7X_VMEM_LIMIT),
        name="sample_ffn",
    )(x, g2, wup, cw, cb, wdn, gf, buf)


def kernel(x_prompt, x_sample, state_mlstm_C, state_mlstm_n, state_mlstm_m, state_ffn_conv, w_in, b_igate, b_fgate, g_norm1, g_vnorm, w_spatial, b_spatial, w_branch_a, w_branch_b, w_out, g_norm2, w_up, conv_w, conv_b, w_down, g_final):
    depth = w_in.shape[0]
    n_dec = x_sample.shape[0]
    xp = x_prompt
    xs = x_sample.reshape(n_dec, D_MODEL)
    gf = g_final.reshape(1, D_MODEL)
    outs = [[] for _ in range(9)]
    for l in range(depth):
        wm = jnp.concatenate([w_in[l, :, :GATE_LO], w_in[l, :, GATE_HI:]], axis=1).astype(BF16)
        w_gate = w_in[l, :, GATE_LO:GATE_HI]
        row_pad = ((0, GATE_ROWS - N_HEADS), (0, 0))
        wgt = jnp.concatenate([jnp.pad(w_gate[:, :N_HEADS].T, row_pad),
                               jnp.pad(w_gate[:, N_HEADS:].T, row_pad)], axis=0).astype(BF16)
        wgc = jnp.concatenate(
            [jnp.pad(w_gate[:, :N_HEADS], ((0, 0), (0, V7X_LANES - N_HEADS))),
             jnp.pad(w_gate[:, N_HEADS:], ((0, 0), (0, V7X_LANES - N_HEADS)))], axis=1).astype(BF16)
        gbias_col = jnp.concatenate(
            [jnp.pad(b_igate[l], (0, GATE_ROWS - N_HEADS)),
             jnp.pad(b_fgate[l], (0, GATE_ROWS - N_HEADS))]).reshape(2 * GATE_ROWS, 1)
        gbias_row = jnp.concatenate(
            [jnp.pad(b_igate[l], (0, V7X_LANES - N_HEADS)),
             jnp.pad(b_fgate[l], (0, V7X_LANES - N_HEADS))]).reshape(1, 2 * V7X_LANES)
        g1 = g_norm1[l].reshape(1, D_MODEL)
        gv = g_vnorm[l].reshape(1, D_B)
        g2 = g_norm2[l].reshape(1, D_MODEL)
        bst = b_spatial[l].T
        ws0 = jnp.repeat(w_spatial[l, :, 0, 0], DG).reshape(1, D_B)
        bs0 = jnp.repeat(b_spatial[l, :, 0], DG).reshape(1, D_B)
        wa = w_branch_a[l].astype(BF16)
        wb = w_branch_b[l].astype(BF16)
        wo = w_out[l].astype(BF16)
        wup = w_up[l].astype(BF16)
        wdn = w_down[l].astype(BF16)
        cw = conv_w[l]
        cb = conv_b[l].reshape(1, 2 * D_FF)
        final = l == depth - 1

        xp, c_p, n_p, m_p = _prompt_mixer(xp, wm, wgt, gbias_col, g1, gv, w_spatial[l], bst, wa, wb, wo)
        xp, conv_p = _prompt_ffn(xp, g2, wup, cw, cb, wdn, gf, final)

        z, gates = _sample_proj(xs, g1, wm, wgc)
        m_pad = jnp.pad(state_mlstm_m[l], ((0, 0), (0, V7X_LANES - N_HEADS)))
        h, c_s, n_s, m_s = _sample_state(
            z, gates, gbias_row, m_pad, state_mlstm_n[l].reshape(n_dec, N_HEADS * DK), state_mlstm_C[l])
        xs, vn_s = _sample_mixer(xs, z, h, gv, ws0, bs0, wa, wb, wo)
        buf = state_ffn_conv[l].reshape(n_dec, (CONV_W - 1) * 2 * D_FF)
        xs, conv_s = _sample_ffn(xs, g2, wup, cw, cb, wdn, gf, buf, final)

        for lst, val in zip(outs, (c_p, n_p, m_p[:, :N_HEADS, 0], conv_p,
                                   c_s, n_s.reshape(n_dec, N_HEADS, DK), m_s[:, :N_HEADS],
                                   conv_s.reshape(n_dec, CONV_W - 1, 2 * D_FF),
                                   vn_s.reshape(n_dec, 1, D_B))):
            lst.append(val)
    return (xp, xs.reshape(n_dec, 1, D_MODEL)) + tuple(jnp.stack(o) for o in outs)
```

```python
import functools

import jax
import jax.numpy as jnp
from jax import lax
from jax.experimental import pallas as pl
from jax.experimental.pallas import tpu as pltpu

D_MODEL = 1024
N_HEADS = 4
DK = 128
DV = 256
CHUNK = 128
D_B = 1024
N_GROUPS = 4
DG = D_B // N_GROUPS
D_FF = 2816
CONV_W = 3
EPS = 1e-6
K_SCALE = DK ** -0.5

OFF_Q = 0
OFF_K = OFF_Q + N_HEADS * DK
OFF_V = OFF_K + N_HEADS * DK
OFF_O = OFF_V + N_HEADS * DV
OFF_U = OFF_O + N_HEADS * DV
OFF_VB = OFF_U + D_B
OFF_GA = OFF_VB + D_B
OFF_GB = OFF_GA + D_MODEL
P_MAIN = OFF_GB + D_MODEL
GATE_LO = 2 * N_HEADS * DK + 2 * N_HEADS * DV
GATE_HI = GATE_LO + 2 * N_HEADS

V7X_LANES = 128
V7X_SUBLANES = 8
GATE_ROWS = V7X_SUBLANES
V7X_VMEM_LIMIT = 56 * 1024 * 1024
FFN_COL_TILE = 256
PROMPT_BLOCK = 256
SAMPLE_BLOCK = 8

F32 = jnp.float32
BF16 = jnp.bfloat16


def _dot(a, b):
    return jnp.dot(a, b, preferred_element_type=F32)


def _dot_nt(a, b):
    return lax.dot_general(a, b, (((1,), (1,)), ((), ())), preferred_element_type=F32)


def _rmsnorm(x, g):
    r = lax.rsqrt(jnp.mean(x * x, axis=-1, keepdims=True) + EPS)
    return x * r * g


def _log_sigmoid(x):
    return jnp.minimum(x, 0.0) - jnp.log1p(jnp.exp(-jnp.abs(x)))


def _scan_lanes(x, op, fill):
    lane = lax.broadcasted_iota(jnp.int32, x.shape, 1)
    k = 1
    while k < x.shape[1]:
        shifted = pltpu.roll(x, k, 1)
        x = op(x, jnp.where(lane >= k, shifted, fill))
        k *= 2
    return x


def _mixer_kernel(x_ref, wm_ref, wgt_ref, gbias_ref, g1_ref, gv_ref, ws_ref, bst_ref,
                  wa_ref, wb_ref, wo_ref,
                  xo_ref, c_ref, n_ref, m_ref,
                  ct_s, n_s, m_s, h_s, um_s, *, block):
    s = pl.program_id(1)

    @pl.when(s == 0)
    def _():
        ct_s[...] = jnp.zeros_like(ct_s)
        n_s[...] = jnp.zeros_like(n_s)
        m_s[...] = jnp.zeros_like(m_s)

    x = x_ref[...]
    xn = _rmsnorm(x, g1_ref[...]).astype(BF16)

    gates = _dot_nt(wgt_ref[...], xn) + gbias_ref[...]

    q = _dot(xn, wm_ref[:, OFF_Q:OFF_K])
    k = _dot(xn, wm_ref[:, OFF_K:OFF_V]) * K_SCALE
    v = _dot(xn, wm_ref[:, OFF_V:OFF_O])

    row_i = lax.broadcasted_iota(jnp.int32, (CHUNK, CHUNK), 0)
    col_i = lax.broadcasted_iota(jnp.int32, (CHUNK, CHUNK), 1)
    causal = row_i >= col_i

    for c in range(block // CHUNK):
        r0 = c * CHUNK
        ig = gates[0:GATE_ROWS, r0:r0 + CHUNK]
        lf = _log_sigmoid(gates[GATE_ROWS:2 * GATE_ROWS, r0:r0 + CHUNK])
        b = _scan_lanes(lf, jnp.add, 0.0)
        a = ig - b
        m_prev = m_s[...]
        gmax = jnp.maximum(m_prev, _scan_lanes(a, jnp.maximum, -jnp.inf))
        m_t = b + gmax
        w_inter = jnp.exp(m_prev - gmax)
        g_last = gmax[:, CHUNK - 1:CHUNK]
        w_last = jnp.exp(a - g_last)
        floor = jnp.exp(-m_t)
        decay = w_inter[:, CHUNK - 1:CHUNK]
        m_s[...] = jnp.broadcast_to(m_t[:, CHUNK - 1:CHUNK], m_s.shape)

        rows = jnp.concatenate(
            [gmax, w_inter, w_last, floor,
             jnp.zeros((CHUNK - 4 * GATE_ROWS, CHUNK), F32)], axis=0)
        cols = rows.T

        for h in range(N_HEADS):
            qh = q[r0:r0 + CHUNK, h * DK:(h + 1) * DK]
            kh = k[r0:r0 + CHUNK, h * DK:(h + 1) * DK]
            vh = v[r0:r0 + CHUNK, h * DV:(h + 1) * DV]
            qb = qh.astype(BF16)
            kt = kh.T.astype(BF16)
            gmax_c = cols[:, h:h + 1]
            w_inter_c = cols[:, GATE_ROWS + h:GATE_ROWS + h + 1]
            w_last_c = cols[:, 2 * GATE_ROWS + h:2 * GATE_ROWS + h + 1]
            floor_c = cols[:, 3 * GATE_ROWS + h:3 * GATE_ROWS + h + 1]

            dmat = jnp.where(causal, jnp.exp(a[h:h + 1, :] - gmax_c), 0.0)
            sd = _dot(qb, kt) * dmat
            ct = ct_s[h]
            num = _dot(sd.astype(BF16), vh.astype(BF16)) + w_inter_c * _dot(qb, ct.astype(BF16))
            nh = n_s[h:h + 1, :]
            den = (jnp.sum(sd, axis=-1, keepdims=True)
                   + w_inter_c * jnp.sum(qh * nh, axis=-1, keepdims=True))
            hh = num * (1.0 / jnp.maximum(jnp.abs(den), floor_c))
            h_s[r0:r0 + CHUNK, h * DV:(h + 1) * DV] = hh

            dec = decay[h:h + 1, :]
            ct_s[h] = dec * ct + _dot(kt, (w_last_c * vh).astype(BF16))
            n_s[h:h + 1, :] = dec * nh + jnp.sum(w_last_c * kh, axis=0, keepdims=True)

    o = _dot(xn, wm_ref[:, OFF_O:OFF_U])
    y_a = _dot((jax.nn.sigmoid(o) * h_s[...]).astype(BF16), wa_ref[...])

    u = jax.nn.gelu(_dot(xn, wm_ref[:, OFF_U:OFF_VB]))
    vn = _rmsnorm(jax.nn.gelu(_dot(xn, wm_ref[:, OFF_VB:OFF_GA])), gv_ref[...])
    for g in range(N_GROUPS):
        w_tri = jnp.where(causal, ws_ref[g], 0.0).astype(BF16)
        bias_c = bst_ref[:, g:g + 1]
        for c in range(block // CHUNK):
            r0 = c * CHUNK
            mixed = _dot(w_tri, vn[r0:r0 + CHUNK, g * DG:(g + 1) * DG].astype(BF16)) + bias_c
            um_s[r0:r0 + CHUNK, g * DG:(g + 1) * DG] = u[r0:r0 + CHUNK, g * DG:(g + 1) * DG] * mixed
    y_b = _dot(um_s[...].astype(BF16), wb_ref[...])

    ga = _dot(xn, wm_ref[:, OFF_GA:OFF_GB])
    gb = _dot(xn, wm_ref[:, OFF_GB:P_MAIN])
    merged = jax.nn.sigmoid(ga) * y_a + jax.nn.sigmoid(gb) * y_b
    xo_ref[...] = x + _dot(merged.astype(BF16), wo_ref[...])

    @pl.when(s == pl.num_programs(1) - 1)
    def _():
        for h in range(N_HEADS):
            c_ref[h] = ct_s[h].T
        n_ref[...] = n_s[...]
        m_ref[...] = m_s[...]


def _resident(shape):
    nd = len(shape)
    return pl.BlockSpec(shape, lambda *_: (0,) * nd, pipeline_mode=pl.Buffered(1))


def _prompt_mixer(x, wm, wgt, gbias, g1, gv, ws, bst, wa, wb, wo):
    nb, seq, _ = x.shape
    block = PROMPT_BLOCK
    grid = (nb, seq // block)
    xspec = pl.BlockSpec((None, block, D_MODEL), lambda b, s: (b, s, 0))
    out_shape = (
        jax.ShapeDtypeStruct((nb, seq, D_MODEL), F32),
        jax.ShapeDtypeStruct((nb, N_HEADS, DV, DK), F32),
        jax.ShapeDtypeStruct((nb, N_HEADS, DK), F32),
        jax.ShapeDtypeStruct((nb, GATE_ROWS, V7X_LANES), F32),
    )
    out_specs = (
        xspec,
        pl.BlockSpec((None, N_HEADS, DV, DK), lambda b, s: (b, 0, 0, 0)),
        pl.BlockSpec((None, N_HEADS, DK), lambda b, s: (b, 0, 0)),
        pl.BlockSpec((None, GATE_ROWS, V7X_LANES), lambda b, s: (b, 0, 0)),
    )
    in_specs = [xspec] + [_resident(a.shape) for a in (wm, wgt, gbias, g1, gv, ws, bst, wa, wb, wo)]
    scratch = [
        pltpu.VMEM((N_HEADS, DK, DV), F32),
        pltpu.VMEM((N_HEADS, DK), F32),
        pltpu.VMEM((GATE_ROWS, V7X_LANES), F32),
        pltpu.VMEM((block, N_HEADS * DV), F32),
        pltpu.VMEM((block, D_B), F32),
    ]
    return pl.pallas_call(
        functools.partial(_mixer_kernel, block=block),
        grid=grid, in_specs=in_specs, out_specs=out_specs, out_shape=out_shape,
        scratch_shapes=scratch,
        compiler_params=pltpu.CompilerParams(
            dimension_semantics=("arbitrary", "arbitrary"), vmem_limit_bytes=V7X_VMEM_LIMIT),
        name="prompt_mixer",
    )(x, wm, wgt, gbias, g1, gv, ws, bst, wa, wb, wo)


def _conv_taps(up, carry_s, cw_ref, cb_ref, cols):
    sub = V7X_SUBLANES
    rows = up.shape[0]
    last1 = up[rows - sub:rows]
    last2 = up[rows - 2 * sub:rows - sub]
    first = lax.broadcasted_iota(jnp.int32, last1.shape, 0) == 0
    back1 = jnp.where(first, carry_s[sub - 1:sub, cols], pltpu.roll(last1, 1, 0))
    back2 = jnp.where(first, carry_s[sub - 2:sub - 1, cols], pltpu.roll(last2, 1, 0))
    carry_s[sub - 2:sub - 1, cols] = last2[sub - 1:sub]
    carry_s[sub - 1:sub, cols] = last1[sub - 1:sub]
    m1 = jnp.concatenate([back1, up[0:rows - sub]], axis=0)
    m2 = jnp.concatenate([back2, back1, up[0:rows - 2 * sub]], axis=0)
    return (cb_ref[:, cols] + cw_ref[0:1, cols] * m2 + cw_ref[1:2, cols] * m1
            + cw_ref[2:3, cols] * up)


def _perm_pitch(block):
    return block // V7X_SUBLANES + V7X_SUBLANES


def _ffn_kernel(x_ref, g2_ref, wup_ref, cw_ref, cb_ref, wdn_ref, gf_ref,
                xo_ref, conv_ref, carry_s, act_s, perm_s, *, block, final):
    s = pl.program_id(1)
    sub, lanes = V7X_SUBLANES, V7X_LANES
    groups = block // sub
    chunks = D_MODEL // lanes

    @pl.when(s == 0)
    def _():
        carry_s[...] = jnp.zeros_like(carry_s)

    pitch = _perm_pitch(block)
    for c in range(chunks):
        for i in range(sub):
            perm_s[c, i * pitch:i * pitch + groups] = x_ref[i * groups:(i + 1) * groups, c * lanes:(c + 1) * lanes]
    x = jnp.concatenate(
        [jnp.concatenate([perm_s[c, pl.ds(r, sub, stride=pitch), :] for c in range(chunks)], axis=1)
         for r in range(groups)], axis=0)
    xn = _rmsnorm(x, g2_ref[...]).astype(BF16)
    for j in range(D_FF // FFN_COL_TILE):
        halves = []
        for half in range(2):
            cols = slice(half * D_FF + j * FFN_COL_TILE, half * D_FF + (j + 1) * FFN_COL_TILE)
            up = _dot(xn, wup_ref[:, cols])
            halves.append(_conv_taps(up, carry_s, cw_ref, cb_ref, cols))
        act_s[:, j * FFN_COL_TILE:(j + 1) * FFN_COL_TILE] = (jax.nn.silu(halves[0]) * halves[1]).astype(BF16)
    y = x + _dot(act_s[...], wdn_ref[...])
    if final:
        y = _rmsnorm(y, gf_ref[...])
    for r in range(groups):
        for c in range(chunks):
            perm_s[c, pl.ds(r, sub, stride=pitch), :] = y[r * sub:(r + 1) * sub, c * lanes:(c + 1) * lanes]
    for c in range(chunks):
        for i in range(sub):
            xo_ref[i * groups:(i + 1) * groups, c * lanes:(c + 1) * lanes] = perm_s[c, i * pitch:i * pitch + groups]

    @pl.when(s == pl.num_programs(1) - 1)
    def _():
        conv_ref[...] = carry_s[V7X_SUBLANES - (CONV_W - 1):V7X_SUBLANES, :]


def _prompt_ffn(x, g2, wup, cw, cb, wdn, gf, final):
    nb, seq, _ = x.shape
    block = PROMPT_BLOCK
    grid = (nb, seq // block)
    xspec = pl.BlockSpec((None, block, D_MODEL), lambda b, s: (b, s, 0))
    out_shape = (
        jax.ShapeDtypeStruct((nb, seq, D_MODEL), F32),
        jax.ShapeDtypeStruct((nb, CONV_W - 1, 2 * D_FF), F32),
    )
    out_specs = (xspec, pl.BlockSpec((None, CONV_W - 1, 2 * D_FF), lambda b, s: (b, 0, 0)))
    in_specs = [xspec] + [_resident(a.shape) for a in (g2, wup, cw, cb, wdn, gf)]
    return pl.pallas_call(
        functools.partial(_ffn_kernel, block=block, final=final),
        grid=grid, in_specs=in_specs, out_specs=out_specs, out_shape=out_shape,
        scratch_shapes=[pltpu.VMEM((V7X_SUBLANES, 2 * D_FF), F32),
                        pltpu.VMEM((block, D_FF), BF16),
                        pltpu.VMEM((D_MODEL // V7X_LANES, V7X_SUBLANES * _perm_pitch(block), V7X_LANES),
                                   F32)],
        compiler_params=pltpu.CompilerParams(
            dimension_semantics=("arbitrary", "arbitrary"), vmem_limit_bytes=V7X_VMEM_LIMIT),
        name="prompt_ffn",
    )(x, g2, wup, cw, cb, wdn, gf)


def _sample_proj_kernel(x_ref, g1_ref, wm_ref, wgc_ref, z_ref, gates_ref):
    xn = _rmsnorm(x_ref[...], g1_ref[...]).astype(BF16)
    z_ref[...] = _dot(xn, wm_ref[...])
    gates_ref[...] = _dot(xn, wgc_ref[...])


def _sample_proj(x, g1, wm, wgc):
    n = x.shape[0]
    return pl.pallas_call(
        _sample_proj_kernel,
        out_shape=(jax.ShapeDtypeStruct((n, P_MAIN), F32),
                   jax.ShapeDtypeStruct((n, 2 * V7X_LANES), F32)),
        compiler_params=pltpu.CompilerParams(vmem_limit_bytes=V7X_VMEM_LIMIT),
        name="sample_proj",
    )(x, g1, wm, wgc)


def _sample_state_kernel(q_ref, k_ref, v_ref, gates_ref, gbias_ref, m_ref, n_ref, c_ref, *rest):
    h_ref, co_ref, no_ref, mo_ref = rest[-4:]
    tb = SAMPLE_BLOCK
    ig = gates_ref[:, 0:V7X_LANES] + gbias_ref[:, 0:V7X_LANES]
    lf = _log_sigmoid(gates_ref[:, V7X_LANES:] + gbias_ref[:, V7X_LANES:])
    inter = lf + m_ref[...]
    m_t = jnp.maximum(inter, ig)
    d_in = jnp.exp(ig - m_t)
    w_inter = jnp.exp(inter - m_t)
    floor = jnp.exp(-m_t)
    mo_ref[...] = m_t

    row8 = lax.broadcasted_iota(jnp.int32, (CHUNK, DK), 0)
    for h in range(N_HEADS):
        q8 = q_ref[:, h * DK:(h + 1) * DK]
        k8 = k_ref[:, h * DK:(h + 1) * DK] * K_SCALE
        v8 = v_ref[:, h * DV:(h + 1) * DV]
        d_h = d_in[:, h:h + 1]
        w_h = w_inter[:, h:h + 1]
        n8 = n_ref[:, h * DK:(h + 1) * DK]
        s = jnp.sum(q8 * k8, axis=-1, keepdims=True) * d_h
        den = s + w_h * jnp.sum(q8 * n8, axis=-1, keepdims=True)
        qb = q8.astype(BF16)
        inter_rows = [_dot_nt(qb, c_ref[j, h].astype(BF16))[j:j + 1] for j in range(tb)]
        num = s * v8 + w_h * jnp.concatenate(inter_rows, axis=0)
        h_ref[:, h * DV:(h + 1) * DV] = num / jnp.maximum(jnp.abs(den), floor[:, h:h + 1])
        no_ref[:, h * DK:(h + 1) * DK] = w_h * n8 + d_h * k8

        vt = jnp.concatenate([d_h * v8, jnp.zeros((CHUNK - tb, DV), F32)], axis=0).T.astype(BF16)
        kpad = jnp.concatenate([k8, jnp.zeros((CHUNK - tb, DK), F32)], axis=0)
        for j in range(tb):
            kj = jnp.where(row8 == j, kpad, 0.0).astype(BF16)
            co_ref[j, h] = w_inter[j:j + 1, h:h + 1] * c_ref[j, h] + _dot(vt, kj)


def _sample_state(z, gates, gbias, m_pad, n_state, c_state, layer, c_stack):
    n = z.shape[0]
    tb = SAMPLE_BLOCK
    c_block = pl.BlockSpec((None, tb, N_HEADS, DV, DK), lambda i: (layer, i, 0, 0, 0))
    in_specs = [
        pl.BlockSpec((tb, N_HEADS * DK), lambda i: (i, OFF_Q // (N_HEADS * DK))),
        pl.BlockSpec((tb, N_HEADS * DK), lambda i: (i, OFF_K // (N_HEADS * DK))),
        pl.BlockSpec((tb, N_HEADS * DV), lambda i: (i, OFF_V // (N_HEADS * DV))),
        pl.BlockSpec((tb, 2 * V7X_LANES), lambda i: (i, 0)),
        pl.BlockSpec((1, 2 * V7X_LANES), lambda i: (0, 0)),
        pl.BlockSpec((tb, V7X_LANES), lambda i: (i, 0)),
        pl.BlockSpec((tb, N_HEADS * DK), lambda i: (i, 0)),
        c_block,
    ]
    args = [z, z, z, gates, gbias, m_pad, n_state, c_state]
    aliases = {}
    if c_stack is not None:
        in_specs.append(pl.BlockSpec(memory_space=pl.ANY))
        aliases = {len(args): 1}
        args.append(c_stack)
    out_shape = (
        jax.ShapeDtypeStruct((n, N_HEADS * DV), F32),
        jax.ShapeDtypeStruct(c_state.shape, F32),
        jax.ShapeDtypeStruct((n, N_HEADS * DK), F32),
        jax.ShapeDtypeStruct((n, V7X_LANES), F32),
    )
    out_specs = (
        pl.BlockSpec((tb, N_HEADS * DV), lambda i: (i, 0)),
        c_block,
        pl.BlockSpec((tb, N_HEADS * DK), lambda i: (i, 0)),
        pl.BlockSpec((tb, V7X_LANES), lambda i: (i, 0)),
    )
    return pl.pallas_call(
        _sample_state_kernel,
        grid=(n // tb,), in_specs=in_specs, out_specs=out_specs, out_shape=out_shape,
        input_output_aliases=aliases,
        compiler_params=pltpu.CompilerParams(
            dimension_semantics=("arbitrary",), vmem_limit_bytes=V7X_VMEM_LIMIT),
        name="sample_state",
    )(*args)


def _sample_mixer_kernel(x_ref, z_ref, h_ref, gv_ref, ws0_ref, bs0_ref, wa_ref, wb_ref, wo_ref,
                         xo_ref, vn_ref):
    y_a = _dot((jax.nn.sigmoid(z_ref[:, OFF_O:OFF_U]) * h_ref[...]).astype(BF16), wa_ref[...])
    u = jax.nn.gelu(z_ref[:, OFF_U:OFF_VB])
    vn = _rmsnorm(jax.nn.gelu(z_ref[:, OFF_VB:OFF_GA]), gv_ref[...])
    vn_ref[...] = vn
    mixed = ws0_ref[...] * vn + bs0_ref[...]
    y_b = _dot((u * mixed).astype(BF16), wb_ref[...])
    merged = (jax.nn.sigmoid(z_ref[:, OFF_GA:OFF_GB]) * y_a
              + jax.nn.sigmoid(z_ref[:, OFF_GB:P_MAIN]) * y_b)
    xo_ref[...] = x_ref[...] + _dot(merged.astype(BF16), wo_ref[...])


def _sample_mixer(x, z, h, gv, ws0, bs0, wa, wb, wo):
    n = x.shape[0]
    return pl.pallas_call(
        _sample_mixer_kernel,
        out_shape=(jax.ShapeDtypeStruct((n, D_MODEL), F32), jax.ShapeDtypeStruct((n, D_B), F32)),
        compiler_params=pltpu.CompilerParams(vmem_limit_bytes=V7X_VMEM_LIMIT),
        name="sample_mixer",
    )(x, z, h, gv, ws0, bs0, wa, wb, wo)


def _sample_ffn_kernel(x_ref, g2_ref, wup_ref, cw_ref, cb_ref, wdn_ref, gf_ref, buf_ref,
                       xo_ref, nbuf_ref, *, final):
    x = x_ref[...]
    xn = _rmsnorm(x, g2_ref[...]).astype(BF16)
    width = 2 * D_FF
    acc = jnp.zeros(x.shape, F32)
    for j in range(D_FF // FFN_COL_TILE):
        halves = []
        for half in range(2):
            col0 = half * D_FF + j * FFN_COL_TILE
            cols = slice(col0, col0 + FFN_COL_TILE)
            up = _dot(xn, wup_ref[:, cols])
            b0 = buf_ref[:, col0:col0 + FFN_COL_TILE]
            b1 = buf_ref[:, width + col0:width + col0 + FFN_COL_TILE]
            nbuf_ref[:, col0:col0 + FFN_COL_TILE] = b1
            nbuf_ref[:, width + col0:width + col0 + FFN_COL_TILE] = up
            halves.append(cb_ref[:, cols] + cw_ref[0:1, cols] * b0 + cw_ref[1:2, cols] * b1
                          + cw_ref[2:3, cols] * up)
        act = (jax.nn.silu(halves[0]) * halves[1]).astype(BF16)
        acc = acc + _dot(act, wdn_ref[j * FFN_COL_TILE:(j + 1) * FFN_COL_TILE, :])
    y = x + acc
    if final:
        y = _rmsnorm(y, gf_ref[...])
    xo_ref[...] = y


def _sample_ffn(x, g2, wup, cw, cb, wdn, gf, buf, final):
    n = x.shape[0]
    return pl.pallas_call(
        functools.partial(_sample_ffn_kernel, final=final),
        out_shape=(jax.ShapeDtypeStruct((n, D_MODEL), F32),
                   jax.ShapeDtypeStruct((n, (CONV_W - 1) * 2 * D_FF), F32)),
        compiler_params=pltpu.CompilerParams(vmem_limit_bytes=V7X_VMEM_LIMIT),
        name="sample_ffn",
    )(x, g2, wup, cw, cb, wdn, gf, buf)


def kernel(x_prompt, x_sample, state_mlstm_C, state_mlstm_n, state_mlstm_m, state_ffn_conv, w_in, b_igate, b_fgate, g_norm1, g_vnorm, w_spatial, b_spatial, w_branch_a, w_branch_b, w_out, g_norm2, w_up, conv_w, conv_b, w_down, g_final):
    depth = w_in.shape[0]
    n_dec = x_sample.shape[0]
    xp = x_prompt
    xs = x_sample.reshape(n_dec, D_MODEL)
    gf = g_final.reshape(1, D_MODEL)
    small = [[] for _ in range(8)]
    c_stack = None
    for l in range(depth):
        wm = jnp.concatenate([w_in[l, :, :GATE_LO], w_in[l, :, GATE_HI:]], axis=1).astype(BF16)
        w_gate = w_in[l, :, GATE_LO:GATE_HI]
        row_pad = ((0, GATE_ROWS - N_HEADS), (0, 0))
        wgt = jnp.concatenate([jnp.pad(w_gate[:, :N_HEADS].T, row_pad),
                               jnp.pad(w_gate[:, N_HEADS:].T, row_pad)], axis=0).astype(BF16)
        wgc = jnp.concatenate(
            [jnp.pad(w_gate[:, :N_HEADS], ((0, 0), (0, V7X_LANES - N_HEADS))),
             jnp.pad(w_gate[:, N_HEADS:], ((0, 0), (0, V7X_LANES - N_HEADS)))], axis=1).astype(BF16)
        gbias_col = jnp.concatenate(
            [jnp.pad(b_igate[l], (0, GATE_ROWS - N_HEADS)),
             jnp.pad(b_fgate[l], (0, GATE_ROWS - N_HEADS))]).reshape(2 * GATE_ROWS, 1)
        gbias_row = jnp.concatenate(
            [jnp.pad(b_igate[l], (0, V7X_LANES - N_HEADS)),
             jnp.pad(b_fgate[l], (0, V7X_LANES - N_HEADS))]).reshape(1, 2 * V7X_LANES)
        g1 = g_norm1[l].reshape(1, D_MODEL)
        gv = g_vnorm[l].reshape(1, D_B)
        g2 = g_norm2[l].reshape(1, D_MODEL)
        bst = b_spatial[l].T
        ws0 = jnp.repeat(w_spatial[l, :, 0, 0], DG).reshape(1, D_B)
        bs0 = jnp.repeat(b_spatial[l, :, 0], DG).reshape(1, D_B)
        wa = w_branch_a[l].astype(BF16)
        wb = w_branch_b[l].astype(BF16)
        wo = w_out[l].astype(BF16)
        wup = w_up[l].astype(BF16)
        wdn = w_down[l].astype(BF16)
        cw = conv_w[l]
        cb = conv_b[l].reshape(1, 2 * D_FF)
        final = l == depth - 1

        xp, c_p, n_p, m_p = _prompt_mixer(xp, wm, wgt, gbias_col, g1, gv, w_spatial[l], bst, wa, wb, wo)
        xp, conv_p = _prompt_ffn(xp, g2, wup, cw, cb, wdn, gf, final)

        z, gates = _sample_proj(xs, g1, wm, wgc)
        m_pad = jnp.pad(state_mlstm_m[l], ((0, 0), (0, V7X_LANES - N_HEADS)))
        h, c_stack, n_s, m_s = _sample_state(
            z, gates, gbias_row, m_pad, state_mlstm_n[l].reshape(n_dec, N_HEADS * DK), state_mlstm_C,
            l, c_stack)
        xs, vn_s = _sample_mixer(xs, z, h, gv, ws0, bs0, wa, wb, wo)
        buf = state_ffn_conv[l].reshape(n_dec, (CONV_W - 1) * 2 * D_FF)
        xs, conv_s = _sample_ffn(xs, g2, wup, cw, cb, wdn, gf, buf, final)

        for lst, val in zip(small, (c_p, n_p, m_p[:, :N_HEADS, 0], conv_p,
                                    n_s.reshape(n_dec, N_HEADS, DK), m_s[:, :N_HEADS],
                                    conv_s.reshape(n_dec, CONV_W - 1, 2 * D_FF),
                                    vn_s.reshape(n_dec, 1, D_B))):
            lst.append(val)
    st = [jnp.stack(o) for o in small]
    return (xp, xs.reshape(n_dec, 1, D_MODEL), st[0], st[1], st[2], st[3], c_stack, st[4], st[5], st[6], st[7])
```

```python
import functools

import jax
import jax.numpy as jnp
from jax import lax
from jax.experimental import pallas as pl
from jax.experimental.pallas import tpu as pltpu

D_MODEL = 1024
N_HEADS = 4
DK = 128
DV = 256
CHUNK = 128
D_B = 1024
N_GROUPS = 4
DG = D_B // N_GROUPS
D_FF = 2816
CONV_W = 3
EPS = 1e-6
K_SCALE = DK ** -0.5

OFF_Q = 0
OFF_K = OFF_Q + N_HEADS * DK
OFF_V = OFF_K + N_HEADS * DK
OFF_O = OFF_V + N_HEADS * DV
OFF_U = OFF_O + N_HEADS * DV
OFF_VB = OFF_U + D_B
OFF_GA = OFF_VB + D_B
OFF_GB = OFF_GA + D_MODEL
P_MAIN = OFF_GB + D_MODEL
GATE_LO = 2 * N_HEADS * DK + 2 * N_HEADS * DV
GATE_HI = GATE_LO + 2 * N_HEADS

V7X_LANES = 128
V7X_SUBLANES = 8
GATE_ROWS = V7X_SUBLANES
V7X_VMEM_LIMIT = 56 * 1024 * 1024
FFN_COL_TILE = 256
PROJ_TILE = 256
MIXER_BLOCK = 256
FFN_BLOCK = 512
SAMPLE_BLOCK = 8

F32 = jnp.float32
BF16 = jnp.bfloat16


def _dot(a, b):
    return jnp.dot(a, b, preferred_element_type=F32)


def _dot_nt(a, b):
    return lax.dot_general(a, b, (((1,), (1,)), ((), ())), preferred_element_type=F32)


def _rmsnorm(x, g):
    r = lax.rsqrt(jnp.mean(x * x, axis=-1, keepdims=True) + EPS)
    return x * r * g


def _log_sigmoid(x):
    return jnp.minimum(x, 0.0) - jnp.log1p(jnp.exp(-jnp.abs(x)))


def _scan_lanes(x, op, fill):
    lane = lax.broadcasted_iota(jnp.int32, x.shape, 1)
    k = 1
    while k < x.shape[1]:
        shifted = pltpu.roll(x, k, 1)
        x = op(x, jnp.where(lane >= k, shifted, fill))
        k *= 2
    return x


def _mixer_kernel(x_ref, wm_ref, wgt_ref, gbias_ref, g1_ref, gv_ref, ws_ref, bst_ref,
                  wa_ref, wb_ref, wo_ref,
                  xo_ref, c_ref, n_ref, m_ref,
                  ct_s, n_s, m_s, q_s, k_s, v_s, h_s, so_s, u_s, vb_s, sg_s, um_s, *, block):
    s = pl.program_id(1)
    n_chunks = block // CHUNK

    @pl.when(s == 0)
    def _():
        ct_s[...] = jnp.zeros_like(ct_s)
        n_s[...] = jnp.zeros_like(n_s)
        m_s[...] = jnp.zeros_like(m_s)

    xn = _rmsnorm(x_ref[...], g1_ref[...]).astype(BF16)

    gates = _dot_nt(wgt_ref[...], xn) + gbias_ref[...]

    def proj(off, t):
        return _dot(xn, wm_ref[:, off + t * PROJ_TILE:off + (t + 1) * PROJ_TILE])

    def tile(t):
        return slice(t * PROJ_TILE, (t + 1) * PROJ_TILE)

    for t in range(N_HEADS * DK // PROJ_TILE):
        q_s[:, tile(t)] = proj(OFF_Q, t)
        k_s[:, tile(t)] = proj(OFF_K, t) * K_SCALE
    for t in range(N_HEADS * DV // PROJ_TILE):
        v_s[:, tile(t)] = proj(OFF_V, t)

    sumsq = [jnp.zeros((block, 1), F32)]

    def vb_tile(t):
        g = jax.nn.gelu(proj(OFF_VB, t))
        vb_s[:, tile(t)] = g
        sumsq[0] = sumsq[0] + jnp.sum(g * g, axis=-1, keepdims=True)

    def u_tile(t):
        u_s[:, tile(t)] = jax.nn.gelu(proj(OFF_U, t))

    def o_tile(t):
        so_s[:, tile(t)] = jax.nn.sigmoid(proj(OFF_O, t))

    def ga_tile(t):
        sg_s[:, tile(t)] = jax.nn.sigmoid(proj(OFF_GA, t))

    def gb_tile(t):
        sg_s[:, D_MODEL + t * PROJ_TILE:D_MODEL + (t + 1) * PROJ_TILE] = jax.nn.sigmoid(proj(OFF_GB, t))

    jobs = [(f, t) for f in (vb_tile, u_tile, o_tile, ga_tile, gb_tile) for t in range(D_MODEL // PROJ_TILE)]

    def run_jobs(count):
        for _ in range(min(count, len(jobs))):
            f, t = jobs.pop(0)
            f(t)

    row_i = lax.broadcasted_iota(jnp.int32, (CHUNK, CHUNK), 0)
    col_i = lax.broadcasted_iota(jnp.int32, (CHUNK, CHUNK), 1)
    causal = row_i >= col_i
    heads = range(N_HEADS)

    for c in range(n_chunks):
        r0 = c * CHUNK
        ig = gates[0:GATE_ROWS, r0:r0 + CHUNK]
        lf = _log_sigmoid(gates[GATE_ROWS:2 * GATE_ROWS, r0:r0 + CHUNK])
        b = _scan_lanes(lf, jnp.add, 0.0)
        a = ig - b
        m_prev = m_s[...]
        gmax = jnp.maximum(m_prev, _scan_lanes(a, jnp.maximum, -jnp.inf))
        m_t = b + gmax
        w_inter = jnp.exp(m_prev - gmax)
        g_last = gmax[:, CHUNK - 1:CHUNK]
        w_last = jnp.exp(a - g_last)
        floor = jnp.exp(-m_t)
        decay = w_inter[:, CHUNK - 1:CHUNK]
        m_s[...] = jnp.broadcast_to(m_t[:, CHUNK - 1:CHUNK], m_s.shape)

        rows = jnp.concatenate(
            [gmax, w_inter, w_last, floor,
             jnp.zeros((CHUNK - 4 * GATE_ROWS, CHUNK), F32)], axis=0)
        cols = rows.T

        def col(kind, h):
            return cols[:, kind * GATE_ROWS + h:kind * GATE_ROWS + h + 1]

        qf = [q_s[r0:r0 + CHUNK, h * DK:(h + 1) * DK] for h in heads]
        kf = [k_s[r0:r0 + CHUNK, h * DK:(h + 1) * DK] for h in heads]
        vf = [v_s[r0:r0 + CHUNK, h * DV:(h + 1) * DV] for h in heads]
        qb = [x.astype(BF16) for x in qf]
        kt = [x.T.astype(BF16) for x in kf]
        run_jobs(2)
        sc = [_dot(qb[h], kt[h]) for h in heads]
        dmat = [jnp.where(causal, jnp.exp(a[h:h + 1, :] - col(0, h)), 0.0) for h in heads]
        run_jobs(1)
        sd = [sc[h] * dmat[h] for h in heads]
        ct = [ct_s[h] for h in heads]
        num = [_dot(sd[h].astype(BF16), vf[h].astype(BF16)) + col(1, h) * _dot(qb[h], ct[h].astype(BF16))
               for h in heads]
        run_jobs(2)
        for h in heads:
            nh = n_s[h:h + 1, :]
            den = (jnp.sum(sd[h], axis=-1, keepdims=True)
                   + col(1, h) * jnp.sum(qf[h] * nh, axis=-1, keepdims=True))
            h_s[r0:r0 + CHUNK, h * DV:(h + 1) * DV] = num[h] * (1.0 / jnp.maximum(jnp.abs(den), col(3, h)))
            dec = decay[h:h + 1, :]
            n_s[h:h + 1, :] = dec * nh + jnp.sum(col(2, h) * kf[h], axis=0, keepdims=True)
        run_jobs(1)
        for h in heads:
            ct_s[h] = decay[h:h + 1, :] * ct[h] + _dot(kt[h], (col(2, h) * vf[h]).astype(BF16))
    run_jobs(len(jobs))

    y_a = _dot((so_s[...] * h_s[...]).astype(BF16), wa_ref[...])

    rinv = lax.rsqrt(sumsq[0] * (1.0 / D_B) + EPS)
    for g in range(N_GROUPS):
        gcols = slice(g * DG, (g + 1) * DG)
        w_tri = jnp.where(causal, ws_ref[g], 0.0).astype(BF16)
        bias_c = bst_ref[:, g:g + 1]
        for c in range(n_chunks):
            rws = slice(c * CHUNK, (c + 1) * CHUNK)
            vn = vb_s[rws, gcols] * rinv[rws] * gv_ref[:, gcols]
            um_s[rws, gcols] = u_s[rws, gcols] * (_dot(w_tri, vn.astype(BF16)) + bias_c)
    y_b = _dot(um_s[...].astype(BF16), wb_ref[...])

    merged = sg_s[:, 0:D_MODEL] * y_a + sg_s[:, D_MODEL:2 * D_MODEL] * y_b
    xo_ref[...] = x_ref[...] + _dot(merged.astype(BF16), wo_ref[...])

    @pl.when(s == pl.num_programs(1) - 1)
    def _():
        for h in range(N_HEADS):
            c_ref[h] = ct_s[h].T
        n_ref[...] = n_s[...]
        m_ref[...] = m_s[...]


def _resident(shape):
    nd = len(shape)
    return pl.BlockSpec(shape, lambda *_: (0,) * nd, pipeline_mode=pl.Buffered(1))


def _prompt_mixer(x, wm, wgt, gbias, g1, gv, ws, bst, wa, wb, wo):
    nb, seq, _ = x.shape
    block = MIXER_BLOCK
    grid = (nb, seq // block)
    xspec = pl.BlockSpec((None, block, D_MODEL), lambda b, s: (b, s, 0))
    out_shape = (
        jax.ShapeDtypeStruct((nb, seq, D_MODEL), F32),
        jax.ShapeDtypeStruct((nb, N_HEADS, DV, DK), F32),
        jax.ShapeDtypeStruct((nb, N_HEADS, DK), F32),
        jax.ShapeDtypeStruct((nb, GATE_ROWS, V7X_LANES), F32),
    )
    out_specs = (
        xspec,
        pl.BlockSpec((None, N_HEADS, DV, DK), lambda b, s: (b, 0, 0, 0)),
        pl.BlockSpec((None, N_HEADS, DK), lambda b, s: (b, 0, 0)),
        pl.BlockSpec((None, GATE_ROWS, V7X_LANES), lambda b, s: (b, 0, 0)),
    )
    in_specs = [xspec] + [_resident(a.shape) for a in (wm, wgt, gbias, g1, gv, ws, bst, wa, wb, wo)]
    scratch = [
        pltpu.VMEM((N_HEADS, DK, DV), F32),
        pltpu.VMEM((N_HEADS, DK), F32),
        pltpu.VMEM((GATE_ROWS, V7X_LANES), F32),
        pltpu.VMEM((block, N_HEADS * DK), F32),
        pltpu.VMEM((block, N_HEADS * DK), F32),
        pltpu.VMEM((block, N_HEADS * DV), F32),
        pltpu.VMEM((block, N_HEADS * DV), F32),
        pltpu.VMEM((block, N_HEADS * DV), F32),
        pltpu.VMEM((block, D_B), F32),
        pltpu.VMEM((block, D_B), F32),
        pltpu.VMEM((block, 2 * D_MODEL), F32),
        pltpu.VMEM((block, D_B), F32),
    ]
    return pl.pallas_call(
        functools.partial(_mixer_kernel, block=block),
        grid=grid, in_specs=in_specs, out_specs=out_specs, out_shape=out_shape,
        scratch_shapes=scratch,
        compiler_params=pltpu.CompilerParams(
            dimension_semantics=("arbitrary", "arbitrary"), vmem_limit_bytes=V7X_VMEM_LIMIT),
        name="prompt_mixer",
    )(x, wm, wgt, gbias, g1, gv, ws, bst, wa, wb, wo)


def _conv_taps(up, carry_s, cw_ref, cb_ref, cols):
    sub = V7X_SUBLANES
    rows = up.shape[0]
    last1 = up[rows - sub:rows]
    last2 = up[rows - 2 * sub:rows - sub]
    first = lax.broadcasted_iota(jnp.int32, last1.shape, 0) == 0
    back1 = jnp.where(first, carry_s[sub - 1:sub, cols], pltpu.roll(last1, 1, 0))
    back2 = jnp.where(first, carry_s[sub - 2:sub - 1, cols], pltpu.roll(last2, 1, 0))
    carry_s[sub - 2:sub - 1, cols] = last2[sub - 1:sub]
    carry_s[sub - 1:sub, cols] = last1[sub - 1:sub]
    m1 = jnp.concatenate([back1, up[0:rows - sub]], axis=0)
    m2 = jnp.concatenate([back2, back1, up[0:rows - 2 * sub]], axis=0)
    return (cb_ref[:, cols] + cw_ref[0:1, cols] * m2 + cw_ref[1:2, cols] * m1
            + cw_ref[2:3, cols] * up)


def _perm_pitch(block):
    return block // V7X_SUBLANES + V7X_SUBLANES


def _ffn_kernel(x_ref, g2_ref, wup_ref, cw_ref, cb_ref, wdn_ref, gf_ref,
                xo_ref, conv_ref, carry_s, act_s, perm_s, *, block, final):
    s = pl.program_id(1)
    sub, lanes = V7X_SUBLANES, V7X_LANES
    groups = block // sub
    chunks = D_MODEL // lanes

    @pl.when(s == 0)
    def _():
        carry_s[...] = jnp.zeros_like(carry_s)

    pitch = _perm_pitch(block)
    for c in range(chunks):
        for i in range(sub):
            perm_s[c, i * pitch:i * pitch + groups] = x_ref[i * groups:(i + 1) * groups, c * lanes:(c + 1) * lanes]
    x = jnp.concatenate(
        [jnp.concatenate([perm_s[c, pl.ds(r, sub, stride=pitch), :] for c in range(chunks)], axis=1)
         for r in range(groups)], axis=0)
    xn = _rmsnorm(x, g2_ref[...]).astype(BF16)
    for j in range(D_FF // FFN_COL_TILE):
        halves = []
        for half in range(2):
            cols = slice(half * D_FF + j * FFN_COL_TILE, half * D_FF + (j + 1) * FFN_COL_TILE)
            up = _dot(xn, wup_ref[:, cols])
            halves.append(_conv_taps(up, carry_s, cw_ref, cb_ref, cols))
        act_s[:, j * FFN_COL_TILE:(j + 1) * FFN_COL_TILE] = (jax.nn.silu(halves[0]) * halves[1]).astype(BF16)
    y = x + _dot(act_s[...], wdn_ref[...])
    if final:
        y = _rmsnorm(y, gf_ref[...])
    for r in range(groups):
        for c in range(chunks):
            perm_s[c, pl.ds(r, sub, stride=pitch), :] = y[r * sub:(r + 1) * sub, c * lanes:(c + 1) * lanes]
    for c in range(chunks):
        for i in range(sub):
            xo_ref[i * groups:(i + 1) * groups, c * lanes:(c + 1) * lanes] = perm_s[c, i * pitch:i * pitch + groups]

    @pl.when(s == pl.num_programs(1) - 1)
    def _():
        conv_ref[...] = carry_s[V7X_SUBLANES - (CONV_W - 1):V7X_SUBLANES, :]


def _prompt_ffn(x, g2, wup, cw, cb, wdn, gf, final):
    nb, seq, _ = x.shape
    block = FFN_BLOCK
    grid = (nb, seq // block)
    xspec = pl.BlockSpec((None, block, D_MODEL), lambda b, s: (b, s, 0))
    out_shape = (
        jax.ShapeDtypeStruct((nb, seq, D_MODEL), F32),
        jax.ShapeDtypeStruct((nb, CONV_W - 1, 2 * D_FF), F32),
    )
    out_specs = (xspec, pl.BlockSpec((None, CONV_W - 1, 2 * D_FF), lambda b, s: (b, 0, 0)))
    in_specs = [xspec] + [_resident(a.shape) for a in (g2, wup, cw, cb, wdn, gf)]
    return pl.pallas_call(
        functools.partial(_ffn_kernel, block=block, final=final),
        grid=grid, in_specs=in_specs, out_specs=out_specs, out_shape=out_shape,
        scratch_shapes=[pltpu.VMEM((V7X_SUBLANES, 2 * D_FF), F32),
                        pltpu.VMEM((block, D_FF), BF16),
                        pltpu.VMEM((D_MODEL // V7X_LANES, V7X_SUBLANES * _perm_pitch(block), V7X_LANES),
                                   F32)],
        compiler_params=pltpu.CompilerParams(
            dimension_semantics=("arbitrary", "arbitrary"), vmem_limit_bytes=V7X_VMEM_LIMIT),
        name="prompt_ffn",
    )(x, g2, wup, cw, cb, wdn, gf)


def _sample_proj_kernel(x_ref, g1_ref, wm_ref, wgc_ref, z_ref, gates_ref):
    xn = _rmsnorm(x_ref[...], g1_ref[...]).astype(BF16)
    z_ref[...] = _dot(xn, wm_ref[...])
    gates_ref[...] = _dot(xn, wgc_ref[...])


def _sample_proj(x, g1, wm, wgc):
    n = x.shape[0]
    return pl.pallas_call(
        _sample_proj_kernel,
        out_shape=(jax.ShapeDtypeStruct((n, P_MAIN), F32),
                   jax.ShapeDtypeStruct((n, 2 * V7X_LANES), F32)),
        compiler_params=pltpu.CompilerParams(vmem_limit_bytes=V7X_VMEM_LIMIT),
        name="sample_proj",
    )(x, g1, wm, wgc)


def _sample_state_kernel(q_ref, k_ref, v_ref, gates_ref, gbias_ref, m_ref, n_ref, c_ref, *rest):
    h_ref, co_ref, no_ref, mo_ref = rest[-4:]
    tb = SAMPLE_BLOCK
    ig = gates_ref[:, 0:V7X_LANES] + gbias_ref[:, 0:V7X_LANES]
    lf = _log_sigmoid(gates_ref[:, V7X_LANES:] + gbias_ref[:, V7X_LANES:])
    inter = lf + m_ref[...]
    m_t = jnp.maximum(inter, ig)
    d_in = jnp.exp(ig - m_t)
    w_inter = jnp.exp(inter - m_t)
    floor = jnp.exp(-m_t)
    mo_ref[...] = m_t

    row8 = lax.broadcasted_iota(jnp.int32, (CHUNK, DK), 0)
    for h in range(N_HEADS):
        q8 = q_ref[:, h * DK:(h + 1) * DK]
        k8 = k_ref[:, h * DK:(h + 1) * DK] * K_SCALE
        v8 = v_ref[:, h * DV:(h + 1) * DV]
        d_h = d_in[:, h:h + 1]
        w_h = w_inter[:, h:h + 1]
        n8 = n_ref[:, h * DK:(h + 1) * DK]
        s = jnp.sum(q8 * k8, axis=-1, keepdims=True) * d_h
        den = s + w_h * jnp.sum(q8 * n8, axis=-1, keepdims=True)
        qb = q8.astype(BF16)
        inter_rows = [_dot_nt(qb, c_ref[j, h].astype(BF16))[j:j + 1] for j in range(tb)]
        num = s * v8 + w_h * jnp.concatenate(inter_rows, axis=0)
        h_ref[:, h * DV:(h + 1) * DV] = num / jnp.maximum(jnp.abs(den), floor[:, h:h + 1])
        no_ref[:, h * DK:(h + 1) * DK] = w_h * n8 + d_h * k8

        vt = jnp.concatenate([d_h * v8, jnp.zeros((CHUNK - tb, DV), F32)], axis=0).T.astype(BF16)
        kpad = jnp.concatenate([k8, jnp.zeros((CHUNK - tb, DK), F32)], axis=0)
        for j in range(tb):
            kj = jnp.where(row8 == j, kpad, 0.0).astype(BF16)
            co_ref[j, h] = w_inter[j:j + 1, h:h + 1] * c_ref[j, h] + _dot(vt, kj)


def _sample_state(z, gates, gbias, m_pad, n_state, c_state, layer, c_stack):
    n = z.shape[0]
    tb = SAMPLE_BLOCK
    c_block = pl.BlockSpec((None, tb, N_HEADS, DV, DK), lambda i: (layer, i, 0, 0, 0))
    in_specs = [
        pl.BlockSpec((tb, N_HEADS * DK), lambda i: (i, OFF_Q // (N_HEADS * DK))),
        pl.BlockSpec((tb, N_HEADS * DK), lambda i: (i, OFF_K // (N_HEADS * DK))),
        pl.BlockSpec((tb, N_HEADS * DV), lambda i: (i, OFF_V // (N_HEADS * DV))),
        pl.BlockSpec((tb, 2 * V7X_LANES), lambda i: (i, 0)),
        pl.BlockSpec((1, 2 * V7X_LANES), lambda i: (0, 0)),
        pl.BlockSpec((tb, V7X_LANES), lambda i: (i, 0)),
        pl.BlockSpec((tb, N_HEADS * DK), lambda i: (i, 0)),
        c_block,
    ]
    args = [z, z, z, gates, gbias, m_pad, n_state, c_state]
    aliases = {}
    if c_stack is not None:
        in_specs.append(pl.BlockSpec(memory_space=pl.ANY))
        aliases = {len(args): 1}
        args.append(c_stack)
    out_shape = (
        jax.ShapeDtypeStruct((n, N_HEADS * DV), F32),
        jax.ShapeDtypeStruct(c_state.shape, F32),
        jax.ShapeDtypeStruct((n, N_HEADS * DK), F32),
        jax.ShapeDtypeStruct((n, V7X_LANES), F32),
    )
    out_specs = (
        pl.BlockSpec((tb, N_HEADS * DV), lambda i: (i, 0)),
        c_block,
        pl.BlockSpec((tb, N_HEADS * DK), lambda i: (i, 0)),
        pl.BlockSpec((tb, V7X_LANES), lambda i: (i, 0)),
    )
    return pl.pallas_call(
        _sample_state_kernel,
        grid=(n // tb,), in_specs=in_specs, out_specs=out_specs, out_shape=out_shape,
        input_output_aliases=aliases,
        compiler_params=pltpu.CompilerParams(
            dimension_semantics=("arbitrary",), vmem_limit_bytes=V7X_VMEM_LIMIT),
        name="sample_state",
    )(*args)


def _sample_mixer_kernel(x_ref, z_ref, h_ref, gv_ref, ws0_ref, bs0_ref, wa_ref, wb_ref, wo_ref,
                         xo_ref, vn_ref):
    y_a = _dot((jax.nn.sigmoid(z_ref[:, OFF_O:OFF_U]) * h_ref[...]).astype(BF16), wa_ref[...])
    u = jax.nn.gelu(z_ref[:, OFF_U:OFF_VB])
    vn = _rmsnorm(jax.nn.gelu(z_ref[:, OFF_VB:OFF_GA]), gv_ref[...])
    vn_ref[...] = vn
    mixed = ws0_ref[...] * vn + bs0_ref[...]
    y_b = _dot((u * mixed).astype(BF16), wb_ref[...])
    merged = (jax.nn.sigmoid(z_ref[:, OFF_GA:OFF_GB]) * y_a
              + jax.nn.sigmoid(z_ref[:, OFF_GB:P_MAIN]) * y_b)
    xo_ref[...] = x_ref[...] + _dot(merged.astype(BF16), wo_ref[...])


def _sample_mixer(x, z, h, gv, ws0, bs0, wa, wb, wo):
    n = x.shape[0]
    return pl.pallas_call(
        _sample_mixer_kernel,
        out_shape=(jax.ShapeDtypeStruct((n, D_MODEL), F32), jax.ShapeDtypeStruct((n, D_B), F32)),
        compiler_params=pltpu.CompilerParams(vmem_limit_bytes=V7X_VMEM_LIMIT),
        name="sample_mixer",
    )(x, z, h, gv, ws0, bs0, wa, wb, wo)


def _sample_ffn_kernel(x_ref, g2_ref, wup_ref, cw_ref, cb_ref, wdn_ref, gf_ref, buf_ref,
                       xo_ref, nbuf_ref, *, final):
    x = x_ref[...]
    xn = _rmsnorm(x, g2_ref[...]).astype(BF16)
    width = 2 * D_FF
    acc = jnp.zeros(x.shape, F32)
    for j in range(D_FF // FFN_COL_TILE):
        halves = []
        for half in range(2):
            col0 = half * D_FF + j * FFN_COL_TILE
            cols = slice(col0, col0 + FFN_COL_TILE)
            up = _dot(xn, wup_ref[:, cols])
            b0 = buf_ref[:, col0:col0 + FFN_COL_TILE]
            b1 = buf_ref[:, width + col0:width + col0 + FFN_COL_TILE]
            nbuf_ref[:, col0:col0 + FFN_COL_TILE] = b1
            nbuf_ref[:, width + col0:width + col0 + FFN_COL_TILE] = up
            halves.append(cb_ref[:, cols] + cw_ref[0:1, cols] * b0 + cw_ref[1:2, cols] * b1
                          + cw_ref[2:3, cols] * up)
        act = (jax.nn.silu(halves[0]) * halves[1]).astype(BF16)
        acc = acc + _dot(act, wdn_ref[j * FFN_COL_TILE:(j + 1) * FFN_COL_TILE, :])
    y = x + acc
    if final:
        y = _rmsnorm(y, gf_ref[...])
    xo_ref[...] = y


def _sample_ffn(x, g2, wup, cw, cb, wdn, gf, buf, final):
    n = x.shape[0]
    return pl.pallas_call(
        functools.partial(_sample_ffn_kernel, final=final),
        out_shape=(jax.ShapeDtypeStruct((n, D_MODEL), F32),
                   jax.ShapeDtypeStruct((n, (CONV_W - 1) * 2 * D_FF), F32)),
        compiler_params=pltpu.CompilerParams(vmem_limit_bytes=V7X_VMEM_LIMIT),
        name="sample_ffn",
    )(x, g2, wup, cw, cb, wdn, gf, buf)


def kernel(x_prompt, x_sample, state_mlstm_C, state_mlstm_n, state_mlstm_m, state_ffn_conv, w_in, b_igate, b_fgate, g_norm1, g_vnorm, w_spatial, b_spatial, w_branch_a, w_branch_b, w_out, g_norm2, w_up, conv_w, conv_b, w_down, g_final):
    depth = w_in.shape[0]
    n_dec = x_sample.shape[0]
    xp = x_prompt
    xs = x_sample.reshape(n_dec, D_MODEL)
    gf = g_final.reshape(1, D_MODEL)
    small = [[] for _ in range(8)]
    c_stack = None
    for l in range(depth):
        wm = jnp.concatenate([w_in[l, :, :GATE_LO], w_in[l, :, GATE_HI:]], axis=1).astype(BF16)
        w_gate = w_in[l, :, GATE_LO:GATE_HI]
        row_pad = ((0, GATE_ROWS - N_HEADS), (0, 0))
        wgt = jnp.concatenate([jnp.pad(w_gate[:, :N_HEADS].T, row_pad),
                               jnp.pad(w_gate[:, N_HEADS:].T, row_pad)], axis=0).astype(BF16)
        wgc = jnp.concatenate(
            [jnp.pad(w_gate[:, :N_HEADS], ((0, 0), (0, V7X_LANES - N_HEADS))),
             jnp.pad(w_gate[:, N_HEADS:], ((0, 0), (0, V7X_LANES - N_HEADS)))], axis=1).astype(BF16)
        gbias_col = jnp.concatenate(
            [jnp.pad(b_igate[l], (0, GATE_ROWS - N_HEADS)),
             jnp.pad(b_fgate[l], (0, GATE_ROWS - N_HEADS))]).reshape(2 * GATE_ROWS, 1)
        gbias_row = jnp.concatenate(
            [jnp.pad(b_igate[l], (0, V7X_LANES - N_HEADS)),
             jnp.pad(b_fgate[l], (0, V7X_LANES - N_HEADS))]).reshape(1, 2 * V7X_LANES)
        g1 = g_norm1[l].reshape(1, D_MODEL)
        gv = g_vnorm[l].reshape(1, D_B)
        g2 = g_norm2[l].reshape(1, D_MODEL)
        bst = b_spatial[l].T
        ws0 = jnp.repeat(w_spatial[l, :, 0, 0], DG).reshape(1, D_B)
        bs0 = jnp.repeat(b_spatial[l, :, 0], DG).reshape(1, D_B)
        wa = w_branch_a[l].astype(BF16)
        wb = w_branch_b[l].astype(BF16)
        wo = w_out[l].astype(BF16)
        wup = w_up[l].astype(BF16)
        wdn = w_down[l].astype(BF16)
        cw = conv_w[l]
        cb = conv_b[l].reshape(1, 2 * D_FF)
        final = l == depth - 1

        xp, c_p, n_p, m_p = _prompt_mixer(xp, wm, wgt, gbias_col, g1, gv, w_spatial[l], bst, wa, wb, wo)
        xp, conv_p = _prompt_ffn(xp, g2, wup, cw, cb, wdn, gf, final)

        z, gates = _sample_proj(xs, g1, wm, wgc)
        m_pad = jnp.pad(state_mlstm_m[l], ((0, 0), (0, V7X_LANES - N_HEADS)))
        h, c_stack, n_s, m_s = _sample_state(
            z, gates, gbias_row, m_pad, state_mlstm_n[l].reshape(n_dec, N_HEADS * DK), state_mlstm_C,
            l, c_stack)
        xs, vn_s = _sample_mixer(xs, z, h, gv, ws0, bs0, wa, wb, wo)
        buf = state_ffn_conv[l].reshape(n_dec, (CONV_W - 1) * 2 * D_FF)
        xs, conv_s = _sample_ffn(xs, g2, wup, cw, cb, wdn, gf, buf, final)

        for lst, val in zip(small, (c_p, n_p, m_p[:, :N_HEADS, 0], conv_p,
                                    n_s.reshape(n_dec, N_HEADS, DK), m_s[:, :N_HEADS],
                                    conv_s.reshape(n_dec, CONV_W - 1, 2 * D_FF),
                                    vn_s.reshape(n_dec, 1, D_B))):
            lst.append(val)
    st = [jnp.stack(o) for o in small]
    return (xp, xs.reshape(n_dec, 1, D_MODEL), st[0], st[1], st[2], st[3], c_stack, st[4], st[5], st[6], st[7])
```

```python
import functools

import jax
import jax.numpy as jnp
from jax import lax
from jax.experimental import pallas as pl
from jax.experimental.pallas import tpu as pltpu

D_MODEL = 1024
N_HEADS = 4
DK = 128
DV = 256
CHUNK = 128
D_B = 1024
N_GROUPS = 4
DG = D_B // N_GROUPS
D_FF = 2816
CONV_W = 3
EPS = 1e-6
K_SCALE = DK ** -0.5

OFF_Q = 0
OFF_K = OFF_Q + N_HEADS * DK
OFF_V = OFF_K + N_HEADS * DK
OFF_O = OFF_V + N_HEADS * DV
OFF_U = OFF_O + N_HEADS * DV
OFF_VB = OFF_U + D_B
OFF_GA = OFF_VB + D_B
OFF_GB = OFF_GA + D_MODEL
P_MAIN = OFF_GB + D_MODEL
GATE_LO = 2 * N_HEADS * DK + 2 * N_HEADS * DV
GATE_HI = GATE_LO + 2 * N_HEADS

V7X_LANES = 128
V7X_SUBLANES = 8
GATE_ROWS = V7X_SUBLANES
V7X_VMEM_LIMIT = 56 * 1024 * 1024
FFN_COL_TILE = 256
PROJ_TILE = 256
MIXER_BLOCK = 256
FFN_BLOCK = 512
SAMPLE_BLOCK = 8
PACK_ROWS = 128

F32 = jnp.float32
BF16 = jnp.bfloat16


def _dot(a, b):
    return jnp.dot(a, b, preferred_element_type=F32)


def _dot_nt(a, b):
    return lax.dot_general(a, b, (((1,), (1,)), ((), ())), preferred_element_type=F32)


def _rmsnorm(x, g):
    r = lax.rsqrt(jnp.mean(x * x, axis=-1, keepdims=True) + EPS)
    return x * r * g


def _log_sigmoid(x):
    return jnp.minimum(x, 0.0) - jnp.log1p(jnp.exp(-jnp.abs(x)))


def _scan_lanes(x, op, fill):
    lane = lax.broadcasted_iota(jnp.int32, x.shape, 1)
    k = 1
    while k < x.shape[1]:
        shifted = pltpu.roll(x, k, 1)
        x = op(x, jnp.where(lane >= k, shifted, fill))
        k *= 2
    return x


def _mixer_kernel(x_ref, wm_ref, wgt_ref, gbias_ref, g1_ref, gv_ref, ws_ref, bst_ref,
                  wa_ref, wb_ref, wo_ref,
                  xo_ref, c_ref, n_ref, m_ref,
                  ct_s, n_s, m_s, q_s, k_s, v_s, h_s, so_s, u_s, vb_s, sg_s, um_s, *, block):
    s = pl.program_id(1)
    n_chunks = block // CHUNK

    @pl.when(s == 0)
    def _():
        ct_s[...] = jnp.zeros_like(ct_s)
        n_s[...] = jnp.zeros_like(n_s)
        m_s[...] = jnp.zeros_like(m_s)

    xn = _rmsnorm(x_ref[...], g1_ref[...]).astype(BF16)

    gates = _dot_nt(wgt_ref[...], xn) + gbias_ref[...]

    def proj(off, t):
        return _dot(xn, wm_ref[:, off + t * PROJ_TILE:off + (t + 1) * PROJ_TILE])

    def tile(t):
        return slice(t * PROJ_TILE, (t + 1) * PROJ_TILE)

    for t in range(N_HEADS * DK // PROJ_TILE):
        q_s[:, tile(t)] = proj(OFF_Q, t)
        k_s[:, tile(t)] = proj(OFF_K, t) * K_SCALE
    for t in range(N_HEADS * DV // PROJ_TILE):
        v_s[:, tile(t)] = proj(OFF_V, t)

    sumsq = [jnp.zeros((block, 1), F32)]

    def vb_tile(t):
        g = jax.nn.gelu(proj(OFF_VB, t))
        vb_s[:, tile(t)] = g
        sumsq[0] = sumsq[0] + jnp.sum(g * g, axis=-1, keepdims=True)

    def u_tile(t):
        u_s[:, tile(t)] = jax.nn.gelu(proj(OFF_U, t))

    def o_tile(t):
        so_s[:, tile(t)] = jax.nn.sigmoid(proj(OFF_O, t))

    def ga_tile(t):
        sg_s[:, tile(t)] = jax.nn.sigmoid(proj(OFF_GA, t))

    def gb_tile(t):
        sg_s[:, D_MODEL + t * PROJ_TILE:D_MODEL + (t + 1) * PROJ_TILE] = jax.nn.sigmoid(proj(OFF_GB, t))

    jobs = [(f, t) for f in (vb_tile, u_tile, o_tile, ga_tile, gb_tile) for t in range(D_MODEL // PROJ_TILE)]

    def run_jobs(count):
        for _ in range(min(count, len(jobs))):
            f, t = jobs.pop(0)
            f(t)

    row_i = lax.broadcasted_iota(jnp.int32, (CHUNK, CHUNK), 0)
    col_i = lax.broadcasted_iota(jnp.int32, (CHUNK, CHUNK), 1)
    causal = row_i >= col_i
    heads = range(N_HEADS)

    for c in range(n_chunks):
        r0 = c * CHUNK
        ig = gates[0:GATE_ROWS, r0:r0 + CHUNK]
        lf = _log_sigmoid(gates[GATE_ROWS:2 * GATE_ROWS, r0:r0 + CHUNK])
        b = _scan_lanes(lf, jnp.add, 0.0)
        a = ig - b
        m_prev = m_s[...]
        gmax = jnp.maximum(m_prev, _scan_lanes(a, jnp.maximum, -jnp.inf))
        m_t = b + gmax
        w_inter = jnp.exp(m_prev - gmax)
        g_last = gmax[:, CHUNK - 1:CHUNK]
        w_last = jnp.exp(a - g_last)
        floor = jnp.exp(-m_t)
        decay = w_inter[:, CHUNK - 1:CHUNK]
        m_s[...] = jnp.broadcast_to(m_t[:, CHUNK - 1:CHUNK], m_s.shape)

        rows = jnp.concatenate(
            [gmax, w_inter, w_last, floor,
             jnp.zeros((CHUNK - 4 * GATE_ROWS, CHUNK), F32)], axis=0)
        cols = rows.T

        def col(kind, h):
            return cols[:, kind * GATE_ROWS + h:kind * GATE_ROWS + h + 1]

        qf = [q_s[r0:r0 + CHUNK, h * DK:(h + 1) * DK] for h in heads]
        kf = [k_s[r0:r0 + CHUNK, h * DK:(h + 1) * DK] for h in heads]
        vf = [v_s[r0:r0 + CHUNK, h * DV:(h + 1) * DV] for h in heads]
        qb = [x.astype(BF16) for x in qf]
        kt = [x.T.astype(BF16) for x in kf]
        run_jobs(2)
        sc = [_dot(qb[h], kt[h]) for h in heads]
        dmat = [jnp.where(causal, jnp.exp(a[h:h + 1, :] - col(0, h)), 0.0) for h in heads]
        run_jobs(1)
        sd = [sc[h] * dmat[h] for h in heads]
        ct = [ct_s[h] for h in heads]
        num = [_dot(sd[h].astype(BF16), vf[h].astype(BF16)) + col(1, h) * _dot(qb[h], ct[h].astype(BF16))
               for h in heads]
        run_jobs(2)
        for h in heads:
            nh = n_s[h:h + 1, :]
            den = (jnp.sum(sd[h], axis=-1, keepdims=True)
                   + col(1, h) * jnp.sum(qf[h] * nh, axis=-1, keepdims=True))
            h_s[r0:r0 + CHUNK, h * DV:(h + 1) * DV] = num[h] * (1.0 / jnp.maximum(jnp.abs(den), col(3, h)))
            dec = decay[h:h + 1, :]
            n_s[h:h + 1, :] = dec * nh + jnp.sum(col(2, h) * kf[h], axis=0, keepdims=True)
        run_jobs(1)
        for h in heads:
            ct_s[h] = decay[h:h + 1, :] * ct[h] + _dot(kt[h], (col(2, h) * vf[h]).astype(BF16))
    run_jobs(len(jobs))

    y_a = _dot((so_s[...] * h_s[...]).astype(BF16), wa_ref[...])

    rinv = lax.rsqrt(sumsq[0] * (1.0 / D_B) + EPS)
    for g in range(N_GROUPS):
        gcols = slice(g * DG, (g + 1) * DG)
        w_tri = jnp.where(causal, ws_ref[g], 0.0).astype(BF16)
        bias_c = bst_ref[:, g:g + 1]
        for c in range(n_chunks):
            rws = slice(c * CHUNK, (c + 1) * CHUNK)
            vn = vb_s[rws, gcols] * rinv[rws] * gv_ref[:, gcols]
            um_s[rws, gcols] = u_s[rws, gcols] * (_dot(w_tri, vn.astype(BF16)) + bias_c)
    y_b = _dot(um_s[...].astype(BF16), wb_ref[...])

    merged = sg_s[:, 0:D_MODEL] * y_a + sg_s[:, D_MODEL:2 * D_MODEL] * y_b
    xo_ref[...] = x_ref[...] + _dot(merged.astype(BF16), wo_ref[...])

    @pl.when(s == pl.num_programs(1) - 1)
    def _():
        for h in range(N_HEADS):
            c_ref[h] = ct_s[h].T
        n_ref[...] = n_s[...]
        m_ref[...] = m_s[...]


def _resident(arr, layer=None):
    if layer is None:
        nd = arr.ndim
        return pl.BlockSpec(arr.shape, lambda *_: (0,) * nd, pipeline_mode=pl.Buffered(1))
    nd = arr.ndim - 1
    return pl.BlockSpec((None,) + arr.shape[1:], lambda *_: (layer,) + (0,) * nd,
                        pipeline_mode=pl.Buffered(1))


def _prompt_mixer(x, layer, wm, wgt, gbias, g1, gv, ws, bst, wa, wb, wo):
    nb, seq, _ = x.shape
    block = MIXER_BLOCK
    grid = (nb, seq // block)
    xspec = pl.BlockSpec((None, block, D_MODEL), lambda b, s: (b, s, 0))
    out_shape = (
        jax.ShapeDtypeStruct((nb, seq, D_MODEL), F32),
        jax.ShapeDtypeStruct((nb, N_HEADS, DV, DK), F32),
        jax.ShapeDtypeStruct((nb, N_HEADS, DK), F32),
        jax.ShapeDtypeStruct((nb, GATE_ROWS, V7X_LANES), F32),
    )
    out_specs = (
        xspec,
        pl.BlockSpec((None, N_HEADS, DV, DK), lambda b, s: (b, 0, 0, 0)),
        pl.BlockSpec((None, N_HEADS, DK), lambda b, s: (b, 0, 0)),
        pl.BlockSpec((None, GATE_ROWS, V7X_LANES), lambda b, s: (b, 0, 0)),
    )
    in_specs = [xspec] + [_resident(a, layer) for a in (wm, wgt, gbias, g1, gv, ws, bst, wa, wb, wo)]
    scratch = [
        pltpu.VMEM((N_HEADS, DK, DV), F32),
        pltpu.VMEM((N_HEADS, DK), F32),
        pltpu.VMEM((GATE_ROWS, V7X_LANES), F32),
        pltpu.VMEM((block, N_HEADS * DK), F32),
        pltpu.VMEM((block, N_HEADS * DK), F32),
        pltpu.VMEM((block, N_HEADS * DV), F32),
        pltpu.VMEM((block, N_HEADS * DV), F32),
        pltpu.VMEM((block, N_HEADS * DV), F32),
        pltpu.VMEM((block, D_B), F32),
        pltpu.VMEM((block, D_B), F32),
        pltpu.VMEM((block, 2 * D_MODEL), F32),
        pltpu.VMEM((block, D_B), F32),
    ]
    return pl.pallas_call(
        functools.partial(_mixer_kernel, block=block),
        grid=grid, in_specs=in_specs, out_specs=out_specs, out_shape=out_shape,
        scratch_shapes=scratch,
        compiler_params=pltpu.CompilerParams(
            dimension_semantics=("arbitrary", "arbitrary"), vmem_limit_bytes=V7X_VMEM_LIMIT),
        name="prompt_mixer",
    )(x, wm, wgt, gbias, g1, gv, ws, bst, wa, wb, wo)


def _conv_taps(up, carry_s, cw_ref, cb_ref, cols):
    sub = V7X_SUBLANES
    rows = up.shape[0]
    last1 = up[rows - sub:rows]
    last2 = up[rows - 2 * sub:rows - sub]
    first = lax.broadcasted_iota(jnp.int32, last1.shape, 0) == 0
    back1 = jnp.where(first, carry_s[sub - 1:sub, cols], pltpu.roll(last1, 1, 0))
    back2 = jnp.where(first, carry_s[sub - 2:sub - 1, cols], pltpu.roll(last2, 1, 0))
    carry_s[sub - 2:sub - 1, cols] = last2[sub - 1:sub]
    carry_s[sub - 1:sub, cols] = last1[sub - 1:sub]
    m1 = jnp.concatenate([back1, up[0:rows - sub]], axis=0)
    m2 = jnp.concatenate([back2, back1, up[0:rows - 2 * sub]], axis=0)
    return (cb_ref[:, cols] + cw_ref[0:1, cols] * m2 + cw_ref[1:2, cols] * m1
            + cw_ref[2:3, cols] * up)


def _perm_pitch(block):
    return block // V7X_SUBLANES + V7X_SUBLANES


def _ffn_kernel(x_ref, g2_ref, wup_ref, cw_ref, cb_ref, wdn_ref, gf_ref,
                xo_ref, conv_ref, carry_s, act_s, perm_s, *, block, final):
    s = pl.program_id(1)
    sub, lanes = V7X_SUBLANES, V7X_LANES
    groups = block // sub
    chunks = D_MODEL // lanes

    @pl.when(s == 0)
    def _():
        carry_s[...] = jnp.zeros_like(carry_s)

    pitch = _perm_pitch(block)
    for c in range(chunks):
        for i in range(sub):
            perm_s[c, i * pitch:i * pitch + groups] = x_ref[i * groups:(i + 1) * groups, c * lanes:(c + 1) * lanes]
    x = jnp.concatenate(
        [jnp.concatenate([perm_s[c, pl.ds(r, sub, stride=pitch), :] for c in range(chunks)], axis=1)
         for r in range(groups)], axis=0)
    xn = _rmsnorm(x, g2_ref[...]).astype(BF16)
    for j in range(D_FF // FFN_COL_TILE):
        halves = []
        for half in range(2):
            cols = slice(half * D_FF + j * FFN_COL_TILE, half * D_FF + (j + 1) * FFN_COL_TILE)
            up = _dot(xn, wup_ref[:, cols])
            halves.append(_conv_taps(up, carry_s, cw_ref, cb_ref, cols))
        act_s[:, j * FFN_COL_TILE:(j + 1) * FFN_COL_TILE] = (jax.nn.silu(halves[0]) * halves[1]).astype(BF16)
    y = x + _dot(act_s[...], wdn_ref[...])
    if final:
        y = _rmsnorm(y, gf_ref[...])
    for r in range(groups):
        for c in range(chunks):
            perm_s[c, pl.ds(r, sub, stride=pitch), :] = y[r * sub:(r + 1) * sub, c * lanes:(c + 1) * lanes]
    for c in range(chunks):
        for i in range(sub):
            xo_ref[i * groups:(i + 1) * groups, c * lanes:(c + 1) * lanes] = perm_s[c, i * pitch:i * pitch + groups]

    @pl.when(s == pl.num_programs(1) - 1)
    def _():
        conv_ref[...] = carry_s[V7X_SUBLANES - (CONV_W - 1):V7X_SUBLANES, :]


def _prompt_ffn(x, layer, g2, wup, cw, cb, wdn, gf, final):
    nb, seq, _ = x.shape
    block = FFN_BLOCK
    grid = (nb, seq // block)
    xspec = pl.BlockSpec((None, block, D_MODEL), lambda b, s: (b, s, 0))
    out_shape = (
        jax.ShapeDtypeStruct((nb, seq, D_MODEL), F32),
        jax.ShapeDtypeStruct((nb, CONV_W - 1, 2 * D_FF), F32),
    )
    out_specs = (xspec, pl.BlockSpec((None, CONV_W - 1, 2 * D_FF), lambda b, s: (b, 0, 0)))
    in_specs = [xspec] + [_resident(a, layer) for a in (g2, wup, cw, cb, wdn)] + [_resident(gf)]
    return pl.pallas_call(
        functools.partial(_ffn_kernel, block=block, final=final),
        grid=grid, in_specs=in_specs, out_specs=out_specs, out_shape=out_shape,
        scratch_shapes=[pltpu.VMEM((V7X_SUBLANES, 2 * D_FF), F32),
                        pltpu.VMEM((block, D_FF), BF16),
                        pltpu.VMEM((D_MODEL // V7X_LANES, V7X_SUBLANES * _perm_pitch(block), V7X_LANES),
                                   F32)],
        compiler_params=pltpu.CompilerParams(
            dimension_semantics=("arbitrary", "arbitrary"), vmem_limit_bytes=V7X_VMEM_LIMIT),
        name="prompt_ffn",
    )(x, g2, wup, cw, cb, wdn, gf)


def _sample_proj_kernel(x_ref, g1_ref, wm_ref, wgc_ref, z_ref, gates_ref):
    xn = _rmsnorm(x_ref[...], g1_ref[...]).astype(BF16)
    z_ref[...] = _dot(xn, wm_ref[...])
    gates_ref[...] = _dot(xn, wgc_ref[...])


def _single_step(kernel_fn, name, out_shape, whole, layered, layer):
    return pl.pallas_call(
        kernel_fn,
        grid=(1,),
        in_specs=[_resident(a) for a in whole] + [_resident(a, layer) for a in layered],
        out_specs=tuple(pl.BlockSpec(o.shape, lambda i, nd=len(o.shape): (0,) * nd) for o in out_shape),
        out_shape=out_shape,
        compiler_params=pltpu.CompilerParams(
            dimension_semantics=("arbitrary",), vmem_limit_bytes=V7X_VMEM_LIMIT),
        name=name,
    )(*whole, *layered)


def _sample_proj(x, layer, g1, wm, wgc):
    n = x.shape[0]
    out_shape = (jax.ShapeDtypeStruct((n, P_MAIN), F32), jax.ShapeDtypeStruct((n, 2 * V7X_LANES), F32))
    return _single_step(_sample_proj_kernel, "sample_proj", out_shape, (x,), (g1, wm, wgc), layer)


def _sample_state_kernel(q_ref, k_ref, v_ref, gates_ref, gbias_ref, m_ref, n_ref, c_ref, *rest):
    h_ref, co_ref, no_ref, mo_ref = rest[-4:]
    tb = SAMPLE_BLOCK
    ig = gates_ref[:, 0:V7X_LANES] + gbias_ref[:, 0:V7X_LANES]
    lf = _log_sigmoid(gates_ref[:, V7X_LANES:] + gbias_ref[:, V7X_LANES:])
    inter = lf + m_ref[...]
    m_t = jnp.maximum(inter, ig)
    d_in = jnp.exp(ig - m_t)
    w_inter = jnp.exp(inter - m_t)
    floor = jnp.exp(-m_t)
    mo_ref[...] = m_t

    row8 = lax.broadcasted_iota(jnp.int32, (CHUNK, DK), 0)
    for h in range(N_HEADS):
        q8 = q_ref[:, h * DK:(h + 1) * DK]
        k8 = k_ref[:, h * DK:(h + 1) * DK] * K_SCALE
        v8 = v_ref[:, h * DV:(h + 1) * DV]
        d_h = d_in[:, h:h + 1]
        w_h = w_inter[:, h:h + 1]
        n8 = n_ref[:, h * DK:(h + 1) * DK]
        s = jnp.sum(q8 * k8, axis=-1, keepdims=True) * d_h
        den = s + w_h * jnp.sum(q8 * n8, axis=-1, keepdims=True)
        qb = q8.astype(BF16)
        inter_rows = [_dot_nt(qb, c_ref[j, h].astype(BF16))[j:j + 1] for j in range(tb)]
        num = s * v8 + w_h * jnp.concatenate(inter_rows, axis=0)
        h_ref[:, h * DV:(h + 1) * DV] = num / jnp.maximum(jnp.abs(den), floor[:, h:h + 1])
        no_ref[:, h * DK:(h + 1) * DK] = w_h * n8 + d_h * k8

        vt = jnp.concatenate([d_h * v8, jnp.zeros((CHUNK - tb, DV), F32)], axis=0).T.astype(BF16)
        kpad = jnp.concatenate([k8, jnp.zeros((CHUNK - tb, DK), F32)], axis=0)
        for j in range(tb):
            kj = jnp.where(row8 == j, kpad, 0.0).astype(BF16)
            co_ref[j, h] = w_inter[j:j + 1, h:h + 1] * c_ref[j, h] + _dot(vt, kj)


def _sample_state(z, gates, gbias, m_pad, n_state, c_state, layer, c_stack):
    n = z.shape[0]
    tb = SAMPLE_BLOCK
    c_block = pl.BlockSpec((None, tb, N_HEADS, DV, DK), lambda i: (layer, i, 0, 0, 0))
    in_specs = [
        pl.BlockSpec((tb, N_HEADS * DK), lambda i: (i, OFF_Q // (N_HEADS * DK))),
        pl.BlockSpec((tb, N_HEADS * DK), lambda i: (i, OFF_K // (N_HEADS * DK))),
        pl.BlockSpec((tb, N_HEADS * DV), lambda i: (i, OFF_V // (N_HEADS * DV))),
        pl.BlockSpec((tb, 2 * V7X_LANES), lambda i: (i, 0)),
        pl.BlockSpec((1, 2 * V7X_LANES), lambda i: (0, 0)),
        pl.BlockSpec((tb, V7X_LANES), lambda i: (i, 0)),
        pl.BlockSpec((tb, N_HEADS * DK), lambda i: (i, 0)),
        c_block,
    ]
    args = [z, z, z, gates, gbias, m_pad, n_state, c_state]
    aliases = {}
    if c_stack is not None:
        in_specs.append(pl.BlockSpec(memory_space=pl.ANY))
        aliases = {len(args): 1}
        args.append(c_stack)
    out_shape = (
        jax.ShapeDtypeStruct((n, N_HEADS * DV), F32),
        jax.ShapeDtypeStruct(c_state.shape, F32),
        jax.ShapeDtypeStruct((n, N_HEADS * DK), F32),
        jax.ShapeDtypeStruct((n, V7X_LANES), F32),
    )
    out_specs = (
        pl.BlockSpec((tb, N_HEADS * DV), lambda i: (i, 0)),
        c_block,
        pl.BlockSpec((tb, N_HEADS * DK), lambda i: (i, 0)),
        pl.BlockSpec((tb, V7X_LANES), lambda i: (i, 0)),
    )
    return pl.pallas_call(
        _sample_state_kernel,
        grid=(n // tb,), in_specs=in_specs, out_specs=out_specs, out_shape=out_shape,
        input_output_aliases=aliases,
        compiler_params=pltpu.CompilerParams(
            dimension_semantics=("arbitrary",), vmem_limit_bytes=V7X_VMEM_LIMIT),
        name="sample_state",
    )(*args)


def _sample_mixer_kernel(x_ref, z_ref, h_ref, gv_ref, ws0_ref, bs0_ref, wa_ref, wb_ref, wo_ref,
                         xo_ref, vn_ref):
    y_a = _dot((jax.nn.sigmoid(z_ref[:, OFF_O:OFF_U]) * h_ref[...]).astype(BF16), wa_ref[...])
    u = jax.nn.gelu(z_ref[:, OFF_U:OFF_VB])
    vn = _rmsnorm(jax.nn.gelu(z_ref[:, OFF_VB:OFF_GA]), gv_ref[...])
    vn_ref[...] = vn
    mixed = ws0_ref[...] * vn + bs0_ref[...]
    y_b = _dot((u * mixed).astype(BF16), wb_ref[...])
    merged = (jax.nn.sigmoid(z_ref[:, OFF_GA:OFF_GB]) * y_a
              + jax.nn.sigmoid(z_ref[:, OFF_GB:P_MAIN]) * y_b)
    xo_ref[...] = x_ref[...] + _dot(merged.astype(BF16), wo_ref[...])


def _sample_mixer(x, z, h, layer, gv, ws0, bs0, wa, wb, wo):
    n = x.shape[0]
    out_shape = (jax.ShapeDtypeStruct((n, D_MODEL), F32), jax.ShapeDtypeStruct((n, D_B), F32))
    return _single_step(_sample_mixer_kernel, "sample_mixer", out_shape,
                        (x, z, h), (gv, ws0, bs0, wa, wb, wo), layer)


def _sample_ffn_kernel(x_ref, buf_ref, gf_ref, g2_ref, wup_ref, cw_ref, cb_ref, wdn_ref,
                       xo_ref, nbuf_ref, *, final):
    x = x_ref[...]
    xn = _rmsnorm(x, g2_ref[...]).astype(BF16)
    width = 2 * D_FF
    acc = jnp.zeros(x.shape, F32)
    for j in range(D_FF // FFN_COL_TILE):
        halves = []
        for half in range(2):
            col0 = half * D_FF + j * FFN_COL_TILE
            cols = slice(col0, col0 + FFN_COL_TILE)
            up = _dot(xn, wup_ref[:, cols])
            b0 = buf_ref[:, col0:col0 + FFN_COL_TILE]
            b1 = buf_ref[:, width + col0:width + col0 + FFN_COL_TILE]
            nbuf_ref[:, col0:col0 + FFN_COL_TILE] = b1
            nbuf_ref[:, width + col0:width + col0 + FFN_COL_TILE] = up
            halves.append(cb_ref[:, cols] + cw_ref[0:1, cols] * b0 + cw_ref[1:2, cols] * b1
                          + cw_ref[2:3, cols] * up)
        act = (jax.nn.silu(halves[0]) * halves[1]).astype(BF16)
        acc = acc + _dot(act, wdn_ref[j * FFN_COL_TILE:(j + 1) * FFN_COL_TILE, :])
    y = x + acc
    if final:
        y = _rmsnorm(y, gf_ref[...])
    xo_ref[...] = y


def _sample_ffn(x, buf, gf, layer, g2, wup, cw, cb, wdn, final):
    n = x.shape[0]
    out_shape = (jax.ShapeDtypeStruct((n, D_MODEL), F32),
                 jax.ShapeDtypeStruct((n, (CONV_W - 1) * 2 * D_FF), F32))
    return _single_step(functools.partial(_sample_ffn_kernel, final=final), "sample_ffn", out_shape,
                        (x, buf, gf), (g2, wup, cw, cb, wdn), layer)


def _pack_kernel(w_ref, o_ref):
    o_ref[:, 0:GATE_LO] = w_ref[:, 0:GATE_LO].astype(BF16)
    o_ref[:, GATE_LO:P_MAIN] = w_ref[:, GATE_HI:GATE_HI + P_MAIN - GATE_LO].astype(BF16)


def _pack_w_in(w_in):
    depth, d, p_in = w_in.shape
    return pl.pallas_call(
        _pack_kernel,
        grid=(depth, d // PACK_ROWS),
        in_specs=[pl.BlockSpec((None, PACK_ROWS, p_in), lambda l, r: (l, r, 0))],
        out_specs=pl.BlockSpec((None, PACK_ROWS, P_MAIN), lambda l, r: (l, r, 0)),
        out_shape=jax.ShapeDtypeStruct((depth, d, P_MAIN), BF16),
        compiler_params=pltpu.CompilerParams(
            dimension_semantics=("arbitrary", "arbitrary"), vmem_limit_bytes=V7X_VMEM_LIMIT),
        name="pack_w_in",
    )(w_in)


def kernel(x_prompt, x_sample, state_mlstm_C, state_mlstm_n, state_mlstm_m, state_ffn_conv, w_in, b_igate, b_fgate, g_norm1, g_vnorm, w_spatial, b_spatial, w_branch_a, w_branch_b, w_out, g_norm2, w_up, conv_w, conv_b, w_down, g_final):
    depth = w_in.shape[0]
    n_dec = x_sample.shape[0]
    xp = x_prompt
    xs = x_sample.reshape(n_dec, D_MODEL)
    gf = g_final.reshape(1, D_MODEL)

    wm = _pack_w_in(w_in)
    w_gate = w_in[:, :, GATE_LO:GATE_HI]
    w_ig_t = jnp.swapaxes(w_gate[:, :, :N_HEADS], 1, 2)
    w_fg_t = jnp.swapaxes(w_gate[:, :, N_HEADS:], 1, 2)
    row_pad = ((0, 0), (0, GATE_ROWS - N_HEADS), (0, 0))
    wgt = jnp.concatenate([jnp.pad(w_ig_t, row_pad), jnp.pad(w_fg_t, row_pad)], axis=1).astype(BF16)
    lane_pad = ((0, 0), (0, 0), (0, V7X_LANES - N_HEADS))
    wgc = jnp.concatenate([jnp.pad(w_gate[:, :, :N_HEADS], lane_pad),
                           jnp.pad(w_gate[:, :, N_HEADS:], lane_pad)], axis=2).astype(BF16)
    head_pad = ((0, 0), (0, GATE_ROWS - N_HEADS))
    gbias_col = jnp.concatenate([jnp.pad(b_igate, head_pad), jnp.pad(b_fgate, head_pad)],
                                axis=1).reshape(depth, 2 * GATE_ROWS, 1)
    head_lanes = ((0, 0), (0, V7X_LANES - N_HEADS))
    gbias_row = jnp.concatenate([jnp.pad(b_igate, head_lanes), jnp.pad(b_fgate, head_lanes)],
                                axis=1).reshape(depth, 1, 2 * V7X_LANES)
    g1 = g_norm1.reshape(depth, 1, D_MODEL)
    gv = g_vnorm.reshape(depth, 1, D_B)
    g2 = g_norm2.reshape(depth, 1, D_MODEL)
    bst = jnp.swapaxes(b_spatial, 1, 2)
    ws0 = jnp.repeat(w_spatial[:, :, 0, 0], DG, axis=1).reshape(depth, 1, D_B)
    bs0 = jnp.repeat(b_spatial[:, :, 0], DG, axis=1).reshape(depth, 1, D_B)
    wa = w_branch_a.astype(BF16)
    wb = w_branch_b.astype(BF16)
    wo = w_out.astype(BF16)
    wup = w_up.astype(BF16)
    wdn = w_down.astype(BF16)
    cb = conv_b.reshape(depth, 1, 2 * D_FF)
    n_state = state_mlstm_n.reshape(depth, n_dec, N_HEADS * DK)
    m_pad = jnp.pad(state_mlstm_m, ((0, 0), (0, 0), (0, V7X_LANES - N_HEADS)))
    conv_state = state_ffn_conv.reshape(depth, n_dec, (CONV_W - 1) * 2 * D_FF)

    small = [[] for _ in range(8)]
    c_stack = None
    for l in range(depth):
        final = l == depth - 1

        xp, c_p, n_p, m_p = _prompt_mixer(xp, l, wm, wgt, gbias_col, g1, gv, w_spatial, bst, wa, wb, wo)
        xp, conv_p = _prompt_ffn(xp, l, g2, wup, conv_w, cb, wdn, gf, final)

        z, gates = _sample_proj(xs, l, g1, wm, wgc)
        h, c_stack, n_s, m_s = _sample_state(
            z, gates, gbias_row[l], m_pad[l], n_state[l], state_mlstm_C, l, c_stack)
        xs, vn_s = _sample_mixer(xs, z, h, l, gv, ws0, bs0, wa, wb, wo)
        xs, conv_s = _sample_ffn(xs, conv_state[l], gf, l, g2, wup, conv_w, cb, wdn, final)

        for lst, val in zip(small, (c_p, n_p, m_p[:, :N_HEADS, 0], conv_p,
                                    n_s.reshape(n_dec, N_HEADS, DK), m_s[:, :N_HEADS],
                                    conv_s.reshape(n_dec, CONV_W - 1, 2 * D_FF),
                                    vn_s.reshape(n_dec, 1, D_B))):
            lst.append(val)
    st = [jnp.stack(o) for o in small]
    return (xp, xs.reshape(n_dec, 1, D_MODEL), st[0], st[1], st[2], st[3], c_stack, st[4], st[5], st[6], st[7])
```

```python
import functools

import jax
import jax.numpy as jnp
from jax import lax
from jax.experimental import pallas as pl
from jax.experimental.pallas import tpu as pltpu

D_MODEL = 1024
N_HEADS = 4
DK = 128
DV = 256
CHUNK = 128
D_B = 1024
N_GROUPS = 4
DG = D_B // N_GROUPS
D_FF = 2816
CONV_W = 3
EPS = 1e-6
K_SCALE = DK ** -0.5

OFF_Q = 0
OFF_K = OFF_Q + N_HEADS * DK
OFF_V = OFF_K + N_HEADS * DK
OFF_O = OFF_V + N_HEADS * DV
OFF_U = OFF_O + N_HEADS * DV
OFF_VB = OFF_U + D_B
OFF_GA = OFF_VB + D_B
OFF_GB = OFF_GA + D_MODEL
P_MAIN = OFF_GB + D_MODEL
GATE_LO = 2 * N_HEADS * DK + 2 * N_HEADS * DV
GATE_HI = GATE_LO + 2 * N_HEADS

V7X_LANES = 128
V7X_SUBLANES = 8
GATE_ROWS = V7X_SUBLANES
V7X_VMEM_LIMIT = 56 * 1024 * 1024
FFN_COL_TILE = 256
PROJ_TILE = 256
MIXER_BLOCK = 512
FFN_BLOCK = 512
SAMPLE_BLOCK = 8
PACK_TILE = 256

F32 = jnp.float32
BF16 = jnp.bfloat16


def _dot(a, b):
    return jnp.dot(a, b, preferred_element_type=F32)


def _dot_nt(a, b):
    return lax.dot_general(a, b, (((1,), (1,)), ((), ())), preferred_element_type=F32)


def _rmsnorm(x, g):
    r = lax.rsqrt(jnp.mean(x * x, axis=-1, keepdims=True) + EPS)
    return x * r * g


def _log_sigmoid(x):
    return jnp.minimum(x, 0.0) - jnp.log1p(jnp.exp(-jnp.abs(x)))


def _scan_lanes(x, op, fill):
    lane = lax.broadcasted_iota(jnp.int32, x.shape, 1)
    k = 1
    while k < x.shape[1]:
        shifted = pltpu.roll(x, k, 1)
        x = op(x, jnp.where(lane >= k, shifted, fill))
        k *= 2
    return x


def _mixer_kernel(x_ref, wm_ref, wgt_ref, gbias_ref, g1_ref, gv_ref, ws_ref, bst_ref,
                  wa_ref, wb_ref, wo_ref,
                  xo_ref, c_ref, n_ref, m_ref,
                  ct_s, n_s, m_s, q_s, k_s, v_s, h_s, so_s, u_s, vb_s, sg_s, um_s, *, block):
    s = pl.program_id(1)
    n_chunks = block // CHUNK

    @pl.when(s == 0)
    def _():
        ct_s[...] = jnp.zeros_like(ct_s)
        n_s[...] = jnp.zeros_like(n_s)
        m_s[...] = jnp.zeros_like(m_s)

    xn = _rmsnorm(x_ref[...], g1_ref[...]).astype(BF16)

    gates = _dot_nt(wgt_ref[...], xn) + gbias_ref[...]

    def proj(off, t):
        return _dot(xn, wm_ref[:, off + t * PROJ_TILE:off + (t + 1) * PROJ_TILE])

    def tile(t):
        return slice(t * PROJ_TILE, (t + 1) * PROJ_TILE)

    for t in range(N_HEADS * DK // PROJ_TILE):
        q_s[:, tile(t)] = proj(OFF_Q, t)
        k_s[:, tile(t)] = proj(OFF_K, t) * K_SCALE
    for t in range(N_HEADS * DV // PROJ_TILE):
        v_s[:, tile(t)] = proj(OFF_V, t)

    sumsq = [jnp.zeros((block, 1), F32)]

    def vb_tile(t):
        g = jax.nn.gelu(proj(OFF_VB, t))
        vb_s[:, tile(t)] = g
        sumsq[0] = sumsq[0] + jnp.sum(g * g, axis=-1, keepdims=True)

    def u_tile(t):
        u_s[:, tile(t)] = jax.nn.gelu(proj(OFF_U, t))

    def o_tile(t):
        so_s[:, tile(t)] = jax.nn.sigmoid(proj(OFF_O, t))

    def ga_tile(t):
        sg_s[:, tile(t)] = jax.nn.sigmoid(proj(OFF_GA, t))

    def gb_tile(t):
        sg_s[:, D_MODEL + t * PROJ_TILE:D_MODEL + (t + 1) * PROJ_TILE] = jax.nn.sigmoid(proj(OFF_GB, t))

    jobs = [(f, t) for f in (vb_tile, u_tile, o_tile, ga_tile, gb_tile) for t in range(D_MODEL // PROJ_TILE)]

    def run_jobs(count):
        for _ in range(min(count, len(jobs))):
            f, t = jobs.pop(0)
            f(t)

    row_i = lax.broadcasted_iota(jnp.int32, (CHUNK, CHUNK), 0)
    col_i = lax.broadcasted_iota(jnp.int32, (CHUNK, CHUNK), 1)
    causal = row_i >= col_i
    heads = range(N_HEADS)

    for c in range(n_chunks):
        r0 = c * CHUNK
        ig = gates[0:GATE_ROWS, r0:r0 + CHUNK]
        lf = _log_sigmoid(gates[GATE_ROWS:2 * GATE_ROWS, r0:r0 + CHUNK])
        b = _scan_lanes(lf, jnp.add, 0.0)
        a = ig - b
        m_prev = m_s[...]
        gmax = jnp.maximum(m_prev, _scan_lanes(a, jnp.maximum, -jnp.inf))
        m_t = b + gmax
        w_inter = jnp.exp(m_prev - gmax)
        g_last = gmax[:, CHUNK - 1:CHUNK]
        w_last = jnp.exp(a - g_last)
        floor = jnp.exp(-m_t)
        decay = w_inter[:, CHUNK - 1:CHUNK]
        m_s[...] = jnp.broadcast_to(m_t[:, CHUNK - 1:CHUNK], m_s.shape)

        rows = jnp.concatenate(
            [gmax, w_inter, w_last, floor,
             jnp.zeros((CHUNK - 4 * GATE_ROWS, CHUNK), F32)], axis=0)
        cols = rows.T

        def col(kind, h):
            return cols[:, kind * GATE_ROWS + h:kind * GATE_ROWS + h + 1]

        qf = [q_s[r0:r0 + CHUNK, h * DK:(h + 1) * DK] for h in heads]
        kf = [k_s[r0:r0 + CHUNK, h * DK:(h + 1) * DK] for h in heads]
        vf = [v_s[r0:r0 + CHUNK, h * DV:(h + 1) * DV] for h in heads]
        qb = [x.astype(BF16) for x in qf]
        kt = [x.T.astype(BF16) for x in kf]
        run_jobs(2)
        sc = [_dot(qb[h], kt[h]) for h in heads]
        dmat = [jnp.where(causal, jnp.exp(a[h:h + 1, :] - col(0, h)), 0.0) for h in heads]
        run_jobs(1)
        sd = [sc[h] * dmat[h] for h in heads]
        ct = [ct_s[h] for h in heads]
        num = [_dot(sd[h].astype(BF16), vf[h].astype(BF16)) + col(1, h) * _dot(qb[h], ct[h].astype(BF16))
               for h in heads]
        run_jobs(2)
        for h in heads:
            nh = n_s[h:h + 1, :]
            den = (jnp.sum(sd[h], axis=-1, keepdims=True)
                   + col(1, h) * jnp.sum(qf[h] * nh, axis=-1, keepdims=True))
            h_s[r0:r0 + CHUNK, h * DV:(h + 1) * DV] = num[h] * (1.0 / jnp.maximum(jnp.abs(den), col(3, h)))
            dec = decay[h:h + 1, :]
            n_s[h:h + 1, :] = dec * nh + jnp.sum(col(2, h) * kf[h], axis=0, keepdims=True)
        run_jobs(1)
        for h in heads:
            ct_s[h] = decay[h:h + 1, :] * ct[h] + _dot(kt[h], (col(2, h) * vf[h]).astype(BF16))
    run_jobs(len(jobs))

    y_a = _dot((so_s[...] * h_s[...]).astype(BF16), wa_ref[...])

    rinv = lax.rsqrt(sumsq[0] * (1.0 / D_B) + EPS)
    for g in range(N_GROUPS):
        gcols = slice(g * DG, (g + 1) * DG)
        w_tri = jnp.where(causal, ws_ref[g], 0.0).astype(BF16)
        bias_c = bst_ref[:, g:g + 1]
        for c in range(n_chunks):
            rws = slice(c * CHUNK, (c + 1) * CHUNK)
            vn = vb_s[rws, gcols] * rinv[rws] * gv_ref[:, gcols]
            um_s[rws, gcols] = u_s[rws, gcols] * (_dot(w_tri, vn.astype(BF16)) + bias_c)
    y_b = _dot(um_s[...].astype(BF16), wb_ref[...])

    merged = sg_s[:, 0:D_MODEL] * y_a + sg_s[:, D_MODEL:2 * D_MODEL] * y_b
    xo_ref[...] = x_ref[...] + _dot(merged.astype(BF16), wo_ref[...])

    @pl.when(s == pl.num_programs(1) - 1)
    def _():
        for h in range(N_HEADS):
            c_ref[h] = ct_s[h].T
        n_ref[...] = n_s[...]
        m_ref[...] = m_s[...]


def _resident(arr, layer=None):
    if layer is None:
        nd = arr.ndim
        return pl.BlockSpec(arr.shape, lambda *_: (0,) * nd, pipeline_mode=pl.Buffered(1))
    nd = arr.ndim - 1
    return pl.BlockSpec((None,) + arr.shape[1:], lambda *_: (layer,) + (0,) * nd,
                        pipeline_mode=pl.Buffered(1))


def _prompt_mixer(x, layer, wm, wgt, gbias, g1, gv, ws, bst, wa, wb, wo):
    nb, seq, _ = x.shape
    block = MIXER_BLOCK
    grid = (nb, seq // block)
    xspec = pl.BlockSpec((None, block, D_MODEL), lambda b, s: (b, s, 0))
    out_shape = (
        jax.ShapeDtypeStruct((nb, seq, D_MODEL), F32),
        jax.ShapeDtypeStruct((nb, N_HEADS, DV, DK), F32),
        jax.ShapeDtypeStruct((nb, N_HEADS, DK), F32),
        jax.ShapeDtypeStruct((nb, GATE_ROWS, V7X_LANES), F32),
    )
    out_specs = (
        xspec,
        pl.BlockSpec((None, N_HEADS, DV, DK), lambda b, s: (b, 0, 0, 0)),
        pl.BlockSpec((None, N_HEADS, DK), lambda b, s: (b, 0, 0)),
        pl.BlockSpec((None, GATE_ROWS, V7X_LANES), lambda b, s: (b, 0, 0)),
    )
    in_specs = [xspec] + [_resident(a, layer) for a in (wm, wgt, gbias, g1, gv, ws, bst, wa, wb, wo)]
    scratch = [
        pltpu.VMEM((N_HEADS, DK, DV), F32),
        pltpu.VMEM((N_HEADS, DK), F32),
        pltpu.VMEM((GATE_ROWS, V7X_LANES), F32),
        pltpu.VMEM((block, N_HEADS * DK), F32),
        pltpu.VMEM((block, N_HEADS * DK), F32),
        pltpu.VMEM((block, N_HEADS * DV), F32),
        pltpu.VMEM((block, N_HEADS * DV), F32),
        pltpu.VMEM((block, N_HEADS * DV), F32),
        pltpu.VMEM((block, D_B), F32),
        pltpu.VMEM((block, D_B), F32),
        pltpu.VMEM((block, 2 * D_MODEL), F32),
        pltpu.VMEM((block, D_B), F32),
    ]
    return pl.pallas_call(
        functools.partial(_mixer_kernel, block=block),
        grid=grid, in_specs=in_specs, out_specs=out_specs, out_shape=out_shape,
        scratch_shapes=scratch,
        compiler_params=pltpu.CompilerParams(
            dimension_semantics=("arbitrary", "arbitrary"), vmem_limit_bytes=V7X_VMEM_LIMIT),
        name="prompt_mixer",
    )(x, wm, wgt, gbias, g1, gv, ws, bst, wa, wb, wo)


def _conv_taps(up, carry_s, cw_ref, cb_ref, cols):
    sub = V7X_SUBLANES
    rows = up.shape[0]
    last1 = up[rows - sub:rows]
    last2 = up[rows - 2 * sub:rows - sub]
    first = lax.broadcasted_iota(jnp.int32, last1.shape, 0) == 0
    back1 = jnp.where(first, carry_s[sub - 1:sub, cols], pltpu.roll(last1, 1, 0))
    back2 = jnp.where(first, carry_s[sub - 2:sub - 1, cols], pltpu.roll(last2, 1, 0))
    carry_s[sub - 2:sub - 1, cols] = last2[sub - 1:sub]
    carry_s[sub - 1:sub, cols] = last1[sub - 1:sub]
    m1 = jnp.concatenate([back1, up[0:rows - sub]], axis=0)
    m2 = jnp.concatenate([back2, back1, up[0:rows - 2 * sub]], axis=0)
    return (cb_ref[:, cols] + cw_ref[0:1, cols] * m2 + cw_ref[1:2, cols] * m1
            + cw_ref[2:3, cols] * up)


def _perm_pitch(block):
    return block // V7X_SUBLANES + V7X_SUBLANES


def _ffn_kernel(x_ref, g2_ref, wup_ref, cw_ref, cb_ref, wdn_ref, gf_ref,
                xo_ref, conv_ref, carry_s, act_s, perm_s, *, block, final):
    s = pl.program_id(1)
    sub, lanes = V7X_SUBLANES, V7X_LANES
    groups = block // sub
    chunks = D_MODEL // lanes

    @pl.when(s == 0)
    def _():
        carry_s[...] = jnp.zeros_like(carry_s)

    pitch = _perm_pitch(block)
    for c in range(chunks):
        for i in range(sub):
            perm_s[c, i * pitch:i * pitch + groups] = x_ref[i * groups:(i + 1) * groups, c * lanes:(c + 1) * lanes]
    x = jnp.concatenate(
        [jnp.concatenate([perm_s[c, pl.ds(r, sub, stride=pitch), :] for c in range(chunks)], axis=1)
         for r in range(groups)], axis=0)
    xn = _rmsnorm(x, g2_ref[...]).astype(BF16)
    for j in range(D_FF // FFN_COL_TILE):
        halves = []
        for half in range(2):
            cols = slice(half * D_FF + j * FFN_COL_TILE, half * D_FF + (j + 1) * FFN_COL_TILE)
            up = _dot(xn, wup_ref[:, cols])
            halves.append(_conv_taps(up, carry_s, cw_ref, cb_ref, cols))
        act_s[:, j * FFN_COL_TILE:(j + 1) * FFN_COL_TILE] = (jax.nn.silu(halves[0]) * halves[1]).astype(BF16)
    y = x + _dot(act_s[...], wdn_ref[...])
    if final:
        y = _rmsnorm(y, gf_ref[...])
    for r in range(groups):
        for c in range(chunks):
            perm_s[c, pl.ds(r, sub, stride=pitch), :] = y[r * sub:(r + 1) * sub, c * lanes:(c + 1) * lanes]
    for c in range(chunks):
        for i in range(sub):
            xo_ref[i * groups:(i + 1) * groups, c * lanes:(c + 1) * lanes] = perm_s[c, i * pitch:i * pitch + groups]

    @pl.when(s == pl.num_programs(1) - 1)
    def _():
        conv_ref[...] = carry_s[V7X_SUBLANES - (CONV_W - 1):V7X_SUBLANES, :]


def _prompt_ffn(x, layer, g2, wup, cw, cb, wdn, gf, final):
    nb, seq, _ = x.shape
    block = FFN_BLOCK
    grid = (nb, seq // block)
    xspec = pl.BlockSpec((None, block, D_MODEL), lambda b, s: (b, s, 0))
    out_shape = (
        jax.ShapeDtypeStruct((nb, seq, D_MODEL), F32),
        jax.ShapeDtypeStruct((nb, CONV_W - 1, 2 * D_FF), F32),
    )
    out_specs = (xspec, pl.BlockSpec((None, CONV_W - 1, 2 * D_FF), lambda b, s: (b, 0, 0)))
    in_specs = [xspec] + [_resident(a, layer) for a in (g2, wup, cw, cb, wdn)] + [_resident(gf)]
    return pl.pallas_call(
        functools.partial(_ffn_kernel, block=block, final=final),
        grid=grid, in_specs=in_specs, out_specs=out_specs, out_shape=out_shape,
        scratch_shapes=[pltpu.VMEM((V7X_SUBLANES, 2 * D_FF), F32),
                        pltpu.VMEM((block, D_FF), BF16),
                        pltpu.VMEM((D_MODEL // V7X_LANES, V7X_SUBLANES * _perm_pitch(block), V7X_LANES),
                                   F32)],
        compiler_params=pltpu.CompilerParams(
            dimension_semantics=("arbitrary", "arbitrary"), vmem_limit_bytes=V7X_VMEM_LIMIT),
        name="prompt_ffn",
    )(x, g2, wup, cw, cb, wdn, gf)


def _sample_proj_kernel(x_ref, g1_ref, wm_ref, wgc_ref, z_ref, gates_ref):
    xn = _rmsnorm(x_ref[...], g1_ref[...]).astype(BF16)
    z_ref[...] = _dot(xn, wm_ref[...])
    gates_ref[...] = _dot(xn, wgc_ref[...])


def _single_step(kernel_fn, name, out_shape, whole, layered, layer):
    return pl.pallas_call(
        kernel_fn,
        grid=(1,),
        in_specs=[_resident(a) for a in whole] + [_resident(a, layer) for a in layered],
        out_specs=tuple(pl.BlockSpec(o.shape, lambda i, nd=len(o.shape): (0,) * nd) for o in out_shape),
        out_shape=out_shape,
        compiler_params=pltpu.CompilerParams(
            dimension_semantics=("arbitrary",), vmem_limit_bytes=V7X_VMEM_LIMIT),
        name=name,
    )(*whole, *layered)


def _sample_proj(x, layer, g1, wm, wgc):
    n = x.shape[0]
    out_shape = (jax.ShapeDtypeStruct((n, P_MAIN), F32), jax.ShapeDtypeStruct((n, 2 * V7X_LANES), F32))
    return _single_step(_sample_proj_kernel, "sample_proj", out_shape, (x,), (g1, wm, wgc), layer)


def _sample_state_kernel(q_ref, k_ref, v_ref, gates_ref, gbias_ref, m_ref, n_ref, c_ref, *rest):
    h_ref, co_ref, no_ref, mo_ref = rest[-4:]
    tb = SAMPLE_BLOCK
    ig = gates_ref[:, 0:V7X_LANES] + gbias_ref[:, 0:V7X_LANES]
    lf = _log_sigmoid(gates_ref[:, V7X_LANES:] + gbias_ref[:, V7X_LANES:])
    inter = lf + m_ref[...]
    m_t = jnp.maximum(inter, ig)
    d_in = jnp.exp(ig - m_t)
    w_inter = jnp.exp(inter - m_t)
    floor = jnp.exp(-m_t)
    mo_ref[...] = m_t

    row8 = lax.broadcasted_iota(jnp.int32, (CHUNK, DK), 0)
    for h in range(N_HEADS):
        q8 = q_ref[:, h * DK:(h + 1) * DK]
        k8 = k_ref[:, h * DK:(h + 1) * DK] * K_SCALE
        v8 = v_ref[:, h * DV:(h + 1) * DV]
        d_h = d_in[:, h:h + 1]
        w_h = w_inter[:, h:h + 1]
        n8 = n_ref[:, h * DK:(h + 1) * DK]
        s = jnp.sum(q8 * k8, axis=-1, keepdims=True) * d_h
        den = s + w_h * jnp.sum(q8 * n8, axis=-1, keepdims=True)
        qb = q8.astype(BF16)
        inter_rows = [_dot_nt(qb, c_ref[j, h].astype(BF16))[j:j + 1] for j in range(tb)]
        num = s * v8 + w_h * jnp.concatenate(inter_rows, axis=0)
        h_ref[:, h * DV:(h + 1) * DV] = num / jnp.maximum(jnp.abs(den), floor[:, h:h + 1])
        no_ref[:, h * DK:(h + 1) * DK] = w_h * n8 + d_h * k8

        vt = jnp.concatenate([d_h * v8, jnp.zeros((CHUNK - tb, DV), F32)], axis=0).T.astype(BF16)
        kpad = jnp.concatenate([k8, jnp.zeros((CHUNK - tb, DK), F32)], axis=0)
        for j in range(tb):
            kj = jnp.where(row8 == j, kpad, 0.0).astype(BF16)
            co_ref[j, h] = w_inter[j:j + 1, h:h + 1] * c_ref[j, h] + _dot(vt, kj)


def _sample_state(z, gates, gbias, m_pad, n_state, c_state, layer, c_stack):
    n = z.shape[0]
    tb = SAMPLE_BLOCK
    c_block = pl.BlockSpec((None, tb, N_HEADS, DV, DK), lambda i: (layer, i, 0, 0, 0))
    in_specs = [
        pl.BlockSpec((tb, N_HEADS * DK), lambda i: (i, OFF_Q // (N_HEADS * DK))),
        pl.BlockSpec((tb, N_HEADS * DK), lambda i: (i, OFF_K // (N_HEADS * DK))),
        pl.BlockSpec((tb, N_HEADS * DV), lambda i: (i, OFF_V // (N_HEADS * DV))),
        pl.BlockSpec((tb, 2 * V7X_LANES), lambda i: (i, 0)),
        pl.BlockSpec((1, 2 * V7X_LANES), lambda i: (0, 0)),
        pl.BlockSpec((tb, V7X_LANES), lambda i: (i, 0)),
        pl.BlockSpec((tb, N_HEADS * DK), lambda i: (i, 0)),
        c_block,
    ]
    args = [z, z, z, gates, gbias, m_pad, n_state, c_state]
    aliases = {}
    if c_stack is not None:
        in_specs.append(pl.BlockSpec(memory_space=pl.ANY))
        aliases = {len(args): 1}
        args.append(c_stack)
    out_shape = (
        jax.ShapeDtypeStruct((n, N_HEADS * DV), F32),
        jax.ShapeDtypeStruct(c_state.shape, F32),
        jax.ShapeDtypeStruct((n, N_HEADS * DK), F32),
        jax.ShapeDtypeStruct((n, V7X_LANES), F32),
    )
    out_specs = (
        pl.BlockSpec((tb, N_HEADS * DV), lambda i: (i, 0)),
        c_block,
        pl.BlockSpec((tb, N_HEADS * DK), lambda i: (i, 0)),
        pl.BlockSpec((tb, V7X_LANES), lambda i: (i, 0)),
    )
    return pl.pallas_call(
        _sample_state_kernel,
        grid=(n // tb,), in_specs=in_specs, out_specs=out_specs, out_shape=out_shape,
        input_output_aliases=aliases,
        compiler_params=pltpu.CompilerParams(
            dimension_semantics=("arbitrary",), vmem_limit_bytes=V7X_VMEM_LIMIT),
        name="sample_state",
    )(*args)


def _sample_mixer_kernel(x_ref, z_ref, h_ref, gv_ref, ws0_ref, bs0_ref, wa_ref, wb_ref, wo_ref,
                         xo_ref, vn_ref):
    y_a = _dot((jax.nn.sigmoid(z_ref[:, OFF_O:OFF_U]) * h_ref[...]).astype(BF16), wa_ref[...])
    u = jax.nn.gelu(z_ref[:, OFF_U:OFF_VB])
    vn = _rmsnorm(jax.nn.gelu(z_ref[:, OFF_VB:OFF_GA]), gv_ref[...])
    vn_ref[...] = vn
    mixed = ws0_ref[...] * vn + bs0_ref[...]
    y_b = _dot((u * mixed).astype(BF16), wb_ref[...])
    merged = (jax.nn.sigmoid(z_ref[:, OFF_GA:OFF_GB]) * y_a
              + jax.nn.sigmoid(z_ref[:, OFF_GB:P_MAIN]) * y_b)
    xo_ref[...] = x_ref[...] + _dot(merged.astype(BF16), wo_ref[...])


def _sample_mixer(x, z, h, layer, gv, ws0, bs0, wa, wb, wo):
    n = x.shape[0]
    out_shape = (jax.ShapeDtypeStruct((n, D_MODEL), F32), jax.ShapeDtypeStruct((n, D_B), F32))
    return _single_step(_sample_mixer_kernel, "sample_mixer", out_shape,
                        (x, z, h), (gv, ws0, bs0, wa, wb, wo), layer)


def _sample_ffn_kernel(x_ref, gf_ref, buf_ref, g2_ref, wup_ref, cw_ref, cb_ref, wdn_ref,
                       xo_ref, nbuf_ref, *, final):
    x = x_ref[...]
    xn = _rmsnorm(x, g2_ref[...]).astype(BF16)
    acc = jnp.zeros(x.shape, F32)
    for j in range(D_FF // FFN_COL_TILE):
        halves = []
        for half in range(2):
            col0 = half * D_FF + j * FFN_COL_TILE
            cols = slice(col0, col0 + FFN_COL_TILE)
            up = _dot(xn, wup_ref[:, cols])
            b0 = buf_ref[:, 0, cols]
            b1 = buf_ref[:, 1, cols]
            nbuf_ref[:, 0, cols] = b1
            nbuf_ref[:, 1, cols] = up
            halves.append(cb_ref[:, cols] + cw_ref[0:1, cols] * b0 + cw_ref[1:2, cols] * b1
                          + cw_ref[2:3, cols] * up)
        act = (jax.nn.silu(halves[0]) * halves[1]).astype(BF16)
        acc = acc + _dot(act, wdn_ref[j * FFN_COL_TILE:(j + 1) * FFN_COL_TILE, :])
    y = x + acc
    if final:
        y = _rmsnorm(y, gf_ref[...])
    xo_ref[...] = y


def _sample_ffn(x, gf, layer, buf, g2, wup, cw, cb, wdn, final):
    n = x.shape[0]
    out_shape = (jax.ShapeDtypeStruct((n, D_MODEL), F32),
                 jax.ShapeDtypeStruct((n, CONV_W - 1, 2 * D_FF), F32))
    return _single_step(functools.partial(_sample_ffn_kernel, final=final), "sample_ffn", out_shape,
                        (x, gf), (buf, g2, wup, cw, cb, wdn), layer)


def _pack_kernel(wt_ref, o_ref):
    for j in range(P_MAIN // PACK_TILE):
        dst = j * PACK_TILE
        src = dst if dst < GATE_LO else dst + (GATE_HI - GATE_LO)
        o_ref[:, dst:dst + PACK_TILE] = wt_ref[src:src + PACK_TILE, :].T.astype(BF16)


def _pack_w_in(w_in_t):
    depth, p_in, d = w_in_t.shape
    return pl.pallas_call(
        _pack_kernel,
        grid=(depth, d // V7X_LANES),
        in_specs=[pl.BlockSpec((None, p_in, V7X_LANES), lambda l, r: (l, 0, r))],
        out_specs=pl.BlockSpec((None, V7X_LANES, P_MAIN), lambda l, r: (l, r, 0)),
        out_shape=jax.ShapeDtypeStruct((depth, d, P_MAIN), BF16),
        compiler_params=pltpu.CompilerParams(
            dimension_semantics=("arbitrary", "arbitrary"), vmem_limit_bytes=V7X_VMEM_LIMIT),
        name="pack_w_in",
    )(w_in_t)


def kernel(x_prompt, x_sample, state_mlstm_C, state_mlstm_n, state_mlstm_m, state_ffn_conv, w_in, b_igate, b_fgate, g_norm1, g_vnorm, w_spatial, b_spatial, w_branch_a, w_branch_b, w_out, g_norm2, w_up, conv_w, conv_b, w_down, g_final):
    depth = w_in.shape[0]
    n_dec = x_sample.shape[0]
    xp = x_prompt
    xs = x_sample.reshape(n_dec, D_MODEL)
    gf = g_final.reshape(1, D_MODEL)

    w_in_t = jnp.swapaxes(w_in, 1, 2)
    wm = _pack_w_in(w_in_t)
    w_gate_t = w_in_t[:, GATE_LO:GATE_HI, :]
    w_gate = jnp.swapaxes(w_gate_t, 1, 2)
    row_pad = ((0, 0), (0, GATE_ROWS - N_HEADS), (0, 0))
    wgt = jnp.concatenate([jnp.pad(w_gate_t[:, :N_HEADS], row_pad),
                           jnp.pad(w_gate_t[:, N_HEADS:], row_pad)], axis=1).astype(BF16)
    lane_pad = ((0, 0), (0, 0), (0, V7X_LANES - N_HEADS))
    wgc = jnp.concatenate([jnp.pad(w_gate[:, :, :N_HEADS], lane_pad),
                           jnp.pad(w_gate[:, :, N_HEADS:], lane_pad)], axis=2).astype(BF16)
    head_pad = ((0, 0), (0, GATE_ROWS - N_HEADS))
    gbias_col = jnp.concatenate([jnp.pad(b_igate, head_pad), jnp.pad(b_fgate, head_pad)],
                                axis=1).reshape(depth, 2 * GATE_ROWS, 1)
    head_lanes = ((0, 0), (0, V7X_LANES - N_HEADS))
    gbias_row = jnp.concatenate([jnp.pad(b_igate, head_lanes), jnp.pad(b_fgate, head_lanes)],
                                axis=1).reshape(depth, 1, 2 * V7X_LANES)
    g1 = g_norm1.reshape(depth, 1, D_MODEL)
    gv = g_vnorm.reshape(depth, 1, D_B)
    g2 = g_norm2.reshape(depth, 1, D_MODEL)
    bst = jnp.swapaxes(b_spatial, 1, 2)
    ws0 = jnp.repeat(w_spatial[:, :, 0, 0], DG, axis=1).reshape(depth, 1, D_B)
    bs0 = jnp.repeat(b_spatial[:, :, 0], DG, axis=1).reshape(depth, 1, D_B)
    wa = w_branch_a.astype(BF16)
    wb = w_branch_b.astype(BF16)
    wo = w_out.astype(BF16)
    wup = w_up.astype(BF16)
    wdn = w_down.astype(BF16)
    cb = conv_b.reshape(depth, 1, 2 * D_FF)
    n_state = state_mlstm_n.reshape(depth, n_dec, N_HEADS * DK)
    m_pad = jnp.pad(state_mlstm_m, ((0, 0), (0, 0), (0, V7X_LANES - N_HEADS)))

    small = [[] for _ in range(8)]
    c_stack = None
    for l in range(depth):
        final = l == depth - 1

        xp, c_p, n_p, m_p = _prompt_mixer(xp, l, wm, wgt, gbias_col, g1, gv, w_spatial, bst, wa, wb, wo)
        xp, conv_p = _prompt_ffn(xp, l, g2, wup, conv_w, cb, wdn, gf, final)

        z, gates = _sample_proj(xs, l, g1, wm, wgc)
        h, c_stack, n_s, m_s = _sample_state(
            z, gates, gbias_row[l], m_pad[l], n_state[l], state_mlstm_C, l, c_stack)
        xs, vn_s = _sample_mixer(xs, z, h, l, gv, ws0, bs0, wa, wb, wo)
        xs, conv_s = _sample_ffn(xs, gf, l, state_ffn_conv, g2, wup, conv_w, cb, wdn, final)

        for lst, val in zip(small, (c_p, n_p, m_p[:, :N_HEADS, 0], conv_p,
                                    n_s.reshape(n_dec, N_HEADS, DK), m_s[:, :N_HEADS],
                                    conv_s,
                                    vn_s.reshape(n_dec, 1, D_B))):
            lst.append(val)
    st = [jnp.stack(o) for o in small]
    return (xp, xs.reshape(n_dec, 1, D_MODEL), st[0], st[1], st[2], st[3], c_stack, st[4], st[5], st[6], st[7])
```

```python
import functools

import jax
import jax.numpy as jnp
from jax import lax
from jax.experimental import pallas as pl
from jax.experimental.pallas import tpu as pltpu

D_MODEL = 1024
N_HEADS = 4
DK = 128
DV = 256
CHUNK = 128
D_B = 1024
N_GROUPS = 4
DG = D_B // N_GROUPS
D_FF = 2816
CONV_W = 3
EPS = 1e-6
K_SCALE = DK ** -0.5

OFF_Q = 0
OFF_K = OFF_Q + N_HEADS * DK
OFF_V = OFF_K + N_HEADS * DK
OFF_O = OFF_V + N_HEADS * DV
OFF_U = OFF_O + N_HEADS * DV
OFF_VB = OFF_U + D_B
OFF_GA = OFF_VB + D_B
OFF_GB = OFF_GA + D_MODEL
P_MAIN = OFF_GB + D_MODEL
GATE_LO = 2 * N_HEADS * DK + 2 * N_HEADS * DV
GATE_HI = GATE_LO + 2 * N_HEADS

V7X_LANES = 128
V7X_SUBLANES = 8
GATE_ROWS = V7X_SUBLANES
V7X_VMEM_LIMIT = 56 * 1024 * 1024
FFN_COL_TILE = 256
PROJ_TILE = 256
MIXER_BLOCK = 512
FFN_BLOCK = 512
SAMPLE_BLOCK = 8
PACK_TILE = 256

F32 = jnp.float32
BF16 = jnp.bfloat16


def _dot(a, b):
    return jnp.dot(a, b, preferred_element_type=F32)


def _dot_nt(a, b):
    return lax.dot_general(a, b, (((1,), (1,)), ((), ())), preferred_element_type=F32)


def _rmsnorm(x, g):
    r = lax.rsqrt(jnp.mean(x * x, axis=-1, keepdims=True) + EPS)
    return x * r * g


def _log_sigmoid(x):
    return jnp.minimum(x, 0.0) - jnp.log1p(jnp.exp(-jnp.abs(x)))


def _scan_lanes(x, op, fill):
    lane = lax.broadcasted_iota(jnp.int32, x.shape, 1)
    k = 1
    while k < x.shape[1]:
        shifted = pltpu.roll(x, k, 1)
        x = op(x, jnp.where(lane >= k, shifted, fill))
        k *= 2
    return x


def _mixer_kernel(x_ref, wm_ref, wgt_ref, gbias_ref, g1_ref, gv_ref, ws_ref, bst_ref,
                  wa_ref, wb_ref, wo_ref,
                  xo_ref, c_ref, n_ref, m_ref,
                  ct_s, n_s, m_s, q_s, k_s, v_s, h_s, so_s, u_s, vb_s, sg_s, um_s, *, block):
    s = pl.program_id(1)
    n_chunks = block // CHUNK

    @pl.when(s == 0)
    def _():
        ct_s[...] = jnp.zeros_like(ct_s)
        n_s[...] = jnp.zeros_like(n_s)
        m_s[...] = jnp.zeros_like(m_s)

    xn = _rmsnorm(x_ref[...], g1_ref[...]).astype(BF16)

    gates = _dot_nt(wgt_ref[...], xn) + gbias_ref[...]

    def proj(off, t):
        return _dot(xn, wm_ref[:, off + t * PROJ_TILE:off + (t + 1) * PROJ_TILE])

    def tile(t):
        return slice(t * PROJ_TILE, (t + 1) * PROJ_TILE)

    for t in range(N_HEADS * DK // PROJ_TILE):
        q_s[:, tile(t)] = proj(OFF_Q, t)
        k_s[:, tile(t)] = proj(OFF_K, t) * K_SCALE
    for t in range(N_HEADS * DV // PROJ_TILE):
        v_s[:, tile(t)] = proj(OFF_V, t)

    sumsq = [jnp.zeros((block, 1), F32)]

    def vb_tile(t):
        g = jax.nn.gelu(proj(OFF_VB, t))
        vb_s[:, tile(t)] = g
        sumsq[0] = sumsq[0] + jnp.sum(g * g, axis=-1, keepdims=True)

    def u_tile(t):
        u_s[:, tile(t)] = jax.nn.gelu(proj(OFF_U, t))

    def o_tile(t):
        so_s[:, tile(t)] = jax.nn.sigmoid(proj(OFF_O, t))

    def ga_tile(t):
        sg_s[:, tile(t)] = jax.nn.sigmoid(proj(OFF_GA, t))

    def gb_tile(t):
        sg_s[:, D_MODEL + t * PROJ_TILE:D_MODEL + (t + 1) * PROJ_TILE] = jax.nn.sigmoid(proj(OFF_GB, t))

    jobs = [(f, t) for f in (vb_tile, u_tile, o_tile, ga_tile, gb_tile) for t in range(D_MODEL // PROJ_TILE)]

    def run_jobs(count):
        for _ in range(min(count, len(jobs))):
            f, t = jobs.pop(0)
            f(t)

    row_i = lax.broadcasted_iota(jnp.int32, (CHUNK, CHUNK), 0)
    col_i = lax.broadcasted_iota(jnp.int32, (CHUNK, CHUNK), 1)
    causal = row_i >= col_i
    heads = range(N_HEADS)

    for c in range(n_chunks):
        r0 = c * CHUNK
        ig = gates[0:GATE_ROWS, r0:r0 + CHUNK]
        lf = _log_sigmoid(gates[GATE_ROWS:2 * GATE_ROWS, r0:r0 + CHUNK])
        b = _scan_lanes(lf, jnp.add, 0.0)
        a = ig - b
        m_prev = m_s[...]
        gmax = jnp.maximum(m_prev, _scan_lanes(a, jnp.maximum, -jnp.inf))
        m_t = b + gmax
        w_inter = jnp.exp(m_prev - gmax)
        g_last = gmax[:, CHUNK - 1:CHUNK]
        w_last = jnp.exp(a - g_last)
        floor = jnp.exp(-m_t)
        decay = w_inter[:, CHUNK - 1:CHUNK]
        m_s[...] = jnp.broadcast_to(m_t[:, CHUNK - 1:CHUNK], m_s.shape)

        rows = jnp.concatenate(
            [gmax, w_inter, w_last, floor,
             jnp.zeros((CHUNK - 4 * GATE_ROWS, CHUNK), F32)], axis=0)
        cols = rows.T

        def col(kind, h):
            return cols[:, kind * GATE_ROWS + h:kind * GATE_ROWS + h + 1]

        qf = [q_s[r0:r0 + CHUNK, h * DK:(h + 1) * DK] for h in heads]
        kf = [k_s[r0:r0 + CHUNK, h * DK:(h + 1) * DK] for h in heads]
        vf = [v_s[r0:r0 + CHUNK, h * DV:(h + 1) * DV] for h in heads]
        qb = [x.astype(BF16) for x in qf]
        kt = [x.T.astype(BF16) for x in kf]
        run_jobs(2)
        zero = jnp.zeros((DK, CHUNK), BF16)
        sc = []
        for h in range(0, N_HEADS, 2):
            kk = jnp.concatenate([jnp.concatenate([kt[h], zero], axis=1),
                                  jnp.concatenate([zero, kt[h + 1]], axis=1)], axis=0)
            pair = _dot(jnp.concatenate([qb[h], qb[h + 1]], axis=1), kk)
            sc += [pair[:, :CHUNK], pair[:, CHUNK:]]
        dmat = [jnp.where(causal, jnp.exp(a[h:h + 1, :] - col(0, h)), 0.0) for h in heads]
        run_jobs(1)
        sd = [sc[h] * dmat[h] for h in heads]
        ct = [ct_s[h] for h in heads]
        num = [_dot(jnp.concatenate([sd[h], col(1, h) * qf[h]], axis=1).astype(BF16),
                    jnp.concatenate([vf[h], ct[h]], axis=0).astype(BF16)) for h in heads]
        run_jobs(2)
        for h in heads:
            nh = n_s[h:h + 1, :]
            den = (jnp.sum(sd[h], axis=-1, keepdims=True)
                   + col(1, h) * jnp.sum(qf[h] * nh, axis=-1, keepdims=True))
            h_s[r0:r0 + CHUNK, h * DV:(h + 1) * DV] = num[h] * (1.0 / jnp.maximum(jnp.abs(den), col(3, h)))
            dec = decay[h:h + 1, :]
            n_s[h:h + 1, :] = dec * nh + jnp.sum(col(2, h) * kf[h], axis=0, keepdims=True)
        run_jobs(1)
        for h in heads:
            ct_s[h] = decay[h:h + 1, :] * ct[h] + _dot(kt[h], (col(2, h) * vf[h]).astype(BF16))
    run_jobs(len(jobs))

    y_a = _dot((so_s[...] * h_s[...]).astype(BF16), wa_ref[...])

    rinv = lax.rsqrt(sumsq[0] * (1.0 / D_B) + EPS)
    for g in range(N_GROUPS):
        gcols = slice(g * DG, (g + 1) * DG)
        w_tri = jnp.where(causal, ws_ref[g], 0.0).astype(BF16)
        bias_c = bst_ref[:, g:g + 1]
        for c in range(n_chunks):
            rws = slice(c * CHUNK, (c + 1) * CHUNK)
            vn = vb_s[rws, gcols] * rinv[rws] * gv_ref[:, gcols]
            um_s[rws, gcols] = u_s[rws, gcols] * (_dot(w_tri, vn.astype(BF16)) + bias_c)
    y_b = _dot(um_s[...].astype(BF16), wb_ref[...])

    merged = sg_s[:, 0:D_MODEL] * y_a + sg_s[:, D_MODEL:2 * D_MODEL] * y_b
    xo_ref[...] = x_ref[...] + _dot(merged.astype(BF16), wo_ref[...])

    @pl.when(s == pl.num_programs(1) - 1)
    def _():
        for h in range(N_HEADS):
            c_ref[h] = ct_s[h].T
        n_ref[...] = n_s[...]
        m_ref[...] = m_s[...]


def _resident(arr, layer=None):
    if layer is None:
        nd = arr.ndim
        return pl.BlockSpec(arr.shape, lambda *_: (0,) * nd, pipeline_mode=pl.Buffered(1))
    nd = arr.ndim - 1
    return pl.BlockSpec((None,) + arr.shape[1:], lambda *_: (layer,) + (0,) * nd,
                        pipeline_mode=pl.Buffered(1))


def _prompt_mixer(x, layer, wm, wgt, gbias, g1, gv, ws, bst, wa, wb, wo):
    nb, seq, _ = x.shape
    block = MIXER_BLOCK
    grid = (nb, seq // block)
    xspec = pl.BlockSpec((None, block, D_MODEL), lambda b, s: (b, s, 0))
    out_shape = (
        jax.ShapeDtypeStruct((nb, seq, D_MODEL), F32),
        jax.ShapeDtypeStruct((nb, N_HEADS, DV, DK), F32),
        jax.ShapeDtypeStruct((nb, N_HEADS, DK), F32),
        jax.ShapeDtypeStruct((nb, GATE_ROWS, V7X_LANES), F32),
    )
    out_specs = (
        xspec,
        pl.BlockSpec((None, N_HEADS, DV, DK), lambda b, s: (b, 0, 0, 0)),
        pl.BlockSpec((None, N_HEADS, DK), lambda b, s: (b, 0, 0)),
        pl.BlockSpec((None, GATE_ROWS, V7X_LANES), lambda b, s: (b, 0, 0)),
    )
    in_specs = [xspec] + [_resident(a, layer) for a in (wm, wgt, gbias, g1, gv, ws, bst, wa, wb, wo)]
    scratch = [
        pltpu.VMEM((N_HEADS, DK, DV), F32),
        pltpu.VMEM((N_HEADS, DK), F32),
        pltpu.VMEM((GATE_ROWS, V7X_LANES), F32),
        pltpu.VMEM((block, N_HEADS * DK), F32),
        pltpu.VMEM((block, N_HEADS * DK), F32),
        pltpu.VMEM((block, N_HEADS * DV), F32),
        pltpu.VMEM((block, N_HEADS * DV), F32),
        pltpu.VMEM((block, N_HEADS * DV), F32),
        pltpu.VMEM((block, D_B), F32),
        pltpu.VMEM((block, D_B), F32),
        pltpu.VMEM((block, 2 * D_MODEL), F32),
        pltpu.VMEM((block, D_B), F32),
    ]
    return pl.pallas_call(
        functools.partial(_mixer_kernel, block=block),
        grid=grid, in_specs=in_specs, out_specs=out_specs, out_shape=out_shape,
        scratch_shapes=scratch,
        compiler_params=pltpu.CompilerParams(
            dimension_semantics=("arbitrary", "arbitrary"), vmem_limit_bytes=V7X_VMEM_LIMIT),
        name="prompt_mixer",
    )(x, wm, wgt, gbias, g1, gv, ws, bst, wa, wb, wo)


def _conv_taps(up, carry_s, cw_ref, cb_ref, cols):
    sub = V7X_SUBLANES
    rows = up.shape[0]
    last1 = up[rows - sub:rows]
    last2 = up[rows - 2 * sub:rows - sub]
    first = lax.broadcasted_iota(jnp.int32, last1.shape, 0) == 0
    back1 = jnp.where(first, carry_s[sub - 1:sub, cols], pltpu.roll(last1, 1, 0))
    back2 = jnp.where(first, carry_s[sub - 2:sub - 1, cols], pltpu.roll(last2, 1, 0))
    carry_s[sub - 2:sub - 1, cols] = last2[sub - 1:sub]
    carry_s[sub - 1:sub, cols] = last1[sub - 1:sub]
    m1 = jnp.concatenate([back1, up[0:rows - sub]], axis=0)
    m2 = jnp.concatenate([back2, back1, up[0:rows - 2 * sub]], axis=0)
    return (cb_ref[:, cols] + cw_ref[0:1, cols] * m2 + cw_ref[1:2, cols] * m1
            + cw_ref[2:3, cols] * up)


def _perm_pitch(block):
    return block // V7X_SUBLANES + V7X_SUBLANES


def _ffn_kernel(x_ref, g2_ref, wup_ref, cw_ref, cb_ref, wdn_ref, gf_ref,
                xo_ref, conv_ref, carry_s, act_s, perm_s, *, block, final):
    s = pl.program_id(1)
    sub, lanes = V7X_SUBLANES, V7X_LANES
    groups = block // sub
    chunks = D_MODEL // lanes

    @pl.when(s == 0)
    def _():
        carry_s[...] = jnp.zeros_like(carry_s)

    pitch = _perm_pitch(block)
    for c in range(chunks):
        for i in range(sub):
            perm_s[c, i * pitch:i * pitch + groups] = x_ref[i * groups:(i + 1) * groups, c * lanes:(c + 1) * lanes]
    x = jnp.concatenate(
        [jnp.concatenate([perm_s[c, pl.ds(r, sub, stride=pitch), :] for c in range(chunks)], axis=1)
         for r in range(groups)], axis=0)
    xn = _rmsnorm(x, g2_ref[...]).astype(BF16)
    for j in range(D_FF // FFN_COL_TILE):
        halves = []
        for half in range(2):
            cols = slice(half * D_FF + j * FFN_COL_TILE, half * D_FF + (j + 1) * FFN_COL_TILE)
            up = _dot(xn, wup_ref[:, cols])
            halves.append(_conv_taps(up, carry_s, cw_ref, cb_ref, cols))
        act_s[:, j * FFN_COL_TILE:(j + 1) * FFN_COL_TILE] = (jax.nn.silu(halves[0]) * halves[1]).astype(BF16)
    y = x + _dot(act_s[...], wdn_ref[...])
    if final:
        y = _rmsnorm(y, gf_ref[...])
    for r in range(groups):
        for c in range(chunks):
            perm_s[c, pl.ds(r, sub, stride=pitch), :] = y[r * sub:(r + 1) * sub, c * lanes:(c + 1) * lanes]
    for c in range(chunks):
        for i in range(sub):
            xo_ref[i * groups:(i + 1) * groups, c * lanes:(c + 1) * lanes] = perm_s[c, i * pitch:i * pitch + groups]

    @pl.when(s == pl.num_programs(1) - 1)
    def _():
        conv_ref[...] = carry_s[V7X_SUBLANES - (CONV_W - 1):V7X_SUBLANES, :]


def _prompt_ffn(x, layer, g2, wup, cw, cb, wdn, gf, final):
    nb, seq, _ = x.shape
    block = FFN_BLOCK
    grid = (nb, seq // block)
    xspec = pl.BlockSpec((None, block, D_MODEL), lambda b, s: (b, s, 0))
    out_shape = (
        jax.ShapeDtypeStruct((nb, seq, D_MODEL), F32),
        jax.ShapeDtypeStruct((nb, CONV_W - 1, 2 * D_FF), F32),
    )
    out_specs = (xspec, pl.BlockSpec((None, CONV_W - 1, 2 * D_FF), lambda b, s: (b, 0, 0)))
    in_specs = [xspec] + [_resident(a, layer) for a in (g2, wup, cw, cb, wdn)] + [_resident(gf)]
    return pl.pallas_call(
        functools.partial(_ffn_kernel, block=block, final=final),
        grid=grid, in_specs=in_specs, out_specs=out_specs, out_shape=out_shape,
        scratch_shapes=[pltpu.VMEM((V7X_SUBLANES, 2 * D_FF), F32),
                        pltpu.VMEM((block, D_FF), BF16),
                        pltpu.VMEM((D_MODEL // V7X_LANES, V7X_SUBLANES * _perm_pitch(block), V7X_LANES),
                                   F32)],
        compiler_params=pltpu.CompilerParams(
            dimension_semantics=("arbitrary", "arbitrary"), vmem_limit_bytes=V7X_VMEM_LIMIT),
        name="prompt_ffn",
    )(x, g2, wup, cw, cb, wdn, gf)


def _sample_proj_kernel(x_ref, g1_ref, wm_ref, wgt_ref, z_ref, gates_ref):
    xn = _rmsnorm(x_ref[...], g1_ref[...]).astype(BF16)
    z_ref[...] = _dot(xn, wm_ref[...])
    gt = _dot_nt(wgt_ref[...], xn)
    gt = jnp.concatenate([gt, jnp.zeros((V7X_LANES - gt.shape[0], gt.shape[1]), F32)], axis=0)
    g = gt.T
    gates_ref[...] = jnp.concatenate([g, pltpu.roll(g, V7X_LANES - GATE_ROWS, 1)], axis=1)


def _single_step(kernel_fn, name, out_shape, whole, layered, layer, stacked=None):
    in_specs = [_resident(a) for a in whole] + [_resident(a, layer) for a in layered]
    args = list(whole) + list(layered)
    out_specs = [pl.BlockSpec(o.shape, lambda i, nd=len(o.shape): (0,) * nd) for o in out_shape]
    aliases = {}
    if stacked is not None:
        k, prev = stacked
        nd = len(out_shape[k].shape) - 1
        out_specs[k] = pl.BlockSpec((None,) + out_shape[k].shape[1:], lambda i: (layer,) + (0,) * nd)
        if prev is not None:
            in_specs.append(pl.BlockSpec(memory_space=pl.ANY))
            aliases = {len(args): k}
            args.append(prev)
    return pl.pallas_call(
        kernel_fn,
        grid=(1,),
        in_specs=in_specs, out_specs=tuple(out_specs), out_shape=out_shape,
        input_output_aliases=aliases,
        compiler_params=pltpu.CompilerParams(
            dimension_semantics=("arbitrary",), vmem_limit_bytes=V7X_VMEM_LIMIT),
        name=name,
    )(*args)


def _sample_proj(x, layer, g1, wm, wgt):
    n = x.shape[0]
    assert n == V7X_LANES, "the gate transpose assumes one lane tile of sample rows"
    out_shape = (jax.ShapeDtypeStruct((n, P_MAIN), F32), jax.ShapeDtypeStruct((n, 2 * V7X_LANES), F32))
    return _single_step(_sample_proj_kernel, "sample_proj", out_shape, (x,), (g1, wm, wgt), layer)


def _sample_state_kernel(q_ref, k_ref, v_ref, gates_ref, gbias_ref, m_ref, n_ref, c_ref, *rest):
    h_ref, co_ref, no_ref, mo_ref = rest[-4:]
    tb = SAMPLE_BLOCK
    ig = gates_ref[:, 0:V7X_LANES] + gbias_ref[:, 0:V7X_LANES]
    lf = _log_sigmoid(gates_ref[:, V7X_LANES:] + gbias_ref[:, V7X_LANES:])
    inter = lf + m_ref[...]
    m_t = jnp.maximum(inter, ig)
    d_in = jnp.exp(ig - m_t)
    w_inter = jnp.exp(inter - m_t)
    floor = jnp.exp(-m_t)
    mo_ref[...] = m_t

    row8 = lax.broadcasted_iota(jnp.int32, (CHUNK, DK), 0)
    for h in range(N_HEADS):
        q8 = q_ref[:, h * DK:(h + 1) * DK]
        k8 = k_ref[:, h * DK:(h + 1) * DK] * K_SCALE
        v8 = v_ref[:, h * DV:(h + 1) * DV]
        d_h = d_in[:, h:h + 1]
        w_h = w_inter[:, h:h + 1]
        n8 = n_ref[:, h * DK:(h + 1) * DK]
        s = jnp.sum(q8 * k8, axis=-1, keepdims=True) * d_h
        den = s + w_h * jnp.sum(q8 * n8, axis=-1, keepdims=True)
        qb = q8.astype(BF16)
        inter_rows = [_dot_nt(qb, c_ref[j, h].astype(BF16))[j:j + 1] for j in range(tb)]
        num = s * v8 + w_h * jnp.concatenate(inter_rows, axis=0)
        h_ref[:, h * DV:(h + 1) * DV] = num / jnp.maximum(jnp.abs(den), floor[:, h:h + 1])
        no_ref[:, h * DK:(h + 1) * DK] = w_h * n8 + d_h * k8

        vt = jnp.concatenate([d_h * v8, jnp.zeros((CHUNK - tb, DV), F32)], axis=0).T.astype(BF16)
        kpad = jnp.concatenate([k8, jnp.zeros((CHUNK - tb, DK), F32)], axis=0)
        for j in range(tb):
            kj = jnp.where(row8 == j, kpad, 0.0).astype(BF16)
            co_ref[j, h] = w_inter[j:j + 1, h:h + 1] * c_ref[j, h] + _dot(vt, kj)


def _sample_state(z, gates, gbias, m_pad, n_state, c_state, layer, c_stack):
    n = z.shape[0]
    tb = SAMPLE_BLOCK
    c_block = pl.BlockSpec((None, tb, N_HEADS, DV, DK), lambda i: (layer, i, 0, 0, 0))
    in_specs = [
        pl.BlockSpec((tb, N_HEADS * DK), lambda i: (i, OFF_Q // (N_HEADS * DK))),
        pl.BlockSpec((tb, N_HEADS * DK), lambda i: (i, OFF_K // (N_HEADS * DK))),
        pl.BlockSpec((tb, N_HEADS * DV), lambda i: (i, OFF_V // (N_HEADS * DV))),
        pl.BlockSpec((tb, 2 * V7X_LANES), lambda i: (i, 0)),
        pl.BlockSpec((1, 2 * V7X_LANES), lambda i: (0, 0)),
        pl.BlockSpec((tb, V7X_LANES), lambda i: (i, 0)),
        pl.BlockSpec((tb, N_HEADS * DK), lambda i: (i, 0)),
        c_block,
    ]
    args = [z, z, z, gates, gbias, m_pad, n_state, c_state]
    aliases = {}
    if c_stack is not None:
        in_specs.append(pl.BlockSpec(memory_space=pl.ANY))
        aliases = {len(args): 1}
        args.append(c_stack)
    out_shape = (
        jax.ShapeDtypeStruct((n, N_HEADS * DV), F32),
        jax.ShapeDtypeStruct(c_state.shape, F32),
        jax.ShapeDtypeStruct((n, N_HEADS * DK), F32),
        jax.ShapeDtypeStruct((n, V7X_LANES), F32),
    )
    out_specs = (
        pl.BlockSpec((tb, N_HEADS * DV), lambda i: (i, 0)),
        c_block,
        pl.BlockSpec((tb, N_HEADS * DK), lambda i: (i, 0)),
        pl.BlockSpec((tb, V7X_LANES), lambda i: (i, 0)),
    )
    return pl.pallas_call(
        _sample_state_kernel,
        grid=(n // tb,), in_specs=in_specs, out_specs=out_specs, out_shape=out_shape,
        input_output_aliases=aliases,
        compiler_params=pltpu.CompilerParams(
            dimension_semantics=("arbitrary",), vmem_limit_bytes=V7X_VMEM_LIMIT),
        name="sample_state",
    )(*args)


def _sample_mixer_kernel(x_ref, z_ref, h_ref, gv_ref, ws0_ref, bs0_ref, wa_ref, wb_ref, wo_ref,
                         xo_ref, vn_ref):
    y_a = _dot((jax.nn.sigmoid(z_ref[:, OFF_O:OFF_U]) * h_ref[...]).astype(BF16), wa_ref[...])
    u = jax.nn.gelu(z_ref[:, OFF_U:OFF_VB])
    vn = _rmsnorm(jax.nn.gelu(z_ref[:, OFF_VB:OFF_GA]), gv_ref[...])
    vn_ref[...] = vn
    mixed = ws0_ref[...] * vn + bs0_ref[...]
    y_b = _dot((u * mixed).astype(BF16), wb_ref[...])
    merged = (jax.nn.sigmoid(z_ref[:, OFF_GA:OFF_GB]) * y_a
              + jax.nn.sigmoid(z_ref[:, OFF_GB:P_MAIN]) * y_b)
    xo_ref[...] = x_ref[...] + _dot(merged.astype(BF16), wo_ref[...])


def _sample_mixer(x, z, h, layer, gv, ws0, bs0, wa, wb, wo):
    n = x.shape[0]
    out_shape = (jax.ShapeDtypeStruct((n, D_MODEL), F32), jax.ShapeDtypeStruct((n, D_B), F32))
    return _single_step(_sample_mixer_kernel, "sample_mixer", out_shape,
                        (x, z, h), (gv, ws0, bs0, wa, wb, wo), layer)


def _sample_ffn_kernel(x_ref, gf_ref, buf_ref, g2_ref, wup_ref, cw_ref, cb_ref, wdn_ref, *rest, final):
    xo_ref, nbuf_ref = rest[-2:]
    x = x_ref[...]
    xn = _rmsnorm(x, g2_ref[...]).astype(BF16)
    acc = jnp.zeros(x.shape, F32)
    for j in range(D_FF // FFN_COL_TILE):
        halves = []
        for half in range(2):
            col0 = half * D_FF + j * FFN_COL_TILE
            cols = slice(col0, col0 + FFN_COL_TILE)
            up = _dot(xn, wup_ref[:, cols])
            b0 = buf_ref[:, 0, cols]
            b1 = buf_ref[:, 1, cols]
            nbuf_ref[:, 0, cols] = b1
            nbuf_ref[:, 1, cols] = up
            halves.append(cb_ref[:, cols] + cw_ref[0:1, cols] * b0 + cw_ref[1:2, cols] * b1
                          + cw_ref[2:3, cols] * up)
        act = (jax.nn.silu(halves[0]) * halves[1]).astype(BF16)
        acc = acc + _dot(act, wdn_ref[j * FFN_COL_TILE:(j + 1) * FFN_COL_TILE, :])
    y = x + acc
    if final:
        y = _rmsnorm(y, gf_ref[...])
    xo_ref[...] = y


def _sample_ffn(x, gf, layer, buf, g2, wup, cw, cb, wdn, final, conv_stack):
    n = x.shape[0]
    out_shape = (jax.ShapeDtypeStruct((n, D_MODEL), F32), jax.ShapeDtypeStruct(buf.shape, F32))
    return _single_step(functools.partial(_sample_ffn_kernel, final=final), "sample_ffn", out_shape,
                        (x, gf), (buf, g2, wup, cw, cb, wdn), layer, stacked=(1, conv_stack))


def _pack_kernel(wt_ref, o_ref, wgt_ref):
    g8 = wt_ref[GATE_LO:GATE_HI, :]
    head = lax.broadcasted_iota(jnp.int32, g8.shape, 0) < N_HEADS
    wgt_ref[...] = jnp.concatenate(
        [jnp.where(head, g8, 0.0), jnp.where(head, pltpu.roll(g8, N_HEADS, 0), 0.0)], axis=0).astype(BF16)
    for j in range(P_MAIN // PACK_TILE):
        dst = j * PACK_TILE
        src = dst if dst < GATE_LO else dst + (GATE_HI - GATE_LO)
        o_ref[:, dst:dst + PACK_TILE] = wt_ref[src:src + PACK_TILE, :].T.astype(BF16)


def _pack_w_in(w_in_t):
    depth, p_in, d = w_in_t.shape
    return pl.pallas_call(
        _pack_kernel,
        grid=(depth, d // V7X_LANES),
        in_specs=[pl.BlockSpec((None, p_in, V7X_LANES), lambda l, r: (l, 0, r))],
        out_specs=(pl.BlockSpec((None, V7X_LANES, P_MAIN), lambda l, r: (l, r, 0)),
                   pl.BlockSpec((None, 2 * GATE_ROWS, V7X_LANES), lambda l, r: (l, 0, r))),
        out_shape=(jax.ShapeDtypeStruct((depth, d, P_MAIN), BF16),
                   jax.ShapeDtypeStruct((depth, 2 * GATE_ROWS, d), BF16)),
        compiler_params=pltpu.CompilerParams(
            dimension_semantics=("arbitrary", "arbitrary"), vmem_limit_bytes=V7X_VMEM_LIMIT),
        name="pack_w_in",
    )(w_in_t)


def kernel(x_prompt, x_sample, state_mlstm_C, state_mlstm_n, state_mlstm_m, state_ffn_conv, w_in, b_igate, b_fgate, g_norm1, g_vnorm, w_spatial, b_spatial, w_branch_a, w_branch_b, w_out, g_norm2, w_up, conv_w, conv_b, w_down, g_final):
    depth = w_in.shape[0]
    n_dec = x_sample.shape[0]
    xp = x_prompt
    xs = x_sample.reshape(n_dec, D_MODEL)
    gf = g_final.reshape(1, D_MODEL)

    wm, wgt = _pack_w_in(jnp.swapaxes(w_in, 1, 2))
    head_pad = ((0, 0), (0, GATE_ROWS - N_HEADS))
    gbias_col = jnp.concatenate([jnp.pad(b_igate, head_pad), jnp.pad(b_fgate, head_pad)],
                                axis=1).reshape(depth, 2 * GATE_ROWS, 1)
    head_lanes = ((0, 0), (0, V7X_LANES - N_HEADS))
    gbias_row = jnp.concatenate([jnp.pad(b_igate, head_lanes), jnp.pad(b_fgate, head_lanes)],
                                axis=1).reshape(depth, 1, 2 * V7X_LANES)
    g1 = g_norm1.reshape(depth, 1, D_MODEL)
    gv = g_vnorm.reshape(depth, 1, D_B)
    g2 = g_norm2.reshape(depth, 1, D_MODEL)
    bst = jnp.swapaxes(b_spatial, 1, 2)
    ws0 = jnp.repeat(w_spatial[:, :, 0, 0], DG, axis=1).reshape(depth, 1, D_B)
    bs0 = jnp.repeat(b_spatial[:, :, 0], DG, axis=1).reshape(depth, 1, D_B)
    wa = w_branch_a.astype(BF16)
    wb = w_branch_b.astype(BF16)
    wo = w_out.astype(BF16)
    wup = w_up.astype(BF16)
    wdn = w_down.astype(BF16)
    cb = conv_b.reshape(depth, 1, 2 * D_FF)
    n_state = state_mlstm_n.reshape(depth, n_dec, N_HEADS * DK)
    m_pad = jnp.pad(state_mlstm_m, ((0, 0), (0, 0), (0, V7X_LANES - N_HEADS)))

    small = [[] for _ in range(7)]
    c_stack = None
    conv_stack = None
    for l in range(depth):
        final = l == depth - 1

        xp, c_p, n_p, m_p = _prompt_mixer(xp, l, wm, wgt, gbias_col, g1, gv, w_spatial, bst, wa, wb, wo)
        xp, conv_p = _prompt_ffn(xp, l, g2, wup, conv_w, cb, wdn, gf, final)

        z, gates = _sample_proj(xs, l, g1, wm, wgt)
        h, c_stack, n_s, m_s = _sample_state(
            z, gates, gbias_row[l], m_pad[l], n_state[l], state_mlstm_C, l, c_stack)
        xs, vn_s = _sample_mixer(xs, z, h, l, gv, ws0, bs0, wa, wb, wo)
        xs, conv_stack = _sample_ffn(xs, gf, l, state_ffn_conv, g2, wup, conv_w, cb, wdn, final, conv_stack)

        for lst, val in zip(small, (c_p, n_p, m_p[:, :N_HEADS, 0], conv_p,
                                    n_s.reshape(n_dec, N_HEADS, DK), m_s[:, :N_HEADS],
                                    vn_s.reshape(n_dec, 1, D_B))):
            lst.append(val)
    st = [jnp.stack(o) for o in small]
    return (xp, xs.reshape(n_dec, 1, D_MODEL), st[0], st[1], st[2], st[3], c_stack, st[4], st[5], conv_stack, st[6])
```

```python
import functools

import jax
import jax.numpy as jnp
from jax import lax
from jax.experimental import pallas as pl
from jax.experimental.pallas import tpu as pltpu

D_MODEL = 1024
N_HEADS = 4
DK = 128
DV = 256
CHUNK = 128
D_B = 1024
N_GROUPS = 4
DG = D_B // N_GROUPS
D_FF = 2816
CONV_W = 3
EPS = 1e-6
K_SCALE = DK ** -0.5

OFF_Q = 0
OFF_K = OFF_Q + N_HEADS * DK
OFF_V = OFF_K + N_HEADS * DK
OFF_O = OFF_V + N_HEADS * DV
OFF_U = OFF_O + N_HEADS * DV
OFF_VB = OFF_U + D_B
OFF_GA = OFF_VB + D_B
OFF_GB = OFF_GA + D_MODEL
P_MAIN = OFF_GB + D_MODEL
GATE_LO = 2 * N_HEADS * DK + 2 * N_HEADS * DV
GATE_HI = GATE_LO + 2 * N_HEADS

V7X_LANES = 128
V7X_SUBLANES = 8
GATE_ROWS = V7X_SUBLANES
V7X_VMEM_LIMIT = 56 * 1024 * 1024
FFN_COL_TILE = 256
PROJ_TILE = 256
MIXER_BLOCK = 512
FFN_BLOCK = 512
SAMPLE_BLOCK = 8
PACK_TILE = 256

F32 = jnp.float32
BF16 = jnp.bfloat16


def _dot(a, b):
    return jnp.dot(a, b, preferred_element_type=F32)


def _dot_nt(a, b):
    return lax.dot_general(a, b, (((1,), (1,)), ((), ())), preferred_element_type=F32)


def _rmsnorm(x, g):
    r = lax.rsqrt(jnp.mean(x * x, axis=-1, keepdims=True) + EPS)
    return x * r * g


def _log_sigmoid(x):
    return jnp.minimum(x, 0.0) - jnp.log1p(jnp.exp(-jnp.abs(x)))


def _scan_lanes(x, op, fill):
    lane = lax.broadcasted_iota(jnp.int32, x.shape, 1)
    k = 1
    while k < x.shape[1]:
        shifted = pltpu.roll(x, k, 1)
        x = op(x, jnp.where(lane >= k, shifted, fill))
        k *= 2
    return x


def _anchor_zero(x):
    sub, lanes = V7X_SUBLANES, V7X_LANES
    acc = jnp.zeros((sub, lanes), jnp.uint32)
    for r in range(x.shape[0] // sub):
        for c in range(x.shape[1] // lanes):
            piece = pltpu.bitcast(x[r * sub:(r + 1) * sub, c * lanes:(c + 1) * lanes], jnp.uint32)
            acc = acc | ((piece >> 16) >> 16)
    return pltpu.bitcast(acc, F32)


def _next_block_map(nb, steps):
    def index_map(b, s):
        nxt = jnp.minimum(b * steps + s + 1, nb * steps - 1)
        return (nxt // steps, nxt % steps, 0)
    return index_map


def _mixer_kernel(x_ref, wm_ref, wgt_ref, gbias_ref, g1_ref, gv_ref, ws_ref, bst_ref,
                  wa_ref, wb_ref, wo_ref,
                  xo_ref, c_ref, n_ref, m_ref,
                  ct_s, n_s, m_s, q_s, k_s, v_s, h_s, so_s, u_s, vb_s, sg_s, um_s, *, block):
    s = pl.program_id(1)
    n_chunks = block // CHUNK

    @pl.when(s == 0)
    def _():
        ct_s[...] = jnp.zeros_like(ct_s)
        n_s[...] = jnp.zeros_like(n_s)
        m_s[...] = jnp.zeros_like(m_s)

    xn = _rmsnorm(x_ref[...], g1_ref[...]).astype(BF16)

    gates = _dot_nt(wgt_ref[...], xn) + gbias_ref[...]

    def proj(off, t):
        return _dot(xn, wm_ref[:, off + t * PROJ_TILE:off + (t + 1) * PROJ_TILE])

    def tile(t):
        return slice(t * PROJ_TILE, (t + 1) * PROJ_TILE)

    for t in range(N_HEADS * DK // PROJ_TILE):
        q_s[:, tile(t)] = proj(OFF_Q, t)
        k_s[:, tile(t)] = proj(OFF_K, t) * K_SCALE
    for t in range(N_HEADS * DV // PROJ_TILE):
        v_s[:, tile(t)] = proj(OFF_V, t)

    sumsq = [jnp.zeros((block, 1), F32)]

    def vb_tile(t):
        g = jax.nn.gelu(proj(OFF_VB, t))
        vb_s[:, tile(t)] = g
        sumsq[0] = sumsq[0] + jnp.sum(g * g, axis=-1, keepdims=True)

    def u_tile(t):
        u_s[:, tile(t)] = jax.nn.gelu(proj(OFF_U, t))

    def o_tile(t):
        so_s[:, tile(t)] = jax.nn.sigmoid(proj(OFF_O, t))

    def ga_tile(t):
        sg_s[:, tile(t)] = jax.nn.sigmoid(proj(OFF_GA, t))

    def gb_tile(t):
        sg_s[:, D_MODEL + t * PROJ_TILE:D_MODEL + (t + 1) * PROJ_TILE] = jax.nn.sigmoid(proj(OFF_GB, t))

    jobs = [(f, t) for f in (vb_tile, u_tile, o_tile, ga_tile, gb_tile) for t in range(D_MODEL // PROJ_TILE)]

    def run_jobs(count):
        for _ in range(min(count, len(jobs))):
            f, t = jobs.pop(0)
            f(t)

    row_i = lax.broadcasted_iota(jnp.int32, (CHUNK, CHUNK), 0)
    col_i = lax.broadcasted_iota(jnp.int32, (CHUNK, CHUNK), 1)
    causal = row_i >= col_i
    heads = range(N_HEADS)

    for c in range(n_chunks):
        r0 = c * CHUNK
        ig = gates[0:GATE_ROWS, r0:r0 + CHUNK]
        lf = _log_sigmoid(gates[GATE_ROWS:2 * GATE_ROWS, r0:r0 + CHUNK])
        b = _scan_lanes(lf, jnp.add, 0.0)
        a = ig - b
        m_prev = m_s[...]
        gmax = jnp.maximum(m_prev, _scan_lanes(a, jnp.maximum, -jnp.inf))
        m_t = b + gmax
        w_inter = jnp.exp(m_prev - gmax)
        g_last = gmax[:, CHUNK - 1:CHUNK]
        w_last = jnp.exp(a - g_last)
        floor = jnp.exp(-m_t)
        decay = w_inter[:, CHUNK - 1:CHUNK]
        m_s[...] = jnp.broadcast_to(m_t[:, CHUNK - 1:CHUNK], m_s.shape)

        rows = jnp.concatenate(
            [gmax, w_inter, w_last, floor,
             jnp.zeros((CHUNK - 4 * GATE_ROWS, CHUNK), F32)], axis=0)
        cols = rows.T

        def col(kind, h):
            return cols[:, kind * GATE_ROWS + h:kind * GATE_ROWS + h + 1]

        qf = [q_s[r0:r0 + CHUNK, h * DK:(h + 1) * DK] for h in heads]
        kf = [k_s[r0:r0 + CHUNK, h * DK:(h + 1) * DK] for h in heads]
        vf = [v_s[r0:r0 + CHUNK, h * DV:(h + 1) * DV] for h in heads]
        qb = [x.astype(BF16) for x in qf]
        kt = [x.T.astype(BF16) for x in kf]
        run_jobs(2)
        zero = jnp.zeros((DK, CHUNK), BF16)
        sc = []
        for h in range(0, N_HEADS, 2):
            kk = jnp.concatenate([jnp.concatenate([kt[h], zero], axis=1),
                                  jnp.concatenate([zero, kt[h + 1]], axis=1)], axis=0)
            pair = _dot(jnp.concatenate([qb[h], qb[h + 1]], axis=1), kk)
            sc += [pair[:, :CHUNK], pair[:, CHUNK:]]
        dmat = [jnp.where(causal, jnp.exp(a[h:h + 1, :] - col(0, h)), 0.0) for h in heads]
        run_jobs(1)
        sd = [sc[h] * dmat[h] for h in heads]
        ct = [ct_s[h] for h in heads]
        num = [_dot(jnp.concatenate([sd[h], col(1, h) * qf[h]], axis=1).astype(BF16),
                    jnp.concatenate([vf[h], ct[h]], axis=0).astype(BF16)) for h in heads]
        run_jobs(2)
        for h in heads:
            nh = n_s[h:h + 1, :]
            den = (jnp.sum(sd[h], axis=-1, keepdims=True)
                   + col(1, h) * jnp.sum(qf[h] * nh, axis=-1, keepdims=True))
            h_s[r0:r0 + CHUNK, h * DV:(h + 1) * DV] = num[h] * (1.0 / jnp.maximum(jnp.abs(den), col(3, h)))
            dec = decay[h:h + 1, :]
            n_s[h:h + 1, :] = dec * nh + jnp.sum(col(2, h) * kf[h], axis=0, keepdims=True)
        run_jobs(1)
        for h in heads:
            ct_s[h] = decay[h:h + 1, :] * ct[h] + _dot(kt[h], (col(2, h) * vf[h]).astype(BF16))
    run_jobs(len(jobs))

    y_a = _dot((so_s[...] * h_s[...]).astype(BF16), wa_ref[...])

    rinv = lax.rsqrt(sumsq[0] * (1.0 / D_B) + EPS)
    for g in range(N_GROUPS):
        gcols = slice(g * DG, (g + 1) * DG)
        w_tri = jnp.where(causal, ws_ref[g], 0.0).astype(BF16)
        bias_c = bst_ref[:, g:g + 1]
        for c in range(n_chunks):
            rws = slice(c * CHUNK, (c + 1) * CHUNK)
            vn = vb_s[rws, gcols] * rinv[rws] * gv_ref[:, gcols]
            um_s[rws, gcols] = u_s[rws, gcols] * (_dot(w_tri, vn.astype(BF16)) + bias_c)
    y_b = _dot(um_s[...].astype(BF16), wb_ref[...])

    merged = sg_s[:, 0:D_MODEL] * y_a + sg_s[:, D_MODEL:2 * D_MODEL] * y_b
    xo_ref[...] = x_ref[...] + _dot(merged.astype(BF16), wo_ref[...])

    @pl.when(s == pl.num_programs(1) - 1)
    def _():
        for h in range(N_HEADS):
            c_ref[h] = ct_s[h].T
        n_ref[...] = n_s[...]
        m_ref[...] = m_s[...]


def _resident(arr, layer=None):
    if layer is None:
        nd = arr.ndim
        return pl.BlockSpec(arr.shape, lambda *_: (0,) * nd, pipeline_mode=pl.Buffered(1))
    nd = arr.ndim - 1
    return pl.BlockSpec((None,) + arr.shape[1:], lambda *_: (layer,) + (0,) * nd,
                        pipeline_mode=pl.Buffered(1))


def _prompt_mixer(x, layer, wm, wgt, gbias, g1, gv, ws, bst, wa, wb, wo):
    nb, seq, _ = x.shape
    block = MIXER_BLOCK
    grid = (nb, seq // block)
    xspec = pl.BlockSpec((None, block, D_MODEL), lambda b, s: (b, s, 0))
    out_shape = (
        jax.ShapeDtypeStruct((nb, seq, D_MODEL), F32),
        jax.ShapeDtypeStruct((nb, N_HEADS, DV, DK), F32),
        jax.ShapeDtypeStruct((nb, N_HEADS, DK), F32),
        jax.ShapeDtypeStruct((nb, GATE_ROWS, V7X_LANES), F32),
    )
    out_specs = (
        xspec,
        pl.BlockSpec((None, N_HEADS, DV, DK), lambda b, s: (b, 0, 0, 0)),
        pl.BlockSpec((None, N_HEADS, DK), lambda b, s: (b, 0, 0)),
        pl.BlockSpec((None, GATE_ROWS, V7X_LANES), lambda b, s: (b, 0, 0)),
    )
    in_specs = [xspec] + [_resident(a, layer) for a in (wm, wgt, gbias, g1, gv, ws, bst, wa, wb, wo)]
    scratch = [
        pltpu.VMEM((N_HEADS, DK, DV), F32),
        pltpu.VMEM((N_HEADS, DK), F32),
        pltpu.VMEM((GATE_ROWS, V7X_LANES), F32),
        pltpu.VMEM((block, N_HEADS * DK), F32),
        pltpu.VMEM((block, N_HEADS * DK), F32),
        pltpu.VMEM((block, N_HEADS * DV), F32),
        pltpu.VMEM((block, N_HEADS * DV), F32),
        pltpu.VMEM((block, N_HEADS * DV), F32),
        pltpu.VMEM((block, D_B), F32),
        pltpu.VMEM((block, D_B), F32),
        pltpu.VMEM((block, 2 * D_MODEL), F32),
        pltpu.VMEM((block, D_B), F32),
    ]
    return pl.pallas_call(
        functools.partial(_mixer_kernel, block=block),
        grid=grid, in_specs=in_specs, out_specs=out_specs, out_shape=out_shape,
        scratch_shapes=scratch,
        compiler_params=pltpu.CompilerParams(
            dimension_semantics=("arbitrary", "arbitrary"), vmem_limit_bytes=V7X_VMEM_LIMIT),
        name="prompt_mixer",
    )(x, wm, wgt, gbias, g1, gv, ws, bst, wa, wb, wo)


def _conv_taps(up, carry_s, cw_ref, cb_ref, cols):
    sub = V7X_SUBLANES
    rows = up.shape[0]
    last1 = up[rows - sub:rows]
    last2 = up[rows - 2 * sub:rows - sub]
    first = lax.broadcasted_iota(jnp.int32, last1.shape, 0) == 0
    back1 = jnp.where(first, carry_s[sub - 1:sub, cols], pltpu.roll(last1, 1, 0))
    back2 = jnp.where(first, carry_s[sub - 2:sub - 1, cols], pltpu.roll(last2, 1, 0))
    carry_s[sub - 2:sub - 1, cols] = last2[sub - 1:sub]
    carry_s[sub - 1:sub, cols] = last1[sub - 1:sub]
    m1 = jnp.concatenate([back1, up[0:rows - sub]], axis=0)
    m2 = jnp.concatenate([back2, back1, up[0:rows - 2 * sub]], axis=0)
    return (cb_ref[:, cols] + cw_ref[0:1, cols] * m2 + cw_ref[1:2, cols] * m1
            + cw_ref[2:3, cols] * up)


def _perm_pitch(block):
    return block // V7X_SUBLANES + V7X_SUBLANES


def _ffn_kernel(x_ref, xnext_ref, g2_ref, wup_ref, cw_ref, cb_ref, wdn_ref, gf_ref,
                xo_ref, conv_ref, carry_s, act_s, perm_s, unperm_s, xn_s, *, block, final):
    s = pl.program_id(1)
    sub, lanes = V7X_SUBLANES, V7X_LANES
    groups = block // sub
    chunks = D_MODEL // lanes
    pitch = _perm_pitch(block)

    @pl.when(s == 0)
    def _():
        carry_s[...] = jnp.zeros_like(carry_s)

    def stage(src_ref):
        for c in range(chunks):
            for i in range(sub):
                perm_s[c, i * pitch:i * pitch + groups] = src_ref[i * groups:(i + 1) * groups,
                                                                  c * lanes:(c + 1) * lanes]
        x = jnp.concatenate(
            [jnp.concatenate([perm_s[c, pl.ds(r, sub, stride=pitch), :] for c in range(chunks)], axis=1)
             for r in range(groups)], axis=0)
        xn = _rmsnorm(x, g2_ref[...])
        xn_s[...] = xn.astype(BF16)
        return xn

    @pl.when((pl.program_id(0) == 0) & (s == 0))
    def _():
        stage(x_ref)

    xn = xn_s[...]
    for j in range(D_FF // FFN_COL_TILE):
        halves = []
        for half in range(2):
            cols = slice(half * D_FF + j * FFN_COL_TILE, half * D_FF + (j + 1) * FFN_COL_TILE)
            up = _dot(xn, wup_ref[:, cols])
            halves.append(_conv_taps(up, carry_s, cw_ref, cb_ref, cols))
        act_s[:, j * FFN_COL_TILE:(j + 1) * FFN_COL_TILE] = (jax.nn.silu(halves[0]) * halves[1]).astype(BF16)
    staged = stage(xnext_ref)
    down = _dot(act_s[...], wdn_ref[...])
    anchor = _anchor_zero(staged)
    for r in range(groups):
        for c in range(chunks):
            tile_rc = down[r * sub:(r + 1) * sub, c * lanes:(c + 1) * lanes]
            if r == 0 and c == 0:
                tile_rc = tile_rc + anchor
            unperm_s[c, pl.ds(r, sub, stride=pitch), :] = tile_rc
    for i in range(sub):
        rows = slice(i * groups, (i + 1) * groups)
        y = x_ref[rows, :] + jnp.concatenate(
            [unperm_s[c, i * pitch:i * pitch + groups] for c in range(chunks)], axis=1)
        if final:
            y = _rmsnorm(y, gf_ref[...])
        xo_ref[rows, :] = y

    @pl.when(s == pl.num_programs(1) - 1)
    def _():
        conv_ref[...] = carry_s[V7X_SUBLANES - (CONV_W - 1):V7X_SUBLANES, :]


def _prompt_ffn(x, layer, g2, wup, cw, cb, wdn, gf, final):
    nb, seq, _ = x.shape
    block = FFN_BLOCK
    steps = seq // block
    grid = (nb, steps)
    xspec = pl.BlockSpec((None, block, D_MODEL), lambda b, s: (b, s, 0))
    out_shape = (
        jax.ShapeDtypeStruct((nb, seq, D_MODEL), F32),
        jax.ShapeDtypeStruct((nb, CONV_W - 1, 2 * D_FF), F32),
    )
    out_specs = (xspec, pl.BlockSpec((None, CONV_W - 1, 2 * D_FF), lambda b, s: (b, 0, 0)))
    in_specs = ([xspec, pl.BlockSpec((None, block, D_MODEL), _next_block_map(nb, steps))]
                + [_resident(a, layer) for a in (g2, wup, cw, cb, wdn)] + [_resident(gf)])
    reorder = pltpu.VMEM((D_MODEL // V7X_LANES, V7X_SUBLANES * _perm_pitch(block), V7X_LANES), F32)
    return pl.pallas_call(
        functools.partial(_ffn_kernel, block=block, final=final),
        grid=grid, in_specs=in_specs, out_specs=out_specs, out_shape=out_shape,
        scratch_shapes=[pltpu.VMEM((V7X_SUBLANES, 2 * D_FF), F32),
                        pltpu.VMEM((block, D_FF), BF16),
                        reorder, reorder,
                        pltpu.VMEM((block, D_MODEL), BF16)],
        compiler_params=pltpu.CompilerParams(
            dimension_semantics=("arbitrary", "arbitrary"), vmem_limit_bytes=V7X_VMEM_LIMIT),
        name="prompt_ffn",
    )(x, x, g2, wup, cw, cb, wdn, gf)


def _sample_proj_kernel(x_ref, g1_ref, wm_ref, wgt_ref, z_ref, gates_ref):
    xn = _rmsnorm(x_ref[...], g1_ref[...]).astype(BF16)
    z_ref[...] = _dot(xn, wm_ref[...])
    gt = _dot_nt(wgt_ref[...], xn)
    gt = jnp.concatenate([gt, jnp.zeros((V7X_LANES - gt.shape[0], gt.shape[1]), F32)], axis=0)
    g = gt.T
    gates_ref[...] = jnp.concatenate([g, pltpu.roll(g, V7X_LANES - GATE_ROWS, 1)], axis=1)


def _single_step(kernel_fn, name, out_shape, whole, layered, layer, stacked=None):
    in_specs = [_resident(a) for a in whole] + [_resident(a, layer) for a in layered]
    args = list(whole) + list(layered)
    out_specs = [pl.BlockSpec(o.shape, lambda i, nd=len(o.shape): (0,) * nd) for o in out_shape]
    aliases = {}
    if stacked is not None:
        k, prev = stacked
        nd = len(out_shape[k].shape) - 1
        out_specs[k] = pl.BlockSpec((None,) + out_shape[k].shape[1:], lambda i: (layer,) + (0,) * nd)
        if prev is not None:
            in_specs.append(pl.BlockSpec(memory_space=pl.ANY))
            aliases = {len(args): k}
            args.append(prev)
    return pl.pallas_call(
        kernel_fn,
        grid=(1,),
        in_specs=in_specs, out_specs=tuple(out_specs), out_shape=out_shape,
        input_output_aliases=aliases,
        compiler_params=pltpu.CompilerParams(
            dimension_semantics=("arbitrary",), vmem_limit_bytes=V7X_VMEM_LIMIT),
        name=name,
    )(*args)


def _sample_proj(x, layer, g1, wm, wgt):
    n = x.shape[0]
    assert n == V7X_LANES, "the gate transpose assumes one lane tile of sample rows"
    out_shape = (jax.ShapeDtypeStruct((n, P_MAIN), F32), jax.ShapeDtypeStruct((n, 2 * V7X_LANES), F32))
    return _single_step(_sample_proj_kernel, "sample_proj", out_shape, (x,), (g1, wm, wgt), layer)


def _sample_state_kernel(q_ref, k_ref, v_ref, gates_ref, gbias_ref, m_ref, n_ref, c_ref, *rest):
    h_ref, co_ref, no_ref, mo_ref = rest[-4:]
    tb = SAMPLE_BLOCK
    ig = gates_ref[:, 0:V7X_LANES] + gbias_ref[:, 0:V7X_LANES]
    lf = _log_sigmoid(gates_ref[:, V7X_LANES:] + gbias_ref[:, V7X_LANES:])
    inter = lf + m_ref[...]
    m_t = jnp.maximum(inter, ig)
    d_in = jnp.exp(ig - m_t)
    w_inter = jnp.exp(inter - m_t)
    floor = jnp.exp(-m_t)
    mo_ref[...] = m_t

    row8 = lax.broadcasted_iota(jnp.int32, (CHUNK, DK), 0)
    for h in range(N_HEADS):
        q8 = q_ref[:, h * DK:(h + 1) * DK]
        k8 = k_ref[:, h * DK:(h + 1) * DK] * K_SCALE
        v8 = v_ref[:, h * DV:(h + 1) * DV]
        d_h = d_in[:, h:h + 1]
        w_h = w_inter[:, h:h + 1]
        n8 = n_ref[:, h * DK:(h + 1) * DK]
        s = jnp.sum(q8 * k8, axis=-1, keepdims=True) * d_h
        den = s + w_h * jnp.sum(q8 * n8, axis=-1, keepdims=True)
        qb = q8.astype(BF16)
        inter_rows = [_dot_nt(qb, c_ref[j, h].astype(BF16))[j:j + 1] for j in range(tb)]
        num = s * v8 + w_h * jnp.concatenate(inter_rows, axis=0)
        h_ref[:, h * DV:(h + 1) * DV] = num / jnp.maximum(jnp.abs(den), floor[:, h:h + 1])
        no_ref[:, h * DK:(h + 1) * DK] = w_h * n8 + d_h * k8

        vt = jnp.concatenate([d_h * v8, jnp.zeros((CHUNK - tb, DV), F32)], axis=0).T.astype(BF16)
        kpad = jnp.concatenate([k8, jnp.zeros((CHUNK - tb, DK), F32)], axis=0)
        for j in range(tb):
            kj = jnp.where(row8 == j, kpad, 0.0).astype(BF16)
            co_ref[j, h] = w_inter[j:j + 1, h:h + 1] * c_ref[j, h] + _dot(vt, kj)


def _sample_state(z, gates, gbias, m_pad, n_state, c_state, layer, c_stack):
    n = z.shape[0]
    tb = SAMPLE_BLOCK
    c_block = pl.BlockSpec((None, tb, N_HEADS, DV, DK), lambda i: (layer, i, 0, 0, 0))
    in_specs = [
        pl.BlockSpec((tb, N_HEADS * DK), lambda i: (i, OFF_Q // (N_HEADS * DK))),
        pl.BlockSpec((tb, N_HEADS * DK), lambda i: (i, OFF_K // (N_HEADS * DK))),
        pl.BlockSpec((tb, N_HEADS * DV), lambda i: (i, OFF_V // (N_HEADS * DV))),
        pl.BlockSpec((tb, 2 * V7X_LANES), lambda i: (i, 0)),
        pl.BlockSpec((1, 2 * V7X_LANES), lambda i: (0, 0)),
        pl.BlockSpec((tb, V7X_LANES), lambda i: (i, 0)),
        pl.BlockSpec((tb, N_HEADS * DK), lambda i: (i, 0)),
        c_block,
    ]
    args = [z, z, z, gates, gbias, m_pad, n_state, c_state]
    aliases = {}
    if c_stack is not None:
        in_specs.append(pl.BlockSpec(memory_space=pl.ANY))
        aliases = {len(args): 1}
        args.append(c_stack)
    out_shape = (
        jax.ShapeDtypeStruct((n, N_HEADS * DV), F32),
        jax.ShapeDtypeStruct(c_state.shape, F32),
        jax.ShapeDtypeStruct((n, N_HEADS * DK), F32),
        jax.ShapeDtypeStruct((n, V7X_LANES), F32),
    )
    out_specs = (
        pl.BlockSpec((tb, N_HEADS * DV), lambda i: (i, 0)),
        c_block,
        pl.BlockSpec((tb, N_HEADS * DK), lambda i: (i, 0)),
        pl.BlockSpec((tb, V7X_LANES), lambda i: (i, 0)),
    )
    return pl.pallas_call(
        _sample_state_kernel,
        grid=(n // tb,), in_specs=in_specs, out_specs=out_specs, out_shape=out_shape,
        input_output_aliases=aliases,
        compiler_params=pltpu.CompilerParams(
            dimension_semantics=("arbitrary",), vmem_limit_bytes=V7X_VMEM_LIMIT),
        name="sample_state",
    )(*args)


def _sample_mixer_kernel(x_ref, z_ref, h_ref, gv_ref, ws0_ref, bs0_ref, wa_ref, wb_ref, wo_ref,
                         xo_ref, vn_ref):
    y_a = _dot((jax.nn.sigmoid(z_ref[:, OFF_O:OFF_U]) * h_ref[...]).astype(BF16), wa_ref[...])
    u = jax.nn.gelu(z_ref[:, OFF_U:OFF_VB])
    vn = _rmsnorm(jax.nn.gelu(z_ref[:, OFF_VB:OFF_GA]), gv_ref[...])
    vn_ref[...] = vn
    mixed = ws0_ref[...] * vn + bs0_ref[...]
    y_b = _dot((u * mixed).astype(BF16), wb_ref[...])
    merged = (jax.nn.sigmoid(z_ref[:, OFF_GA:OFF_GB]) * y_a
              + jax.nn.sigmoid(z_ref[:, OFF_GB:P_MAIN]) * y_b)
    xo_ref[...] = x_ref[...] + _dot(merged.astype(BF16), wo_ref[...])


def _sample_mixer(x, z, h, layer, gv, ws0, bs0, wa, wb, wo):
    n = x.shape[0]
    out_shape = (jax.ShapeDtypeStruct((n, D_MODEL), F32), jax.ShapeDtypeStruct((n, D_B), F32))
    return _single_step(_sample_mixer_kernel, "sample_mixer", out_shape,
                        (x, z, h), (gv, ws0, bs0, wa, wb, wo), layer)


def _sample_ffn_kernel(x_ref, gf_ref, buf_ref, g2_ref, wup_ref, cw_ref, cb_ref, wdn_ref, *rest, final):
    xo_ref, nbuf_ref = rest[-2:]
    x = x_ref[...]
    xn = _rmsnorm(x, g2_ref[...]).astype(BF16)
    acc = jnp.zeros(x.shape, F32)
    for j in range(D_FF // FFN_COL_TILE):
        halves = []
        for half in range(2):
            col0 = half * D_FF + j * FFN_COL_TILE
            cols = slice(col0, col0 + FFN_COL_TILE)
            up = _dot(xn, wup_ref[:, cols])
            b0 = buf_ref[:, 0, cols]
            b1 = buf_ref[:, 1, cols]
            nbuf_ref[:, 0, cols] = b1
            nbuf_ref[:, 1, cols] = up
            halves.append(cb_ref[:, cols] + cw_ref[0:1, cols] * b0 + cw_ref[1:2, cols] * b1
                          + cw_ref[2:3, cols] * up)
        act = (jax.nn.silu(halves[0]) * halves[1]).astype(BF16)
        acc = acc + _dot(act, wdn_ref[j * FFN_COL_TILE:(j + 1) * FFN_COL_TILE, :])
    y = x + acc
    if final:
        y = _rmsnorm(y, gf_ref[...])
    xo_ref[...] = y


def _sample_ffn(x, gf, layer, buf, g2, wup, cw, cb, wdn, final, conv_stack):
    n = x.shape[0]
    out_shape = (jax.ShapeDtypeStruct((n, D_MODEL), F32), jax.ShapeDtypeStruct(buf.shape, F32))
    return _single_step(functools.partial(_sample_ffn_kernel, final=final), "sample_ffn", out_shape,
                        (x, gf), (buf, g2, wup, cw, cb, wdn), layer, stacked=(1, conv_stack))


def _pack_kernel(wt_ref, o_ref, wgt_ref):
    g8 = wt_ref[GATE_LO:GATE_HI, :]
    head = lax.broadcasted_iota(jnp.int32, g8.shape, 0) < N_HEADS
    wgt_ref[...] = jnp.concatenate(
        [jnp.where(head, g8, 0.0), jnp.where(head, pltpu.roll(g8, N_HEADS, 0), 0.0)], axis=0).astype(BF16)
    for j in range(P_MAIN // PACK_TILE):
        dst = j * PACK_TILE
        src = dst if dst < GATE_LO else dst + (GATE_HI - GATE_LO)
        o_ref[:, dst:dst + PACK_TILE] = wt_ref[src:src + PACK_TILE, :].T.astype(BF16)


def _pack_w_in(w_in_t):
    depth, p_in, d = w_in_t.shape
    return pl.pallas_call(
        _pack_kernel,
        grid=(depth, d // V7X_LANES),
        in_specs=[pl.BlockSpec((None, p_in, V7X_LANES), lambda l, r: (l, 0, r))],
        out_specs=(pl.BlockSpec((None, V7X_LANES, P_MAIN), lambda l, r: (l, r, 0)),
                   pl.BlockSpec((None, 2 * GATE_ROWS, V7X_LANES), lambda l, r: (l, 0, r))),
        out_shape=(jax.ShapeDtypeStruct((depth, d, P_MAIN), BF16),
                   jax.ShapeDtypeStruct((depth, 2 * GATE_ROWS, d), BF16)),
        compiler_params=pltpu.CompilerParams(
            dimension_semantics=("arbitrary", "arbitrary"), vmem_limit_bytes=V7X_VMEM_LIMIT),
        name="pack_w_in",
    )(w_in_t)


def kernel(x_prompt, x_sample, state_mlstm_C, state_mlstm_n, state_mlstm_m, state_ffn_conv, w_in, b_igate, b_fgate, g_norm1, g_vnorm, w_spatial, b_spatial, w_branch_a, w_branch_b, w_out, g_norm2, w_up, conv_w, conv_b, w_down, g_final):
    depth = w_in.shape[0]
    n_dec = x_sample.shape[0]
    xp = x_prompt
    xs = x_sample.reshape(n_dec, D_MODEL)
    gf = g_final.reshape(1, D_MODEL)

    wm, wgt = _pack_w_in(jnp.swapaxes(w_in, 1, 2))
    head_pad = ((0, 0), (0, GATE_ROWS - N_HEADS))
    gbias_col = jnp.concatenate([jnp.pad(b_igate, head_pad), jnp.pad(b_fgate, head_pad)],
                                axis=1).reshape(depth, 2 * GATE_ROWS, 1)
    head_lanes = ((0, 0), (0, V7X_LANES - N_HEADS))
    gbias_row = jnp.concatenate([jnp.pad(b_igate, head_lanes), jnp.pad(b_fgate, head_lanes)],
                                axis=1).reshape(depth, 1, 2 * V7X_LANES)
    g1 = g_norm1.reshape(depth, 1, D_MODEL)
    gv = g_vnorm.reshape(depth, 1, D_B)
    g2 = g_norm2.reshape(depth, 1, D_MODEL)
    bst = jnp.swapaxes(b_spatial, 1, 2)
    ws0 = jnp.repeat(w_spatial[:, :, 0, 0], DG, axis=1).reshape(depth, 1, D_B)
    bs0 = jnp.repeat(b_spatial[:, :, 0], DG, axis=1).reshape(depth, 1, D_B)
    wa = w_branch_a.astype(BF16)
    wb = w_branch_b.astype(BF16)
    wo = w_out.astype(BF16)
    wup = w_up.astype(BF16)
    wdn = w_down.astype(BF16)
    cb = conv_b.reshape(depth, 1, 2 * D_FF)
    n_state = state_mlstm_n.reshape(depth, n_dec, N_HEADS * DK)
    m_pad = jnp.pad(state_mlstm_m, ((0, 0), (0, 0), (0, V7X_LANES - N_HEADS)))

    small = [[] for _ in range(7)]
    c_stack = None
    conv_stack = None
    for l in range(depth):
        final = l == depth - 1

        xp, c_p, n_p, m_p = _prompt_mixer(xp, l, wm, wgt, gbias_col, g1, gv, w_spatial, bst, wa, wb, wo)
        xp, conv_p = _prompt_ffn(xp, l, g2, wup, conv_w, cb, wdn, gf, final)

        z, gates = _sample_proj(xs, l, g1, wm, wgt)
        h, c_stack, n_s, m_s = _sample_state(
            z, gates, gbias_row[l], m_pad[l], n_state[l], state_mlstm_C, l, c_stack)
        xs, vn_s = _sample_mixer(xs, z, h, l, gv, ws0, bs0, wa, wb, wo)
        xs, conv_stack = _sample_ffn(xs, gf, l, state_ffn_conv, g2, wup, conv_w, cb, wdn, final, conv_stack)

        for lst, val in zip(small, (c_p, n_p, m_p[:, :N_HEADS, 0], conv_p,
                                    n_s.reshape(n_dec, N_HEADS, DK), m_s[:, :N_HEADS],
                                    vn_s.reshape(n_dec, 1, D_B))):
            lst.append(val)
    st = [jnp.stack(o) for o in small]
    return (xp, xs.reshape(n_dec, 1, D_MODEL), st[0], st[1], st[2], st[3], c_stack, st[4], st[5], conv_stack, st[6])
```

```python
import functools

import jax
import jax.numpy as jnp
from jax import lax
from jax.experimental import pallas as pl
from jax.experimental.pallas import tpu as pltpu

D_MODEL = 1024
N_HEADS = 4
DK = 128
DV = 256
CHUNK = 128
D_B = 1024
N_GROUPS = 4
DG = D_B // N_GROUPS
D_FF = 2816
CONV_W = 3
EPS = 1e-6
K_SCALE = DK ** -0.5

OFF_Q = 0
OFF_K = OFF_Q + N_HEADS * DK
OFF_V = OFF_K + N_HEADS * DK
OFF_O = OFF_V + N_HEADS * DV
OFF_U = OFF_O + N_HEADS * DV
OFF_VB = OFF_U + D_B
OFF_GA = OFF_VB + D_B
OFF_GB = OFF_GA + D_MODEL
P_MAIN = OFF_GB + D_MODEL
GATE_LO = 2 * N_HEADS * DK + 2 * N_HEADS * DV
GATE_HI = GATE_LO + 2 * N_HEADS

V7X_LANES = 128
V7X_SUBLANES = 8
GATE_ROWS = V7X_SUBLANES
V7X_VMEM_LIMIT = 56 * 1024 * 1024
FFN_COL_TILE = 256
PROJ_TILE = 256
MIXER_BLOCK = 512
FFN_BLOCK = 512
SAMPLE_BLOCK = 8
PACK_TILE = 256

F32 = jnp.float32
BF16 = jnp.bfloat16


def _dot(a, b):
    return jnp.dot(a, b, preferred_element_type=F32)


def _dot_nt(a, b):
    return lax.dot_general(a, b, (((1,), (1,)), ((), ())), preferred_element_type=F32)


def _rmsnorm(x, g):
    r = lax.rsqrt(jnp.mean(x * x, axis=-1, keepdims=True) + EPS)
    return x * r * g


def _log_sigmoid(x):
    return jnp.minimum(x, 0.0) - jnp.log1p(jnp.exp(-jnp.abs(x)))


def _scan_lanes(x, op, fill):
    lane = lax.broadcasted_iota(jnp.int32, x.shape, 1)
    k = 1
    while k < x.shape[1]:
        shifted = pltpu.roll(x, k, 1)
        x = op(x, jnp.where(lane >= k, shifted, fill))
        k *= 2
    return x


def _anchor_zero(x):
    sub, lanes = V7X_SUBLANES, V7X_LANES
    acc = jnp.zeros((sub, lanes), jnp.uint32)
    for r in range(x.shape[0] // sub):
        for c in range(x.shape[1] // lanes):
            piece = pltpu.bitcast(x[r * sub:(r + 1) * sub, c * lanes:(c + 1) * lanes], jnp.uint32)
            acc = acc | ((piece >> 16) >> 16)
    return pltpu.bitcast(acc, F32)


def _next_block_map(nb, steps):
    def index_map(b, s):
        nxt = jnp.minimum(b * steps + s + 1, nb * steps - 1)
        return (nxt // steps, nxt % steps, 0)
    return index_map


def _mixer_kernel(x_ref, xnext_ref, wm_ref, wgt_ref, gbias_ref, g1_ref, gv_ref, ws_ref, bst_ref,
                  wa_ref, wb_ref, wo_ref,
                  xo_ref, c_ref, n_ref, m_ref,
                  ct_s, n_s, m_s, xn_s, xn_next_s, q_s, k_s, v_s, h_s, so_s, u_s, vb_s, sg_s, um_s, *, block):
    s = pl.program_id(1)
    n_chunks = block // CHUNK

    @pl.when(s == 0)
    def _():
        ct_s[...] = jnp.zeros_like(ct_s)
        n_s[...] = jnp.zeros_like(n_s)
        m_s[...] = jnp.zeros_like(m_s)

    @pl.when((pl.program_id(0) == 0) & (s == 0))
    def _():
        xn_s[...] = _rmsnorm(x_ref[...], g1_ref[...]).astype(BF16)

    xn = xn_s[...]
    staged = _rmsnorm(xnext_ref[...], g1_ref[...])
    xn_next_s[...] = staged.astype(BF16)

    gates = _dot_nt(wgt_ref[...], xn) + gbias_ref[...]

    def proj(off, t):
        return _dot(xn, wm_ref[:, off + t * PROJ_TILE:off + (t + 1) * PROJ_TILE])

    def tile(t):
        return slice(t * PROJ_TILE, (t + 1) * PROJ_TILE)

    for t in range(N_HEADS * DK // PROJ_TILE):
        q_s[:, tile(t)] = proj(OFF_Q, t)
        k_s[:, tile(t)] = proj(OFF_K, t) * K_SCALE
    v_tiles = N_HEADS * DV // PROJ_TILE
    for t in range(v_tiles):
        v_t = proj(OFF_V, t)
        if t == v_tiles - 1:
            first = (slice(0, V7X_SUBLANES), slice(0, V7X_LANES))
            v_s[:, tile(t)] = v_t
            v_s[first[0], t * PROJ_TILE:t * PROJ_TILE + V7X_LANES] = v_t[first] + _anchor_zero(staged)
        else:
            v_s[:, tile(t)] = v_t

    sumsq = [jnp.zeros((block, 1), F32)]

    def vb_tile(t):
        g = jax.nn.gelu(proj(OFF_VB, t))
        vb_s[:, tile(t)] = g
        sumsq[0] = sumsq[0] + jnp.sum(g * g, axis=-1, keepdims=True)

    def u_tile(t):
        u_s[:, tile(t)] = jax.nn.gelu(proj(OFF_U, t))

    def o_tile(t):
        so_s[:, tile(t)] = jax.nn.sigmoid(proj(OFF_O, t))

    def ga_tile(t):
        sg_s[:, tile(t)] = jax.nn.sigmoid(proj(OFF_GA, t))

    def gb_tile(t):
        sg_s[:, D_MODEL + t * PROJ_TILE:D_MODEL + (t + 1) * PROJ_TILE] = jax.nn.sigmoid(proj(OFF_GB, t))

    jobs = [(f, t) for f in (vb_tile, u_tile, o_tile, ga_tile, gb_tile) for t in range(D_MODEL // PROJ_TILE)]

    def run_jobs(count):
        for _ in range(min(count, len(jobs))):
            f, t = jobs.pop(0)
            f(t)

    row_i = lax.broadcasted_iota(jnp.int32, (CHUNK, CHUNK), 0)
    col_i = lax.broadcasted_iota(jnp.int32, (CHUNK, CHUNK), 1)
    causal = row_i >= col_i
    heads = range(N_HEADS)

    for c in range(n_chunks):
        r0 = c * CHUNK
        ig = gates[0:GATE_ROWS, r0:r0 + CHUNK]
        lf = _log_sigmoid(gates[GATE_ROWS:2 * GATE_ROWS, r0:r0 + CHUNK])
        b = _scan_lanes(lf, jnp.add, 0.0)
        a = ig - b
        m_prev = m_s[...]
        gmax = jnp.maximum(m_prev, _scan_lanes(a, jnp.maximum, -jnp.inf))
        m_t = b + gmax
        w_inter = jnp.exp(m_prev - gmax)
        g_last = gmax[:, CHUNK - 1:CHUNK]
        w_last = jnp.exp(a - g_last)
        floor = jnp.exp(-m_t)
        decay = w_inter[:, CHUNK - 1:CHUNK]
        m_s[...] = jnp.broadcast_to(m_t[:, CHUNK - 1:CHUNK], m_s.shape)

        rows = jnp.concatenate(
            [gmax, w_inter, w_last, floor,
             jnp.zeros((CHUNK - 4 * GATE_ROWS, CHUNK), F32)], axis=0)
        cols = rows.T

        def col(kind, h):
            return cols[:, kind * GATE_ROWS + h:kind * GATE_ROWS + h + 1]

        qf = [q_s[r0:r0 + CHUNK, h * DK:(h + 1) * DK] for h in heads]
        kf = [k_s[r0:r0 + CHUNK, h * DK:(h + 1) * DK] for h in heads]
        vf = [v_s[r0:r0 + CHUNK, h * DV:(h + 1) * DV] for h in heads]
        qb = [x.astype(BF16) for x in qf]
        kt = [x.T.astype(BF16) for x in kf]
        run_jobs(2)
        zero = jnp.zeros((DK, CHUNK), BF16)
        sc = []
        for h in range(0, N_HEADS, 2):
            kk = jnp.concatenate([jnp.concatenate([kt[h], zero], axis=1),
                                  jnp.concatenate([zero, kt[h + 1]], axis=1)], axis=0)
            pair = _dot(jnp.concatenate([qb[h], qb[h + 1]], axis=1), kk)
            sc += [pair[:, :CHUNK], pair[:, CHUNK:]]
        dmat = [jnp.where(causal, jnp.exp(a[h:h + 1, :] - col(0, h)), 0.0) for h in heads]
        run_jobs(1)
        sd = [sc[h] * dmat[h] for h in heads]
        ct = [ct_s[h] for h in heads]
        num = [_dot(jnp.concatenate([sd[h], col(1, h) * qf[h]], axis=1).astype(BF16),
                    jnp.concatenate([vf[h], ct[h]], axis=0).astype(BF16)) for h in heads]
        run_jobs(2)
        for h in heads:
            nh = n_s[h:h + 1, :]
            den = (jnp.sum(sd[h], axis=-1, keepdims=True)
                   + col(1, h) * jnp.sum(qf[h] * nh, axis=-1, keepdims=True))
            h_s[r0:r0 + CHUNK, h * DV:(h + 1) * DV] = num[h] * (1.0 / jnp.maximum(jnp.abs(den), col(3, h)))
            dec = decay[h:h + 1, :]
            n_s[h:h + 1, :] = dec * nh + jnp.sum(col(2, h) * kf[h], axis=0, keepdims=True)
        run_jobs(1)
        for h in heads:
            ct_s[h] = decay[h:h + 1, :] * ct[h] + _dot(kt[h], (col(2, h) * vf[h]).astype(BF16))
    run_jobs(len(jobs))

    y_a = _dot((so_s[...] * h_s[...]).astype(BF16), wa_ref[...])

    rinv = lax.rsqrt(sumsq[0] * (1.0 / D_B) + EPS)
    for g in range(N_GROUPS):
        gcols = slice(g * DG, (g + 1) * DG)
        w_tri = jnp.where(causal, ws_ref[g], 0.0).astype(BF16)
        bias_c = bst_ref[:, g:g + 1]
        for c in range(n_chunks):
            rws = slice(c * CHUNK, (c + 1) * CHUNK)
            vn = vb_s[rws, gcols] * rinv[rws] * gv_ref[:, gcols]
            um_s[rws, gcols] = u_s[rws, gcols] * (_dot(w_tri, vn.astype(BF16)) + bias_c)
    y_b = _dot(um_s[...].astype(BF16), wb_ref[...])

    merged = sg_s[:, 0:D_MODEL] * y_a + sg_s[:, D_MODEL:2 * D_MODEL] * y_b
    xo_ref[...] = x_ref[...] + _dot(merged.astype(BF16), wo_ref[...])
    xn_s[...] = xn_next_s[...]

    @pl.when(s == pl.num_programs(1) - 1)
    def _():
        for h in range(N_HEADS):
            c_ref[h] = ct_s[h].T
        n_ref[...] = n_s[...]
        m_ref[...] = m_s[...]


def _resident(arr, layer=None):
    if layer is None:
        nd = arr.ndim
        return pl.BlockSpec(arr.shape, lambda *_: (0,) * nd, pipeline_mode=pl.Buffered(1))
    nd = arr.ndim - 1
    return pl.BlockSpec((None,) + arr.shape[1:], lambda *_: (layer,) + (0,) * nd,
                        pipeline_mode=pl.Buffered(1))


def _prompt_mixer(x, layer, wm, wgt, gbias, g1, gv, ws, bst, wa, wb, wo):
    nb, seq, _ = x.shape
    block = MIXER_BLOCK
    steps = seq // block
    grid = (nb, steps)
    xspec = pl.BlockSpec((None, block, D_MODEL), lambda b, s: (b, s, 0))
    xnext_spec = pl.BlockSpec((None, block, D_MODEL), _next_block_map(nb, steps))
    out_shape = (
        jax.ShapeDtypeStruct((nb, seq, D_MODEL), F32),
        jax.ShapeDtypeStruct((nb, N_HEADS, DV, DK), F32),
        jax.ShapeDtypeStruct((nb, N_HEADS, DK), F32),
        jax.ShapeDtypeStruct((nb, GATE_ROWS, V7X_LANES), F32),
    )
    out_specs = (
        xspec,
        pl.BlockSpec((None, N_HEADS, DV, DK), lambda b, s: (b, 0, 0, 0)),
        pl.BlockSpec((None, N_HEADS, DK), lambda b, s: (b, 0, 0)),
        pl.BlockSpec((None, GATE_ROWS, V7X_LANES), lambda b, s: (b, 0, 0)),
    )
    in_specs = [xspec, xnext_spec] + [_resident(a, layer) for a in (wm, wgt, gbias, g1, gv, ws, bst, wa, wb, wo)]
    scratch = [
        pltpu.VMEM((N_HEADS, DK, DV), F32),
        pltpu.VMEM((N_HEADS, DK), F32),
        pltpu.VMEM((GATE_ROWS, V7X_LANES), F32),
        pltpu.VMEM((block, D_MODEL), BF16),
        pltpu.VMEM((block, D_MODEL), BF16),
        pltpu.VMEM((block, N_HEADS * DK), F32),
        pltpu.VMEM((block, N_HEADS * DK), F32),
        pltpu.VMEM((block, N_HEADS * DV), F32),
        pltpu.VMEM((block, N_HEADS * DV), F32),
        pltpu.VMEM((block, N_HEADS * DV), F32),
        pltpu.VMEM((block, D_B), F32),
        pltpu.VMEM((block, D_B), F32),
        pltpu.VMEM((block, 2 * D_MODEL), F32),
        pltpu.VMEM((block, D_B), F32),
    ]
    return pl.pallas_call(
        functools.partial(_mixer_kernel, block=block),
        grid=grid, in_specs=in_specs, out_specs=out_specs, out_shape=out_shape,
        scratch_shapes=scratch,
        compiler_params=pltpu.CompilerParams(
            dimension_semantics=("arbitrary", "arbitrary"), vmem_limit_bytes=V7X_VMEM_LIMIT),
        name="prompt_mixer",
    )(x, x, wm, wgt, gbias, g1, gv, ws, bst, wa, wb, wo)


def _conv_taps(up, carry_s, cw_ref, cb_ref, cols):
    sub = V7X_SUBLANES
    rows = up.shape[0]
    last1 = up[rows - sub:rows]
    last2 = up[rows - 2 * sub:rows - sub]
    first = lax.broadcasted_iota(jnp.int32, last1.shape, 0) == 0
    back1 = jnp.where(first, carry_s[sub - 1:sub, cols], pltpu.roll(last1, 1, 0))
    back2 = jnp.where(first, carry_s[sub - 2:sub - 1, cols], pltpu.roll(last2, 1, 0))
    carry_s[sub - 2:sub - 1, cols] = last2[sub - 1:sub]
    carry_s[sub - 1:sub, cols] = last1[sub - 1:sub]
    m1 = jnp.concatenate([back1, up[0:rows - sub]], axis=0)
    m2 = jnp.concatenate([back2, back1, up[0:rows - 2 * sub]], axis=0)
    return (cb_ref[:, cols] + cw_ref[0:1, cols] * m2 + cw_ref[1:2, cols] * m1
            + cw_ref[2:3, cols] * up)


def _perm_pitch(block):
    return block // V7X_SUBLANES + V7X_SUBLANES


def _ffn_kernel(x_ref, xnext_ref, g2_ref, wup_ref, cw_ref, cb_ref, wdn_ref, gf_ref,
                xo_ref, conv_ref, carry_s, act_s, perm_s, unperm_s, xn_s, *, block, final):
    s = pl.program_id(1)
    sub, lanes = V7X_SUBLANES, V7X_LANES
    groups = block // sub
    chunks = D_MODEL // lanes
    pitch = _perm_pitch(block)

    @pl.when(s == 0)
    def _():
        carry_s[...] = jnp.zeros_like(carry_s)

    def stage(src_ref):
        for c in range(chunks):
            for i in range(sub):
                perm_s[c, i * pitch:i * pitch + groups] = src_ref[i * groups:(i + 1) * groups,
                                                                  c * lanes:(c + 1) * lanes]
        x = jnp.concatenate(
            [jnp.concatenate([perm_s[c, pl.ds(r, sub, stride=pitch), :] for c in range(chunks)], axis=1)
             for r in range(groups)], axis=0)
        xn = _rmsnorm(x, g2_ref[...])
        xn_s[...] = xn.astype(BF16)
        return xn

    @pl.when((pl.program_id(0) == 0) & (s == 0))
    def _():
        stage(x_ref)

    xn = xn_s[...]
    for j in range(D_FF // FFN_COL_TILE):
        halves = []
        for half in range(2):
            cols = slice(half * D_FF + j * FFN_COL_TILE, half * D_FF + (j + 1) * FFN_COL_TILE)
            up = _dot(xn, wup_ref[:, cols])
            halves.append(_conv_taps(up, carry_s, cw_ref, cb_ref, cols))
        act_s[:, j * FFN_COL_TILE:(j + 1) * FFN_COL_TILE] = (jax.nn.silu(halves[0]) * halves[1]).astype(BF16)
    staged = stage(xnext_ref)
    down = _dot(act_s[...], wdn_ref[...])
    anchor = _anchor_zero(staged)
    for r in range(groups):
        for c in range(chunks):
            tile_rc = down[r * sub:(r + 1) * sub, c * lanes:(c + 1) * lanes]
            if r == 0 and c == 0:
                tile_rc = tile_rc + anchor
            unperm_s[c, pl.ds(r, sub, stride=pitch), :] = tile_rc
    for i in range(sub):
        rows = slice(i * groups, (i + 1) * groups)
        y = x_ref[rows, :] + jnp.concatenate(
            [unperm_s[c, i * pitch:i * pitch + groups] for c in range(chunks)], axis=1)
        if final:
            y = _rmsnorm(y, gf_ref[...])
        xo_ref[rows, :] = y

    @pl.when(s == pl.num_programs(1) - 1)
    def _():
        conv_ref[...] = carry_s[V7X_SUBLANES - (CONV_W - 1):V7X_SUBLANES, :]


def _prompt_ffn(x, layer, g2, wup, cw, cb, wdn, gf, final):
    nb, seq, _ = x.shape
    block = FFN_BLOCK
    steps = seq // block
    grid = (nb, steps)
    xspec = pl.BlockSpec((None, block, D_MODEL), lambda b, s: (b, s, 0))
    out_shape = (
        jax.ShapeDtypeStruct((nb, seq, D_MODEL), F32),
        jax.ShapeDtypeStruct((nb, CONV_W - 1, 2 * D_FF), F32),
    )
    out_specs = (xspec, pl.BlockSpec((None, CONV_W - 1, 2 * D_FF), lambda b, s: (b, 0, 0)))
    in_specs = ([xspec, pl.BlockSpec((None, block, D_MODEL), _next_block_map(nb, steps))]
                + [_resident(a, layer) for a in (g2, wup, cw, cb, wdn)] + [_resident(gf)])
    reorder = pltpu.VMEM((D_MODEL // V7X_LANES, V7X_SUBLANES * _perm_pitch(block), V7X_LANES), F32)
    return pl.pallas_call(
        functools.partial(_ffn_kernel, block=block, final=final),
        grid=grid, in_specs=in_specs, out_specs=out_specs, out_shape=out_shape,
        scratch_shapes=[pltpu.VMEM((V7X_SUBLANES, 2 * D_FF), F32),
                        pltpu.VMEM((block, D_FF), BF16),
                        reorder, reorder,
                        pltpu.VMEM((block, D_MODEL), BF16)],
        compiler_params=pltpu.CompilerParams(
            dimension_semantics=("arbitrary", "arbitrary"), vmem_limit_bytes=V7X_VMEM_LIMIT),
        name="prompt_ffn",
    )(x, x, g2, wup, cw, cb, wdn, gf)


def _sample_proj_kernel(x_ref, g1_ref, wm_ref, wgt_ref, z_ref, gates_ref):
    xn = _rmsnorm(x_ref[...], g1_ref[...]).astype(BF16)
    z_ref[...] = _dot(xn, wm_ref[...])
    gt = _dot_nt(wgt_ref[...], xn)
    gt = jnp.concatenate([gt, jnp.zeros((V7X_LANES - gt.shape[0], gt.shape[1]), F32)], axis=0)
    g = gt.T
    gates_ref[...] = jnp.concatenate([g, pltpu.roll(g, V7X_LANES - GATE_ROWS, 1)], axis=1)


def _single_step(kernel_fn, name, out_shape, whole, layered, layer, stacked=None):
    in_specs = [_resident(a) for a in whole] + [_resident(a, layer) for a in layered]
    args = list(whole) + list(layered)
    out_specs = [pl.BlockSpec(o.shape, lambda i, nd=len(o.shape): (0,) * nd) for o in out_shape]
    aliases = {}
    if stacked is not None:
        k, prev = stacked
        nd = len(out_shape[k].shape) - 1
        out_specs[k] = pl.BlockSpec((None,) + out_shape[k].shape[1:], lambda i: (layer,) + (0,) * nd)
        if prev is not None:
            in_specs.append(pl.BlockSpec(memory_space=pl.ANY))
            aliases = {len(args): k}
            args.append(prev)
    return pl.pallas_call(
        kernel_fn,
        grid=(1,),
        in_specs=in_specs, out_specs=tuple(out_specs), out_shape=out_shape,
        input_output_aliases=aliases,
        compiler_params=pltpu.CompilerParams(
            dimension_semantics=("arbitrary",), vmem_limit_bytes=V7X_VMEM_LIMIT),
        name=name,
    )(*args)


def _sample_proj(x, layer, g1, wm, wgt):
    n = x.shape[0]
    assert n == V7X_LANES, "the gate transpose assumes one lane tile of sample rows"
    out_shape = (jax.ShapeDtypeStruct((n, P_MAIN), F32), jax.ShapeDtypeStruct((n, 2 * V7X_LANES), F32))
    return _single_step(_sample_proj_kernel, "sample_proj", out_shape, (x,), (g1, wm, wgt), layer)


def _sample_state_kernel(q_ref, k_ref, v_ref, gates_ref, gbias_ref, m_ref, n_ref, c_ref, *rest):
    h_ref, co_ref, no_ref, mo_ref = rest[-4:]
    tb = SAMPLE_BLOCK
    ig = gates_ref[:, 0:V7X_LANES] + gbias_ref[:, 0:V7X_LANES]
    lf = _log_sigmoid(gates_ref[:, V7X_LANES:] + gbias_ref[:, V7X_LANES:])
    inter = lf + m_ref[...]
    m_t = jnp.maximum(inter, ig)
    d_in = jnp.exp(ig - m_t)
    w_inter = jnp.exp(inter - m_t)
    floor = jnp.exp(-m_t)
    mo_ref[...] = m_t

    row8 = lax.broadcasted_iota(jnp.int32, (CHUNK, DK), 0)
    for h in range(N_HEADS):
        q8 = q_ref[:, h * DK:(h + 1) * DK]
        k8 = k_ref[:, h * DK:(h + 1) * DK] * K_SCALE
        v8 = v_ref[:, h * DV:(h + 1) * DV]
        d_h = d_in[:, h:h + 1]
        w_h = w_inter[:, h:h + 1]
        n8 = n_ref[:, h * DK:(h + 1) * DK]
        s = jnp.sum(q8 * k8, axis=-1, keepdims=True) * d_h
        den = s + w_h * jnp.sum(q8 * n8, axis=-1, keepdims=True)
        qb = q8.astype(BF16)
        inter_rows = [_dot_nt(qb, c_ref[j, h].astype(BF16))[j:j + 1] for j in range(tb)]
        num = s * v8 + w_h * jnp.concatenate(inter_rows, axis=0)
        h_ref[:, h * DV:(h + 1) * DV] = num / jnp.maximum(jnp.abs(den), floor[:, h:h + 1])
        no_ref[:, h * DK:(h + 1) * DK] = w_h * n8 + d_h * k8

        vt = jnp.concatenate([d_h * v8, jnp.zeros((CHUNK - tb, DV), F32)], axis=0).T.astype(BF16)
        kpad = jnp.concatenate([k8, jnp.zeros((CHUNK - tb, DK), F32)], axis=0)
        for j in range(tb):
            kj = jnp.where(row8 == j, kpad, 0.0).astype(BF16)
            co_ref[j, h] = w_inter[j:j + 1, h:h + 1] * c_ref[j, h] + _dot(vt, kj)


def _sample_state(z, gates, gbias, m_pad, n_state, c_state, layer, c_stack):
    n = z.shape[0]
    tb = SAMPLE_BLOCK
    c_block = pl.BlockSpec((None, tb, N_HEADS, DV, DK), lambda i: (layer, i, 0, 0, 0))
    in_specs = [
        pl.BlockSpec((tb, N_HEADS * DK), lambda i: (i, OFF_Q // (N_HEADS * DK))),
        pl.BlockSpec((tb, N_HEADS * DK), lambda i: (i, OFF_K // (N_HEADS * DK))),
        pl.BlockSpec((tb, N_HEADS * DV), lambda i: (i, OFF_V // (N_HEADS * DV))),
        pl.BlockSpec((tb, 2 * V7X_LANES), lambda i: (i, 0)),
        pl.BlockSpec((1, 2 * V7X_LANES), lambda i: (0, 0)),
        pl.BlockSpec((tb, V7X_LANES), lambda i: (i, 0)),
        pl.BlockSpec((tb, N_HEADS * DK), lambda i: (i, 0)),
        c_block,
    ]
    args = [z, z, z, gates, gbias, m_pad, n_state, c_state]
    aliases = {}
    if c_stack is not None:
        in_specs.append(pl.BlockSpec(memory_space=pl.ANY))
        aliases = {len(args): 1}
        args.append(c_stack)
    out_shape = (
        jax.ShapeDtypeStruct((n, N_HEADS * DV), F32),
        jax.ShapeDtypeStruct(c_state.shape, F32),
        jax.ShapeDtypeStruct((n, N_HEADS * DK), F32),
        jax.ShapeDtypeStruct((n, V7X_LANES), F32),
    )
    out_specs = (
        pl.BlockSpec((tb, N_HEADS * DV), lambda i: (i, 0)),
        c_block,
        pl.BlockSpec((tb, N_HEADS * DK), lambda i: (i, 0)),
        pl.BlockSpec((tb, V7X_LANES), lambda i: (i, 0)),
    )
    return pl.pallas_call(
        _sample_state_kernel,
        grid=(n // tb,), in_specs=in_specs, out_specs=out_specs, out_shape=out_shape,
        input_output_aliases=aliases,
        compiler_params=pltpu.CompilerParams(
            dimension_semantics=("arbitrary",), vmem_limit_bytes=V7X_VMEM_LIMIT),
        name="sample_state",
    )(*args)


def _sample_mixer_kernel(x_ref, z_ref, h_ref, gv_ref, ws0_ref, bs0_ref, wa_ref, wb_ref, wo_ref,
                         xo_ref, vn_ref):
    y_a = _dot((jax.nn.sigmoid(z_ref[:, OFF_O:OFF_U]) * h_ref[...]).astype(BF16), wa_ref[...])
    u = jax.nn.gelu(z_ref[:, OFF_U:OFF_VB])
    vn = _rmsnorm(jax.nn.gelu(z_ref[:, OFF_VB:OFF_GA]), gv_ref[...])
    vn_ref[...] = vn
    mixed = ws0_ref[...] * vn + bs0_ref[...]
    y_b = _dot((u * mixed).astype(BF16), wb_ref[...])
    merged = (jax.nn.sigmoid(z_ref[:, OFF_GA:OFF_GB]) * y_a
              + jax.nn.sigmoid(z_ref[:, OFF_GB:P_MAIN]) * y_b)
    xo_ref[...] = x_ref[...] + _dot(merged.astype(BF16), wo_ref[...])


def _sample_mixer(x, z, h, layer, gv, ws0, bs0, wa, wb, wo):
    n = x.shape[0]
    out_shape = (jax.ShapeDtypeStruct((n, D_MODEL), F32), jax.ShapeDtypeStruct((n, D_B), F32))
    return _single_step(_sample_mixer_kernel, "sample_mixer", out_shape,
                        (x, z, h), (gv, ws0, bs0, wa, wb, wo), layer)


def _sample_ffn_kernel(x_ref, gf_ref, buf_ref, g2_ref, wup_ref, cw_ref, cb_ref, wdn_ref, *rest, final):
    xo_ref, nbuf_ref = rest[-2:]
    x = x_ref[...]
    xn = _rmsnorm(x, g2_ref[...]).astype(BF16)
    acc = jnp.zeros(x.shape, F32)
    for j in range(D_FF // FFN_COL_TILE):
        halves = []
        for half in range(2):
            col0 = half * D_FF + j * FFN_COL_TILE
            cols = slice(col0, col0 + FFN_COL_TILE)
            up = _dot(xn, wup_ref[:, cols])
            b0 = buf_ref[:, 0, cols]
            b1 = buf_ref[:, 1, cols]
            nbuf_ref[:, 0, cols] = b1
            nbuf_ref[:, 1, cols] = up
            halves.append(cb_ref[:, cols] + cw_ref[0:1, cols] * b0 + cw_ref[1:2, cols] * b1
                          + cw_ref[2:3, cols] * up)
        act = (jax.nn.silu(halves[0]) * halves[1]).astype(BF16)
        acc = acc + _dot(act, wdn_ref[j * FFN_COL_TILE:(j + 1) * FFN_COL_TILE, :])
    y = x + acc
    if final:
        y = _rmsnorm(y, gf_ref[...])
    xo_ref[...] = y


def _sample_ffn(x, gf, layer, buf, g2, wup, cw, cb, wdn, final, conv_stack):
    n = x.shape[0]
    out_shape = (jax.ShapeDtypeStruct((n, D_MODEL), F32), jax.ShapeDtypeStruct(buf.shape, F32))
    return _single_step(functools.partial(_sample_ffn_kernel, final=final), "sample_ffn", out_shape,
                        (x, gf), (buf, g2, wup, cw, cb, wdn), layer, stacked=(1, conv_stack))


def _pack_kernel(wt_ref, o_ref, wgt_ref):
    g8 = wt_ref[GATE_LO:GATE_HI, :]
    head = lax.broadcasted_iota(jnp.int32, g8.shape, 0) < N_HEADS
    wgt_ref[...] = jnp.concatenate(
        [jnp.where(head, g8, 0.0), jnp.where(head, pltpu.roll(g8, N_HEADS, 0), 0.0)], axis=0).astype(BF16)
    for j in range(P_MAIN // PACK_TILE):
        dst = j * PACK_TILE
        src = dst if dst < GATE_LO else dst + (GATE_HI - GATE_LO)
        o_ref[:, dst:dst + PACK_TILE] = wt_ref[src:src + PACK_TILE, :].T.astype(BF16)


def _pack_w_in(w_in_t):
    depth, p_in, d = w_in_t.shape
    return pl.pallas_call(
        _pack_kernel,
        grid=(depth, d // V7X_LANES),
        in_specs=[pl.BlockSpec((None, p_in, V7X_LANES), lambda l, r: (l, 0, r))],
        out_specs=(pl.BlockSpec((None, V7X_LANES, P_MAIN), lambda l, r: (l, r, 0)),
                   pl.BlockSpec((None, 2 * GATE_ROWS, V7X_LANES), lambda l, r: (l, 0, r))),
        out_shape=(jax.ShapeDtypeStruct((depth, d, P_MAIN), BF16),
                   jax.ShapeDtypeStruct((depth, 2 * GATE_ROWS, d), BF16)),
        compiler_params=pltpu.CompilerParams(
            dimension_semantics=("arbitrary", "arbitrary"), vmem_limit_bytes=V7X_VMEM_LIMIT),
        name="pack_w_in",
    )(w_in_t)


def kernel(x_prompt, x_sample, state_mlstm_C, state_mlstm_n, state_mlstm_m, state_ffn_conv, w_in, b_igate, b_fgate, g_norm1, g_vnorm, w_spatial, b_spatial, w_branch_a, w_branch_b, w_out, g_norm2, w_up, conv_w, conv_b, w_down, g_final):
    depth = w_in.shape[0]
    n_dec = x_sample.shape[0]
    xp = x_prompt
    xs = x_sample.reshape(n_dec, D_MODEL)
    gf = g_final.reshape(1, D_MODEL)

    wm, wgt = _pack_w_in(jnp.swapaxes(w_in, 1, 2))
    head_pad = ((0, 0), (0, GATE_ROWS - N_HEADS))
    gbias_col = jnp.concatenate([jnp.pad(b_igate, head_pad), jnp.pad(b_fgate, head_pad)],
                                axis=1).reshape(depth, 2 * GATE_ROWS, 1)
    head_lanes = ((0, 0), (0, V7X_LANES - N_HEADS))
    gbias_row = jnp.concatenate([jnp.pad(b_igate, head_lanes), jnp.pad(b_fgate, head_lanes)],
                                axis=1).reshape(depth, 1, 2 * V7X_LANES)
    g1 = g_norm1.reshape(depth, 1, D_MODEL)
    gv = g_vnorm.reshape(depth, 1, D_B)
    g2 = g_norm2.reshape(depth, 1, D_MODEL)
    bst = jnp.swapaxes(b_spatial, 1, 2)
    ws0 = jnp.repeat(w_spatial[:, :, 0, 0], DG, axis=1).reshape(depth, 1, D_B)
    bs0 = jnp.repeat(b_spatial[:, :, 0], DG, axis=1).reshape(depth, 1, D_B)
    wa = w_branch_a.astype(BF16)
    wb = w_branch_b.astype(BF16)
    wo = w_out.astype(BF16)
    wup = w_up.astype(BF16)
    wdn = w_down.astype(BF16)
    cb = conv_b.reshape(depth, 1, 2 * D_FF)
    n_state = state_mlstm_n.reshape(depth, n_dec, N_HEADS * DK)
    m_pad = jnp.pad(state_mlstm_m, ((0, 0), (0, 0), (0, V7X_LANES - N_HEADS)))

    small = [[] for _ in range(7)]
    c_stack = None
    conv_stack = None
    for l in range(depth):
        final = l == depth - 1

        xp, c_p, n_p, m_p = _prompt_mixer(xp, l, wm, wgt, gbias_col, g1, gv, w_spatial, bst, wa, wb, wo)
        xp, conv_p = _prompt_ffn(xp, l, g2, wup, conv_w, cb, wdn, gf, final)

        z, gates = _sample_proj(xs, l, g1, wm, wgt)
        h, c_stack, n_s, m_s = _sample_state(
            z, gates, gbias_row[l], m_pad[l], n_state[l], state_mlstm_C, l, c_stack)
        xs, vn_s = _sample_mixer(xs, z, h, l, gv, ws0, bs0, wa, wb, wo)
        xs, conv_stack = _sample_ffn(xs, gf, l, state_ffn_conv, g2, wup, conv_w, cb, wdn, final, conv_stack)

        for lst, val in zip(small, (c_p, n_p, m_p[:, :N_HEADS, 0], conv_p,
                                    n_s.reshape(n_dec, N_HEADS, DK), m_s[:, :N_HEADS],
                                    vn_s.reshape(n_dec, 1, D_B))):
            lst.append(val)
    st = [jnp.stack(o) for o in small]
    return (xp, xs.reshape(n_dec, 1, D_MODEL), st[0], st[1], st[2], st[3], c_stack, st[4], st[5], conv_stack, st[6])
```

```python
import functools

import jax
import jax.numpy as jnp
from jax import lax
from jax.experimental import pallas as pl
from jax.experimental.pallas import tpu as pltpu

D_MODEL = 1024
N_HEADS = 4
DK = 128
DV = 256
CHUNK = 128
D_B = 1024
N_GROUPS = 4
DG = D_B // N_GROUPS
D_FF = 2816
CONV_W = 3
EPS = 1e-6
K_SCALE = DK ** -0.5

OFF_Q = 0
OFF_K = OFF_Q + N_HEADS * DK
OFF_V = OFF_K + N_HEADS * DK
OFF_O = OFF_V + N_HEADS * DV
OFF_U = OFF_O + N_HEADS * DV
OFF_VB = OFF_U + D_B
OFF_GA = OFF_VB + D_B
OFF_GB = OFF_GA + D_MODEL
P_MAIN = OFF_GB + D_MODEL
GATE_LO = 2 * N_HEADS * DK + 2 * N_HEADS * DV
GATE_HI = GATE_LO + 2 * N_HEADS

V7X_LANES = 128
V7X_SUBLANES = 8
GATE_ROWS = V7X_SUBLANES
V7X_VMEM_LIMIT = 56 * 1024 * 1024
FFN_COL_TILE = 256
PROJ_TILE = 256
MIXER_BLOCK = 512
FFN_BLOCK = 512
SAMPLE_BLOCK = 16
PACK_TILE = 256

F32 = jnp.float32
BF16 = jnp.bfloat16


def _dot(a, b):
    return jnp.dot(a, b, preferred_element_type=F32)


def _dot_nt(a, b):
    return lax.dot_general(a, b, (((1,), (1,)), ((), ())), preferred_element_type=F32)


def _rmsnorm(x, g):
    r = lax.rsqrt(jnp.mean(x * x, axis=-1, keepdims=True) + EPS)
    return x * r * g


def _log_sigmoid(x):
    return jnp.minimum(x, 0.0) - jnp.log1p(jnp.exp(-jnp.abs(x)))


def _scan_lanes(x, op, fill):
    lane = lax.broadcasted_iota(jnp.int32, x.shape, 1)
    k = 1
    while k < x.shape[1]:
        shifted = pltpu.roll(x, k, 1)
        x = op(x, jnp.where(lane >= k, shifted, fill))
        k *= 2
    return x


def _anchor_zero(x):
    sub, lanes = V7X_SUBLANES, V7X_LANES
    acc = jnp.zeros((sub, lanes), jnp.uint32)
    for r in range(x.shape[0] // sub):
        for c in range(x.shape[1] // lanes):
            piece = pltpu.bitcast(x[r * sub:(r + 1) * sub, c * lanes:(c + 1) * lanes], jnp.uint32)
            acc = acc | ((piece >> 16) >> 16)
    return pltpu.bitcast(acc, F32)


def _next_block_map(nb, steps):
    def index_map(b, s):
        nxt = jnp.minimum(b * steps + s + 1, nb * steps - 1)
        return (nxt // steps, nxt % steps, 0)
    return index_map


def _mixer_kernel(x_ref, wm_ref, wgt_ref, gbias_ref, g1_ref, gv_ref, ws_ref, bst_ref,
                  wa_ref, wb_ref, wo_ref,
                  xo_ref, c_ref, n_ref, m_ref,
                  ct_s, n_s, m_s, q_s, k_s, v_s, h_s, so_s, u_s, vb_s, sg_s, um_s, *, block):
    s = pl.program_id(1)
    n_chunks = block // CHUNK

    @pl.when(s == 0)
    def _():
        ct_s[...] = jnp.zeros_like(ct_s)
        n_s[...] = jnp.zeros_like(n_s)
        m_s[...] = jnp.zeros_like(m_s)

    xn = _rmsnorm(x_ref[...], g1_ref[...]).astype(BF16)

    gates = _dot_nt(wgt_ref[...], xn) + gbias_ref[...]

    def proj(off, t):
        return _dot(xn, wm_ref[:, off + t * PROJ_TILE:off + (t + 1) * PROJ_TILE])

    def tile(t):
        return slice(t * PROJ_TILE, (t + 1) * PROJ_TILE)

    for t in range(N_HEADS * DK // PROJ_TILE):
        q_s[:, tile(t)] = proj(OFF_Q, t)
        k_s[:, tile(t)] = proj(OFF_K, t) * K_SCALE
    for t in range(N_HEADS * DV // PROJ_TILE):
        v_s[:, tile(t)] = proj(OFF_V, t)

    sumsq = [jnp.zeros((block, 1), F32)]

    def vb_tile(t):
        g = jax.nn.gelu(proj(OFF_VB, t))
        vb_s[:, tile(t)] = g
        sumsq[0] = sumsq[0] + jnp.sum(g * g, axis=-1, keepdims=True)

    def u_tile(t):
        u_s[:, tile(t)] = jax.nn.gelu(proj(OFF_U, t))

    def o_tile(t):
        so_s[:, tile(t)] = jax.nn.sigmoid(proj(OFF_O, t))

    def ga_tile(t):
        sg_s[:, tile(t)] = jax.nn.sigmoid(proj(OFF_GA, t))

    def gb_tile(t):
        sg_s[:, D_MODEL + t * PROJ_TILE:D_MODEL + (t + 1) * PROJ_TILE] = jax.nn.sigmoid(proj(OFF_GB, t))

    jobs = [(f, t) for f in (vb_tile, u_tile, o_tile, ga_tile, gb_tile) for t in range(D_MODEL // PROJ_TILE)]

    def run_jobs(count):
        for _ in range(min(count, len(jobs))):
            f, t = jobs.pop(0)
            f(t)

    row_i = lax.broadcasted_iota(jnp.int32, (CHUNK, CHUNK), 0)
    col_i = lax.broadcasted_iota(jnp.int32, (CHUNK, CHUNK), 1)
    causal = row_i >= col_i
    heads = range(N_HEADS)

    for c in range(n_chunks):
        r0 = c * CHUNK
        ig = gates[0:GATE_ROWS, r0:r0 + CHUNK]
        lf = _log_sigmoid(gates[GATE_ROWS:2 * GATE_ROWS, r0:r0 + CHUNK])
        b = _scan_lanes(lf, jnp.add, 0.0)
        a = ig - b
        m_prev = m_s[...]
        gmax = jnp.maximum(m_prev, _scan_lanes(a, jnp.maximum, -jnp.inf))
        m_t = b + gmax
        w_inter = jnp.exp(m_prev - gmax)
        g_last = gmax[:, CHUNK - 1:CHUNK]
        w_last = jnp.exp(a - g_last)
        floor = jnp.exp(-m_t)
        decay = w_inter[:, CHUNK - 1:CHUNK]
        m_s[...] = jnp.broadcast_to(m_t[:, CHUNK - 1:CHUNK], m_s.shape)

        rows = jnp.concatenate(
            [gmax, w_inter, w_last, floor,
             jnp.zeros((CHUNK - 4 * GATE_ROWS, CHUNK), F32)], axis=0)
        cols = rows.T

        def col(kind, h):
            return cols[:, kind * GATE_ROWS + h:kind * GATE_ROWS + h + 1]

        qf = [q_s[r0:r0 + CHUNK, h * DK:(h + 1) * DK] for h in heads]
        kf = [k_s[r0:r0 + CHUNK, h * DK:(h + 1) * DK] for h in heads]
        vf = [v_s[r0:r0 + CHUNK, h * DV:(h + 1) * DV] for h in heads]
        qb = [x.astype(BF16) for x in qf]
        kt = [x.T.astype(BF16) for x in kf]
        run_jobs(2)
        zero = jnp.zeros((DK, CHUNK), BF16)
        sc = []
        for h in range(0, N_HEADS, 2):
            kk = jnp.concatenate([jnp.concatenate([kt[h], zero], axis=1),
                                  jnp.concatenate([zero, kt[h + 1]], axis=1)], axis=0)
            pair = _dot(jnp.concatenate([qb[h], qb[h + 1]], axis=1), kk)
            sc += [pair[:, :CHUNK], pair[:, CHUNK:]]
        dmat = [jnp.where(causal, jnp.exp(a[h:h + 1, :] - col(0, h)), 0.0) for h in heads]
        run_jobs(1)
        sd = [sc[h] * dmat[h] for h in heads]
        ct = [ct_s[h] for h in heads]
        num = [_dot(jnp.concatenate([sd[h], col(1, h) * qf[h]], axis=1).astype(BF16),
                    jnp.concatenate([vf[h], ct[h]], axis=0).astype(BF16)) for h in heads]
        run_jobs(2)
        for h in heads:
            nh = n_s[h:h + 1, :]
            den = (jnp.sum(sd[h], axis=-1, keepdims=True)
                   + col(1, h) * jnp.sum(qf[h] * nh, axis=-1, keepdims=True))
            h_s[r0:r0 + CHUNK, h * DV:(h + 1) * DV] = num[h] * (1.0 / jnp.maximum(jnp.abs(den), col(3, h)))
            dec = decay[h:h + 1, :]
            n_s[h:h + 1, :] = dec * nh + jnp.sum(col(2, h) * kf[h], axis=0, keepdims=True)
        run_jobs(1)
        for h in heads:
            ct_s[h] = decay[h:h + 1, :] * ct[h] + _dot(kt[h], (col(2, h) * vf[h]).astype(BF16))
    run_jobs(len(jobs))

    y_a = _dot((so_s[...] * h_s[...]).astype(BF16), wa_ref[...])

    rinv = lax.rsqrt(sumsq[0] * (1.0 / D_B) + EPS)
    for g in range(N_GROUPS):
        gcols = slice(g * DG, (g + 1) * DG)
        w_tri = jnp.where(causal, ws_ref[g], 0.0).astype(BF16)
        bias_c = bst_ref[:, g:g + 1]
        for c in range(n_chunks):
            rws = slice(c * CHUNK, (c + 1) * CHUNK)
            vn = vb_s[rws, gcols] * rinv[rws] * gv_ref[:, gcols]
            um_s[rws, gcols] = u_s[rws, gcols] * (_dot(w_tri, vn.astype(BF16)) + bias_c)
    y_b = _dot(um_s[...].astype(BF16), wb_ref[...])

    merged = sg_s[:, 0:D_MODEL] * y_a + sg_s[:, D_MODEL:2 * D_MODEL] * y_b
    xo_ref[...] = x_ref[...] + _dot(merged.astype(BF16), wo_ref[...])

    @pl.when(s == pl.num_programs(1) - 1)
    def _():
        for h in range(N_HEADS):
            c_ref[h] = ct_s[h].T
        n_ref[...] = n_s[...]
        m_ref[...] = m_s[...]


def _resident(arr, layer=None):
    if layer is None:
        nd = arr.ndim
        return pl.BlockSpec(arr.shape, lambda *_: (0,) * nd, pipeline_mode=pl.Buffered(1))
    nd = arr.ndim - 1
    return pl.BlockSpec((None,) + arr.shape[1:], lambda *_: (layer,) + (0,) * nd,
                        pipeline_mode=pl.Buffered(1))


def _prompt_mixer(x, layer, wm, wgt, gbias, g1, gv, ws, bst, wa, wb, wo):
    nb, seq, _ = x.shape
    block = MIXER_BLOCK
    grid = (nb, seq // block)
    xspec = pl.BlockSpec((None, block, D_MODEL), lambda b, s: (b, s, 0))
    out_shape = (
        jax.ShapeDtypeStruct((nb, seq, D_MODEL), F32),
        jax.ShapeDtypeStruct((nb, N_HEADS, DV, DK), F32),
        jax.ShapeDtypeStruct((nb, N_HEADS, DK), F32),
        jax.ShapeDtypeStruct((nb, GATE_ROWS, V7X_LANES), F32),
    )
    out_specs = (
        xspec,
        pl.BlockSpec((None, N_HEADS, DV, DK), lambda b, s: (b, 0, 0, 0)),
        pl.BlockSpec((None, N_HEADS, DK), lambda b, s: (b, 0, 0)),
        pl.BlockSpec((None, GATE_ROWS, V7X_LANES), lambda b, s: (b, 0, 0)),
    )
    in_specs = [xspec] + [_resident(a, layer) for a in (wm, wgt, gbias, g1, gv, ws, bst, wa, wb, wo)]
    scratch = [
        pltpu.VMEM((N_HEADS, DK, DV), F32),
        pltpu.VMEM((N_HEADS, DK), F32),
        pltpu.VMEM((GATE_ROWS, V7X_LANES), F32),
        pltpu.VMEM((block, N_HEADS * DK), F32),
        pltpu.VMEM((block, N_HEADS * DK), F32),
        pltpu.VMEM((block, N_HEADS * DV), F32),
        pltpu.VMEM((block, N_HEADS * DV), F32),
        pltpu.VMEM((block, N_HEADS * DV), F32),
        pltpu.VMEM((block, D_B), F32),
        pltpu.VMEM((block, D_B), F32),
        pltpu.VMEM((block, 2 * D_MODEL), F32),
        pltpu.VMEM((block, D_B), F32),
    ]
    return pl.pallas_call(
        functools.partial(_mixer_kernel, block=block),
        grid=grid, in_specs=in_specs, out_specs=out_specs, out_shape=out_shape,
        scratch_shapes=scratch,
        compiler_params=pltpu.CompilerParams(
            dimension_semantics=("arbitrary", "arbitrary"), vmem_limit_bytes=V7X_VMEM_LIMIT),
        name="prompt_mixer",
    )(x, wm, wgt, gbias, g1, gv, ws, bst, wa, wb, wo)


def _conv_taps(up, carry_s, cw_ref, cb_ref, cols):
    sub = V7X_SUBLANES
    rows = up.shape[0]
    last1 = up[rows - sub:rows]
    last2 = up[rows - 2 * sub:rows - sub]
    first = lax.broadcasted_iota(jnp.int32, last1.shape, 0) == 0
    back1 = jnp.where(first, carry_s[sub - 1:sub, cols], pltpu.roll(last1, 1, 0))
    back2 = jnp.where(first, carry_s[sub - 2:sub - 1, cols], pltpu.roll(last2, 1, 0))
    carry_s[sub - 2:sub - 1, cols] = last2[sub - 1:sub]
    carry_s[sub - 1:sub, cols] = last1[sub - 1:sub]
    m1 = jnp.concatenate([back1, up[0:rows - sub]], axis=0)
    m2 = jnp.concatenate([back2, back1, up[0:rows - 2 * sub]], axis=0)
    return (cb_ref[:, cols] + cw_ref[0:1, cols] * m2 + cw_ref[1:2, cols] * m1
            + cw_ref[2:3, cols] * up)


def _perm_pitch(block):
    return block // V7X_SUBLANES + V7X_SUBLANES


def _ffn_kernel(x_ref, xnext_ref, g2_ref, wup_ref, cw_ref, cb_ref, wdn_ref, gf_ref,
                xo_ref, conv_ref, carry_s, act_s, perm_s, unperm_s, xn_s, *, block, final):
    s = pl.program_id(1)
    sub, lanes = V7X_SUBLANES, V7X_LANES
    groups = block // sub
    chunks = D_MODEL // lanes
    pitch = _perm_pitch(block)

    @pl.when(s == 0)
    def _():
        carry_s[...] = jnp.zeros_like(carry_s)

    def stage(src_ref):
        for c in range(chunks):
            for i in range(sub):
                perm_s[c, i * pitch:i * pitch + groups] = src_ref[i * groups:(i + 1) * groups,
                                                                  c * lanes:(c + 1) * lanes]
        x = jnp.concatenate(
            [jnp.concatenate([perm_s[c, pl.ds(r, sub, stride=pitch), :] for c in range(chunks)], axis=1)
             for r in range(groups)], axis=0)
        xn = _rmsnorm(x, g2_ref[...])
        xn_s[...] = xn.astype(BF16)
        return xn

    @pl.when((pl.program_id(0) == 0) & (s == 0))
    def _():
        stage(x_ref)

    xn = xn_s[...]
    for j in range(D_FF // FFN_COL_TILE):
        halves = []
        for half in range(2):
            cols = slice(half * D_FF + j * FFN_COL_TILE, half * D_FF + (j + 1) * FFN_COL_TILE)
            up = _dot(xn, wup_ref[:, cols])
            halves.append(_conv_taps(up, carry_s, cw_ref, cb_ref, cols))
        act_s[:, j * FFN_COL_TILE:(j + 1) * FFN_COL_TILE] = (jax.nn.silu(halves[0]) * halves[1]).astype(BF16)
    staged = stage(xnext_ref)
    down = _dot(act_s[...], wdn_ref[...])
    anchor = _anchor_zero(staged)
    for r in range(groups):
        for c in range(chunks):
            tile_rc = down[r * sub:(r + 1) * sub, c * lanes:(c + 1) * lanes]
            if r == 0 and c == 0:
                tile_rc = tile_rc + anchor
            unperm_s[c, pl.ds(r, sub, stride=pitch), :] = tile_rc
    for i in range(sub):
        rows = slice(i * groups, (i + 1) * groups)
        y = x_ref[rows, :] + jnp.concatenate(
            [unperm_s[c, i * pitch:i * pitch + groups] for c in range(chunks)], axis=1)
        if final:
            y = _rmsnorm(y, gf_ref[...])
        xo_ref[rows, :] = y

    @pl.when(s == pl.num_programs(1) - 1)
    def _():
        conv_ref[...] = carry_s[V7X_SUBLANES - (CONV_W - 1):V7X_SUBLANES, :]


def _prompt_ffn(x, layer, g2, wup, cw, cb, wdn, gf, final):
    nb, seq, _ = x.shape
    block = FFN_BLOCK
    steps = seq // block
    grid = (nb, steps)
    xspec = pl.BlockSpec((None, block, D_MODEL), lambda b, s: (b, s, 0))
    out_shape = (
        jax.ShapeDtypeStruct((nb, seq, D_MODEL), F32),
        jax.ShapeDtypeStruct((nb, CONV_W - 1, 2 * D_FF), F32),
    )
    out_specs = (xspec, pl.BlockSpec((None, CONV_W - 1, 2 * D_FF), lambda b, s: (b, 0, 0)))
    in_specs = ([xspec, pl.BlockSpec((None, block, D_MODEL), _next_block_map(nb, steps))]
                + [_resident(a, layer) for a in (g2, wup, cw, cb, wdn)] + [_resident(gf)])
    reorder = pltpu.VMEM((D_MODEL // V7X_LANES, V7X_SUBLANES * _perm_pitch(block), V7X_LANES), F32)
    return pl.pallas_call(
        functools.partial(_ffn_kernel, block=block, final=final),
        grid=grid, in_specs=in_specs, out_specs=out_specs, out_shape=out_shape,
        scratch_shapes=[pltpu.VMEM((V7X_SUBLANES, 2 * D_FF), F32),
                        pltpu.VMEM((block, D_FF), BF16),
                        reorder, reorder,
                        pltpu.VMEM((block, D_MODEL), BF16)],
        compiler_params=pltpu.CompilerParams(
            dimension_semantics=("arbitrary", "arbitrary"), vmem_limit_bytes=V7X_VMEM_LIMIT),
        name="prompt_ffn",
    )(x, x, g2, wup, cw, cb, wdn, gf)


def _sample_proj_kernel(x_ref, g1_ref, wm_ref, wgt_ref, z_ref, gates_ref):
    xn = _rmsnorm(x_ref[...], g1_ref[...]).astype(BF16)
    z_ref[...] = _dot(xn, wm_ref[...])
    gt = _dot_nt(wgt_ref[...], xn)
    gt = jnp.concatenate([gt, jnp.zeros((V7X_LANES - gt.shape[0], gt.shape[1]), F32)], axis=0)
    g = gt.T
    gates_ref[...] = jnp.concatenate([g, pltpu.roll(g, V7X_LANES - GATE_ROWS, 1)], axis=1)


def _single_step(kernel_fn, name, out_shape, whole, layered, layer, stacked=None):
    in_specs = [_resident(a) for a in whole] + [_resident(a, layer) for a in layered]
    args = list(whole) + list(layered)
    out_specs = [pl.BlockSpec(o.shape, lambda i, nd=len(o.shape): (0,) * nd) for o in out_shape]
    aliases = {}
    if stacked is not None:
        k, prev = stacked
        nd = len(out_shape[k].shape) - 1
        out_specs[k] = pl.BlockSpec((None,) + out_shape[k].shape[1:], lambda i: (layer,) + (0,) * nd)
        if prev is not None:
            in_specs.append(pl.BlockSpec(memory_space=pl.ANY))
            aliases = {len(args): k}
            args.append(prev)
    return pl.pallas_call(
        kernel_fn,
        grid=(1,),
        in_specs=in_specs, out_specs=tuple(out_specs), out_shape=out_shape,
        input_output_aliases=aliases,
        compiler_params=pltpu.CompilerParams(
            dimension_semantics=("arbitrary",), vmem_limit_bytes=V7X_VMEM_LIMIT),
        name=name,
    )(*args)


def _sample_proj(x, layer, g1, wm, wgt):
    n = x.shape[0]
    assert n == V7X_LANES, "the gate transpose assumes one lane tile of sample rows"
    out_shape = (jax.ShapeDtypeStruct((n, P_MAIN), F32), jax.ShapeDtypeStruct((n, 2 * V7X_LANES), F32))
    return _single_step(_sample_proj_kernel, "sample_proj", out_shape, (x,), (g1, wm, wgt), layer)


def _sample_state_kernel(q_ref, k_ref, v_ref, gates_ref, gbias_ref, m_ref, n_ref, c_ref, *rest):
    h_ref, co_ref, no_ref, mo_ref = rest[-4:]
    tb = SAMPLE_BLOCK
    ig = gates_ref[:, 0:V7X_LANES] + gbias_ref[:, 0:V7X_LANES]
    lf = _log_sigmoid(gates_ref[:, V7X_LANES:] + gbias_ref[:, V7X_LANES:])
    inter = lf + m_ref[...]
    m_t = jnp.maximum(inter, ig)
    d_in = jnp.exp(ig - m_t)
    w_inter = jnp.exp(inter - m_t)
    floor = jnp.exp(-m_t)
    mo_ref[...] = m_t

    row8 = lax.broadcasted_iota(jnp.int32, (CHUNK, DK), 0)
    for h in range(N_HEADS):
        q8 = q_ref[:, h * DK:(h + 1) * DK]
        k8 = k_ref[:, h * DK:(h + 1) * DK] * K_SCALE
        v8 = v_ref[:, h * DV:(h + 1) * DV]
        d_h = d_in[:, h:h + 1]
        w_h = w_inter[:, h:h + 1]
        n8 = n_ref[:, h * DK:(h + 1) * DK]
        s = jnp.sum(q8 * k8, axis=-1, keepdims=True) * d_h
        den = s + w_h * jnp.sum(q8 * n8, axis=-1, keepdims=True)
        qb = q8.astype(BF16)
        inter_rows = [_dot_nt(qb, c_ref[j, h].astype(BF16))[j:j + 1] for j in range(tb)]
        num = s * v8 + w_h * jnp.concatenate(inter_rows, axis=0)
        h_ref[:, h * DV:(h + 1) * DV] = num / jnp.maximum(jnp.abs(den), floor[:, h:h + 1])
        no_ref[:, h * DK:(h + 1) * DK] = w_h * n8 + d_h * k8

        vt = jnp.concatenate([d_h * v8, jnp.zeros((CHUNK - tb, DV), F32)], axis=0).T.astype(BF16)
        kpad = jnp.concatenate([k8, jnp.zeros((CHUNK - tb, DK), F32)], axis=0)
        for j in range(tb):
            kj = jnp.where(row8 == j, kpad, 0.0).astype(BF16)
            co_ref[j, h] = w_inter[j:j + 1, h:h + 1] * c_ref[j, h] + _dot(vt, kj)


def _sample_state(z, gates, gbias, m_pad, n_state, c_state, layer, c_stack):
    n = z.shape[0]
    tb = SAMPLE_BLOCK
    c_block = pl.BlockSpec((None, tb, N_HEADS, DV, DK), lambda i: (layer, i, 0, 0, 0))
    in_specs = [
        pl.BlockSpec((tb, N_HEADS * DK), lambda i: (i, OFF_Q // (N_HEADS * DK))),
        pl.BlockSpec((tb, N_HEADS * DK), lambda i: (i, OFF_K // (N_HEADS * DK))),
        pl.BlockSpec((tb, N_HEADS * DV), lambda i: (i, OFF_V // (N_HEADS * DV))),
        pl.BlockSpec((tb, 2 * V7X_LANES), lambda i: (i, 0)),
        pl.BlockSpec((1, 2 * V7X_LANES), lambda i: (0, 0)),
        pl.BlockSpec((tb, V7X_LANES), lambda i: (i, 0)),
        pl.BlockSpec((tb, N_HEADS * DK), lambda i: (i, 0)),
        c_block,
    ]
    args = [z, z, z, gates, gbias, m_pad, n_state, c_state]
    aliases = {}
    if c_stack is not None:
        in_specs.append(pl.BlockSpec(memory_space=pl.ANY))
        aliases = {len(args): 1}
        args.append(c_stack)
    out_shape = (
        jax.ShapeDtypeStruct((n, N_HEADS * DV), F32),
        jax.ShapeDtypeStruct(c_state.shape, F32),
        jax.ShapeDtypeStruct((n, N_HEADS * DK), F32),
        jax.ShapeDtypeStruct((n, V7X_LANES), F32),
    )
    out_specs = (
        pl.BlockSpec((tb, N_HEADS * DV), lambda i: (i, 0)),
        c_block,
        pl.BlockSpec((tb, N_HEADS * DK), lambda i: (i, 0)),
        pl.BlockSpec((tb, V7X_LANES), lambda i: (i, 0)),
    )
    return pl.pallas_call(
        _sample_state_kernel,
        grid=(n // tb,), in_specs=in_specs, out_specs=out_specs, out_shape=out_shape,
        input_output_aliases=aliases,
        compiler_params=pltpu.CompilerParams(
            dimension_semantics=("arbitrary",), vmem_limit_bytes=V7X_VMEM_LIMIT),
        name="sample_state",
    )(*args)


def _sample_mixer_kernel(x_ref, z_ref, h_ref, gv_ref, ws0_ref, bs0_ref, wa_ref, wb_ref, wo_ref,
                         xo_ref, vn_ref):
    y_a = _dot((jax.nn.sigmoid(z_ref[:, OFF_O:OFF_U]) * h_ref[...]).astype(BF16), wa_ref[...])
    u = jax.nn.gelu(z_ref[:, OFF_U:OFF_VB])
    vn = _rmsnorm(jax.nn.gelu(z_ref[:, OFF_VB:OFF_GA]), gv_ref[...])
    vn_ref[...] = vn
    mixed = ws0_ref[...] * vn + bs0_ref[...]
    y_b = _dot((u * mixed).astype(BF16), wb_ref[...])
    merged = (jax.nn.sigmoid(z_ref[:, OFF_GA:OFF_GB]) * y_a
              + jax.nn.sigmoid(z_ref[:, OFF_GB:P_MAIN]) * y_b)
    xo_ref[...] = x_ref[...] + _dot(merged.astype(BF16), wo_ref[...])


def _sample_mixer(x, z, h, layer, gv, ws0, bs0, wa, wb, wo):
    n = x.shape[0]
    out_shape = (jax.ShapeDtypeStruct((n, D_MODEL), F32), jax.ShapeDtypeStruct((n, D_B), F32))
    return _single_step(_sample_mixer_kernel, "sample_mixer", out_shape,
                        (x, z, h), (gv, ws0, bs0, wa, wb, wo), layer)


def _sample_ffn_kernel(x_ref, gf_ref, buf_ref, g2_ref, wup_ref, cw_ref, cb_ref, wdn_ref, *rest, final):
    xo_ref, nbuf_ref = rest[-2:]
    x = x_ref[...]
    xn = _rmsnorm(x, g2_ref[...]).astype(BF16)
    acc = jnp.zeros(x.shape, F32)
    for j in range(D_FF // FFN_COL_TILE):
        halves = []
        for half in range(2):
            col0 = half * D_FF + j * FFN_COL_TILE
            cols = slice(col0, col0 + FFN_COL_TILE)
            up = _dot(xn, wup_ref[:, cols])
            b0 = buf_ref[:, 0, cols]
            b1 = buf_ref[:, 1, cols]
            nbuf_ref[:, 0, cols] = b1
            nbuf_ref[:, 1, cols] = up
            halves.append(cb_ref[:, cols] + cw_ref[0:1, cols] * b0 + cw_ref[1:2, cols] * b1
                          + cw_ref[2:3, cols] * up)
        act = (jax.nn.silu(halves[0]) * halves[1]).astype(BF16)
        acc = acc + _dot(act, wdn_ref[j * FFN_COL_TILE:(j + 1) * FFN_COL_TILE, :])
    y = x + acc
    if final:
        y = _rmsnorm(y, gf_ref[...])
    xo_ref[...] = y


def _sample_ffn(x, gf, layer, buf, g2, wup, cw, cb, wdn, final, conv_stack):
    n = x.shape[0]
    out_shape = (jax.ShapeDtypeStruct((n, D_MODEL), F32), jax.ShapeDtypeStruct(buf.shape, F32))
    return _single_step(functools.partial(_sample_ffn_kernel, final=final), "sample_ffn", out_shape,
                        (x, gf), (buf, g2, wup, cw, cb, wdn), layer, stacked=(1, conv_stack))


def _pack_kernel(wt_ref, o_ref, wgt_ref):
    g8 = wt_ref[GATE_LO:GATE_HI, :]
    head = lax.broadcasted_iota(jnp.int32, g8.shape, 0) < N_HEADS
    wgt_ref[...] = jnp.concatenate(
        [jnp.where(head, g8, 0.0), jnp.where(head, pltpu.roll(g8, N_HEADS, 0), 0.0)], axis=0).astype(BF16)
    for j in range(P_MAIN // PACK_TILE):
        dst = j * PACK_TILE
        src = dst if dst < GATE_LO else dst + (GATE_HI - GATE_LO)
        o_ref[:, dst:dst + PACK_TILE] = wt_ref[src:src + PACK_TILE, :].T.astype(BF16)


def _pack_w_in(w_in_t):
    depth, p_in, d = w_in_t.shape
    return pl.pallas_call(
        _pack_kernel,
        grid=(depth, d // V7X_LANES),
        in_specs=[pl.BlockSpec((None, p_in, V7X_LANES), lambda l, r: (l, 0, r))],
        out_specs=(pl.BlockSpec((None, V7X_LANES, P_MAIN), lambda l, r: (l, r, 0)),
                   pl.BlockSpec((None, 2 * GATE_ROWS, V7X_LANES), lambda l, r: (l, 0, r))),
        out_shape=(jax.ShapeDtypeStruct((depth, d, P_MAIN), BF16),
                   jax.ShapeDtypeStruct((depth, 2 * GATE_ROWS, d), BF16)),
        compiler_params=pltpu.CompilerParams(
            dimension_semantics=("arbitrary", "arbitrary"), vmem_limit_bytes=V7X_VMEM_LIMIT),
        name="pack_w_in",
    )(w_in_t)


def kernel(x_prompt, x_sample, state_mlstm_C, state_mlstm_n, state_mlstm_m, state_ffn_conv, w_in, b_igate, b_fgate, g_norm1, g_vnorm, w_spatial, b_spatial, w_branch_a, w_branch_b, w_out, g_norm2, w_up, conv_w, conv_b, w_down, g_final):
    depth = w_in.shape[0]
    n_dec = x_sample.shape[0]
    xp = x_prompt
    xs = x_sample.reshape(n_dec, D_MODEL)
    gf = g_final.reshape(1, D_MODEL)

    wm, wgt = _pack_w_in(jnp.swapaxes(w_in, 1, 2))
    head_pad = ((0, 0), (0, GATE_ROWS - N_HEADS))
    gbias_col = jnp.concatenate([jnp.pad(b_igate, head_pad), jnp.pad(b_fgate, head_pad)],
                                axis=1).reshape(depth, 2 * GATE_ROWS, 1)
    head_lanes = ((0, 0), (0, V7X_LANES - N_HEADS))
    gbias_row = jnp.concatenate([jnp.pad(b_igate, head_lanes), jnp.pad(b_fgate, head_lanes)],
                                axis=1).reshape(depth, 1, 2 * V7X_LANES)
    g1 = g_norm1.reshape(depth, 1, D_MODEL)
    gv = g_vnorm.reshape(depth, 1, D_B)
    g2 = g_norm2.reshape(depth, 1, D_MODEL)
    bst = jnp.swapaxes(b_spatial, 1, 2)
    ws0 = jnp.repeat(w_spatial[:, :, 0, 0], DG, axis=1).reshape(depth, 1, D_B)
    bs0 = jnp.repeat(b_spatial[:, :, 0], DG, axis=1).reshape(depth, 1, D_B)
    wa = w_branch_a.astype(BF16)
    wb = w_branch_b.astype(BF16)
    wo = w_out.astype(BF16)
    wup = w_up.astype(BF16)
    wdn = w_down.astype(BF16)
    cb = conv_b.reshape(depth, 1, 2 * D_FF)
    n_state = state_mlstm_n.reshape(depth, n_dec, N_HEADS * DK)
    m_pad = jnp.pad(state_mlstm_m, ((0, 0), (0, 0), (0, V7X_LANES - N_HEADS)))

    small = [[] for _ in range(7)]
    c_stack = None
    conv_stack = None
    for l in range(depth):
        final = l == depth - 1

        xp, c_p, n_p, m_p = _prompt_mixer(xp, l, wm, wgt, gbias_col, g1, gv, w_spatial, bst, wa, wb, wo)
        xp, conv_p = _prompt_ffn(xp, l, g2, wup, conv_w, cb, wdn, gf, final)

        z, gates = _sample_proj(xs, l, g1, wm, wgt)
        h, c_stack, n_s, m_s = _sample_state(
            z, gates, gbias_row[l], m_pad[l], n_state[l], state_mlstm_C, l, c_stack)
        xs, vn_s = _sample_mixer(xs, z, h, l, gv, ws0, bs0, wa, wb, wo)
        xs, conv_stack = _sample_ffn(xs, gf, l, state_ffn_conv, g2, wup, conv_w, cb, wdn, final, conv_stack)

        for lst, val in zip(small, (c_p, n_p, m_p[:, :N_HEADS, 0], conv_p,
                                    n_s.reshape(n_dec, N_HEADS, DK), m_s[:, :N_HEADS],
                                    vn_s.reshape(n_dec, 1, D_B))):
            lst.append(val)
    st = [jnp.stack(o) for o in small]
    return (xp, xs.reshape(n_dec, 1, D_MODEL), st[0], st[1], st[2], st[3], c_stack, st[4], st[5], conv_stack, st[6])
```

```python
import functools

import jax
import jax.numpy as jnp
from jax import lax
from jax.experimental import pallas as pl
from jax.experimental.pallas import tpu as pltpu

D_MODEL = 1024
N_HEADS = 4
DK = 128
DV = 256
CHUNK = 128
D_B = 1024
N_GROUPS = 4
DG = D_B // N_GROUPS
D_FF = 2816
CONV_W = 3
EPS = 1e-6
K_SCALE = DK ** -0.5

OFF_Q = 0
OFF_K = OFF_Q + N_HEADS * DK
OFF_V = OFF_K + N_HEADS * DK
OFF_O = OFF_V + N_HEADS * DV
OFF_U = OFF_O + N_HEADS * DV
OFF_VB = OFF_U + D_B
OFF_GA = OFF_VB + D_B
OFF_GB = OFF_GA + D_MODEL
P_MAIN = OFF_GB + D_MODEL
GATE_LO = 2 * N_HEADS * DK + 2 * N_HEADS * DV
GATE_HI = GATE_LO + 2 * N_HEADS

V7X_LANES = 128
V7X_SUBLANES = 8
GATE_ROWS = V7X_SUBLANES
V7X_VMEM_LIMIT = 56 * 1024 * 1024
FFN_COL_TILE = 256
PROJ_TILE = 256
MIXER_BLOCK = 512
FFN_BLOCK = 512
SAMPLE_BLOCK = 16
PACK_TILE = 256
PACK_DIMS = 256
SAMPLE_PROJ_TILE = 1024

F32 = jnp.float32
BF16 = jnp.bfloat16


def _dot(a, b):
    return jnp.dot(a, b, preferred_element_type=F32)


def _dot_nt(a, b):
    return lax.dot_general(a, b, (((1,), (1,)), ((), ())), preferred_element_type=F32)


def _rmsnorm(x, g):
    r = lax.rsqrt(jnp.mean(x * x, axis=-1, keepdims=True) + EPS)
    return x * r * g


def _log_sigmoid(x):
    return jnp.minimum(x, 0.0) - jnp.log1p(jnp.exp(-jnp.abs(x)))


def _scan_lanes(x, op, fill):
    lane = lax.broadcasted_iota(jnp.int32, x.shape, 1)
    k = 1
    while k < x.shape[1]:
        shifted = pltpu.roll(x, k, 1)
        x = op(x, jnp.where(lane >= k, shifted, fill))
        k *= 2
    return x


def _anchor_zero(x):
    sub, lanes = V7X_SUBLANES, V7X_LANES
    acc = jnp.zeros((sub, lanes), jnp.uint32)
    for r in range(x.shape[0] // sub):
        for c in range(x.shape[1] // lanes):
            piece = pltpu.bitcast(x[r * sub:(r + 1) * sub, c * lanes:(c + 1) * lanes], jnp.uint32)
            acc = acc | ((piece >> 16) >> 16)
    return pltpu.bitcast(acc, F32)


def _next_block_map(nb, steps):
    def index_map(b, s):
        nxt = jnp.minimum(b * steps + s + 1, nb * steps - 1)
        return (nxt // steps, nxt % steps, 0)
    return index_map


def _mixer_kernel(x_ref, wm_ref, wgt_ref, gbias_ref, g1_ref, gv_ref, ws_ref, bst_ref,
                  wa_ref, wb_ref, wo_ref,
                  xo_ref, c_ref, n_ref, m_ref,
                  ct_s, n_s, m_s, q_s, k_s, v_s, h_s, so_s, u_s, vb_s, sg_s, um_s, *, block):
    s = pl.program_id(1)
    n_chunks = block // CHUNK

    @pl.when(s == 0)
    def _():
        ct_s[...] = jnp.zeros_like(ct_s)
        n_s[...] = jnp.zeros_like(n_s)
        m_s[...] = jnp.zeros_like(m_s)

    xn = _rmsnorm(x_ref[...], g1_ref[...]).astype(BF16)

    gates = _dot_nt(wgt_ref[...], xn) + gbias_ref[...]

    def proj(off, t):
        return _dot(xn, wm_ref[:, off + t * PROJ_TILE:off + (t + 1) * PROJ_TILE])

    def tile(t):
        return slice(t * PROJ_TILE, (t + 1) * PROJ_TILE)

    for t in range(N_HEADS * DK // PROJ_TILE):
        q_s[:, tile(t)] = proj(OFF_Q, t)
        k_s[:, tile(t)] = proj(OFF_K, t) * K_SCALE
    for t in range(N_HEADS * DV // PROJ_TILE):
        v_s[:, tile(t)] = proj(OFF_V, t)

    sumsq = [jnp.zeros((block, 1), F32)]

    def vb_tile(t):
        g = jax.nn.gelu(proj(OFF_VB, t))
        vb_s[:, tile(t)] = g
        sumsq[0] = sumsq[0] + jnp.sum(g * g, axis=-1, keepdims=True)

    def u_tile(t):
        u_s[:, tile(t)] = jax.nn.gelu(proj(OFF_U, t))

    def o_tile(t):
        so_s[:, tile(t)] = jax.nn.sigmoid(proj(OFF_O, t))

    def ga_tile(t):
        sg_s[:, tile(t)] = jax.nn.sigmoid(proj(OFF_GA, t))

    def gb_tile(t):
        sg_s[:, D_MODEL + t * PROJ_TILE:D_MODEL + (t + 1) * PROJ_TILE] = jax.nn.sigmoid(proj(OFF_GB, t))

    jobs = [(f, t) for f in (vb_tile, u_tile, o_tile, ga_tile, gb_tile) for t in range(D_MODEL // PROJ_TILE)]

    def run_jobs(count):
        for _ in range(min(count, len(jobs))):
            f, t = jobs.pop(0)
            f(t)

    row_i = lax.broadcasted_iota(jnp.int32, (CHUNK, CHUNK), 0)
    col_i = lax.broadcasted_iota(jnp.int32, (CHUNK, CHUNK), 1)
    causal = row_i >= col_i
    heads = range(N_HEADS)

    for c in range(n_chunks):
        r0 = c * CHUNK
        ig = gates[0:GATE_ROWS, r0:r0 + CHUNK]
        lf = _log_sigmoid(gates[GATE_ROWS:2 * GATE_ROWS, r0:r0 + CHUNK])
        b = _scan_lanes(lf, jnp.add, 0.0)
        a = ig - b
        m_prev = m_s[...]
        gmax = jnp.maximum(m_prev, _scan_lanes(a, jnp.maximum, -jnp.inf))
        m_t = b + gmax
        w_inter = jnp.exp(m_prev - gmax)
        g_last = gmax[:, CHUNK - 1:CHUNK]
        w_last = jnp.exp(a - g_last)
        floor = jnp.exp(-m_t)
        decay = w_inter[:, CHUNK - 1:CHUNK]
        m_s[...] = jnp.broadcast_to(m_t[:, CHUNK - 1:CHUNK], m_s.shape)

        rows = jnp.concatenate(
            [gmax, w_inter, w_last, floor,
             jnp.zeros((CHUNK - 4 * GATE_ROWS, CHUNK), F32)], axis=0)
        cols = rows.T

        def col(kind, h):
            return cols[:, kind * GATE_ROWS + h:kind * GATE_ROWS + h + 1]

        qf = [q_s[r0:r0 + CHUNK, h * DK:(h + 1) * DK] for h in heads]
        kf = [k_s[r0:r0 + CHUNK, h * DK:(h + 1) * DK] for h in heads]
        vf = [v_s[r0:r0 + CHUNK, h * DV:(h + 1) * DV] for h in heads]
        qb = [x.astype(BF16) for x in qf]
        kt = [x.T.astype(BF16) for x in kf]
        run_jobs(2)
        zero = jnp.zeros((DK, CHUNK), BF16)
        sc = []
        for h in range(0, N_HEADS, 2):
            kk = jnp.concatenate([jnp.concatenate([kt[h], zero], axis=1),
                                  jnp.concatenate([zero, kt[h + 1]], axis=1)], axis=0)
            pair = _dot(jnp.concatenate([qb[h], qb[h + 1]], axis=1), kk)
            sc += [pair[:, :CHUNK], pair[:, CHUNK:]]
        dmat = [jnp.where(causal, jnp.exp(a[h:h + 1, :] - col(0, h)), 0.0) for h in heads]
        run_jobs(1)
        sd = [sc[h] * dmat[h] for h in heads]
        ct = [ct_s[h] for h in heads]
        num = [_dot(jnp.concatenate([sd[h], col(1, h) * qf[h]], axis=1).astype(BF16),
                    jnp.concatenate([vf[h], ct[h]], axis=0).astype(BF16)) for h in heads]
        run_jobs(2)
        for h in heads:
            nh = n_s[h:h + 1, :]
            den = (jnp.sum(sd[h], axis=-1, keepdims=True)
                   + col(1, h) * jnp.sum(qf[h] * nh, axis=-1, keepdims=True))
            h_s[r0:r0 + CHUNK, h * DV:(h + 1) * DV] = num[h] * (1.0 / jnp.maximum(jnp.abs(den), col(3, h)))
            dec = decay[h:h + 1, :]
            n_s[h:h + 1, :] = dec * nh + jnp.sum(col(2, h) * kf[h], axis=0, keepdims=True)
        run_jobs(1)
        for h in heads:
            ct_s[h] = decay[h:h + 1, :] * ct[h] + _dot(kt[h], (col(2, h) * vf[h]).astype(BF16))
    run_jobs(len(jobs))

    y_a = _dot((so_s[...] * h_s[...]).astype(BF16), wa_ref[...])

    rinv = lax.rsqrt(sumsq[0] * (1.0 / D_B) + EPS)
    for g in range(N_GROUPS):
        gcols = slice(g * DG, (g + 1) * DG)
        w_tri = jnp.where(causal, ws_ref[g], 0.0).astype(BF16)
        bias_c = bst_ref[:, g:g + 1]
        for c in range(n_chunks):
            rws = slice(c * CHUNK, (c + 1) * CHUNK)
            vn = vb_s[rws, gcols] * rinv[rws] * gv_ref[:, gcols]
            um_s[rws, gcols] = u_s[rws, gcols] * (_dot(w_tri, vn.astype(BF16)) + bias_c)
    y_b = _dot(um_s[...].astype(BF16), wb_ref[...])

    merged = sg_s[:, 0:D_MODEL] * y_a + sg_s[:, D_MODEL:2 * D_MODEL] * y_b
    xo_ref[...] = x_ref[...] + _dot(merged.astype(BF16), wo_ref[...])

    @pl.when(s == pl.num_programs(1) - 1)
    def _():
        for h in range(N_HEADS):
            c_ref[h] = ct_s[h].T
        n_ref[...] = n_s[...]
        m_ref[...] = m_s[...]


def _resident(arr, layer=None):
    if layer is None:
        nd = arr.ndim
        return pl.BlockSpec(arr.shape, lambda *_: (0,) * nd, pipeline_mode=pl.Buffered(1))
    nd = arr.ndim - 1
    return pl.BlockSpec((None,) + arr.shape[1:], lambda *_: (layer,) + (0,) * nd,
                        pipeline_mode=pl.Buffered(1))


def _prompt_mixer(x, layer, wm, wgt, gbias, g1, gv, ws, bst, wa, wb, wo):
    nb, seq, _ = x.shape
    block = MIXER_BLOCK
    grid = (nb, seq // block)
    xspec = pl.BlockSpec((None, block, D_MODEL), lambda b, s: (b, s, 0))
    out_shape = (
        jax.ShapeDtypeStruct((nb, seq, D_MODEL), F32),
        jax.ShapeDtypeStruct((nb, N_HEADS, DV, DK), F32),
        jax.ShapeDtypeStruct((nb, N_HEADS, DK), F32),
        jax.ShapeDtypeStruct((nb, GATE_ROWS, V7X_LANES), F32),
    )
    out_specs = (
        xspec,
        pl.BlockSpec((None, N_HEADS, DV, DK), lambda b, s: (b, 0, 0, 0)),
        pl.BlockSpec((None, N_HEADS, DK), lambda b, s: (b, 0, 0)),
        pl.BlockSpec((None, GATE_ROWS, V7X_LANES), lambda b, s: (b, 0, 0)),
    )
    in_specs = [xspec] + [_resident(a, layer) for a in (wm, wgt, gbias, g1, gv, ws, bst, wa, wb, wo)]
    scratch = [
        pltpu.VMEM((N_HEADS, DK, DV), F32),
        pltpu.VMEM((N_HEADS, DK), F32),
        pltpu.VMEM((GATE_ROWS, V7X_LANES), F32),
        pltpu.VMEM((block, N_HEADS * DK), F32),
        pltpu.VMEM((block, N_HEADS * DK), F32),
        pltpu.VMEM((block, N_HEADS * DV), F32),
        pltpu.VMEM((block, N_HEADS * DV), F32),
        pltpu.VMEM((block, N_HEADS * DV), F32),
        pltpu.VMEM((block, D_B), F32),
        pltpu.VMEM((block, D_B), F32),
        pltpu.VMEM((block, 2 * D_MODEL), F32),
        pltpu.VMEM((block, D_B), F32),
    ]
    return pl.pallas_call(
        functools.partial(_mixer_kernel, block=block),
        grid=grid, in_specs=in_specs, out_specs=out_specs, out_shape=out_shape,
        scratch_shapes=scratch,
        compiler_params=pltpu.CompilerParams(
            dimension_semantics=("arbitrary", "arbitrary"), vmem_limit_bytes=V7X_VMEM_LIMIT),
        name="prompt_mixer",
    )(x, wm, wgt, gbias, g1, gv, ws, bst, wa, wb, wo)


def _conv_taps(up, carry_s, cw_ref, cb_ref, cols):
    sub = V7X_SUBLANES
    rows = up.shape[0]
    last1 = up[rows - sub:rows]
    last2 = up[rows - 2 * sub:rows - sub]
    first = lax.broadcasted_iota(jnp.int32, last1.shape, 0) == 0
    back1 = jnp.where(first, carry_s[sub - 1:sub, cols], pltpu.roll(last1, 1, 0))
    back2 = jnp.where(first, carry_s[sub - 2:sub - 1, cols], pltpu.roll(last2, 1, 0))
    carry_s[sub - 2:sub - 1, cols] = last2[sub - 1:sub]
    carry_s[sub - 1:sub, cols] = last1[sub - 1:sub]
    m1 = jnp.concatenate([back1, up[0:rows - sub]], axis=0)
    m2 = jnp.concatenate([back2, back1, up[0:rows - 2 * sub]], axis=0)
    return (cb_ref[:, cols] + cw_ref[0:1, cols] * m2 + cw_ref[1:2, cols] * m1
            + cw_ref[2:3, cols] * up)


def _perm_pitch(block):
    return block // V7X_SUBLANES + V7X_SUBLANES


def _ffn_kernel(x_ref, xnext_ref, g2_ref, wup_ref, cw_ref, cb_ref, wdn_ref, gf_ref,
                xo_ref, conv_ref, carry_s, act_s, perm_s, unperm_s, xn_s, *, block, final):
    s = pl.program_id(1)
    sub, lanes = V7X_SUBLANES, V7X_LANES
    groups = block // sub
    chunks = D_MODEL // lanes
    pitch = _perm_pitch(block)

    @pl.when(s == 0)
    def _():
        carry_s[...] = jnp.zeros_like(carry_s)

    def stage(src_ref):
        for c in range(chunks):
            for i in range(sub):
                perm_s[c, i * pitch:i * pitch + groups] = src_ref[i * groups:(i + 1) * groups,
                                                                  c * lanes:(c + 1) * lanes]
        x = jnp.concatenate(
            [jnp.concatenate([perm_s[c, pl.ds(r, sub, stride=pitch), :] for c in range(chunks)], axis=1)
             for r in range(groups)], axis=0)
        xn = _rmsnorm(x, g2_ref[...])
        xn_s[...] = xn.astype(BF16)
        return xn

    @pl.when((pl.program_id(0) == 0) & (s == 0))
    def _():
        stage(x_ref)

    xn = xn_s[...]
    for j in range(D_FF // FFN_COL_TILE):
        halves = []
        for half in range(2):
            cols = slice(half * D_FF + j * FFN_COL_TILE, half * D_FF + (j + 1) * FFN_COL_TILE)
            up = _dot(xn, wup_ref[:, cols])
            halves.append(_conv_taps(up, carry_s, cw_ref, cb_ref, cols))
        act_s[:, j * FFN_COL_TILE:(j + 1) * FFN_COL_TILE] = (jax.nn.silu(halves[0]) * halves[1]).astype(BF16)
    staged = stage(xnext_ref)
    down = _dot(act_s[...], wdn_ref[...])
    anchor = _anchor_zero(staged)
    for r in range(groups):
        for c in range(chunks):
            tile_rc = down[r * sub:(r + 1) * sub, c * lanes:(c + 1) * lanes]
            if r == 0 and c == 0:
                tile_rc = tile_rc + anchor
            unperm_s[c, pl.ds(r, sub, stride=pitch), :] = tile_rc
    for i in range(sub):
        rows = slice(i * groups, (i + 1) * groups)
        y = x_ref[rows, :] + jnp.concatenate(
            [unperm_s[c, i * pitch:i * pitch + groups] for c in range(chunks)], axis=1)
        if final:
            y = _rmsnorm(y, gf_ref[...])
        xo_ref[rows, :] = y

    @pl.when(s == pl.num_programs(1) - 1)
    def _():
        conv_ref[...] = carry_s[V7X_SUBLANES - (CONV_W - 1):V7X_SUBLANES, :]


def _prompt_ffn(x, layer, g2, wup, cw, cb, wdn, gf, final):
    nb, seq, _ = x.shape
    block = FFN_BLOCK
    steps = seq // block
    grid = (nb, steps)
    xspec = pl.BlockSpec((None, block, D_MODEL), lambda b, s: (b, s, 0))
    out_shape = (
        jax.ShapeDtypeStruct((nb, seq, D_MODEL), F32),
        jax.ShapeDtypeStruct((nb, CONV_W - 1, 2 * D_FF), F32),
    )
    out_specs = (xspec, pl.BlockSpec((None, CONV_W - 1, 2 * D_FF), lambda b, s: (b, 0, 0)))
    in_specs = ([xspec, pl.BlockSpec((None, block, D_MODEL), _next_block_map(nb, steps))]
                + [_resident(a, layer) for a in (g2, wup, cw, cb, wdn)] + [_resident(gf)])
    reorder = pltpu.VMEM((D_MODEL // V7X_LANES, V7X_SUBLANES * _perm_pitch(block), V7X_LANES), F32)
    return pl.pallas_call(
        functools.partial(_ffn_kernel, block=block, final=final),
        grid=grid, in_specs=in_specs, out_specs=out_specs, out_shape=out_shape,
        scratch_shapes=[pltpu.VMEM((V7X_SUBLANES, 2 * D_FF), F32),
                        pltpu.VMEM((block, D_FF), BF16),
                        reorder, reorder,
                        pltpu.VMEM((block, D_MODEL), BF16)],
        compiler_params=pltpu.CompilerParams(
            dimension_semantics=("arbitrary", "arbitrary"), vmem_limit_bytes=V7X_VMEM_LIMIT),
        name="prompt_ffn",
    )(x, x, g2, wup, cw, cb, wdn, gf)


def _sample_proj_kernel(x_ref, g1_ref, wgt_ref, wm_ref, z_ref, gates_ref):
    xn = _rmsnorm(x_ref[...], g1_ref[...]).astype(BF16)
    z_ref[...] = _dot(xn, wm_ref[...])

    @pl.when(pl.program_id(0) == 0)
    def _():
        gt = _dot_nt(wgt_ref[...], xn)
        gt = jnp.concatenate([gt, jnp.zeros((V7X_LANES - gt.shape[0], gt.shape[1]), F32)], axis=0)
        g = gt.T
        gates_ref[...] = jnp.concatenate([g, pltpu.roll(g, V7X_LANES - GATE_ROWS, 1)], axis=1)


def _single_step(kernel_fn, name, out_shape, whole, layered, layer, stacked=None):
    in_specs = [_resident(a) for a in whole] + [_resident(a, layer) for a in layered]
    args = list(whole) + list(layered)
    out_specs = [pl.BlockSpec(o.shape, lambda i, nd=len(o.shape): (0,) * nd) for o in out_shape]
    aliases = {}
    if stacked is not None:
        k, prev = stacked
        nd = len(out_shape[k].shape) - 1
        out_specs[k] = pl.BlockSpec((None,) + out_shape[k].shape[1:], lambda i: (layer,) + (0,) * nd)
        if prev is not None:
            in_specs.append(pl.BlockSpec(memory_space=pl.ANY))
            aliases = {len(args): k}
            args.append(prev)
    return pl.pallas_call(
        kernel_fn,
        grid=(1,),
        in_specs=in_specs, out_specs=tuple(out_specs), out_shape=out_shape,
        input_output_aliases=aliases,
        compiler_params=pltpu.CompilerParams(
            dimension_semantics=("arbitrary",), vmem_limit_bytes=V7X_VMEM_LIMIT),
        name=name,
    )(*args)


def _sample_proj(x, layer, g1, wm, wgt):
    n = x.shape[0]
    assert n == V7X_LANES, "the gate transpose assumes one lane tile of sample rows"
    tile = SAMPLE_PROJ_TILE
    return pl.pallas_call(
        _sample_proj_kernel,
        grid=(P_MAIN // tile,),
        in_specs=[_resident(x), _resident(g1, layer), _resident(wgt, layer),
                  pl.BlockSpec((None, D_MODEL, tile), lambda j: (layer, 0, j))],
        out_specs=(pl.BlockSpec((n, tile), lambda j: (0, j)),
                   pl.BlockSpec((n, 2 * V7X_LANES), lambda j: (0, 0))),
        out_shape=(jax.ShapeDtypeStruct((n, P_MAIN), F32), jax.ShapeDtypeStruct((n, 2 * V7X_LANES), F32)),
        compiler_params=pltpu.CompilerParams(
            dimension_semantics=("arbitrary",), vmem_limit_bytes=V7X_VMEM_LIMIT),
        name="sample_proj",
    )(x, g1, wgt, wm)


def _sample_state_kernel(q_ref, k_ref, v_ref, gates_ref, gbias_ref, m_ref, n_ref, c_ref, *rest):
    h_ref, co_ref, no_ref, mo_ref = rest[-4:]
    tb = SAMPLE_BLOCK
    ig = gates_ref[:, 0:V7X_LANES] + gbias_ref[:, 0:V7X_LANES]
    lf = _log_sigmoid(gates_ref[:, V7X_LANES:] + gbias_ref[:, V7X_LANES:])
    inter = lf + m_ref[...]
    m_t = jnp.maximum(inter, ig)
    d_in = jnp.exp(ig - m_t)
    w_inter = jnp.exp(inter - m_t)
    floor = jnp.exp(-m_t)
    mo_ref[...] = m_t

    row8 = lax.broadcasted_iota(jnp.int32, (CHUNK, DK), 0)
    for h in range(N_HEADS):
        q8 = q_ref[:, h * DK:(h + 1) * DK]
        k8 = k_ref[:, h * DK:(h + 1) * DK] * K_SCALE
        v8 = v_ref[:, h * DV:(h + 1) * DV]
        d_h = d_in[:, h:h + 1]
        w_h = w_inter[:, h:h + 1]
        n8 = n_ref[:, h * DK:(h + 1) * DK]
        s = jnp.sum(q8 * k8, axis=-1, keepdims=True) * d_h
        den = s + w_h * jnp.sum(q8 * n8, axis=-1, keepdims=True)
        qb = q8.astype(BF16)
        inter_rows = [_dot_nt(qb, c_ref[j, h].astype(BF16))[j:j + 1] for j in range(tb)]
        num = s * v8 + w_h * jnp.concatenate(inter_rows, axis=0)
        h_ref[:, h * DV:(h + 1) * DV] = num / jnp.maximum(jnp.abs(den), floor[:, h:h + 1])
        no_ref[:, h * DK:(h + 1) * DK] = w_h * n8 + d_h * k8

        vt = jnp.concatenate([d_h * v8, jnp.zeros((CHUNK - tb, DV), F32)], axis=0).T.astype(BF16)
        kpad = jnp.concatenate([k8, jnp.zeros((CHUNK - tb, DK), F32)], axis=0)
        for j in range(tb):
            kj = jnp.where(row8 == j, kpad, 0.0).astype(BF16)
            co_ref[j, h] = w_inter[j:j + 1, h:h + 1] * c_ref[j, h] + _dot(vt, kj)


def _sample_state(z, gates, gbias, m_pad, n_state, c_state, layer, c_stack):
    n = z.shape[0]
    tb = SAMPLE_BLOCK
    c_block = pl.BlockSpec((None, tb, N_HEADS, DV, DK), lambda i: (layer, i, 0, 0, 0))
    in_specs = [
        pl.BlockSpec((tb, N_HEADS * DK), lambda i: (i, OFF_Q // (N_HEADS * DK))),
        pl.BlockSpec((tb, N_HEADS * DK), lambda i: (i, OFF_K // (N_HEADS * DK))),
        pl.BlockSpec((tb, N_HEADS * DV), lambda i: (i, OFF_V // (N_HEADS * DV))),
        pl.BlockSpec((tb, 2 * V7X_LANES), lambda i: (i, 0)),
        pl.BlockSpec((1, 2 * V7X_LANES), lambda i: (0, 0)),
        pl.BlockSpec((tb, V7X_LANES), lambda i: (i, 0)),
        pl.BlockSpec((tb, N_HEADS * DK), lambda i: (i, 0)),
        c_block,
    ]
    args = [z, z, z, gates, gbias, m_pad, n_state, c_state]
    aliases = {}
    if c_stack is not None:
        in_specs.append(pl.BlockSpec(memory_space=pl.ANY))
        aliases = {len(args): 1}
        args.append(c_stack)
    out_shape = (
        jax.ShapeDtypeStruct((n, N_HEADS * DV), F32),
        jax.ShapeDtypeStruct(c_state.shape, F32),
        jax.ShapeDtypeStruct((n, N_HEADS * DK), F32),
        jax.ShapeDtypeStruct((n, V7X_LANES), F32),
    )
    out_specs = (
        pl.BlockSpec((tb, N_HEADS * DV), lambda i: (i, 0)),
        c_block,
        pl.BlockSpec((tb, N_HEADS * DK), lambda i: (i, 0)),
        pl.BlockSpec((tb, V7X_LANES), lambda i: (i, 0)),
    )
    return pl.pallas_call(
        _sample_state_kernel,
        grid=(n // tb,), in_specs=in_specs, out_specs=out_specs, out_shape=out_shape,
        input_output_aliases=aliases,
        compiler_params=pltpu.CompilerParams(
            dimension_semantics=("arbitrary",), vmem_limit_bytes=V7X_VMEM_LIMIT),
        name="sample_state",
    )(*args)


def _sample_mixer_kernel(x_ref, z_ref, h_ref, gv_ref, ws0_ref, bs0_ref, wa_ref, wb_ref, wo_ref,
                         xo_ref, vn_ref):
    y_a = _dot((jax.nn.sigmoid(z_ref[:, OFF_O:OFF_U]) * h_ref[...]).astype(BF16), wa_ref[...])
    u = jax.nn.gelu(z_ref[:, OFF_U:OFF_VB])
    vn = _rmsnorm(jax.nn.gelu(z_ref[:, OFF_VB:OFF_GA]), gv_ref[...])
    vn_ref[...] = vn
    mixed = ws0_ref[...] * vn + bs0_ref[...]
    y_b = _dot((u * mixed).astype(BF16), wb_ref[...])
    merged = (jax.nn.sigmoid(z_ref[:, OFF_GA:OFF_GB]) * y_a
              + jax.nn.sigmoid(z_ref[:, OFF_GB:P_MAIN]) * y_b)
    xo_ref[...] = x_ref[...] + _dot(merged.astype(BF16), wo_ref[...])


def _sample_mixer(x, z, h, layer, gv, ws0, bs0, wa, wb, wo):
    n = x.shape[0]
    out_shape = (jax.ShapeDtypeStruct((n, D_MODEL), F32), jax.ShapeDtypeStruct((n, D_B), F32))
    return _single_step(_sample_mixer_kernel, "sample_mixer", out_shape,
                        (x, z, h), (gv, ws0, bs0, wa, wb, wo), layer)


def _sample_ffn_kernel(x_ref, gf_ref, buf_ref, g2_ref, wup_ref, cw_ref, cb_ref, wdn_ref, *rest, final):
    xo_ref, nbuf_ref = rest[-2:]
    x = x_ref[...]
    xn = _rmsnorm(x, g2_ref[...]).astype(BF16)
    acc = jnp.zeros(x.shape, F32)
    for j in range(D_FF // FFN_COL_TILE):
        halves = []
        for half in range(2):
            col0 = half * D_FF + j * FFN_COL_TILE
            cols = slice(col0, col0 + FFN_COL_TILE)
            up = _dot(xn, wup_ref[:, cols])
            b0 = buf_ref[:, 0, cols]
            b1 = buf_ref[:, 1, cols]
            nbuf_ref[:, 0, cols] = b1
            nbuf_ref[:, 1, cols] = up
            halves.append(cb_ref[:, cols] + cw_ref[0:1, cols] * b0 + cw_ref[1:2, cols] * b1
                          + cw_ref[2:3, cols] * up)
        act = (jax.nn.silu(halves[0]) * halves[1]).astype(BF16)
        acc = acc + _dot(act, wdn_ref[j * FFN_COL_TILE:(j + 1) * FFN_COL_TILE, :])
    y = x + acc
    if final:
        y = _rmsnorm(y, gf_ref[...])
    xo_ref[...] = y


def _sample_ffn(x, gf, layer, buf, g2, wup, cw, cb, wdn, final, conv_stack):
    n = x.shape[0]
    out_shape = (jax.ShapeDtypeStruct((n, D_MODEL), F32), jax.ShapeDtypeStruct(buf.shape, F32))
    return _single_step(functools.partial(_sample_ffn_kernel, final=final), "sample_ffn", out_shape,
                        (x, gf), (buf, g2, wup, cw, cb, wdn), layer, stacked=(1, conv_stack))


def _pack_kernel(wt_ref, o_ref, wgt_ref):
    g8 = wt_ref[GATE_LO:GATE_HI, :]
    head = lax.broadcasted_iota(jnp.int32, g8.shape, 0) < N_HEADS
    wgt_ref[...] = jnp.concatenate(
        [jnp.where(head, g8, 0.0), jnp.where(head, pltpu.roll(g8, N_HEADS, 0), 0.0)], axis=0).astype(BF16)
    for j in range(P_MAIN // PACK_TILE):
        dst = j * PACK_TILE
        src = dst if dst < GATE_LO else dst + (GATE_HI - GATE_LO)
        o_ref[:, dst:dst + PACK_TILE] = wt_ref[src:src + PACK_TILE, :].T.astype(BF16)


def _pack_w_in(w_in_t):
    depth, p_in, d = w_in_t.shape
    return pl.pallas_call(
        _pack_kernel,
        grid=(depth, d // PACK_DIMS),
        in_specs=[pl.BlockSpec((None, p_in, PACK_DIMS), lambda l, r: (l, 0, r))],
        out_specs=(pl.BlockSpec((None, PACK_DIMS, P_MAIN), lambda l, r: (l, r, 0)),
                   pl.BlockSpec((None, 2 * GATE_ROWS, PACK_DIMS), lambda l, r: (l, 0, r))),
        out_shape=(jax.ShapeDtypeStruct((depth, d, P_MAIN), BF16),
                   jax.ShapeDtypeStruct((depth, 2 * GATE_ROWS, d), BF16)),
        compiler_params=pltpu.CompilerParams(
            dimension_semantics=("arbitrary", "arbitrary"), vmem_limit_bytes=V7X_VMEM_LIMIT),
        name="pack_w_in",
    )(w_in_t)


def kernel(x_prompt, x_sample, state_mlstm_C, state_mlstm_n, state_mlstm_m, state_ffn_conv, w_in, b_igate, b_fgate, g_norm1, g_vnorm, w_spatial, b_spatial, w_branch_a, w_branch_b, w_out, g_norm2, w_up, conv_w, conv_b, w_down, g_final):
    depth = w_in.shape[0]
    n_dec = x_sample.shape[0]
    xp = x_prompt
    xs = x_sample.reshape(n_dec, D_MODEL)
    gf = g_final.reshape(1, D_MODEL)

    wm, wgt = _pack_w_in(jnp.swapaxes(w_in, 1, 2))
    head_pad = ((0, 0), (0, GATE_ROWS - N_HEADS))
    gbias_col = jnp.concatenate([jnp.pad(b_igate, head_pad), jnp.pad(b_fgate, head_pad)],
                                axis=1).reshape(depth, 2 * GATE_ROWS, 1)
    head_lanes = ((0, 0), (0, V7X_LANES - N_HEADS))
    gbias_row = jnp.concatenate([jnp.pad(b_igate, head_lanes), jnp.pad(b_fgate, head_lanes)],
                                axis=1).reshape(depth, 1, 2 * V7X_LANES)
    g1 = g_norm1.reshape(depth, 1, D_MODEL)
    gv = g_vnorm.reshape(depth, 1, D_B)
    g2 = g_norm2.reshape(depth, 1, D_MODEL)
    bst = jnp.swapaxes(b_spatial, 1, 2)
    ws0 = jnp.repeat(w_spatial[:, :, 0, 0], DG, axis=1).reshape(depth, 1, D_B)
    bs0 = jnp.repeat(b_spatial[:, :, 0], DG, axis=1).reshape(depth, 1, D_B)
    wa = w_branch_a.astype(BF16)
    wb = w_branch_b.astype(BF16)
    wo = w_out.astype(BF16)
    wup = w_up.astype(BF16)
    wdn = w_down.astype(BF16)
    cb = conv_b.reshape(depth, 1, 2 * D_FF)
    n_state = state_mlstm_n.reshape(depth, n_dec, N_HEADS * DK)
    m_pad = jnp.pad(state_mlstm_m, ((0, 0), (0, 0), (0, V7X_LANES - N_HEADS)))

    small = [[] for _ in range(7)]
    c_stack = None
    conv_stack = None
    for l in range(depth):
        final = l == depth - 1

        xp, c_p, n_p, m_p = _prompt_mixer(xp, l, wm, wgt, gbias_col, g1, gv, w_spatial, bst, wa, wb, wo)
        xp, conv_p = _prompt_ffn(xp, l, g2, wup, conv_w, cb, wdn, gf, final)

        z, gates = _sample_proj(xs, l, g1, wm, wgt)
        h, c_stack, n_s, m_s = _sample_state(
            z, gates, gbias_row[l], m_pad[l], n_state[l], state_mlstm_C, l, c_stack)
        xs, vn_s = _sample_mixer(xs, z, h, l, gv, ws0, bs0, wa, wb, wo)
        xs, conv_stack = _sample_ffn(xs, gf, l, state_ffn_conv, g2, wup, conv_w, cb, wdn, final, conv_stack)

        for lst, val in zip(small, (c_p, n_p, m_p[:, :N_HEADS, 0], conv_p,
                                    n_s.reshape(n_dec, N_HEADS, DK), m_s[:, :N_HEADS],
                                    vn_s.reshape(n_dec, 1, D_B))):
            lst.append(val)
    st = [jnp.stack(o) for o in small]
    return (xp, xs.reshape(n_dec, 1, D_MODEL), st[0], st[1], st[2], st[3], c_stack, st[4], st[5], conv_stack, st[6])
```

```python
import functools

import jax
import jax.numpy as jnp
from jax import lax
from jax.experimental import pallas as pl
from jax.experimental.pallas import tpu as pltpu

D_MODEL = 1024
N_HEADS = 4
DK = 128
DV = 256
CHUNK = 128
D_B = 1024
N_GROUPS = 4
DG = D_B // N_GROUPS
D_FF = 2816
CONV_W = 3
EPS = 1e-6
K_SCALE = DK ** -0.5

OFF_Q = 0
OFF_K = OFF_Q + N_HEADS * DK
OFF_V = OFF_K + N_HEADS * DK
OFF_O = OFF_V + N_HEADS * DV
OFF_U = OFF_O + N_HEADS * DV
OFF_VB = OFF_U + D_B
OFF_GA = OFF_VB + D_B
OFF_GB = OFF_GA + D_MODEL
P_MAIN = OFF_GB + D_MODEL
GATE_LO = 2 * N_HEADS * DK + 2 * N_HEADS * DV
GATE_HI = GATE_LO + 2 * N_HEADS

V7X_LANES = 128
V7X_SUBLANES = 8
GATE_ROWS = V7X_SUBLANES
V7X_VMEM_LIMIT = 56 * 1024 * 1024
FFN_COL_TILE = 256
PROJ_TILE = 256
MIXER_BLOCK = 512
FFN_BLOCK = 512
SAMPLE_BLOCK = 16
PACK_TILE = 256
PACK_DIMS = 256

F32 = jnp.float32
BF16 = jnp.bfloat16


def _dot(a, b):
    return jnp.dot(a, b, preferred_element_type=F32)


def _dot_nt(a, b):
    return lax.dot_general(a, b, (((1,), (1,)), ((), ())), preferred_element_type=F32)


def _rmsnorm(x, g):
    r = lax.rsqrt(jnp.mean(x * x, axis=-1, keepdims=True) + EPS)
    return x * r * g


def _log_sigmoid(x):
    return jnp.minimum(x, 0.0) - jnp.log1p(jnp.exp(-jnp.abs(x)))


def _scan_lanes(x, op, fill):
    lane = lax.broadcasted_iota(jnp.int32, x.shape, 1)
    k = 1
    while k < x.shape[1]:
        shifted = pltpu.roll(x, k, 1)
        x = op(x, jnp.where(lane >= k, shifted, fill))
        k *= 2
    return x


def _anchor_zero(x):
    sub, lanes = V7X_SUBLANES, V7X_LANES
    acc = jnp.zeros((sub, lanes), jnp.uint32)
    for r in range(x.shape[0] // sub):
        for c in range(x.shape[1] // lanes):
            piece = pltpu.bitcast(x[r * sub:(r + 1) * sub, c * lanes:(c + 1) * lanes], jnp.uint32)
            acc = acc | ((piece >> 16) >> 16)
    return pltpu.bitcast(acc, F32)


def _next_block_map(nb, steps):
    def index_map(b, s):
        nxt = jnp.minimum(b * steps + s + 1, nb * steps - 1)
        return (nxt // steps, nxt % steps, 0)
    return index_map


def _mixer_kernel(x_ref, wm_ref, wgt_ref, gbias_ref, g1_ref, gv_ref, ws_ref, bst_ref,
                  wa_ref, wb_ref, wo_ref,
                  xo_ref, c_ref, n_ref, m_ref,
                  ct_s, n_s, m_s, q_s, k_s, v_s, h_s, so_s, u_s, vb_s, sg_s, um_s, *, block):
    s = pl.program_id(1)
    n_chunks = block // CHUNK

    @pl.when(s == 0)
    def _():
        ct_s[...] = jnp.zeros_like(ct_s)
        n_s[...] = jnp.zeros_like(n_s)
        m_s[...] = jnp.zeros_like(m_s)

    xn = _rmsnorm(x_ref[...], g1_ref[...]).astype(BF16)

    gates = _dot_nt(wgt_ref[...], xn) + gbias_ref[...]

    def proj(off, t):
        return _dot(xn, wm_ref[:, off + t * PROJ_TILE:off + (t + 1) * PROJ_TILE])

    def tile(t):
        return slice(t * PROJ_TILE, (t + 1) * PROJ_TILE)

    for t in range(N_HEADS * DK // PROJ_TILE):
        q_s[:, tile(t)] = proj(OFF_Q, t)
        k_s[:, tile(t)] = proj(OFF_K, t) * K_SCALE
    for t in range(N_HEADS * DV // PROJ_TILE):
        v_s[:, tile(t)] = proj(OFF_V, t)

    sumsq = [jnp.zeros((block, 1), F32)]

    def vb_tile(t):
        g = jax.nn.gelu(proj(OFF_VB, t))
        vb_s[:, tile(t)] = g
        sumsq[0] = sumsq[0] + jnp.sum(g * g, axis=-1, keepdims=True)

    def u_tile(t):
        u_s[:, tile(t)] = jax.nn.gelu(proj(OFF_U, t))

    def o_tile(t):
        so_s[:, tile(t)] = jax.nn.sigmoid(proj(OFF_O, t))

    def ga_tile(t):
        sg_s[:, tile(t)] = jax.nn.sigmoid(proj(OFF_GA, t))

    def gb_tile(t):
        sg_s[:, D_MODEL + t * PROJ_TILE:D_MODEL + (t + 1) * PROJ_TILE] = jax.nn.sigmoid(proj(OFF_GB, t))

    jobs = [(f, t) for f in (vb_tile, u_tile, o_tile, ga_tile, gb_tile) for t in range(D_MODEL // PROJ_TILE)]

    def run_jobs(count):
        for _ in range(min(count, len(jobs))):
            f, t = jobs.pop(0)
            f(t)

    row_i = lax.broadcasted_iota(jnp.int32, (CHUNK, CHUNK), 0)
    col_i = lax.broadcasted_iota(jnp.int32, (CHUNK, CHUNK), 1)
    causal = row_i >= col_i
    heads = range(N_HEADS)

    for c in range(n_chunks):
        r0 = c * CHUNK
        ig = gates[0:GATE_ROWS, r0:r0 + CHUNK]
        lf = _log_sigmoid(gates[GATE_ROWS:2 * GATE_ROWS, r0:r0 + CHUNK])
        b = _scan_lanes(lf, jnp.add, 0.0)
        a = ig - b
        m_prev = m_s[...]
        gmax = jnp.maximum(m_prev, _scan_lanes(a, jnp.maximum, -jnp.inf))
        m_t = b + gmax
        w_inter = jnp.exp(m_prev - gmax)
        g_last = gmax[:, CHUNK - 1:CHUNK]
        w_last = jnp.exp(a - g_last)
        floor = jnp.exp(-m_t)
        decay = w_inter[:, CHUNK - 1:CHUNK]
        m_s[...] = jnp.broadcast_to(m_t[:, CHUNK - 1:CHUNK], m_s.shape)

        rows = jnp.concatenate(
            [gmax, w_inter, w_last, floor,
             jnp.zeros((CHUNK - 4 * GATE_ROWS, CHUNK), F32)], axis=0)
        cols = rows.T

        def col(kind, h):
            return cols[:, kind * GATE_ROWS + h:kind * GATE_ROWS + h + 1]

        qf = [q_s[r0:r0 + CHUNK, h * DK:(h + 1) * DK] for h in heads]
        kf = [k_s[r0:r0 + CHUNK, h * DK:(h + 1) * DK] for h in heads]
        vf = [v_s[r0:r0 + CHUNK, h * DV:(h + 1) * DV] for h in heads]
        qb = [x.astype(BF16) for x in qf]
        kt = [x.T.astype(BF16) for x in kf]
        run_jobs(2)
        zero = jnp.zeros((DK, CHUNK), BF16)
        sc = []
        for h in range(0, N_HEADS, 2):
            kk = jnp.concatenate([jnp.concatenate([kt[h], zero], axis=1),
                                  jnp.concatenate([zero, kt[h + 1]], axis=1)], axis=0)
            pair = _dot(jnp.concatenate([qb[h], qb[h + 1]], axis=1), kk)
            sc += [pair[:, :CHUNK], pair[:, CHUNK:]]
        dmat = [jnp.where(causal, jnp.exp(a[h:h + 1, :] - col(0, h)), 0.0) for h in heads]
        run_jobs(1)
        sd = [sc[h] * dmat[h] for h in heads]
        ct = [ct_s[h] for h in heads]
        num = [_dot(jnp.concatenate([sd[h], col(1, h) * qf[h]], axis=1).astype(BF16),
                    jnp.concatenate([vf[h], ct[h]], axis=0).astype(BF16)) for h in heads]
        run_jobs(2)
        for h in heads:
            nh = n_s[h:h + 1, :]
            den = (jnp.sum(sd[h], axis=-1, keepdims=True)
                   + col(1, h) * jnp.sum(qf[h] * nh, axis=-1, keepdims=True))
            h_s[r0:r0 + CHUNK, h * DV:(h + 1) * DV] = num[h] * (1.0 / jnp.maximum(jnp.abs(den), col(3, h)))
            dec = decay[h:h + 1, :]
            n_s[h:h + 1, :] = dec * nh + jnp.sum(col(2, h) * kf[h], axis=0, keepdims=True)
        run_jobs(1)
        for h in heads:
            ct_s[h] = decay[h:h + 1, :] * ct[h] + _dot(kt[h], (col(2, h) * vf[h]).astype(BF16))
    run_jobs(len(jobs))

    y_a = _dot((so_s[...] * h_s[...]).astype(BF16), wa_ref[...])

    rinv = lax.rsqrt(sumsq[0] * (1.0 / D_B) + EPS)
    for g in range(N_GROUPS):
        gcols = slice(g * DG, (g + 1) * DG)
        w_tri = jnp.where(causal, ws_ref[g], 0.0).astype(BF16)
        bias_c = bst_ref[:, g:g + 1]
        for c in range(n_chunks):
            rws = slice(c * CHUNK, (c + 1) * CHUNK)
            vn = vb_s[rws, gcols] * rinv[rws] * gv_ref[:, gcols]
            um_s[rws, gcols] = u_s[rws, gcols] * (_dot(w_tri, vn.astype(BF16)) + bias_c)
    y_b = _dot(um_s[...].astype(BF16), wb_ref[...])

    merged = sg_s[:, 0:D_MODEL] * y_a + sg_s[:, D_MODEL:2 * D_MODEL] * y_b
    xo_ref[...] = x_ref[...] + _dot(merged.astype(BF16), wo_ref[...])

    @pl.when(s == pl.num_programs(1) - 1)
    def _():
        for h in range(N_HEADS):
            c_ref[h] = ct_s[h].T
        n_ref[...] = n_s[...]
        m_ref[...] = m_s[...]


def _resident(arr, layer=None):
    if layer is None:
        nd = arr.ndim
        return pl.BlockSpec(arr.shape, lambda *_: (0,) * nd, pipeline_mode=pl.Buffered(1))
    nd = arr.ndim - 1
    return pl.BlockSpec((None,) + arr.shape[1:], lambda *_: (layer,) + (0,) * nd,
                        pipeline_mode=pl.Buffered(1))


def _prompt_mixer(x, layer, wm, wgt, gbias, g1, gv, ws, bst, wa, wb, wo):
    nb, seq, _ = x.shape
    block = MIXER_BLOCK
    grid = (nb, seq // block)
    xspec = pl.BlockSpec((None, block, D_MODEL), lambda b, s: (b, s, 0))
    out_shape = (
        jax.ShapeDtypeStruct((nb, seq, D_MODEL), F32),
        jax.ShapeDtypeStruct((nb, N_HEADS, DV, DK), F32),
        jax.ShapeDtypeStruct((nb, N_HEADS, DK), F32),
        jax.ShapeDtypeStruct((nb, GATE_ROWS, V7X_LANES), F32),
    )
    out_specs = (
        xspec,
        pl.BlockSpec((None, N_HEADS, DV, DK), lambda b, s: (b, 0, 0, 0)),
        pl.BlockSpec((None, N_HEADS, DK), lambda b, s: (b, 0, 0)),
        pl.BlockSpec((None, GATE_ROWS, V7X_LANES), lambda b, s: (b, 0, 0)),
    )
    in_specs = [xspec] + [_resident(a, layer) for a in (wm, wgt, gbias, g1, gv, ws, bst, wa, wb, wo)]
    scratch = [
        pltpu.VMEM((N_HEADS, DK, DV), F32),
        pltpu.VMEM((N_HEADS, DK), F32),
        pltpu.VMEM((GATE_ROWS, V7X_LANES), F32),
        pltpu.VMEM((block, N_HEADS * DK), F32),
        pltpu.VMEM((block, N_HEADS * DK), F32),
        pltpu.VMEM((block, N_HEADS * DV), F32),
        pltpu.VMEM((block, N_HEADS * DV), F32),
        pltpu.VMEM((block, N_HEADS * DV), F32),
        pltpu.VMEM((block, D_B), F32),
        pltpu.VMEM((block, D_B), F32),
        pltpu.VMEM((block, 2 * D_MODEL), F32),
        pltpu.VMEM((block, D_B), F32),
    ]
    return pl.pallas_call(
        functools.partial(_mixer_kernel, block=block),
        grid=grid, in_specs=in_specs, out_specs=out_specs, out_shape=out_shape,
        scratch_shapes=scratch,
        compiler_params=pltpu.CompilerParams(
            dimension_semantics=("arbitrary", "arbitrary"), vmem_limit_bytes=V7X_VMEM_LIMIT),
        name="prompt_mixer",
    )(x, wm, wgt, gbias, g1, gv, ws, bst, wa, wb, wo)


def _conv_taps(up, carry_s, cw_ref, cb_ref, cols):
    sub = V7X_SUBLANES
    rows = up.shape[0]
    last1 = up[rows - sub:rows]
    last2 = up[rows - 2 * sub:rows - sub]
    first = lax.broadcasted_iota(jnp.int32, last1.shape, 0) == 0
    back1 = jnp.where(first, carry_s[sub - 1:sub, cols], pltpu.roll(last1, 1, 0))
    back2 = jnp.where(first, carry_s[sub - 2:sub - 1, cols], pltpu.roll(last2, 1, 0))
    carry_s[sub - 2:sub - 1, cols] = last2[sub - 1:sub]
    carry_s[sub - 1:sub, cols] = last1[sub - 1:sub]
    m1 = jnp.concatenate([back1, up[0:rows - sub]], axis=0)
    m2 = jnp.concatenate([back2, back1, up[0:rows - 2 * sub]], axis=0)
    return (cb_ref[:, cols] + cw_ref[0:1, cols] * m2 + cw_ref[1:2, cols] * m1
            + cw_ref[2:3, cols] * up)


def _perm_pitch(block):
    return block // V7X_SUBLANES + V7X_SUBLANES


def _ffn_kernel(x_ref, xnext_ref, g2_ref, wup_ref, cw_ref, cb_ref, wdn_ref, gf_ref,
                xo_ref, conv_ref, carry_s, act_s, perm_s, unperm_s, xn_s, *, block, final):
    s = pl.program_id(1)
    sub, lanes = V7X_SUBLANES, V7X_LANES
    groups = block // sub
    chunks = D_MODEL // lanes
    pitch = _perm_pitch(block)

    @pl.when(s == 0)
    def _():
        carry_s[...] = jnp.zeros_like(carry_s)

    def stage(src_ref):
        for c in range(chunks):
            for i in range(sub):
                perm_s[c, i * pitch:i * pitch + groups] = src_ref[i * groups:(i + 1) * groups,
                                                                  c * lanes:(c + 1) * lanes]
        x = jnp.concatenate(
            [jnp.concatenate([perm_s[c, pl.ds(r, sub, stride=pitch), :] for c in range(chunks)], axis=1)
             for r in range(groups)], axis=0)
        xn = _rmsnorm(x, g2_ref[...])
        xn_s[...] = xn.astype(BF16)
        return xn

    @pl.when((pl.program_id(0) == 0) & (s == 0))
    def _():
        stage(x_ref)

    xn = xn_s[...]
    for j in range(D_FF // FFN_COL_TILE):
        halves = []
        for half in range(2):
            cols = slice(half * D_FF + j * FFN_COL_TILE, half * D_FF + (j + 1) * FFN_COL_TILE)
            up = _dot(xn, wup_ref[:, cols])
            halves.append(_conv_taps(up, carry_s, cw_ref, cb_ref, cols))
        act_s[:, j * FFN_COL_TILE:(j + 1) * FFN_COL_TILE] = (jax.nn.silu(halves[0]) * halves[1]).astype(BF16)
    staged = stage(xnext_ref)
    down = _dot(act_s[...], wdn_ref[...])
    anchor = _anchor_zero(staged)
    for r in range(groups):
        for c in range(chunks):
            tile_rc = down[r * sub:(r + 1) * sub, c * lanes:(c + 1) * lanes]
            if r == 0 and c == 0:
                tile_rc = tile_rc + anchor
            unperm_s[c, pl.ds(r, sub, stride=pitch), :] = tile_rc
    for i in range(sub):
        rows = slice(i * groups, (i + 1) * groups)
        y = x_ref[rows, :] + jnp.concatenate(
            [unperm_s[c, i * pitch:i * pitch + groups] for c in range(chunks)], axis=1)
        if final:
            y = _rmsnorm(y, gf_ref[...])
        xo_ref[rows, :] = y

    @pl.when(s == pl.num_programs(1) - 1)
    def _():
        conv_ref[...] = carry_s[V7X_SUBLANES - (CONV_W - 1):V7X_SUBLANES, :]


def _prompt_ffn(x, layer, g2, wup, cw, cb, wdn, gf, final):
    nb, seq, _ = x.shape
    block = FFN_BLOCK
    steps = seq // block
    grid = (nb, steps)
    xspec = pl.BlockSpec((None, block, D_MODEL), lambda b, s: (b, s, 0))
    out_shape = (
        jax.ShapeDtypeStruct((nb, seq, D_MODEL), F32),
        jax.ShapeDtypeStruct((nb, CONV_W - 1, 2 * D_FF), F32),
    )
    out_specs = (xspec, pl.BlockSpec((None, CONV_W - 1, 2 * D_FF), lambda b, s: (b, 0, 0)))
    in_specs = ([xspec, pl.BlockSpec((None, block, D_MODEL), _next_block_map(nb, steps))]
                + [_resident(a, layer) for a in (g2, wup, cw, cb, wdn)] + [_resident(gf)])
    reorder = pltpu.VMEM((D_MODEL // V7X_LANES, V7X_SUBLANES * _perm_pitch(block), V7X_LANES), F32)
    return pl.pallas_call(
        functools.partial(_ffn_kernel, block=block, final=final),
        grid=grid, in_specs=in_specs, out_specs=out_specs, out_shape=out_shape,
        scratch_shapes=[pltpu.VMEM((V7X_SUBLANES, 2 * D_FF), F32),
                        pltpu.VMEM((block, D_FF), BF16),
                        reorder, reorder,
                        pltpu.VMEM((block, D_MODEL), BF16)],
        compiler_params=pltpu.CompilerParams(
            dimension_semantics=("arbitrary", "arbitrary"), vmem_limit_bytes=V7X_VMEM_LIMIT),
        name="prompt_ffn",
    )(x, x, g2, wup, cw, cb, wdn, gf)


def _sample_proj_kernel(x_ref, g1_ref, wm_ref, wgt_ref, z_ref, gates_ref):
    xn = _rmsnorm(x_ref[...], g1_ref[...]).astype(BF16)
    z_ref[...] = _dot(xn, wm_ref[...])
    gt = _dot_nt(wgt_ref[...], xn)
    gt = jnp.concatenate([gt, jnp.zeros((V7X_LANES - gt.shape[0], gt.shape[1]), F32)], axis=0)
    g = gt.T
    gates_ref[...] = jnp.concatenate([g, pltpu.roll(g, V7X_LANES - GATE_ROWS, 1)], axis=1)


def _single_step(kernel_fn, name, out_shape, whole, layered, layer, stacked=None):
    in_specs = [_resident(a) for a in whole] + [_resident(a, layer) for a in layered]
    args = list(whole) + list(layered)
    out_specs = [pl.BlockSpec(o.shape, lambda i, nd=len(o.shape): (0,) * nd) for o in out_shape]
    aliases = {}
    if stacked is not None:
        k, prev = stacked
        nd = len(out_shape[k].shape) - 1
        out_specs[k] = pl.BlockSpec((None,) + out_shape[k].shape[1:], lambda i: (layer,) + (0,) * nd)
        if prev is not None:
            in_specs.append(pl.BlockSpec(memory_space=pl.ANY))
            aliases = {len(args): k}
            args.append(prev)
    return pl.pallas_call(
        kernel_fn,
        grid=(1,),
        in_specs=in_specs, out_specs=tuple(out_specs), out_shape=out_shape,
        input_output_aliases=aliases,
        compiler_params=pltpu.CompilerParams(
            dimension_semantics=("arbitrary",), vmem_limit_bytes=V7X_VMEM_LIMIT),
        name=name,
    )(*args)


def _sample_proj(x, layer, g1, wm, wgt):
    n = x.shape[0]
    assert n == V7X_LANES, "the gate transpose assumes one lane tile of sample rows"
    out_shape = (jax.ShapeDtypeStruct((n, P_MAIN), F32), jax.ShapeDtypeStruct((n, 2 * V7X_LANES), F32))
    return _single_step(_sample_proj_kernel, "sample_proj", out_shape, (x,), (g1, wm, wgt), layer)


def _sample_state_kernel(q_ref, k_ref, v_ref, gates_ref, gbias_ref, m_ref, n_ref, c_ref, *rest):
    h_ref, co_ref, no_ref, mo_ref = rest[-4:]
    tb = SAMPLE_BLOCK
    ig = gates_ref[:, 0:V7X_LANES] + gbias_ref[:, 0:V7X_LANES]
    lf = _log_sigmoid(gates_ref[:, V7X_LANES:] + gbias_ref[:, V7X_LANES:])
    inter = lf + m_ref[...]
    m_t = jnp.maximum(inter, ig)
    d_in = jnp.exp(ig - m_t)
    w_inter = jnp.exp(inter - m_t)
    floor = jnp.exp(-m_t)
    mo_ref[...] = m_t

    row8 = lax.broadcasted_iota(jnp.int32, (CHUNK, DK), 0)
    for h in range(N_HEADS):
        q8 = q_ref[:, h * DK:(h + 1) * DK]
        k8 = k_ref[:, h * DK:(h + 1) * DK] * K_SCALE
        v8 = v_ref[:, h * DV:(h + 1) * DV]
        d_h = d_in[:, h:h + 1]
        w_h = w_inter[:, h:h + 1]
        n8 = n_ref[:, h * DK:(h + 1) * DK]
        s = jnp.sum(q8 * k8, axis=-1, keepdims=True) * d_h
        den = s + w_h * jnp.sum(q8 * n8, axis=-1, keepdims=True)
        qb = q8.astype(BF16)
        inter_rows = [_dot_nt(qb, c_ref[j, h].astype(BF16))[j:j + 1] for j in range(tb)]
        num = s * v8 + w_h * jnp.concatenate(inter_rows, axis=0)
        h_ref[:, h * DV:(h + 1) * DV] = num / jnp.maximum(jnp.abs(den), floor[:, h:h + 1])
        no_ref[:, h * DK:(h + 1) * DK] = w_h * n8 + d_h * k8

        vt = jnp.concatenate([d_h * v8, jnp.zeros((CHUNK - tb, DV), F32)], axis=0).T.astype(BF16)
        kpad = jnp.concatenate([k8, jnp.zeros((CHUNK - tb, DK), F32)], axis=0)
        for j in range(tb):
            kj = jnp.where(row8 == j, kpad, 0.0).astype(BF16)
            co_ref[j, h] = w_inter[j:j + 1, h:h + 1] * c_ref[j, h] + _dot(vt, kj)


def _sample_state(z, gates, gbias, m_pad, n_state, c_state, layer, c_stack):
    n = z.shape[0]
    tb = SAMPLE_BLOCK
    c_block = pl.BlockSpec((None, tb, N_HEADS, DV, DK), lambda i: (layer, i, 0, 0, 0))
    in_specs = [
        pl.BlockSpec((tb, N_HEADS * DK), lambda i: (i, OFF_Q // (N_HEADS * DK))),
        pl.BlockSpec((tb, N_HEADS * DK), lambda i: (i, OFF_K // (N_HEADS * DK))),
        pl.BlockSpec((tb, N_HEADS * DV), lambda i: (i, OFF_V // (N_HEADS * DV))),
        pl.BlockSpec((tb, 2 * V7X_LANES), lambda i: (i, 0)),
        pl.BlockSpec((1, 2 * V7X_LANES), lambda i: (0, 0)),
        pl.BlockSpec((tb, V7X_LANES), lambda i: (i, 0)),
        pl.BlockSpec((tb, N_HEADS * DK), lambda i: (i, 0)),
        c_block,
    ]
    args = [z, z, z, gates, gbias, m_pad, n_state, c_state]
    aliases = {}
    if c_stack is not None:
        in_specs.append(pl.BlockSpec(memory_space=pl.ANY))
        aliases = {len(args): 1}
        args.append(c_stack)
    out_shape = (
        jax.ShapeDtypeStruct((n, N_HEADS * DV), F32),
        jax.ShapeDtypeStruct(c_state.shape, F32),
        jax.ShapeDtypeStruct((n, N_HEADS * DK), F32),
        jax.ShapeDtypeStruct((n, V7X_LANES), F32),
    )
    out_specs = (
        pl.BlockSpec((tb, N_HEADS * DV), lambda i: (i, 0)),
        c_block,
        pl.BlockSpec((tb, N_HEADS * DK), lambda i: (i, 0)),
        pl.BlockSpec((tb, V7X_LANES), lambda i: (i, 0)),
    )
    return pl.pallas_call(
        _sample_state_kernel,
        grid=(n // tb,), in_specs=in_specs, out_specs=out_specs, out_shape=out_shape,
        input_output_aliases=aliases,
        compiler_params=pltpu.CompilerParams(
            dimension_semantics=("arbitrary",), vmem_limit_bytes=V7X_VMEM_LIMIT),
        name="sample_state",
    )(*args)


def _sample_mixer_kernel(x_ref, z_ref, h_ref, gv_ref, ws0_ref, bs0_ref, wa_ref, wb_ref, wo_ref,
                         xo_ref, vn_ref):
    y_a = _dot((jax.nn.sigmoid(z_ref[:, OFF_O:OFF_U]) * h_ref[...]).astype(BF16), wa_ref[...])
    u = jax.nn.gelu(z_ref[:, OFF_U:OFF_VB])
    vn = _rmsnorm(jax.nn.gelu(z_ref[:, OFF_VB:OFF_GA]), gv_ref[...])
    vn_ref[...] = vn
    mixed = ws0_ref[...] * vn + bs0_ref[...]
    y_b = _dot((u * mixed).astype(BF16), wb_ref[...])
    merged = (jax.nn.sigmoid(z_ref[:, OFF_GA:OFF_GB]) * y_a
              + jax.nn.sigmoid(z_ref[:, OFF_GB:P_MAIN]) * y_b)
    xo_ref[...] = x_ref[...] + _dot(merged.astype(BF16), wo_ref[...])


def _sample_mixer(x, z, h, layer, gv, ws0, bs0, wa, wb, wo):
    n = x.shape[0]
    out_shape = (jax.ShapeDtypeStruct((n, D_MODEL), F32), jax.ShapeDtypeStruct((n, D_B), F32))
    return _single_step(_sample_mixer_kernel, "sample_mixer", out_shape,
                        (x, z, h), (gv, ws0, bs0, wa, wb, wo), layer)


def _sample_ffn_kernel(x_ref, gf_ref, buf_ref, g2_ref, wup_ref, cw_ref, cb_ref, wdn_ref, *rest, final):
    xo_ref, nbuf_ref = rest[-2:]
    x = x_ref[...]
    xn = _rmsnorm(x, g2_ref[...]).astype(BF16)
    acc = jnp.zeros(x.shape, F32)
    for j in range(D_FF // FFN_COL_TILE):
        halves = []
        for half in range(2):
            col0 = half * D_FF + j * FFN_COL_TILE
            cols = slice(col0, col0 + FFN_COL_TILE)
            up = _dot(xn, wup_ref[:, cols])
            b0 = buf_ref[:, 0, cols]
            b1 = buf_ref[:, 1, cols]
            nbuf_ref[:, 0, cols] = b1
            nbuf_ref[:, 1, cols] = up
            halves.append(cb_ref[:, cols] + cw_ref[0:1, cols] * b0 + cw_ref[1:2, cols] * b1
                          + cw_ref[2:3, cols] * up)
        act = (jax.nn.silu(halves[0]) * halves[1]).astype(BF16)
        acc = acc + _dot(act, wdn_ref[j * FFN_COL_TILE:(j + 1) * FFN_COL_TILE, :])
    y = x + acc
    if final:
        y = _rmsnorm(y, gf_ref[...])
    xo_ref[...] = y


def _sample_ffn(x, gf, layer, buf, g2, wup, cw, cb, wdn, final, conv_stack):
    n = x.shape[0]
    out_shape = (jax.ShapeDtypeStruct((n, D_MODEL), F32), jax.ShapeDtypeStruct(buf.shape, F32))
    return _single_step(functools.partial(_sample_ffn_kernel, final=final), "sample_ffn", out_shape,
                        (x, gf), (buf, g2, wup, cw, cb, wdn), layer, stacked=(1, conv_stack))


def _pack_kernel(wt_ref, o_ref, wgt_ref):
    g8 = wt_ref[GATE_LO:GATE_HI, :]
    head = lax.broadcasted_iota(jnp.int32, g8.shape, 0) < N_HEADS
    wgt_ref[...] = jnp.concatenate(
        [jnp.where(head, g8, 0.0), jnp.where(head, pltpu.roll(g8, N_HEADS, 0), 0.0)], axis=0).astype(BF16)
    for j in range(P_MAIN // PACK_TILE):
        dst = j * PACK_TILE
        src = dst if dst < GATE_LO else dst + (GATE_HI - GATE_LO)
        o_ref[:, dst:dst + PACK_TILE] = wt_ref[src:src + PACK_TILE, :].T.astype(BF16)


def _pack_w_in(w_in_t):
    depth, p_in, d = w_in_t.shape
    return pl.pallas_call(
        _pack_kernel,
        grid=(depth, d // PACK_DIMS),
        in_specs=[pl.BlockSpec((None, p_in, PACK_DIMS), lambda l, r: (l, 0, r))],
        out_specs=(pl.BlockSpec((None, PACK_DIMS, P_MAIN), lambda l, r: (l, r, 0)),
                   pl.BlockSpec((None, 2 * GATE_ROWS, PACK_DIMS), lambda l, r: (l, 0, r))),
        out_shape=(jax.ShapeDtypeStruct((depth, d, P_MAIN), BF16),
                   jax.ShapeDtypeStruct((depth, 2 * GATE_ROWS, d), BF16)),
        compiler_params=pltpu.CompilerParams(
            dimension_semantics=("arbitrary", "arbitrary"), vmem_limit_bytes=V7X_VMEM_LIMIT),
        name="pack_w_in",
    )(w_in_t)


def kernel(x_prompt, x_sample, state_mlstm_C, state_mlstm_n, state_mlstm_m, state_ffn_conv, w_in, b_igate, b_fgate, g_norm1, g_vnorm, w_spatial, b_spatial, w_branch_a, w_branch_b, w_out, g_norm2, w_up, conv_w, conv_b, w_down, g_final):
    depth = w_in.shape[0]
    n_dec = x_sample.shape[0]
    xp = x_prompt
    xs = x_sample.reshape(n_dec, D_MODEL)
    gf = g_final.reshape(1, D_MODEL)

    wm, wgt = _pack_w_in(jnp.swapaxes(w_in, 1, 2))
    head_pad = ((0, 0), (0, GATE_ROWS - N_HEADS))
    gbias_col = jnp.concatenate([jnp.pad(b_igate, head_pad), jnp.pad(b_fgate, head_pad)],
                                axis=1).reshape(depth, 2 * GATE_ROWS, 1)
    head_lanes = ((0, 0), (0, V7X_LANES - N_HEADS))
    gbias_row = jnp.concatenate([jnp.pad(b_igate, head_lanes), jnp.pad(b_fgate, head_lanes)],
                                axis=1).reshape(depth, 1, 2 * V7X_LANES)
    g1 = g_norm1.reshape(depth, 1, D_MODEL)
    gv = g_vnorm.reshape(depth, 1, D_B)
    g2 = g_norm2.reshape(depth, 1, D_MODEL)
    bst = jnp.swapaxes(b_spatial, 1, 2)
    ws0 = jnp.repeat(w_spatial[:, :, 0, 0], DG, axis=1).reshape(depth, 1, D_B)
    bs0 = jnp.repeat(b_spatial[:, :, 0], DG, axis=1).reshape(depth, 1, D_B)
    wa = w_branch_a.astype(BF16)
    wb = w_branch_b.astype(BF16)
    wo = w_out.astype(BF16)
    wup = w_up.astype(BF16)
    wdn = w_down.astype(BF16)
    cb = conv_b.reshape(depth, 1, 2 * D_FF)
    n_state = state_mlstm_n.reshape(depth, n_dec, N_HEADS * DK)
    m_pad = jnp.pad(state_mlstm_m, ((0, 0), (0, 0), (0, V7X_LANES - N_HEADS)))

    small = [[] for _ in range(7)]
    c_stack = None
    conv_stack = None
    for l in range(depth):
        final = l == depth - 1

        xp, c_p, n_p, m_p = _prompt_mixer(xp, l, wm, wgt, gbias_col, g1, gv, w_spatial, bst, wa, wb, wo)
        xp, conv_p = _prompt_ffn(xp, l, g2, wup, conv_w, cb, wdn, gf, final)

        z, gates = _sample_proj(xs, l, g1, wm, wgt)
        h, c_stack, n_s, m_s = _sample_state(
            z, gates, gbias_row[l], m_pad[l], n_state[l], state_mlstm_C, l, c_stack)
        xs, vn_s = _sample_mixer(xs, z, h, l, gv, ws0, bs0, wa, wb, wo)
        xs, conv_stack = _sample_ffn(xs, gf, l, state_ffn_conv, g2, wup, conv_w, cb, wdn, final, conv_stack)

        for lst, val in zip(small, (c_p, n_p, m_p[:, :N_HEADS, 0], conv_p,
                                    n_s.reshape(n_dec, N_HEADS, DK), m_s[:, :N_HEADS],
                                    vn_s.reshape(n_dec, 1, D_B))):
            lst.append(val)
    st = [jnp.stack(o) for o in small]
    return (xp, xs.reshape(n_dec, 1, D_MODEL), st[0], st[1], st[2], st[3], c_stack, st[4], st[5], conv_stack, st[6])
```

```python
import functools

import jax
import jax.numpy as jnp
from jax import lax
from jax.experimental import pallas as pl
from jax.experimental.pallas import tpu as pltpu

D_MODEL = 1024
N_HEADS = 4
DK = 128
DV = 256
CHUNK = 128
D_B = 1024
N_GROUPS = 4
DG = D_B // N_GROUPS
D_FF = 2816
CONV_W = 3
EPS = 1e-6
K_SCALE = DK ** -0.5

OFF_Q = 0
OFF_K = OFF_Q + N_HEADS * DK
OFF_V = OFF_K + N_HEADS * DK
OFF_O = OFF_V + N_HEADS * DV
OFF_U = OFF_O + N_HEADS * DV
OFF_VB = OFF_U + D_B
OFF_GA = OFF_VB + D_B
OFF_GB = OFF_GA + D_MODEL
P_MAIN = OFF_GB + D_MODEL
GATE_LO = 2 * N_HEADS * DK + 2 * N_HEADS * DV
GATE_HI = GATE_LO + 2 * N_HEADS

V7X_LANES = 128
V7X_SUBLANES = 8
GATE_ROWS = V7X_SUBLANES
V7X_VMEM_LIMIT = 56 * 1024 * 1024
FFN_COL_TILE = 256
PROJ_TILE = 256
MIXER_BLOCK = 512
FFN_BLOCK = 512
SAMPLE_BLOCK = 16
PACK_TILE = 256
PACK_DIMS = 256

F32 = jnp.float32
BF16 = jnp.bfloat16


def _dot(a, b):
    return jnp.dot(a, b, preferred_element_type=F32)


def _dot_nt(a, b):
    return lax.dot_general(a, b, (((1,), (1,)), ((), ())), preferred_element_type=F32)


def _rmsnorm(x, g):
    r = lax.rsqrt(jnp.mean(x * x, axis=-1, keepdims=True) + EPS)
    return x * r * g


def _log_sigmoid(x):
    return jnp.minimum(x, 0.0) - jnp.log1p(jnp.exp(-jnp.abs(x)))


def _scan_lanes(x, op, fill):
    lane = lax.broadcasted_iota(jnp.int32, x.shape, 1)
    k = 1
    while k < x.shape[1]:
        shifted = pltpu.roll(x, k, 1)
        x = op(x, jnp.where(lane >= k, shifted, fill))
        k *= 2
    return x


def _anchor_zero(x):
    sub, lanes = V7X_SUBLANES, V7X_LANES
    acc = jnp.zeros((sub, lanes), jnp.uint32)
    for r in range(x.shape[0] // sub):
        for c in range(x.shape[1] // lanes):
            piece = pltpu.bitcast(x[r * sub:(r + 1) * sub, c * lanes:(c + 1) * lanes], jnp.uint32)
            acc = acc | ((piece >> 16) >> 16)
    return pltpu.bitcast(acc, F32)


def _next_block_map(nb, steps):
    def index_map(b, s):
        nxt = jnp.minimum(b * steps + s + 1, nb * steps - 1)
        return (nxt // steps, nxt % steps, 0)
    return index_map


def _mixer_kernel(x_ref, wm_ref, wgt_ref, gbias_ref, g1_ref, gv_ref, ws_ref, bst_ref,
                  wa_ref, wb_ref, wo_ref, *rest, block):
    (xo_ref, c_ref, n_ref, m_ref,
     ct_s, n_s, m_s, q_s, k_s, v_s, h_s, so_s, u_s, vb_s, sg_s, um_s) = rest[-16:]
    s = pl.program_id(1)
    n_chunks = block // CHUNK

    @pl.when(s == 0)
    def _():
        ct_s[...] = jnp.zeros_like(ct_s)
        n_s[...] = jnp.zeros_like(n_s)
        m_s[...] = jnp.zeros_like(m_s)

    xn = _rmsnorm(x_ref[...], g1_ref[...]).astype(BF16)

    gates = _dot_nt(wgt_ref[...], xn) + gbias_ref[...]

    def proj(off, t):
        return _dot(xn, wm_ref[:, off + t * PROJ_TILE:off + (t + 1) * PROJ_TILE])

    def tile(t):
        return slice(t * PROJ_TILE, (t + 1) * PROJ_TILE)

    for t in range(N_HEADS * DK // PROJ_TILE):
        q_s[:, tile(t)] = proj(OFF_Q, t)
        k_s[:, tile(t)] = proj(OFF_K, t) * K_SCALE
    for t in range(N_HEADS * DV // PROJ_TILE):
        v_s[:, tile(t)] = proj(OFF_V, t)

    sumsq = [jnp.zeros((block, 1), F32)]

    def vb_tile(t):
        g = jax.nn.gelu(proj(OFF_VB, t))
        vb_s[:, tile(t)] = g
        sumsq[0] = sumsq[0] + jnp.sum(g * g, axis=-1, keepdims=True)

    def u_tile(t):
        u_s[:, tile(t)] = jax.nn.gelu(proj(OFF_U, t))

    def o_tile(t):
        so_s[:, tile(t)] = jax.nn.sigmoid(proj(OFF_O, t))

    def ga_tile(t):
        sg_s[:, tile(t)] = jax.nn.sigmoid(proj(OFF_GA, t))

    def gb_tile(t):
        sg_s[:, D_MODEL + t * PROJ_TILE:D_MODEL + (t + 1) * PROJ_TILE] = jax.nn.sigmoid(proj(OFF_GB, t))

    jobs = [(f, t) for f in (vb_tile, u_tile, o_tile, ga_tile, gb_tile) for t in range(D_MODEL // PROJ_TILE)]

    def run_jobs(count):
        for _ in range(min(count, len(jobs))):
            f, t = jobs.pop(0)
            f(t)

    row_i = lax.broadcasted_iota(jnp.int32, (CHUNK, CHUNK), 0)
    col_i = lax.broadcasted_iota(jnp.int32, (CHUNK, CHUNK), 1)
    causal = row_i >= col_i
    heads = range(N_HEADS)

    for c in range(n_chunks):
        r0 = c * CHUNK
        ig = gates[0:GATE_ROWS, r0:r0 + CHUNK]
        lf = _log_sigmoid(gates[GATE_ROWS:2 * GATE_ROWS, r0:r0 + CHUNK])
        b = _scan_lanes(lf, jnp.add, 0.0)
        a = ig - b
        m_prev = m_s[...]
        gmax = jnp.maximum(m_prev, _scan_lanes(a, jnp.maximum, -jnp.inf))
        m_t = b + gmax
        w_inter = jnp.exp(m_prev - gmax)
        g_last = gmax[:, CHUNK - 1:CHUNK]
        w_last = jnp.exp(a - g_last)
        floor = jnp.exp(-m_t)
        decay = w_inter[:, CHUNK - 1:CHUNK]
        m_s[...] = jnp.broadcast_to(m_t[:, CHUNK - 1:CHUNK], m_s.shape)

        rows = jnp.concatenate(
            [gmax, w_inter, w_last, floor,
             jnp.zeros((CHUNK - 4 * GATE_ROWS, CHUNK), F32)], axis=0)
        cols = rows.T

        def col(kind, h):
            return cols[:, kind * GATE_ROWS + h:kind * GATE_ROWS + h + 1]

        qf = [q_s[r0:r0 + CHUNK, h * DK:(h + 1) * DK] for h in heads]
        kf = [k_s[r0:r0 + CHUNK, h * DK:(h + 1) * DK] for h in heads]
        vf = [v_s[r0:r0 + CHUNK, h * DV:(h + 1) * DV] for h in heads]
        qb = [x.astype(BF16) for x in qf]
        kt = [x.T.astype(BF16) for x in kf]
        run_jobs(2)
        zero = jnp.zeros((DK, CHUNK), BF16)
        sc = []
        for h in range(0, N_HEADS, 2):
            kk = jnp.concatenate([jnp.concatenate([kt[h], zero], axis=1),
                                  jnp.concatenate([zero, kt[h + 1]], axis=1)], axis=0)
            pair = _dot(jnp.concatenate([qb[h], qb[h + 1]], axis=1), kk)
            sc += [pair[:, :CHUNK], pair[:, CHUNK:]]
        dmat = [jnp.where(causal, jnp.exp(a[h:h + 1, :] - col(0, h)), 0.0) for h in heads]
        run_jobs(1)
        sd = [sc[h] * dmat[h] for h in heads]
        ct = [ct_s[h] for h in heads]
        num = [_dot(jnp.concatenate([sd[h], col(1, h) * qf[h]], axis=1).astype(BF16),
                    jnp.concatenate([vf[h], ct[h]], axis=0).astype(BF16)) for h in heads]
        run_jobs(2)
        for h in heads:
            nh = n_s[h:h + 1, :]
            den = (jnp.sum(sd[h], axis=-1, keepdims=True)
                   + col(1, h) * jnp.sum(qf[h] * nh, axis=-1, keepdims=True))
            h_s[r0:r0 + CHUNK, h * DV:(h + 1) * DV] = num[h] * (1.0 / jnp.maximum(jnp.abs(den), col(3, h)))
            dec = decay[h:h + 1, :]
            n_s[h:h + 1, :] = dec * nh + jnp.sum(col(2, h) * kf[h], axis=0, keepdims=True)
        run_jobs(1)
        for h in heads:
            ct_s[h] = decay[h:h + 1, :] * ct[h] + _dot(kt[h], (col(2, h) * vf[h]).astype(BF16))
    run_jobs(len(jobs))

    y_a = _dot((so_s[...] * h_s[...]).astype(BF16), wa_ref[...])

    rinv = lax.rsqrt(sumsq[0] * (1.0 / D_B) + EPS)
    for g in range(N_GROUPS):
        gcols = slice(g * DG, (g + 1) * DG)
        w_tri = jnp.where(causal, ws_ref[g], 0.0).astype(BF16)
        bias_c = bst_ref[:, g:g + 1]
        for c in range(n_chunks):
            rws = slice(c * CHUNK, (c + 1) * CHUNK)
            vn = vb_s[rws, gcols] * rinv[rws] * gv_ref[:, gcols]
            um_s[rws, gcols] = u_s[rws, gcols] * (_dot(w_tri, vn.astype(BF16)) + bias_c)
    y_b = _dot(um_s[...].astype(BF16), wb_ref[...])

    merged = sg_s[:, 0:D_MODEL] * y_a + sg_s[:, D_MODEL:2 * D_MODEL] * y_b
    xo_ref[...] = x_ref[...] + _dot(merged.astype(BF16), wo_ref[...])

    @pl.when(s == pl.num_programs(1) - 1)
    def _():
        for h in range(N_HEADS):
            c_ref[h] = ct_s[h].T
        n_ref[...] = n_s[...]
        m_ref[...] = m_s[...]


def _resident(arr, layer=None):
    if layer is None:
        nd = arr.ndim
        return pl.BlockSpec(arr.shape, lambda *_: (0,) * nd, pipeline_mode=pl.Buffered(1))
    nd = arr.ndim - 1
    return pl.BlockSpec((None,) + arr.shape[1:], lambda *_: (layer,) + (0,) * nd,
                        pipeline_mode=pl.Buffered(1))


def _prompt_mixer(x, layer, wm, wgt, gbias, g1, gv, ws, bst, wa, wb, wo, c_stack):
    nb, seq, _ = x.shape
    depth = wm.shape[0]
    block = MIXER_BLOCK
    grid = (nb, seq // block)
    xspec = pl.BlockSpec((None, block, D_MODEL), lambda b, s: (b, s, 0))
    out_shape = (
        jax.ShapeDtypeStruct((nb, seq, D_MODEL), F32),
        jax.ShapeDtypeStruct((depth, nb, N_HEADS, DV, DK), F32),
        jax.ShapeDtypeStruct((nb, N_HEADS, DK), F32),
        jax.ShapeDtypeStruct((nb, GATE_ROWS, V7X_LANES), F32),
    )
    out_specs = (
        xspec,
        pl.BlockSpec((None, None, N_HEADS, DV, DK), lambda b, s: (layer, b, 0, 0, 0)),
        pl.BlockSpec((None, N_HEADS, DK), lambda b, s: (b, 0, 0)),
        pl.BlockSpec((None, GATE_ROWS, V7X_LANES), lambda b, s: (b, 0, 0)),
    )
    in_specs = [xspec] + [_resident(a, layer) for a in (wm, wgt, gbias, g1, gv, ws, bst, wa, wb, wo)]
    args = [x, wm, wgt, gbias, g1, gv, ws, bst, wa, wb, wo]
    aliases = {}
    if c_stack is not None:
        in_specs.append(pl.BlockSpec(memory_space=pl.ANY))
        aliases = {len(args): 1}
        args.append(c_stack)
    scratch = [
        pltpu.VMEM((N_HEADS, DK, DV), F32),
        pltpu.VMEM((N_HEADS, DK), F32),
        pltpu.VMEM((GATE_ROWS, V7X_LANES), F32),
        pltpu.VMEM((block, N_HEADS * DK), F32),
        pltpu.VMEM((block, N_HEADS * DK), F32),
        pltpu.VMEM((block, N_HEADS * DV), F32),
        pltpu.VMEM((block, N_HEADS * DV), F32),
        pltpu.VMEM((block, N_HEADS * DV), F32),
        pltpu.VMEM((block, D_B), F32),
        pltpu.VMEM((block, D_B), F32),
        pltpu.VMEM((block, 2 * D_MODEL), F32),
        pltpu.VMEM((block, D_B), F32),
    ]
    return pl.pallas_call(
        functools.partial(_mixer_kernel, block=block),
        grid=grid, in_specs=in_specs, out_specs=out_specs, out_shape=out_shape,
        scratch_shapes=scratch, input_output_aliases=aliases,
        compiler_params=pltpu.CompilerParams(
            dimension_semantics=("arbitrary", "arbitrary"), vmem_limit_bytes=V7X_VMEM_LIMIT),
        name="prompt_mixer",
    )(*args)


def _conv_taps(up, carry_s, cw_ref, cb_ref, cols):
    sub = V7X_SUBLANES
    rows = up.shape[0]
    last1 = up[rows - sub:rows]
    last2 = up[rows - 2 * sub:rows - sub]
    first = lax.broadcasted_iota(jnp.int32, last1.shape, 0) == 0
    back1 = jnp.where(first, carry_s[sub - 1:sub, cols], pltpu.roll(last1, 1, 0))
    back2 = jnp.where(first, carry_s[sub - 2:sub - 1, cols], pltpu.roll(last2, 1, 0))
    carry_s[sub - 2:sub - 1, cols] = last2[sub - 1:sub]
    carry_s[sub - 1:sub, cols] = last1[sub - 1:sub]
    m1 = jnp.concatenate([back1, up[0:rows - sub]], axis=0)
    m2 = jnp.concatenate([back2, back1, up[0:rows - 2 * sub]], axis=0)
    return (cb_ref[:, cols] + cw_ref[0:1, cols] * m2 + cw_ref[1:2, cols] * m1
            + cw_ref[2:3, cols] * up)


def _perm_pitch(block):
    return block // V7X_SUBLANES + V7X_SUBLANES


def _ffn_kernel(x_ref, xnext_ref, g2_ref, wup_ref, cw_ref, cb_ref, wdn_ref, gf_ref, *rest, block, final):
    xo_ref, conv_ref, carry_s, act_s, perm_s, unperm_s, xn_s = rest[-7:]
    s = pl.program_id(1)
    sub, lanes = V7X_SUBLANES, V7X_LANES
    groups = block // sub
    chunks = D_MODEL // lanes
    pitch = _perm_pitch(block)

    @pl.when(s == 0)
    def _():
        carry_s[...] = jnp.zeros_like(carry_s)

    def stage(src_ref):
        for c in range(chunks):
            for i in range(sub):
                perm_s[c, i * pitch:i * pitch + groups] = src_ref[i * groups:(i + 1) * groups,
                                                                  c * lanes:(c + 1) * lanes]
        x = jnp.concatenate(
            [jnp.concatenate([perm_s[c, pl.ds(r, sub, stride=pitch), :] for c in range(chunks)], axis=1)
             for r in range(groups)], axis=0)
        xn = _rmsnorm(x, g2_ref[...])
        xn_s[...] = xn.astype(BF16)
        return xn

    @pl.when((pl.program_id(0) == 0) & (s == 0))
    def _():
        stage(x_ref)

    xn = xn_s[...]
    for j in range(D_FF // FFN_COL_TILE):
        halves = []
        for half in range(2):
            cols = slice(half * D_FF + j * FFN_COL_TILE, half * D_FF + (j + 1) * FFN_COL_TILE)
            up = _dot(xn, wup_ref[:, cols])
            halves.append(_conv_taps(up, carry_s, cw_ref, cb_ref, cols))
        act_s[:, j * FFN_COL_TILE:(j + 1) * FFN_COL_TILE] = (jax.nn.silu(halves[0]) * halves[1]).astype(BF16)
    staged = stage(xnext_ref)
    down = _dot(act_s[...], wdn_ref[...])
    anchor = _anchor_zero(staged)
    for r in range(groups):
        for c in range(chunks):
            tile_rc = down[r * sub:(r + 1) * sub, c * lanes:(c + 1) * lanes]
            if r == 0 and c == 0:
                tile_rc = tile_rc + anchor
            unperm_s[c, pl.ds(r, sub, stride=pitch), :] = tile_rc
    for i in range(sub):
        rows = slice(i * groups, (i + 1) * groups)
        y = x_ref[rows, :] + jnp.concatenate(
            [unperm_s[c, i * pitch:i * pitch + groups] for c in range(chunks)], axis=1)
        if final:
            y = _rmsnorm(y, gf_ref[...])
        xo_ref[rows, :] = y

    @pl.when(s == pl.num_programs(1) - 1)
    def _():
        conv_ref[...] = carry_s[V7X_SUBLANES - (CONV_W - 1):V7X_SUBLANES, :]


def _prompt_ffn(x, layer, g2, wup, cw, cb, wdn, gf, final, conv_stack):
    nb, seq, _ = x.shape
    depth = wup.shape[0]
    block = FFN_BLOCK
    steps = seq // block
    grid = (nb, steps)
    xspec = pl.BlockSpec((None, block, D_MODEL), lambda b, s: (b, s, 0))
    out_shape = (
        jax.ShapeDtypeStruct((nb, seq, D_MODEL), F32),
        jax.ShapeDtypeStruct((depth, nb, CONV_W - 1, 2 * D_FF), F32),
    )
    out_specs = (xspec, pl.BlockSpec((None, None, CONV_W - 1, 2 * D_FF), lambda b, s: (layer, b, 0, 0)))
    in_specs = ([xspec, pl.BlockSpec((None, block, D_MODEL), _next_block_map(nb, steps))]
                + [_resident(a, layer) for a in (g2, wup, cw, cb, wdn)] + [_resident(gf)])
    args = [x, x, g2, wup, cw, cb, wdn, gf]
    aliases = {}
    if conv_stack is not None:
        in_specs.append(pl.BlockSpec(memory_space=pl.ANY))
        aliases = {len(args): 1}
        args.append(conv_stack)
    reorder = pltpu.VMEM((D_MODEL // V7X_LANES, V7X_SUBLANES * _perm_pitch(block), V7X_LANES), F32)
    return pl.pallas_call(
        functools.partial(_ffn_kernel, block=block, final=final),
        grid=grid, in_specs=in_specs, out_specs=out_specs, out_shape=out_shape,
        scratch_shapes=[pltpu.VMEM((V7X_SUBLANES, 2 * D_FF), F32),
                        pltpu.VMEM((block, D_FF), BF16),
                        reorder, reorder,
                        pltpu.VMEM((block, D_MODEL), BF16)],
        input_output_aliases=aliases,
        compiler_params=pltpu.CompilerParams(
            dimension_semantics=("arbitrary", "arbitrary"), vmem_limit_bytes=V7X_VMEM_LIMIT),
        name="prompt_ffn",
    )(*args)


def _sample_proj_kernel(x_ref, g1_ref, wm_ref, wgt_ref, z_ref, gates_ref):
    xn = _rmsnorm(x_ref[...], g1_ref[...]).astype(BF16)
    z_ref[...] = _dot(xn, wm_ref[...])
    gt = _dot_nt(wgt_ref[...], xn)
    gt = jnp.concatenate([gt, jnp.zeros((V7X_LANES - gt.shape[0], gt.shape[1]), F32)], axis=0)
    g = gt.T
    gates_ref[...] = jnp.concatenate([g, pltpu.roll(g, V7X_LANES - GATE_ROWS, 1)], axis=1)


def _single_step(kernel_fn, name, out_shape, whole, layered, layer, stacked=None):
    in_specs = [_resident(a) for a in whole] + [_resident(a, layer) for a in layered]
    args = list(whole) + list(layered)
    out_specs = [pl.BlockSpec(o.shape, lambda i, nd=len(o.shape): (0,) * nd) for o in out_shape]
    aliases = {}
    if stacked is not None:
        k, prev = stacked
        nd = len(out_shape[k].shape) - 1
        out_specs[k] = pl.BlockSpec((None,) + out_shape[k].shape[1:], lambda i: (layer,) + (0,) * nd)
        if prev is not None:
            in_specs.append(pl.BlockSpec(memory_space=pl.ANY))
            aliases = {len(args): k}
            args.append(prev)
    return pl.pallas_call(
        kernel_fn,
        grid=(1,),
        in_specs=in_specs, out_specs=tuple(out_specs), out_shape=out_shape,
        input_output_aliases=aliases,
        compiler_params=pltpu.CompilerParams(
            dimension_semantics=("arbitrary",), vmem_limit_bytes=V7X_VMEM_LIMIT),
        name=name,
    )(*args)


def _sample_proj(x, layer, g1, wm, wgt):
    n = x.shape[0]
    assert n == V7X_LANES, "the gate transpose assumes one lane tile of sample rows"
    out_shape = (jax.ShapeDtypeStruct((n, P_MAIN), F32), jax.ShapeDtypeStruct((n, 2 * V7X_LANES), F32))
    return _single_step(_sample_proj_kernel, "sample_proj", out_shape, (x,), (g1, wm, wgt), layer)


def _sample_state_kernel(q_ref, k_ref, v_ref, gates_ref, gbias_ref, m_ref, n_ref, c_ref, *rest):
    h_ref, co_ref, no_ref, mo_ref = rest[-4:]
    tb = SAMPLE_BLOCK
    ig = gates_ref[:, 0:V7X_LANES] + gbias_ref[:, 0:V7X_LANES]
    lf = _log_sigmoid(gates_ref[:, V7X_LANES:] + gbias_ref[:, V7X_LANES:])
    inter = lf + m_ref[...]
    m_t = jnp.maximum(inter, ig)
    d_in = jnp.exp(ig - m_t)
    w_inter = jnp.exp(inter - m_t)
    floor = jnp.exp(-m_t)
    mo_ref[...] = m_t

    row8 = lax.broadcasted_iota(jnp.int32, (CHUNK, DK), 0)
    for h in range(N_HEADS):
        q8 = q_ref[:, h * DK:(h + 1) * DK]
        k8 = k_ref[:, h * DK:(h + 1) * DK] * K_SCALE
        v8 = v_ref[:, h * DV:(h + 1) * DV]
        d_h = d_in[:, h:h + 1]
        w_h = w_inter[:, h:h + 1]
        n8 = n_ref[:, h * DK:(h + 1) * DK]
        s = jnp.sum(q8 * k8, axis=-1, keepdims=True) * d_h
        den = s + w_h * jnp.sum(q8 * n8, axis=-1, keepdims=True)
        qb = q8.astype(BF16)
        inter_rows = [_dot_nt(qb, c_ref[j, h].astype(BF16))[j:j + 1] for j in range(tb)]
        num = s * v8 + w_h * jnp.concatenate(inter_rows, axis=0)
        h_ref[:, h * DV:(h + 1) * DV] = num / jnp.maximum(jnp.abs(den), floor[:, h:h + 1])
        no_ref[:, h * DK:(h + 1) * DK] = w_h * n8 + d_h * k8

        vt = jnp.concatenate([d_h * v8, jnp.zeros((CHUNK - tb, DV), F32)], axis=0).T.astype(BF16)
        kpad = jnp.concatenate([k8, jnp.zeros((CHUNK - tb, DK), F32)], axis=0)
        for j in range(tb):
            kj = jnp.where(row8 == j, kpad, 0.0).astype(BF16)
            co_ref[j, h] = w_inter[j:j + 1, h:h + 1] * c_ref[j, h] + _dot(vt, kj)


def _sample_state(z, gates, gbias, m_pad, n_state, c_state, layer, c_stack):
    n = z.shape[0]
    tb = SAMPLE_BLOCK
    c_block = pl.BlockSpec((None, tb, N_HEADS, DV, DK), lambda i: (layer, i, 0, 0, 0))
    in_specs = [
        pl.BlockSpec((tb, N_HEADS * DK), lambda i: (i, OFF_Q // (N_HEADS * DK))),
        pl.BlockSpec((tb, N_HEADS * DK), lambda i: (i, OFF_K // (N_HEADS * DK))),
        pl.BlockSpec((tb, N_HEADS * DV), lambda i: (i, OFF_V // (N_HEADS * DV))),
        pl.BlockSpec((tb, 2 * V7X_LANES), lambda i: (i, 0)),
        pl.BlockSpec((1, 2 * V7X_LANES), lambda i: (0, 0)),
        pl.BlockSpec((tb, V7X_LANES), lambda i: (i, 0)),
        pl.BlockSpec((tb, N_HEADS * DK), lambda i: (i, 0)),
        c_block,
    ]
    args = [z, z, z, gates, gbias, m_pad, n_state, c_state]
    aliases = {}
    if c_stack is not None:
        in_specs.append(pl.BlockSpec(memory_space=pl.ANY))
        aliases = {len(args): 1}
        args.append(c_stack)
    out_shape = (
        jax.ShapeDtypeStruct((n, N_HEADS * DV), F32),
        jax.ShapeDtypeStruct(c_state.shape, F32),
        jax.ShapeDtypeStruct((n, N_HEADS * DK), F32),
        jax.ShapeDtypeStruct((n, V7X_LANES), F32),
    )
    out_specs = (
        pl.BlockSpec((tb, N_HEADS * DV), lambda i: (i, 0)),
        c_block,
        pl.BlockSpec((tb, N_HEADS * DK), lambda i: (i, 0)),
        pl.BlockSpec((tb, V7X_LANES), lambda i: (i, 0)),
    )
    return pl.pallas_call(
        _sample_state_kernel,
        grid=(n // tb,), in_specs=in_specs, out_specs=out_specs, out_shape=out_shape,
        input_output_aliases=aliases,
        compiler_params=pltpu.CompilerParams(
            dimension_semantics=("arbitrary",), vmem_limit_bytes=V7X_VMEM_LIMIT),
        name="sample_state",
    )(*args)


def _sample_mixer_kernel(x_ref, z_ref, h_ref, gv_ref, ws0_ref, bs0_ref, wa_ref, wb_ref, wo_ref, *rest):
    xo_ref, vn_ref = rest[-2:]
    y_a = _dot((jax.nn.sigmoid(z_ref[:, OFF_O:OFF_U]) * h_ref[...]).astype(BF16), wa_ref[...])
    u = jax.nn.gelu(z_ref[:, OFF_U:OFF_VB])
    vn = _rmsnorm(jax.nn.gelu(z_ref[:, OFF_VB:OFF_GA]), gv_ref[...])
    vn_ref[:, 0, :] = vn
    mixed = ws0_ref[...] * vn + bs0_ref[...]
    y_b = _dot((u * mixed).astype(BF16), wb_ref[...])
    merged = (jax.nn.sigmoid(z_ref[:, OFF_GA:OFF_GB]) * y_a
              + jax.nn.sigmoid(z_ref[:, OFF_GB:P_MAIN]) * y_b)
    xo_ref[...] = x_ref[...] + _dot(merged.astype(BF16), wo_ref[...])


def _sample_mixer(x, z, h, layer, gv, ws0, bs0, wa, wb, wo, vn_stack):
    n = x.shape[0]
    out_shape = (jax.ShapeDtypeStruct((n, D_MODEL), F32),
                 jax.ShapeDtypeStruct((wa.shape[0], n, 1, D_B), F32))
    return _single_step(_sample_mixer_kernel, "sample_mixer", out_shape,
                        (x, z, h), (gv, ws0, bs0, wa, wb, wo), layer, stacked=(1, vn_stack))


def _sample_ffn_kernel(x_ref, gf_ref, buf_ref, g2_ref, wup_ref, cw_ref, cb_ref, wdn_ref, *rest, final):
    xo_ref, nbuf_ref = rest[-2:]
    x = x_ref[...]
    xn = _rmsnorm(x, g2_ref[...]).astype(BF16)
    acc = jnp.zeros(x.shape, F32)
    for j in range(D_FF // FFN_COL_TILE):
        halves = []
        for half in range(2):
            col0 = half * D_FF + j * FFN_COL_TILE
            cols = slice(col0, col0 + FFN_COL_TILE)
            up = _dot(xn, wup_ref[:, cols])
            b0 = buf_ref[:, 0, cols]
            b1 = buf_ref[:, 1, cols]
            nbuf_ref[:, 0, cols] = b1
            nbuf_ref[:, 1, cols] = up
            halves.append(cb_ref[:, cols] + cw_ref[0:1, cols] * b0 + cw_ref[1:2, cols] * b1
                          + cw_ref[2:3, cols] * up)
        act = (jax.nn.silu(halves[0]) * halves[1]).astype(BF16)
        acc = acc + _dot(act, wdn_ref[j * FFN_COL_TILE:(j + 1) * FFN_COL_TILE, :])
    y = x + acc
    if final:
        xo_ref[:, 0, :] = _rmsnorm(y, gf_ref[...])
    else:
        xo_ref[...] = y


def _sample_ffn(x, gf, layer, buf, g2, wup, cw, cb, wdn, final, conv_stack):
    n = x.shape[0]
    x_out = (n, 1, D_MODEL) if final else (n, D_MODEL)
    out_shape = (jax.ShapeDtypeStruct(x_out, F32), jax.ShapeDtypeStruct(buf.shape, F32))
    return _single_step(functools.partial(_sample_ffn_kernel, final=final), "sample_ffn", out_shape,
                        (x, gf), (buf, g2, wup, cw, cb, wdn), layer, stacked=(1, conv_stack))


def _pack_kernel(wt_ref, o_ref, wgt_ref):
    g8 = wt_ref[GATE_LO:GATE_HI, :]
    head = lax.broadcasted_iota(jnp.int32, g8.shape, 0) < N_HEADS
    wgt_ref[...] = jnp.concatenate(
        [jnp.where(head, g8, 0.0), jnp.where(head, pltpu.roll(g8, N_HEADS, 0), 0.0)], axis=0).astype(BF16)
    for j in range(P_MAIN // PACK_TILE):
        dst = j * PACK_TILE
        src = dst if dst < GATE_LO else dst + (GATE_HI - GATE_LO)
        o_ref[:, dst:dst + PACK_TILE] = wt_ref[src:src + PACK_TILE, :].T.astype(BF16)


def _pack_w_in(w_in_t):
    depth, p_in, d = w_in_t.shape
    return pl.pallas_call(
        _pack_kernel,
        grid=(depth, d // PACK_DIMS),
        in_specs=[pl.BlockSpec((None, p_in, PACK_DIMS), lambda l, r: (l, 0, r))],
        out_specs=(pl.BlockSpec((None, PACK_DIMS, P_MAIN), lambda l, r: (l, r, 0)),
                   pl.BlockSpec((None, 2 * GATE_ROWS, PACK_DIMS), lambda l, r: (l, 0, r))),
        out_shape=(jax.ShapeDtypeStruct((depth, d, P_MAIN), BF16),
                   jax.ShapeDtypeStruct((depth, 2 * GATE_ROWS, d), BF16)),
        compiler_params=pltpu.CompilerParams(
            dimension_semantics=("arbitrary", "arbitrary"), vmem_limit_bytes=V7X_VMEM_LIMIT),
        name="pack_w_in",
    )(w_in_t)


def kernel(x_prompt, x_sample, state_mlstm_C, state_mlstm_n, state_mlstm_m, state_ffn_conv, w_in, b_igate, b_fgate, g_norm1, g_vnorm, w_spatial, b_spatial, w_branch_a, w_branch_b, w_out, g_norm2, w_up, conv_w, conv_b, w_down, g_final):
    depth = w_in.shape[0]
    n_dec = x_sample.shape[0]
    xp = x_prompt
    xs = x_sample.reshape(n_dec, D_MODEL)
    gf = g_final.reshape(1, D_MODEL)

    wm, wgt = _pack_w_in(jnp.swapaxes(w_in, 1, 2))
    head_pad = ((0, 0), (0, GATE_ROWS - N_HEADS))
    gbias_col = jnp.concatenate([jnp.pad(b_igate, head_pad), jnp.pad(b_fgate, head_pad)],
                                axis=1).reshape(depth, 2 * GATE_ROWS, 1)
    head_lanes = ((0, 0), (0, V7X_LANES - N_HEADS))
    gbias_row = jnp.concatenate([jnp.pad(b_igate, head_lanes), jnp.pad(b_fgate, head_lanes)],
                                axis=1).reshape(depth, 1, 2 * V7X_LANES)
    g1 = g_norm1.reshape(depth, 1, D_MODEL)
    gv = g_vnorm.reshape(depth, 1, D_B)
    g2 = g_norm2.reshape(depth, 1, D_MODEL)
    bst = jnp.swapaxes(b_spatial, 1, 2)
    ws0 = jnp.repeat(w_spatial[:, :, 0, 0], DG, axis=1).reshape(depth, 1, D_B)
    bs0 = jnp.repeat(b_spatial[:, :, 0], DG, axis=1).reshape(depth, 1, D_B)
    wa = w_branch_a.astype(BF16)
    wb = w_branch_b.astype(BF16)
    wo = w_out.astype(BF16)
    wup = w_up.astype(BF16)
    wdn = w_down.astype(BF16)
    cb = conv_b.reshape(depth, 1, 2 * D_FF)
    n_state = state_mlstm_n.reshape(depth, n_dec, N_HEADS * DK)
    m_pad = jnp.pad(state_mlstm_m, ((0, 0), (0, 0), (0, V7X_LANES - N_HEADS)))

    small = [[] for _ in range(4)]
    cp_stack = None
    convp_stack = None
    c_stack = None
    vn_stack = None
    conv_stack = None
    for l in range(depth):
        final = l == depth - 1

        xp, cp_stack, n_p, m_p = _prompt_mixer(xp, l, wm, wgt, gbias_col, g1, gv, w_spatial, bst, wa, wb, wo,
                                               cp_stack)
        xp, convp_stack = _prompt_ffn(xp, l, g2, wup, conv_w, cb, wdn, gf, final, convp_stack)

        z, gates = _sample_proj(xs, l, g1, wm, wgt)
        h, c_stack, n_s, m_s = _sample_state(
            z, gates, gbias_row[l], m_pad[l], n_state[l], state_mlstm_C, l, c_stack)
        xs, vn_stack = _sample_mixer(xs, z, h, l, gv, ws0, bs0, wa, wb, wo, vn_stack)
        xs, conv_stack = _sample_ffn(xs, gf, l, state_ffn_conv, g2, wup, conv_w, cb, wdn, final, conv_stack)

        for lst, val in zip(small, (n_p, m_p[:, :N_HEADS, 0],
                                    n_s.reshape(n_dec, N_HEADS, DK), m_s[:, :N_HEADS])):
            lst.append(val)
    st = [jnp.stack(o) for o in small]
    return (xp, xs, cp_stack, st[0], st[1], convp_stack, c_stack, st[2], st[3], conv_stack, vn_stack)
```

```python
import functools

import jax
import jax.numpy as jnp
from jax import lax
from jax.experimental import pallas as pl
from jax.experimental.pallas import tpu as pltpu

D_MODEL = 1024
N_HEADS = 4
DK = 128
DV = 256
CHUNK = 128
D_B = 1024
N_GROUPS = 4
DG = D_B // N_GROUPS
D_FF = 2816
CONV_W = 3
EPS = 1e-6
K_SCALE = DK ** -0.5

OFF_Q = 0
OFF_K = OFF_Q + N_HEADS * DK
OFF_V = OFF_K + N_HEADS * DK
OFF_O = OFF_V + N_HEADS * DV
OFF_U = OFF_O + N_HEADS * DV
OFF_VB = OFF_U + D_B
OFF_GA = OFF_VB + D_B
OFF_GB = OFF_GA + D_MODEL
P_MAIN = OFF_GB + D_MODEL
GATE_LO = 2 * N_HEADS * DK + 2 * N_HEADS * DV
GATE_HI = GATE_LO + 2 * N_HEADS

V7X_LANES = 128
V7X_SUBLANES = 8
GATE_ROWS = V7X_SUBLANES
V7X_VMEM_LIMIT = 56 * 1024 * 1024
FFN_COL_TILE = 256
PROJ_TILE = 256
MIXER_BLOCK = 512
FFN_BLOCK = 512
SAMPLE_BLOCK = 16
PACK_TILE = 256
PACK_DIMS = 256
SAMPLE_PROJ_TILE = P_MAIN // 2

F32 = jnp.float32
BF16 = jnp.bfloat16


def _dot(a, b):
    return jnp.dot(a, b, preferred_element_type=F32)


def _dot_nt(a, b):
    return lax.dot_general(a, b, (((1,), (1,)), ((), ())), preferred_element_type=F32)


def _rmsnorm(x, g):
    r = lax.rsqrt(jnp.mean(x * x, axis=-1, keepdims=True) + EPS)
    return x * r * g


def _log_sigmoid(x):
    return jnp.minimum(x, 0.0) - jnp.log1p(jnp.exp(-jnp.abs(x)))


def _scan_lanes(x, op, fill):
    lane = lax.broadcasted_iota(jnp.int32, x.shape, 1)
    k = 1
    while k < x.shape[1]:
        shifted = pltpu.roll(x, k, 1)
        x = op(x, jnp.where(lane >= k, shifted, fill))
        k *= 2
    return x


def _anchor_zero(x):
    sub, lanes = V7X_SUBLANES, V7X_LANES
    acc = jnp.zeros((sub, lanes), jnp.uint32)
    for r in range(x.shape[0] // sub):
        for c in range(x.shape[1] // lanes):
            piece = pltpu.bitcast(x[r * sub:(r + 1) * sub, c * lanes:(c + 1) * lanes], jnp.uint32)
            acc = acc | ((piece >> 16) >> 16)
    return pltpu.bitcast(acc, F32)


def _next_block_map(nb, steps):
    def index_map(b, s):
        nxt = jnp.minimum(b * steps + s + 1, nb * steps - 1)
        return (nxt // steps, nxt % steps, 0)
    return index_map


def _mixer_kernel(x_ref, wm_ref, wgt_ref, gbias_ref, g1_ref, gv_ref, ws_ref, bst_ref,
                  wa_ref, wb_ref, wo_ref, *rest, block):
    (xo_ref, c_ref, n_ref, m_ref,
     ct_s, n_s, m_s, q_s, k_s, v_s, h_s, so_s, u_s, vb_s, sg_s, um_s) = rest[-16:]
    s = pl.program_id(1)
    n_chunks = block // CHUNK

    @pl.when(s == 0)
    def _():
        ct_s[...] = jnp.zeros_like(ct_s)
        n_s[...] = jnp.zeros_like(n_s)
        m_s[...] = jnp.zeros_like(m_s)

    xn = _rmsnorm(x_ref[...], g1_ref[...]).astype(BF16)

    gates = _dot_nt(wgt_ref[...], xn) + gbias_ref[...]

    def proj(off, t):
        return _dot(xn, wm_ref[:, off + t * PROJ_TILE:off + (t + 1) * PROJ_TILE])

    def tile(t):
        return slice(t * PROJ_TILE, (t + 1) * PROJ_TILE)

    for t in range(N_HEADS * DK // PROJ_TILE):
        q_s[:, tile(t)] = proj(OFF_Q, t)
        k_s[:, tile(t)] = proj(OFF_K, t) * K_SCALE
    for t in range(N_HEADS * DV // PROJ_TILE):
        v_s[:, tile(t)] = proj(OFF_V, t)

    sumsq = [jnp.zeros((block, 1), F32)]

    def vb_tile(t):
        g = jax.nn.gelu(proj(OFF_VB, t))
        vb_s[:, tile(t)] = g
        sumsq[0] = sumsq[0] + jnp.sum(g * g, axis=-1, keepdims=True)

    def u_tile(t):
        u_s[:, tile(t)] = jax.nn.gelu(proj(OFF_U, t))

    def o_tile(t):
        so_s[:, tile(t)] = jax.nn.sigmoid(proj(OFF_O, t))

    def ga_tile(t):
        sg_s[:, tile(t)] = jax.nn.sigmoid(proj(OFF_GA, t))

    def gb_tile(t):
        sg_s[:, D_MODEL + t * PROJ_TILE:D_MODEL + (t + 1) * PROJ_TILE] = jax.nn.sigmoid(proj(OFF_GB, t))

    jobs = [(f, t) for f in (vb_tile, u_tile, o_tile, ga_tile, gb_tile) for t in range(D_MODEL // PROJ_TILE)]

    def run_jobs(count):
        for _ in range(min(count, len(jobs))):
            f, t = jobs.pop(0)
            f(t)

    row_i = lax.broadcasted_iota(jnp.int32, (CHUNK, CHUNK), 0)
    col_i = lax.broadcasted_iota(jnp.int32, (CHUNK, CHUNK), 1)
    causal = row_i >= col_i
    heads = range(N_HEADS)

    for c in range(n_chunks):
        r0 = c * CHUNK
        ig = gates[0:GATE_ROWS, r0:r0 + CHUNK]
        lf = _log_sigmoid(gates[GATE_ROWS:2 * GATE_ROWS, r0:r0 + CHUNK])
        b = _scan_lanes(lf, jnp.add, 0.0)
        a = ig - b
        m_prev = m_s[...]
        gmax = jnp.maximum(m_prev, _scan_lanes(a, jnp.maximum, -jnp.inf))
        m_t = b + gmax
        w_inter = jnp.exp(m_prev - gmax)
        g_last = gmax[:, CHUNK - 1:CHUNK]
        w_last = jnp.exp(a - g_last)
        floor = jnp.exp(-m_t)
        decay = w_inter[:, CHUNK - 1:CHUNK]
        m_s[...] = jnp.broadcast_to(m_t[:, CHUNK - 1:CHUNK], m_s.shape)

        rows = jnp.concatenate(
            [gmax, w_inter, w_last, floor,
             jnp.zeros((CHUNK - 4 * GATE_ROWS, CHUNK), F32)], axis=0)
        cols = rows.T

        def col(kind, h):
            return cols[:, kind * GATE_ROWS + h:kind * GATE_ROWS + h + 1]

        qf = [q_s[r0:r0 + CHUNK, h * DK:(h + 1) * DK] for h in heads]
        kf = [k_s[r0:r0 + CHUNK, h * DK:(h + 1) * DK] for h in heads]
        vf = [v_s[r0:r0 + CHUNK, h * DV:(h + 1) * DV] for h in heads]
        qb = [x.astype(BF16) for x in qf]
        kt = [x.T.astype(BF16) for x in kf]
        run_jobs(2)
        zero = jnp.zeros((DK, CHUNK), BF16)
        sc = []
        for h in range(0, N_HEADS, 2):
            kk = jnp.concatenate([jnp.concatenate([kt[h], zero], axis=1),
                                  jnp.concatenate([zero, kt[h + 1]], axis=1)], axis=0)
            pair = _dot(jnp.concatenate([qb[h], qb[h + 1]], axis=1), kk)
            sc += [pair[:, :CHUNK], pair[:, CHUNK:]]
        dmat = [jnp.where(causal, jnp.exp(a[h:h + 1, :] - col(0, h)), 0.0) for h in heads]
        run_jobs(1)
        sd = [sc[h] * dmat[h] for h in heads]
        ct = [ct_s[h] for h in heads]
        num = [_dot(jnp.concatenate([sd[h], col(1, h) * qf[h]], axis=1).astype(BF16),
                    jnp.concatenate([vf[h], ct[h]], axis=0).astype(BF16)) for h in heads]
        run_jobs(2)
        for h in heads:
            nh = n_s[h:h + 1, :]
            den = (jnp.sum(sd[h], axis=-1, keepdims=True)
                   + col(1, h) * jnp.sum(qf[h] * nh, axis=-1, keepdims=True))
            h_s[r0:r0 + CHUNK, h * DV:(h + 1) * DV] = num[h] * (1.0 / jnp.maximum(jnp.abs(den), col(3, h)))
            dec = decay[h:h + 1, :]
            n_s[h:h + 1, :] = dec * nh + jnp.sum(col(2, h) * kf[h], axis=0, keepdims=True)
        run_jobs(1)
        for h in heads:
            ct_s[h] = decay[h:h + 1, :] * ct[h] + _dot(kt[h], (col(2, h) * vf[h]).astype(BF16))
    run_jobs(len(jobs))

    y_a = _dot((so_s[...] * h_s[...]).astype(BF16), wa_ref[...])

    rinv = lax.rsqrt(sumsq[0] * (1.0 / D_B) + EPS)
    for g in range(N_GROUPS):
        gcols = slice(g * DG, (g + 1) * DG)
        w_tri = jnp.where(causal, ws_ref[g], 0.0).astype(BF16)
        bias_c = bst_ref[:, g:g + 1]
        for c in range(n_chunks):
            rws = slice(c * CHUNK, (c + 1) * CHUNK)
            vn = vb_s[rws, gcols] * rinv[rws] * gv_ref[:, gcols]
            um_s[rws, gcols] = u_s[rws, gcols] * (_dot(w_tri, vn.astype(BF16)) + bias_c)
    y_b = _dot(um_s[...].astype(BF16), wb_ref[...])

    merged = sg_s[:, 0:D_MODEL] * y_a + sg_s[:, D_MODEL:2 * D_MODEL] * y_b
    xo_ref[...] = x_ref[...] + _dot(merged.astype(BF16), wo_ref[...])

    @pl.when(s == pl.num_programs(1) - 1)
    def _():
        for h in range(N_HEADS):
            c_ref[h] = ct_s[h].T
        n_ref[...] = n_s[...]
        m_ref[...] = m_s[...]


def _resident(arr, layer=None):
    if layer is None:
        nd = arr.ndim
        return pl.BlockSpec(arr.shape, lambda *_: (0,) * nd, pipeline_mode=pl.Buffered(1))
    nd = arr.ndim - 1
    return pl.BlockSpec((None,) + arr.shape[1:], lambda *_: (layer,) + (0,) * nd,
                        pipeline_mode=pl.Buffered(1))


def _prompt_mixer(x, layer, wm, wgt, gbias, g1, gv, ws, bst, wa, wb, wo, c_stack):
    nb, seq, _ = x.shape
    depth = wm.shape[0]
    block = MIXER_BLOCK
    grid = (nb, seq // block)
    xspec = pl.BlockSpec((None, block, D_MODEL), lambda b, s: (b, s, 0))
    out_shape = (
        jax.ShapeDtypeStruct((nb, seq, D_MODEL), F32),
        jax.ShapeDtypeStruct((depth, nb, N_HEADS, DV, DK), F32),
        jax.ShapeDtypeStruct((nb, N_HEADS, DK), F32),
        jax.ShapeDtypeStruct((nb, GATE_ROWS, V7X_LANES), F32),
    )
    out_specs = (
        xspec,
        pl.BlockSpec((None, None, N_HEADS, DV, DK), lambda b, s: (layer, b, 0, 0, 0)),
        pl.BlockSpec((None, N_HEADS, DK), lambda b, s: (b, 0, 0)),
        pl.BlockSpec((None, GATE_ROWS, V7X_LANES), lambda b, s: (b, 0, 0)),
    )
    in_specs = [xspec] + [_resident(a, layer) for a in (wm, wgt, gbias, g1, gv, ws, bst, wa, wb, wo)]
    args = [x, wm, wgt, gbias, g1, gv, ws, bst, wa, wb, wo]
    aliases = {}
    if c_stack is not None:
        in_specs.append(pl.BlockSpec(memory_space=pl.ANY))
        aliases = {len(args): 1}
        args.append(c_stack)
    scratch = [
        pltpu.VMEM((N_HEADS, DK, DV), F32),
        pltpu.VMEM((N_HEADS, DK), F32),
        pltpu.VMEM((GATE_ROWS, V7X_LANES), F32),
        pltpu.VMEM((block, N_HEADS * DK), F32),
        pltpu.VMEM((block, N_HEADS * DK), F32),
        pltpu.VMEM((block, N_HEADS * DV), F32),
        pltpu.VMEM((block, N_HEADS * DV), F32),
        pltpu.VMEM((block, N_HEADS * DV), F32),
        pltpu.VMEM((block, D_B), F32),
        pltpu.VMEM((block, D_B), F32),
        pltpu.VMEM((block, 2 * D_MODEL), F32),
        pltpu.VMEM((block, D_B), F32),
    ]
    return pl.pallas_call(
        functools.partial(_mixer_kernel, block=block),
        grid=grid, in_specs=in_specs, out_specs=out_specs, out_shape=out_shape,
        scratch_shapes=scratch, input_output_aliases=aliases,
        compiler_params=pltpu.CompilerParams(
            dimension_semantics=("arbitrary", "arbitrary"), vmem_limit_bytes=V7X_VMEM_LIMIT),
        name="prompt_mixer",
    )(*args)


def _conv_taps(up, carry_s, cw_ref, cb_ref, cols):
    sub = V7X_SUBLANES
    rows = up.shape[0]
    last1 = up[rows - sub:rows]
    last2 = up[rows - 2 * sub:rows - sub]
    first = lax.broadcasted_iota(jnp.int32, last1.shape, 0) == 0
    back1 = jnp.where(first, carry_s[sub - 1:sub, cols], pltpu.roll(last1, 1, 0))
    back2 = jnp.where(first, carry_s[sub - 2:sub - 1, cols], pltpu.roll(last2, 1, 0))
    carry_s[sub - 2:sub - 1, cols] = last2[sub - 1:sub]
    carry_s[sub - 1:sub, cols] = last1[sub - 1:sub]
    m1 = jnp.concatenate([back1, up[0:rows - sub]], axis=0)
    m2 = jnp.concatenate([back2, back1, up[0:rows - 2 * sub]], axis=0)
    return (cb_ref[:, cols] + cw_ref[0:1, cols] * m2 + cw_ref[1:2, cols] * m1
            + cw_ref[2:3, cols] * up)


def _perm_pitch(block):
    return block // V7X_SUBLANES + V7X_SUBLANES


def _ffn_kernel(x_ref, xnext_ref, g2_ref, wup_ref, cw_ref, cb_ref, wdn_ref, gf_ref, *rest, block, final):
    xo_ref, conv_ref, carry_s, act_s, perm_s, unperm_s, xn_s = rest[-7:]
    s = pl.program_id(1)
    sub, lanes = V7X_SUBLANES, V7X_LANES
    groups = block // sub
    chunks = D_MODEL // lanes
    pitch = _perm_pitch(block)

    @pl.when(s == 0)
    def _():
        carry_s[...] = jnp.zeros_like(carry_s)

    def stage(src_ref):
        for c in range(chunks):
            for i in range(sub):
                perm_s[c, i * pitch:i * pitch + groups] = src_ref[i * groups:(i + 1) * groups,
                                                                  c * lanes:(c + 1) * lanes]
        x = jnp.concatenate(
            [jnp.concatenate([perm_s[c, pl.ds(r, sub, stride=pitch), :] for c in range(chunks)], axis=1)
             for r in range(groups)], axis=0)
        xn = _rmsnorm(x, g2_ref[...])
        xn_s[...] = xn.astype(BF16)
        return xn

    @pl.when((pl.program_id(0) == 0) & (s == 0))
    def _():
        stage(x_ref)

    xn = xn_s[...]
    for j in range(D_FF // FFN_COL_TILE):
        halves = []
        for half in range(2):
            cols = slice(half * D_FF + j * FFN_COL_TILE, half * D_FF + (j + 1) * FFN_COL_TILE)
            up = _dot(xn, wup_ref[:, cols])
            halves.append(_conv_taps(up, carry_s, cw_ref, cb_ref, cols))
        act_s[:, j * FFN_COL_TILE:(j + 1) * FFN_COL_TILE] = (jax.nn.silu(halves[0]) * halves[1]).astype(BF16)
    staged = stage(xnext_ref)
    down = _dot(act_s[...], wdn_ref[...])
    anchor = _anchor_zero(staged)
    for r in range(groups):
        for c in range(chunks):
            tile_rc = down[r * sub:(r + 1) * sub, c * lanes:(c + 1) * lanes]
            if r == 0 and c == 0:
                tile_rc = tile_rc + anchor
            unperm_s[c, pl.ds(r, sub, stride=pitch), :] = tile_rc
    for i in range(sub):
        rows = slice(i * groups, (i + 1) * groups)
        y = x_ref[rows, :] + jnp.concatenate(
            [unperm_s[c, i * pitch:i * pitch + groups] for c in range(chunks)], axis=1)
        if final:
            y = _rmsnorm(y, gf_ref[...])
        xo_ref[rows, :] = y

    @pl.when(s == pl.num_programs(1) - 1)
    def _():
        conv_ref[...] = carry_s[V7X_SUBLANES - (CONV_W - 1):V7X_SUBLANES, :]


def _prompt_ffn(x, layer, g2, wup, cw, cb, wdn, gf, final, conv_stack):
    nb, seq, _ = x.shape
    depth = wup.shape[0]
    block = FFN_BLOCK
    steps = seq // block
    grid = (nb, steps)
    xspec = pl.BlockSpec((None, block, D_MODEL), lambda b, s: (b, s, 0))
    out_shape = (
        jax.ShapeDtypeStruct((nb, seq, D_MODEL), F32),
        jax.ShapeDtypeStruct((depth, nb, CONV_W - 1, 2 * D_FF), F32),
    )
    out_specs = (xspec, pl.BlockSpec((None, None, CONV_W - 1, 2 * D_FF), lambda b, s: (layer, b, 0, 0)))
    in_specs = ([xspec, pl.BlockSpec((None, block, D_MODEL), _next_block_map(nb, steps))]
                + [_resident(a, layer) for a in (g2, wup, cw, cb, wdn)] + [_resident(gf)])
    args = [x, x, g2, wup, cw, cb, wdn, gf]
    aliases = {}
    if conv_stack is not None:
        in_specs.append(pl.BlockSpec(memory_space=pl.ANY))
        aliases = {len(args): 1}
        args.append(conv_stack)
    reorder = pltpu.VMEM((D_MODEL // V7X_LANES, V7X_SUBLANES * _perm_pitch(block), V7X_LANES), F32)
    return pl.pallas_call(
        functools.partial(_ffn_kernel, block=block, final=final),
        grid=grid, in_specs=in_specs, out_specs=out_specs, out_shape=out_shape,
        scratch_shapes=[pltpu.VMEM((V7X_SUBLANES, 2 * D_FF), F32),
                        pltpu.VMEM((block, D_FF), BF16),
                        reorder, reorder,
                        pltpu.VMEM((block, D_MODEL), BF16)],
        input_output_aliases=aliases,
        compiler_params=pltpu.CompilerParams(
            dimension_semantics=("arbitrary", "arbitrary"), vmem_limit_bytes=V7X_VMEM_LIMIT),
        name="prompt_ffn",
    )(*args)


def _sample_proj_kernel(x_ref, g1_ref, wgt_ref, wm_ref, z_ref, gates_ref):
    xn = _rmsnorm(x_ref[...], g1_ref[...]).astype(BF16)
    z_ref[...] = _dot(xn, wm_ref[...])

    @pl.when(pl.program_id(0) == 0)
    def _():
        gt = _dot_nt(wgt_ref[...], xn)
        gt = jnp.concatenate([gt, jnp.zeros((V7X_LANES - gt.shape[0], gt.shape[1]), F32)], axis=0)
        g = gt.T
        gates_ref[...] = jnp.concatenate([g, pltpu.roll(g, V7X_LANES - GATE_ROWS, 1)], axis=1)


def _single_step(kernel_fn, name, out_shape, whole, layered, layer, stacked=None):
    in_specs = [_resident(a) for a in whole] + [_resident(a, layer) for a in layered]
    args = list(whole) + list(layered)
    out_specs = [pl.BlockSpec(o.shape, lambda i, nd=len(o.shape): (0,) * nd) for o in out_shape]
    aliases = {}
    if stacked is not None:
        k, prev = stacked
        nd = len(out_shape[k].shape) - 1
        out_specs[k] = pl.BlockSpec((None,) + out_shape[k].shape[1:], lambda i: (layer,) + (0,) * nd)
        if prev is not None:
            in_specs.append(pl.BlockSpec(memory_space=pl.ANY))
            aliases = {len(args): k}
            args.append(prev)
    return pl.pallas_call(
        kernel_fn,
        grid=(1,),
        in_specs=in_specs, out_specs=tuple(out_specs), out_shape=out_shape,
        input_output_aliases=aliases,
        compiler_params=pltpu.CompilerParams(
            dimension_semantics=("arbitrary",), vmem_limit_bytes=V7X_VMEM_LIMIT),
        name=name,
    )(*args)


def _sample_proj(x, layer, g1, wm, wgt):
    n = x.shape[0]
    assert n == V7X_LANES, "the gate transpose assumes one lane tile of sample rows"
    tile = SAMPLE_PROJ_TILE
    return pl.pallas_call(
        _sample_proj_kernel,
        grid=(P_MAIN // tile,),
        in_specs=[_resident(x), _resident(g1, layer), _resident(wgt, layer),
                  pl.BlockSpec((None, D_MODEL, tile), lambda j: (layer, 0, j))],
        out_specs=(pl.BlockSpec((n, tile), lambda j: (0, j)),
                   pl.BlockSpec((n, 2 * V7X_LANES), lambda j: (0, 0))),
        out_shape=(jax.ShapeDtypeStruct((n, P_MAIN), F32), jax.ShapeDtypeStruct((n, 2 * V7X_LANES), F32)),
        compiler_params=pltpu.CompilerParams(
            dimension_semantics=("arbitrary",), vmem_limit_bytes=V7X_VMEM_LIMIT),
        name="sample_proj",
    )(x, g1, wgt, wm)


def _sample_state_kernel(q_ref, k_ref, v_ref, gates_ref, gbias_ref, m_ref, n_ref, c_ref, *rest):
    h_ref, co_ref, no_ref, mo_ref = rest[-4:]
    tb = SAMPLE_BLOCK
    ig = gates_ref[:, 0:V7X_LANES] + gbias_ref[:, 0:V7X_LANES]
    lf = _log_sigmoid(gates_ref[:, V7X_LANES:] + gbias_ref[:, V7X_LANES:])
    inter = lf + m_ref[...]
    m_t = jnp.maximum(inter, ig)
    d_in = jnp.exp(ig - m_t)
    w_inter = jnp.exp(inter - m_t)
    floor = jnp.exp(-m_t)
    mo_ref[...] = m_t

    row8 = lax.broadcasted_iota(jnp.int32, (CHUNK, DK), 0)
    for h in range(N_HEADS):
        q8 = q_ref[:, h * DK:(h + 1) * DK]
        k8 = k_ref[:, h * DK:(h + 1) * DK] * K_SCALE
        v8 = v_ref[:, h * DV:(h + 1) * DV]
        d_h = d_in[:, h:h + 1]
        w_h = w_inter[:, h:h + 1]
        n8 = n_ref[:, h * DK:(h + 1) * DK]
        s = jnp.sum(q8 * k8, axis=-1, keepdims=True) * d_h
        den = s + w_h * jnp.sum(q8 * n8, axis=-1, keepdims=True)
        qb = q8.astype(BF16)
        inter_rows = [_dot_nt(qb, c_ref[j, h].astype(BF16))[j:j + 1] for j in range(tb)]
        num = s * v8 + w_h * jnp.concatenate(inter_rows, axis=0)
        h_ref[:, h * DV:(h + 1) * DV] = num / jnp.maximum(jnp.abs(den), floor[:, h:h + 1])
        no_ref[:, h * DK:(h + 1) * DK] = w_h * n8 + d_h * k8

        vt = jnp.concatenate([d_h * v8, jnp.zeros((CHUNK - tb, DV), F32)], axis=0).T.astype(BF16)
        kpad = jnp.concatenate([k8, jnp.zeros((CHUNK - tb, DK), F32)], axis=0)
        for j in range(tb):
            kj = jnp.where(row8 == j, kpad, 0.0).astype(BF16)
            co_ref[j, h] = w_inter[j:j + 1, h:h + 1] * c_ref[j, h] + _dot(vt, kj)


def _sample_state(z, gates, gbias, m_pad, n_state, c_state, layer, c_stack):
    n = z.shape[0]
    tb = SAMPLE_BLOCK
    c_block = pl.BlockSpec((None, tb, N_HEADS, DV, DK), lambda i: (layer, i, 0, 0, 0))
    in_specs = [
        pl.BlockSpec((tb, N_HEADS * DK), lambda i: (i, OFF_Q // (N_HEADS * DK))),
        pl.BlockSpec((tb, N_HEADS * DK), lambda i: (i, OFF_K // (N_HEADS * DK))),
        pl.BlockSpec((tb, N_HEADS * DV), lambda i: (i, OFF_V // (N_HEADS * DV))),
        pl.BlockSpec((tb, 2 * V7X_LANES), lambda i: (i, 0)),
        pl.BlockSpec((1, 2 * V7X_LANES), lambda i: (0, 0)),
        pl.BlockSpec((tb, V7X_LANES), lambda i: (i, 0)),
        pl.BlockSpec((tb, N_HEADS * DK), lambda i: (i, 0)),
        c_block,
    ]
    args = [z, z, z, gates, gbias, m_pad, n_state, c_state]
    aliases = {}
    if c_stack is not None:
        in_specs.append(pl.BlockSpec(memory_space=pl.ANY))
        aliases = {len(args): 1}
        args.append(c_stack)
    out_shape = (
        jax.ShapeDtypeStruct((n, N_HEADS * DV), F32),
        jax.ShapeDtypeStruct(c_state.shape, F32),
        jax.ShapeDtypeStruct((n, N_HEADS * DK), F32),
        jax.ShapeDtypeStruct((n, V7X_LANES), F32),
    )
    out_specs = (
        pl.BlockSpec((tb, N_HEADS * DV), lambda i: (i, 0)),
        c_block,
        pl.BlockSpec((tb, N_HEADS * DK), lambda i: (i, 0)),
        pl.BlockSpec((tb, V7X_LANES), lambda i: (i, 0)),
    )
    return pl.pallas_call(
        _sample_state_kernel,
        grid=(n // tb,), in_specs=in_specs, out_specs=out_specs, out_shape=out_shape,
        input_output_aliases=aliases,
        compiler_params=pltpu.CompilerParams(
            dimension_semantics=("arbitrary",), vmem_limit_bytes=V7X_VMEM_LIMIT),
        name="sample_state",
    )(*args)


def _sample_mixer_kernel(x_ref, z_ref, h_ref, gv_ref, ws0_ref, bs0_ref, wa_ref, wb_ref, wo_ref, *rest):
    xo_ref, vn_ref = rest[-2:]
    y_a = _dot((jax.nn.sigmoid(z_ref[:, OFF_O:OFF_U]) * h_ref[...]).astype(BF16), wa_ref[...])
    u = jax.nn.gelu(z_ref[:, OFF_U:OFF_VB])
    vn = _rmsnorm(jax.nn.gelu(z_ref[:, OFF_VB:OFF_GA]), gv_ref[...])
    vn_ref[:, 0, :] = vn
    mixed = ws0_ref[...] * vn + bs0_ref[...]
    y_b = _dot((u * mixed).astype(BF16), wb_ref[...])
    merged = (jax.nn.sigmoid(z_ref[:, OFF_GA:OFF_GB]) * y_a
              + jax.nn.sigmoid(z_ref[:, OFF_GB:P_MAIN]) * y_b)
    xo_ref[...] = x_ref[...] + _dot(merged.astype(BF16), wo_ref[...])


def _sample_mixer(x, z, h, layer, gv, ws0, bs0, wa, wb, wo, vn_stack):
    n = x.shape[0]
    out_shape = (jax.ShapeDtypeStruct((n, D_MODEL), F32),
                 jax.ShapeDtypeStruct((wa.shape[0], n, 1, D_B), F32))
    return _single_step(_sample_mixer_kernel, "sample_mixer", out_shape,
                        (x, z, h), (gv, ws0, bs0, wa, wb, wo), layer, stacked=(1, vn_stack))


def _sample_ffn_kernel(x_ref, gf_ref, buf_ref, g2_ref, wup_ref, cw_ref, cb_ref, wdn_ref, *rest, final):
    xo_ref, nbuf_ref = rest[-2:]
    x = x_ref[...]
    xn = _rmsnorm(x, g2_ref[...]).astype(BF16)
    acc = jnp.zeros(x.shape, F32)
    for j in range(D_FF // FFN_COL_TILE):
        halves = []
        for half in range(2):
            col0 = half * D_FF + j * FFN_COL_TILE
            cols = slice(col0, col0 + FFN_COL_TILE)
            up = _dot(xn, wup_ref[:, cols])
            b0 = buf_ref[:, 0, cols]
            b1 = buf_ref[:, 1, cols]
            nbuf_ref[:, 0, cols] = b1
            nbuf_ref[:, 1, cols] = up
            halves.append(cb_ref[:, cols] + cw_ref[0:1, cols] * b0 + cw_ref[1:2, cols] * b1
                          + cw_ref[2:3, cols] * up)
        act = (jax.nn.silu(halves[0]) * halves[1]).astype(BF16)
        acc = acc + _dot(act, wdn_ref[j * FFN_COL_TILE:(j + 1) * FFN_COL_TILE, :])
    y = x + acc
    if final:
        xo_ref[:, 0, :] = _rmsnorm(y, gf_ref[...])
    else:
        xo_ref[...] = y


def _sample_ffn(x, gf, layer, buf, g2, wup, cw, cb, wdn, final, conv_stack):
    n = x.shape[0]
    x_out = (n, 1, D_MODEL) if final else (n, D_MODEL)
    out_shape = (jax.ShapeDtypeStruct(x_out, F32), jax.ShapeDtypeStruct(buf.shape, F32))
    return _single_step(functools.partial(_sample_ffn_kernel, final=final), "sample_ffn", out_shape,
                        (x, gf), (buf, g2, wup, cw, cb, wdn), layer, stacked=(1, conv_stack))


def _pack_kernel(wt_ref, o_ref, wgt_ref):
    g8 = wt_ref[GATE_LO:GATE_HI, :]
    head = lax.broadcasted_iota(jnp.int32, g8.shape, 0) < N_HEADS
    wgt_ref[...] = jnp.concatenate(
        [jnp.where(head, g8, 0.0), jnp.where(head, pltpu.roll(g8, N_HEADS, 0), 0.0)], axis=0).astype(BF16)
    for j in range(P_MAIN // PACK_TILE):
        dst = j * PACK_TILE
        src = dst if dst < GATE_LO else dst + (GATE_HI - GATE_LO)
        o_ref[:, dst:dst + PACK_TILE] = wt_ref[src:src + PACK_TILE, :].T.astype(BF16)


def _pack_w_in(w_in_t):
    depth, p_in, d = w_in_t.shape
    return pl.pallas_call(
        _pack_kernel,
        grid=(depth, d // PACK_DIMS),
        in_specs=[pl.BlockSpec((None, p_in, PACK_DIMS), lambda l, r: (l, 0, r))],
        out_specs=(pl.BlockSpec((None, PACK_DIMS, P_MAIN), lambda l, r: (l, r, 0)),
                   pl.BlockSpec((None, 2 * GATE_ROWS, PACK_DIMS), lambda l, r: (l, 0, r))),
        out_shape=(jax.ShapeDtypeStruct((depth, d, P_MAIN), BF16),
                   jax.ShapeDtypeStruct((depth, 2 * GATE_ROWS, d), BF16)),
        compiler_params=pltpu.CompilerParams(
            dimension_semantics=("arbitrary", "arbitrary"), vmem_limit_bytes=V7X_VMEM_LIMIT),
        name="pack_w_in",
    )(w_in_t)


def kernel(x_prompt, x_sample, state_mlstm_C, state_mlstm_n, state_mlstm_m, state_ffn_conv, w_in, b_igate, b_fgate, g_norm1, g_vnorm, w_spatial, b_spatial, w_branch_a, w_branch_b, w_out, g_norm2, w_up, conv_w, conv_b, w_down, g_final):
    depth = w_in.shape[0]
    n_dec = x_sample.shape[0]
    xp = x_prompt
    xs = x_sample.reshape(n_dec, D_MODEL)
    gf = g_final.reshape(1, D_MODEL)

    wm, wgt = _pack_w_in(jnp.swapaxes(w_in, 1, 2))
    head_pad = ((0, 0), (0, GATE_ROWS - N_HEADS))
    gbias_col = jnp.concatenate([jnp.pad(b_igate, head_pad), jnp.pad(b_fgate, head_pad)],
                                axis=1).reshape(depth, 2 * GATE_ROWS, 1)
    head_lanes = ((0, 0), (0, V7X_LANES - N_HEADS))
    gbias_row = jnp.concatenate([jnp.pad(b_igate, head_lanes), jnp.pad(b_fgate, head_lanes)],
                                axis=1).reshape(depth, 1, 2 * V7X_LANES)
    g1 = g_norm1.reshape(depth, 1, D_MODEL)
    gv = g_vnorm.reshape(depth, 1, D_B)
    g2 = g_norm2.reshape(depth, 1, D_MODEL)
    bst = jnp.swapaxes(b_spatial, 1, 2)
    ws0 = jnp.repeat(w_spatial[:, :, 0, 0], DG, axis=1).reshape(depth, 1, D_B)
    bs0 = jnp.repeat(b_spatial[:, :, 0], DG, axis=1).reshape(depth, 1, D_B)
    wa = w_branch_a.astype(BF16)
    wb = w_branch_b.astype(BF16)
    wo = w_out.astype(BF16)
    wup = w_up.astype(BF16)
    wdn = w_down.astype(BF16)
    cb = conv_b.reshape(depth, 1, 2 * D_FF)
    n_state = state_mlstm_n.reshape(depth, n_dec, N_HEADS * DK)
    m_pad = jnp.pad(state_mlstm_m, ((0, 0), (0, 0), (0, V7X_LANES - N_HEADS)))

    small = [[] for _ in range(4)]
    cp_stack = None
    convp_stack = None
    c_stack = None
    vn_stack = None
    conv_stack = None
    for l in range(depth):
        final = l == depth - 1

        xp, cp_stack, n_p, m_p = _prompt_mixer(xp, l, wm, wgt, gbias_col, g1, gv, w_spatial, bst, wa, wb, wo,
                                               cp_stack)
        xp, convp_stack = _prompt_ffn(xp, l, g2, wup, conv_w, cb, wdn, gf, final, convp_stack)

        z, gates = _sample_proj(xs, l, g1, wm, wgt)
        h, c_stack, n_s, m_s = _sample_state(
            z, gates, gbias_row[l], m_pad[l], n_state[l], state_mlstm_C, l, c_stack)
        xs, vn_stack = _sample_mixer(xs, z, h, l, gv, ws0, bs0, wa, wb, wo, vn_stack)
        xs, conv_stack = _sample_ffn(xs, gf, l, state_ffn_conv, g2, wup, conv_w, cb, wdn, final, conv_stack)

        for lst, val in zip(small, (n_p, m_p[:, :N_HEADS, 0],
                                    n_s.reshape(n_dec, N_HEADS, DK), m_s[:, :N_HEADS])):
            lst.append(val)
    st = [jnp.stack(o) for o in small]
    return (xp, xs, cp_stack, st[0], st[1], convp_stack, c_stack, st[2], st[3], conv_stack, vn_stack)
```

```python
import functools

import jax
import jax.numpy as jnp
from jax import lax
from jax.experimental import pallas as pl
from jax.experimental.pallas import tpu as pltpu

D_MODEL = 1024
N_HEADS = 4
DK = 128
DV = 256
CHUNK = 128
D_B = 1024
N_GROUPS = 4
DG = D_B // N_GROUPS
D_FF = 2816
CONV_W = 3
EPS = 1e-6
K_SCALE = DK ** -0.5

OFF_Q = 0
OFF_K = OFF_Q + N_HEADS * DK
OFF_V = OFF_K + N_HEADS * DK
OFF_O = OFF_V + N_HEADS * DV
OFF_U = OFF_O + N_HEADS * DV
OFF_VB = OFF_U + D_B
OFF_GA = OFF_VB + D_B
OFF_GB = OFF_GA + D_MODEL
P_MAIN = OFF_GB + D_MODEL
GATE_LO = 2 * N_HEADS * DK + 2 * N_HEADS * DV
GATE_HI = GATE_LO + 2 * N_HEADS

V7X_LANES = 128
V7X_SUBLANES = 8
GATE_ROWS = V7X_SUBLANES
V7X_VMEM_LIMIT = 56 * 1024 * 1024
FFN_COL_TILE = 256
PROJ_TILE = 256
MIXER_BLOCK = 512
FFN_BLOCK = 512
SAMPLE_BLOCK = 16
PACK_TILE = 256
PACK_DIMS = 256
SAMPLE_PROJ_TILE = P_MAIN // 2
SAMPLE_FFN_STEPS = 2

F32 = jnp.float32
BF16 = jnp.bfloat16


def _dot(a, b):
    return jnp.dot(a, b, preferred_element_type=F32)


def _dot_nt(a, b):
    return lax.dot_general(a, b, (((1,), (1,)), ((), ())), preferred_element_type=F32)


def _rmsnorm(x, g):
    r = lax.rsqrt(jnp.mean(x * x, axis=-1, keepdims=True) + EPS)
    return x * r * g


def _log_sigmoid(x):
    return jnp.minimum(x, 0.0) - jnp.log1p(jnp.exp(-jnp.abs(x)))


def _scan_lanes(x, op, fill):
    lane = lax.broadcasted_iota(jnp.int32, x.shape, 1)
    k = 1
    while k < x.shape[1]:
        shifted = pltpu.roll(x, k, 1)
        x = op(x, jnp.where(lane >= k, shifted, fill))
        k *= 2
    return x


def _anchor_zero(x):
    sub, lanes = V7X_SUBLANES, V7X_LANES
    acc = jnp.zeros((sub, lanes), jnp.uint32)
    for r in range(x.shape[0] // sub):
        for c in range(x.shape[1] // lanes):
            piece = pltpu.bitcast(x[r * sub:(r + 1) * sub, c * lanes:(c + 1) * lanes], jnp.uint32)
            acc = acc | ((piece >> 16) >> 16)
    return pltpu.bitcast(acc, F32)


def _next_block_map(nb, steps):
    def index_map(b, s):
        nxt = jnp.minimum(b * steps + s + 1, nb * steps - 1)
        return (nxt // steps, nxt % steps, 0)
    return index_map


def _mixer_kernel(x_ref, wm_ref, wgt_ref, gbias_ref, g1_ref, gv_ref, ws_ref, bst_ref,
                  wa_ref, wb_ref, wo_ref, *rest, block):
    (xo_ref, c_ref, n_ref, m_ref,
     ct_s, n_s, m_s, q_s, k_s, v_s, h_s, so_s, u_s, vb_s, sg_s, um_s) = rest[-16:]
    s = pl.program_id(1)
    n_chunks = block // CHUNK

    @pl.when(s == 0)
    def _():
        ct_s[...] = jnp.zeros_like(ct_s)
        n_s[...] = jnp.zeros_like(n_s)
        m_s[...] = jnp.zeros_like(m_s)

    xn = _rmsnorm(x_ref[...], g1_ref[...]).astype(BF16)

    gates = _dot_nt(wgt_ref[...], xn) + gbias_ref[...]

    def proj(off, t):
        return _dot(xn, wm_ref[:, off + t * PROJ_TILE:off + (t + 1) * PROJ_TILE])

    def tile(t):
        return slice(t * PROJ_TILE, (t + 1) * PROJ_TILE)

    for t in range(N_HEADS * DK // PROJ_TILE):
        q_s[:, tile(t)] = proj(OFF_Q, t)
        k_s[:, tile(t)] = proj(OFF_K, t) * K_SCALE
    for t in range(N_HEADS * DV // PROJ_TILE):
        v_s[:, tile(t)] = proj(OFF_V, t)

    sumsq = [jnp.zeros((block, 1), F32)]

    def vb_tile(t):
        g = jax.nn.gelu(proj(OFF_VB, t))
        vb_s[:, tile(t)] = g
        sumsq[0] = sumsq[0] + jnp.sum(g * g, axis=-1, keepdims=True)

    def u_tile(t):
        u_s[:, tile(t)] = jax.nn.gelu(proj(OFF_U, t))

    def o_tile(t):
        so_s[:, tile(t)] = jax.nn.sigmoid(proj(OFF_O, t))

    def ga_tile(t):
        sg_s[:, tile(t)] = jax.nn.sigmoid(proj(OFF_GA, t))

    def gb_tile(t):
        sg_s[:, D_MODEL + t * PROJ_TILE:D_MODEL + (t + 1) * PROJ_TILE] = jax.nn.sigmoid(proj(OFF_GB, t))

    jobs = [(f, t) for f in (vb_tile, u_tile, o_tile, ga_tile, gb_tile) for t in range(D_MODEL // PROJ_TILE)]

    def run_jobs(count):
        for _ in range(min(count, len(jobs))):
            f, t = jobs.pop(0)
            f(t)

    row_i = lax.broadcasted_iota(jnp.int32, (CHUNK, CHUNK), 0)
    col_i = lax.broadcasted_iota(jnp.int32, (CHUNK, CHUNK), 1)
    causal = row_i >= col_i
    heads = range(N_HEADS)

    for c in range(n_chunks):
        r0 = c * CHUNK
        ig = gates[0:GATE_ROWS, r0:r0 + CHUNK]
        lf = _log_sigmoid(gates[GATE_ROWS:2 * GATE_ROWS, r0:r0 + CHUNK])
        b = _scan_lanes(lf, jnp.add, 0.0)
        a = ig - b
        m_prev = m_s[...]
        gmax = jnp.maximum(m_prev, _scan_lanes(a, jnp.maximum, -jnp.inf))
        m_t = b + gmax
        w_inter = jnp.exp(m_prev - gmax)
        g_last = gmax[:, CHUNK - 1:CHUNK]
        w_last = jnp.exp(a - g_last)
        floor = jnp.exp(-m_t)
        decay = w_inter[:, CHUNK - 1:CHUNK]
        m_s[...] = jnp.broadcast_to(m_t[:, CHUNK - 1:CHUNK], m_s.shape)

        rows = jnp.concatenate(
            [gmax, w_inter, w_last, floor,
             jnp.zeros((CHUNK - 4 * GATE_ROWS, CHUNK), F32)], axis=0)
        cols = rows.T

        def col(kind, h):
            return cols[:, kind * GATE_ROWS + h:kind * GATE_ROWS + h + 1]

        qf = [q_s[r0:r0 + CHUNK, h * DK:(h + 1) * DK] for h in heads]
        kf = [k_s[r0:r0 + CHUNK, h * DK:(h + 1) * DK] for h in heads]
        vf = [v_s[r0:r0 + CHUNK, h * DV:(h + 1) * DV] for h in heads]
        qb = [x.astype(BF16) for x in qf]
        kt = [x.T.astype(BF16) for x in kf]
        run_jobs(2)
        zero = jnp.zeros((DK, CHUNK), BF16)
        sc = []
        for h in range(0, N_HEADS, 2):
            kk = jnp.concatenate([jnp.concatenate([kt[h], zero], axis=1),
                                  jnp.concatenate([zero, kt[h + 1]], axis=1)], axis=0)
            pair = _dot(jnp.concatenate([qb[h], qb[h + 1]], axis=1), kk)
            sc += [pair[:, :CHUNK], pair[:, CHUNK:]]
        dmat = [jnp.where(causal, jnp.exp(a[h:h + 1, :] - col(0, h)), 0.0) for h in heads]
        run_jobs(1)
        sd = [sc[h] * dmat[h] for h in heads]
        ct = [ct_s[h] for h in heads]
        num = [_dot(jnp.concatenate([sd[h], col(1, h) * qf[h]], axis=1).astype(BF16),
                    jnp.concatenate([vf[h], ct[h]], axis=0).astype(BF16)) for h in heads]
        run_jobs(2)
        for h in heads:
            nh = n_s[h:h + 1, :]
            den = (jnp.sum(sd[h], axis=-1, keepdims=True)
                   + col(1, h) * jnp.sum(qf[h] * nh, axis=-1, keepdims=True))
            h_s[r0:r0 + CHUNK, h * DV:(h + 1) * DV] = num[h] * (1.0 / jnp.maximum(jnp.abs(den), col(3, h)))
            dec = decay[h:h + 1, :]
            n_s[h:h + 1, :] = dec * nh + jnp.sum(col(2, h) * kf[h], axis=0, keepdims=True)
        run_jobs(1)
        for h in heads:
            ct_s[h] = decay[h:h + 1, :] * ct[h] + _dot(kt[h], (col(2, h) * vf[h]).astype(BF16))
    run_jobs(len(jobs))

    y_a = _dot((so_s[...] * h_s[...]).astype(BF16), wa_ref[...])

    rinv = lax.rsqrt(sumsq[0] * (1.0 / D_B) + EPS)
    for g in range(N_GROUPS):
        gcols = slice(g * DG, (g + 1) * DG)
        w_tri = jnp.where(causal, ws_ref[g], 0.0).astype(BF16)
        bias_c = bst_ref[:, g:g + 1]
        for c in range(n_chunks):
            rws = slice(c * CHUNK, (c + 1) * CHUNK)
            vn = vb_s[rws, gcols] * rinv[rws] * gv_ref[:, gcols]
            um_s[rws, gcols] = u_s[rws, gcols] * (_dot(w_tri, vn.astype(BF16)) + bias_c)
    y_b = _dot(um_s[...].astype(BF16), wb_ref[...])

    merged = sg_s[:, 0:D_MODEL] * y_a + sg_s[:, D_MODEL:2 * D_MODEL] * y_b
    xo_ref[...] = x_ref[...] + _dot(merged.astype(BF16), wo_ref[...])

    @pl.when(s == pl.num_programs(1) - 1)
    def _():
        for h in range(N_HEADS):
            c_ref[h] = ct_s[h].T
        n_ref[...] = n_s[...]
        m_ref[...] = m_s[...]


def _resident(arr, layer=None):
    if layer is None:
        nd = arr.ndim
        return pl.BlockSpec(arr.shape, lambda *_: (0,) * nd, pipeline_mode=pl.Buffered(1))
    nd = arr.ndim - 1
    return pl.BlockSpec((None,) + arr.shape[1:], lambda *_: (layer,) + (0,) * nd,
                        pipeline_mode=pl.Buffered(1))


def _prompt_mixer(x, layer, wm, wgt, gbias, g1, gv, ws, bst, wa, wb, wo, c_stack):
    nb, seq, _ = x.shape
    depth = wm.shape[0]
    block = MIXER_BLOCK
    grid = (nb, seq // block)
    xspec = pl.BlockSpec((None, block, D_MODEL), lambda b, s: (b, s, 0))
    out_shape = (
        jax.ShapeDtypeStruct((nb, seq, D_MODEL), F32),
        jax.ShapeDtypeStruct((depth, nb, N_HEADS, DV, DK), F32),
        jax.ShapeDtypeStruct((nb, N_HEADS, DK), F32),
        jax.ShapeDtypeStruct((nb, GATE_ROWS, V7X_LANES), F32),
    )
    out_specs = (
        xspec,
        pl.BlockSpec((None, None, N_HEADS, DV, DK), lambda b, s: (layer, b, 0, 0, 0)),
        pl.BlockSpec((None, N_HEADS, DK), lambda b, s: (b, 0, 0)),
        pl.BlockSpec((None, GATE_ROWS, V7X_LANES), lambda b, s: (b, 0, 0)),
    )
    in_specs = [xspec] + [_resident(a, layer) for a in (wm, wgt, gbias, g1, gv, ws, bst, wa, wb, wo)]
    args = [x, wm, wgt, gbias, g1, gv, ws, bst, wa, wb, wo]
    aliases = {}
    if c_stack is not None:
        in_specs.append(pl.BlockSpec(memory_space=pl.ANY))
        aliases = {len(args): 1}
        args.append(c_stack)
    scratch = [
        pltpu.VMEM((N_HEADS, DK, DV), F32),
        pltpu.VMEM((N_HEADS, DK), F32),
        pltpu.VMEM((GATE_ROWS, V7X_LANES), F32),
        pltpu.VMEM((block, N_HEADS * DK), F32),
        pltpu.VMEM((block, N_HEADS * DK), F32),
        pltpu.VMEM((block, N_HEADS * DV), F32),
        pltpu.VMEM((block, N_HEADS * DV), F32),
        pltpu.VMEM((block, N_HEADS * DV), F32),
        pltpu.VMEM((block, D_B), F32),
        pltpu.VMEM((block, D_B), F32),
        pltpu.VMEM((block, 2 * D_MODEL), F32),
        pltpu.VMEM((block, D_B), F32),
    ]
    return pl.pallas_call(
        functools.partial(_mixer_kernel, block=block),
        grid=grid, in_specs=in_specs, out_specs=out_specs, out_shape=out_shape,
        scratch_shapes=scratch, input_output_aliases=aliases,
        compiler_params=pltpu.CompilerParams(
            dimension_semantics=("arbitrary", "arbitrary"), vmem_limit_bytes=V7X_VMEM_LIMIT),
        name="prompt_mixer",
    )(*args)


def _conv_taps(up, carry_s, cw_ref, cb_ref, cols):
    sub = V7X_SUBLANES
    rows = up.shape[0]
    last1 = up[rows - sub:rows]
    last2 = up[rows - 2 * sub:rows - sub]
    first = lax.broadcasted_iota(jnp.int32, last1.shape, 0) == 0
    back1 = jnp.where(first, carry_s[sub - 1:sub, cols], pltpu.roll(last1, 1, 0))
    back2 = jnp.where(first, carry_s[sub - 2:sub - 1, cols], pltpu.roll(last2, 1, 0))
    carry_s[sub - 2:sub - 1, cols] = last2[sub - 1:sub]
    carry_s[sub - 1:sub, cols] = last1[sub - 1:sub]
    m1 = jnp.concatenate([back1, up[0:rows - sub]], axis=0)
    m2 = jnp.concatenate([back2, back1, up[0:rows - 2 * sub]], axis=0)
    return (cb_ref[:, cols] + cw_ref[0:1, cols] * m2 + cw_ref[1:2, cols] * m1
            + cw_ref[2:3, cols] * up)


def _perm_pitch(block):
    return block // V7X_SUBLANES + V7X_SUBLANES


def _ffn_kernel(x_ref, xnext_ref, g2_ref, wup_ref, cw_ref, cb_ref, wdn_ref, gf_ref, *rest, block, final):
    xo_ref, conv_ref, carry_s, act_s, perm_s, unperm_s, xn_s = rest[-7:]
    s = pl.program_id(1)
    sub, lanes = V7X_SUBLANES, V7X_LANES
    groups = block // sub
    chunks = D_MODEL // lanes
    pitch = _perm_pitch(block)

    @pl.when(s == 0)
    def _():
        carry_s[...] = jnp.zeros_like(carry_s)

    def stage(src_ref):
        for c in range(chunks):
            for i in range(sub):
                perm_s[c, i * pitch:i * pitch + groups] = src_ref[i * groups:(i + 1) * groups,
                                                                  c * lanes:(c + 1) * lanes]
        x = jnp.concatenate(
            [jnp.concatenate([perm_s[c, pl.ds(r, sub, stride=pitch), :] for c in range(chunks)], axis=1)
             for r in range(groups)], axis=0)
        xn = _rmsnorm(x, g2_ref[...])
        xn_s[...] = xn.astype(BF16)
        return xn

    @pl.when((pl.program_id(0) == 0) & (s == 0))
    def _():
        stage(x_ref)

    xn = xn_s[...]
    for j in range(D_FF // FFN_COL_TILE):
        halves = []
        for half in range(2):
            cols = slice(half * D_FF + j * FFN_COL_TILE, half * D_FF + (j + 1) * FFN_COL_TILE)
            up = _dot(xn, wup_ref[:, cols])
            halves.append(_conv_taps(up, carry_s, cw_ref, cb_ref, cols))
        act_s[:, j * FFN_COL_TILE:(j + 1) * FFN_COL_TILE] = (jax.nn.silu(halves[0]) * halves[1]).astype(BF16)
    staged = stage(xnext_ref)
    down = _dot(act_s[...], wdn_ref[...])
    anchor = _anchor_zero(staged)
    for r in range(groups):
        for c in range(chunks):
            tile_rc = down[r * sub:(r + 1) * sub, c * lanes:(c + 1) * lanes]
            if r == 0 and c == 0:
                tile_rc = tile_rc + anchor
            unperm_s[c, pl.ds(r, sub, stride=pitch), :] = tile_rc
    for i in range(sub):
        rows = slice(i * groups, (i + 1) * groups)
        y = x_ref[rows, :] + jnp.concatenate(
            [unperm_s[c, i * pitch:i * pitch + groups] for c in range(chunks)], axis=1)
        if final:
            y = _rmsnorm(y, gf_ref[...])
        xo_ref[rows, :] = y

    @pl.when(s == pl.num_programs(1) - 1)
    def _():
        conv_ref[...] = carry_s[V7X_SUBLANES - (CONV_W - 1):V7X_SUBLANES, :]


def _prompt_ffn(x, layer, g2, wup, cw, cb, wdn, gf, final, conv_stack):
    nb, seq, _ = x.shape
    depth = wup.shape[0]
    block = FFN_BLOCK
    steps = seq // block
    grid = (nb, steps)
    xspec = pl.BlockSpec((None, block, D_MODEL), lambda b, s: (b, s, 0))
    out_shape = (
        jax.ShapeDtypeStruct((nb, seq, D_MODEL), F32),
        jax.ShapeDtypeStruct((depth, nb, CONV_W - 1, 2 * D_FF), F32),
    )
    out_specs = (xspec, pl.BlockSpec((None, None, CONV_W - 1, 2 * D_FF), lambda b, s: (layer, b, 0, 0)))
    in_specs = ([xspec, pl.BlockSpec((None, block, D_MODEL), _next_block_map(nb, steps))]
                + [_resident(a, layer) for a in (g2, wup, cw, cb, wdn)] + [_resident(gf)])
    args = [x, x, g2, wup, cw, cb, wdn, gf]
    aliases = {}
    if conv_stack is not None:
        in_specs.append(pl.BlockSpec(memory_space=pl.ANY))
        aliases = {len(args): 1}
        args.append(conv_stack)
    reorder = pltpu.VMEM((D_MODEL // V7X_LANES, V7X_SUBLANES * _perm_pitch(block), V7X_LANES), F32)
    return pl.pallas_call(
        functools.partial(_ffn_kernel, block=block, final=final),
        grid=grid, in_specs=in_specs, out_specs=out_specs, out_shape=out_shape,
        scratch_shapes=[pltpu.VMEM((V7X_SUBLANES, 2 * D_FF), F32),
                        pltpu.VMEM((block, D_FF), BF16),
                        reorder, reorder,
                        pltpu.VMEM((block, D_MODEL), BF16)],
        input_output_aliases=aliases,
        compiler_params=pltpu.CompilerParams(
            dimension_semantics=("arbitrary", "arbitrary"), vmem_limit_bytes=V7X_VMEM_LIMIT),
        name="prompt_ffn",
    )(*args)


def _sample_proj_kernel(x_ref, g1_ref, wgt_ref, wm_ref, z_ref, gates_ref):
    xn = _rmsnorm(x_ref[...], g1_ref[...]).astype(BF16)
    z_ref[...] = _dot(xn, wm_ref[...])

    @pl.when(pl.program_id(0) == 0)
    def _():
        gt = _dot_nt(wgt_ref[...], xn)
        gt = jnp.concatenate([gt, jnp.zeros((V7X_LANES - gt.shape[0], gt.shape[1]), F32)], axis=0)
        g = gt.T
        gates_ref[...] = jnp.concatenate([g, pltpu.roll(g, V7X_LANES - GATE_ROWS, 1)], axis=1)


def _single_step(kernel_fn, name, out_shape, whole, layered, layer, stacked=None):
    in_specs = [_resident(a) for a in whole] + [_resident(a, layer) for a in layered]
    args = list(whole) + list(layered)
    out_specs = [pl.BlockSpec(o.shape, lambda i, nd=len(o.shape): (0,) * nd) for o in out_shape]
    aliases = {}
    if stacked is not None:
        k, prev = stacked
        nd = len(out_shape[k].shape) - 1
        out_specs[k] = pl.BlockSpec((None,) + out_shape[k].shape[1:], lambda i: (layer,) + (0,) * nd)
        if prev is not None:
            in_specs.append(pl.BlockSpec(memory_space=pl.ANY))
            aliases = {len(args): k}
            args.append(prev)
    return pl.pallas_call(
        kernel_fn,
        grid=(1,),
        in_specs=in_specs, out_specs=tuple(out_specs), out_shape=out_shape,
        input_output_aliases=aliases,
        compiler_params=pltpu.CompilerParams(
            dimension_semantics=("arbitrary",), vmem_limit_bytes=V7X_VMEM_LIMIT),
        name=name,
    )(*args)


def _sample_proj(x, layer, g1, wm, wgt):
    n = x.shape[0]
    assert n == V7X_LANES, "the gate transpose assumes one lane tile of sample rows"
    tile = SAMPLE_PROJ_TILE
    return pl.pallas_call(
        _sample_proj_kernel,
        grid=(P_MAIN // tile,),
        in_specs=[_resident(x), _resident(g1, layer), _resident(wgt, layer),
                  pl.BlockSpec((None, D_MODEL, tile), lambda j: (layer, 0, j))],
        out_specs=(pl.BlockSpec((n, tile), lambda j: (0, j)),
                   pl.BlockSpec((n, 2 * V7X_LANES), lambda j: (0, 0))),
        out_shape=(jax.ShapeDtypeStruct((n, P_MAIN), F32), jax.ShapeDtypeStruct((n, 2 * V7X_LANES), F32)),
        compiler_params=pltpu.CompilerParams(
            dimension_semantics=("arbitrary",), vmem_limit_bytes=V7X_VMEM_LIMIT),
        name="sample_proj",
    )(x, g1, wgt, wm)


def _sample_state_kernel(q_ref, k_ref, v_ref, gates_ref, gbias_ref, m_ref, n_ref, c_ref, *rest):
    h_ref, co_ref, no_ref, mo_ref = rest[-4:]
    tb = SAMPLE_BLOCK
    ig = gates_ref[:, 0:V7X_LANES] + gbias_ref[:, 0:V7X_LANES]
    lf = _log_sigmoid(gates_ref[:, V7X_LANES:] + gbias_ref[:, V7X_LANES:])
    inter = lf + m_ref[...]
    m_t = jnp.maximum(inter, ig)
    d_in = jnp.exp(ig - m_t)
    w_inter = jnp.exp(inter - m_t)
    floor = jnp.exp(-m_t)
    mo_ref[...] = m_t

    row8 = lax.broadcasted_iota(jnp.int32, (CHUNK, DK), 0)
    for h in range(N_HEADS):
        q8 = q_ref[:, h * DK:(h + 1) * DK]
        k8 = k_ref[:, h * DK:(h + 1) * DK] * K_SCALE
        v8 = v_ref[:, h * DV:(h + 1) * DV]
        d_h = d_in[:, h:h + 1]
        w_h = w_inter[:, h:h + 1]
        n8 = n_ref[:, h * DK:(h + 1) * DK]
        s = jnp.sum(q8 * k8, axis=-1, keepdims=True) * d_h
        den = s + w_h * jnp.sum(q8 * n8, axis=-1, keepdims=True)
        qb = q8.astype(BF16)
        inter_rows = [_dot_nt(qb, c_ref[j, h].astype(BF16))[j:j + 1] for j in range(tb)]
        num = s * v8 + w_h * jnp.concatenate(inter_rows, axis=0)
        h_ref[:, h * DV:(h + 1) * DV] = num / jnp.maximum(jnp.abs(den), floor[:, h:h + 1])
        no_ref[:, h * DK:(h + 1) * DK] = w_h * n8 + d_h * k8

        vt = jnp.concatenate([d_h * v8, jnp.zeros((CHUNK - tb, DV), F32)], axis=0).T.astype(BF16)
        kpad = jnp.concatenate([k8, jnp.zeros((CHUNK - tb, DK), F32)], axis=0)
        for j in range(tb):
            kj = jnp.where(row8 == j, kpad, 0.0).astype(BF16)
            co_ref[j, h] = w_inter[j:j + 1, h:h + 1] * c_ref[j, h] + _dot(vt, kj)


def _sample_state(z, gates, gbias, m_pad, n_state, c_state, layer, c_stack):
    n = z.shape[0]
    tb = SAMPLE_BLOCK
    c_block = pl.BlockSpec((None, tb, N_HEADS, DV, DK), lambda i: (layer, i, 0, 0, 0))
    in_specs = [
        pl.BlockSpec((tb, N_HEADS * DK), lambda i: (i, OFF_Q // (N_HEADS * DK))),
        pl.BlockSpec((tb, N_HEADS * DK), lambda i: (i, OFF_K // (N_HEADS * DK))),
        pl.BlockSpec((tb, N_HEADS * DV), lambda i: (i, OFF_V // (N_HEADS * DV))),
        pl.BlockSpec((tb, 2 * V7X_LANES), lambda i: (i, 0)),
        pl.BlockSpec((1, 2 * V7X_LANES), lambda i: (0, 0)),
        pl.BlockSpec((tb, V7X_LANES), lambda i: (i, 0)),
        pl.BlockSpec((tb, N_HEADS * DK), lambda i: (i, 0)),
        c_block,
    ]
    args = [z, z, z, gates, gbias, m_pad, n_state, c_state]
    aliases = {}
    if c_stack is not None:
        in_specs.append(pl.BlockSpec(memory_space=pl.ANY))
        aliases = {len(args): 1}
        args.append(c_stack)
    out_shape = (
        jax.ShapeDtypeStruct((n, N_HEADS * DV), F32),
        jax.ShapeDtypeStruct(c_state.shape, F32),
        jax.ShapeDtypeStruct((n, N_HEADS * DK), F32),
        jax.ShapeDtypeStruct((n, V7X_LANES), F32),
    )
    out_specs = (
        pl.BlockSpec((tb, N_HEADS * DV), lambda i: (i, 0)),
        c_block,
        pl.BlockSpec((tb, N_HEADS * DK), lambda i: (i, 0)),
        pl.BlockSpec((tb, V7X_LANES), lambda i: (i, 0)),
    )
    return pl.pallas_call(
        _sample_state_kernel,
        grid=(n // tb,), in_specs=in_specs, out_specs=out_specs, out_shape=out_shape,
        input_output_aliases=aliases,
        compiler_params=pltpu.CompilerParams(
            dimension_semantics=("arbitrary",), vmem_limit_bytes=V7X_VMEM_LIMIT),
        name="sample_state",
    )(*args)


def _sample_mixer_kernel(x_ref, z_ref, h_ref, gv_ref, ws0_ref, bs0_ref, wa_ref, wb_ref, wo_ref, *rest):
    xo_ref, vn_ref = rest[-2:]
    y_a = _dot((jax.nn.sigmoid(z_ref[:, OFF_O:OFF_U]) * h_ref[...]).astype(BF16), wa_ref[...])
    u = jax.nn.gelu(z_ref[:, OFF_U:OFF_VB])
    vn = _rmsnorm(jax.nn.gelu(z_ref[:, OFF_VB:OFF_GA]), gv_ref[...])
    vn_ref[:, 0, :] = vn
    mixed = ws0_ref[...] * vn + bs0_ref[...]
    y_b = _dot((u * mixed).astype(BF16), wb_ref[...])
    merged = (jax.nn.sigmoid(z_ref[:, OFF_GA:OFF_GB]) * y_a
              + jax.nn.sigmoid(z_ref[:, OFF_GB:P_MAIN]) * y_b)
    xo_ref[...] = x_ref[...] + _dot(merged.astype(BF16), wo_ref[...])


def _sample_mixer(x, z, h, layer, gv, ws0, bs0, wa, wb, wo, vn_stack):
    n = x.shape[0]
    out_shape = (jax.ShapeDtypeStruct((n, D_MODEL), F32),
                 jax.ShapeDtypeStruct((wa.shape[0], n, 1, D_B), F32))
    return _single_step(_sample_mixer_kernel, "sample_mixer", out_shape,
                        (x, z, h), (gv, ws0, bs0, wa, wb, wo), layer, stacked=(1, vn_stack))


def _sample_ffn_kernel(x_ref, gf_ref, g2_ref, bufa_ref, bufg_ref, wupa_ref, wupg_ref, cwa_ref, cwg_ref,
                       cba_ref, cbg_ref, wdn_ref, *rest, final):
    xo_ref, nbuf_ref, acc_s = rest[-3:]
    j = pl.program_id(0)
    last = SAMPLE_FFN_STEPS - 1
    width = D_FF // SAMPLE_FFN_STEPS
    x = x_ref[...]
    xn = _rmsnorm(x, g2_ref[...]).astype(BF16)

    def conv(up, buf_ref, cw_ref, cb_ref):
        b0 = buf_ref[:, 0, :]
        b1 = buf_ref[:, 1, :]
        return b1, cb_ref[...] + cw_ref[0:1, :] * b0 + cw_ref[1:2, :] * b1 + cw_ref[2:3, :] * up

    up_a = _dot(xn, wupa_ref[...])
    up_g = _dot(xn, wupg_ref[...])
    prev_a, conv_a = conv(up_a, bufa_ref, cwa_ref, cba_ref)
    prev_g, conv_g = conv(up_g, bufg_ref, cwg_ref, cbg_ref)
    part = _dot((jax.nn.silu(conv_a) * conv_g).astype(BF16), wdn_ref[...])

    for jj in range(SAMPLE_FFN_STEPS):
        @pl.when(j == jj)
        def _():
            for col0, prev, up in ((jj * width, prev_a, up_a), (D_FF + jj * width, prev_g, up_g)):
                nbuf_ref[:, 0, col0:col0 + width] = prev
                nbuf_ref[:, 1, col0:col0 + width] = up

    @pl.when(j == 0)
    def _():
        acc_s[...] = part

    @pl.when((j > 0) & (j < last))
    def _():
        acc_s[...] += part

    @pl.when(j == last)
    def _():
        y = x + (acc_s[...] + part)
        if final:
            xo_ref[:, 0, :] = _rmsnorm(y, gf_ref[...])
        else:
            xo_ref[...] = y


def _sample_ffn(x, gf, layer, buf, g2, wup, cw, cb, wdn, final, conv_stack):
    n = x.shape[0]
    steps = SAMPLE_FFN_STEPS
    width = D_FF // steps

    def cols(shape, gate):
        nd = len(shape)
        return pl.BlockSpec((None,) + shape[:-1] + (width,),
                            lambda j: (layer,) + (0,) * (nd - 1) + (gate * steps + j,))

    in_specs = [_resident(x), _resident(gf), _resident(g2, layer),
                cols(buf.shape[1:], 0), cols(buf.shape[1:], 1),
                cols(wup.shape[1:], 0), cols(wup.shape[1:], 1),
                cols(cw.shape[1:], 0), cols(cw.shape[1:], 1),
                cols(cb.shape[1:], 0), cols(cb.shape[1:], 1),
                pl.BlockSpec((None, width, D_MODEL), lambda j: (layer, j, 0))]
    args = [x, gf, g2, buf, buf, wup, wup, cw, cw, cb, cb, wdn]
    aliases = {}
    if conv_stack is not None:
        in_specs.append(pl.BlockSpec(memory_space=pl.ANY))
        aliases = {len(args): 1}
        args.append(conv_stack)
    x_out = (n, 1, D_MODEL) if final else (n, D_MODEL)
    nd_out = len(x_out)
    return pl.pallas_call(
        functools.partial(_sample_ffn_kernel, final=final),
        grid=(steps,),
        in_specs=in_specs,
        out_specs=(pl.BlockSpec(x_out, lambda j: (0,) * nd_out),
                   pl.BlockSpec((None,) + buf.shape[1:], lambda j: (layer, 0, 0, 0))),
        out_shape=(jax.ShapeDtypeStruct(x_out, F32), jax.ShapeDtypeStruct(buf.shape, F32)),
        scratch_shapes=[pltpu.VMEM((n, D_MODEL), F32)],
        input_output_aliases=aliases,
        compiler_params=pltpu.CompilerParams(
            dimension_semantics=("arbitrary",), vmem_limit_bytes=V7X_VMEM_LIMIT),
        name="sample_ffn",
    )(*args)


def _pack_kernel(wt_ref, o_ref, wgt_ref):
    g8 = wt_ref[GATE_LO:GATE_HI, :]
    head = lax.broadcasted_iota(jnp.int32, g8.shape, 0) < N_HEADS
    wgt_ref[...] = jnp.concatenate(
        [jnp.where(head, g8, 0.0), jnp.where(head, pltpu.roll(g8, N_HEADS, 0), 0.0)], axis=0).astype(BF16)
    for j in range(P_MAIN // PACK_TILE):
        dst = j * PACK_TILE
        src = dst if dst < GATE_LO else dst + (GATE_HI - GATE_LO)
        o_ref[:, dst:dst + PACK_TILE] = wt_ref[src:src + PACK_TILE, :].T.astype(BF16)


def _pack_w_in(w_in_t):
    depth, p_in, d = w_in_t.shape
    return pl.pallas_call(
        _pack_kernel,
        grid=(depth, d // PACK_DIMS),
        in_specs=[pl.BlockSpec((None, p_in, PACK_DIMS), lambda l, r: (l, 0, r))],
        out_specs=(pl.BlockSpec((None, PACK_DIMS, P_MAIN), lambda l, r: (l, r, 0)),
                   pl.BlockSpec((None, 2 * GATE_ROWS, PACK_DIMS), lambda l, r: (l, 0, r))),
        out_shape=(jax.ShapeDtypeStruct((depth, d, P_MAIN), BF16),
                   jax.ShapeDtypeStruct((depth, 2 * GATE_ROWS, d), BF16)),
        compiler_params=pltpu.CompilerParams(
            dimension_semantics=("arbitrary", "arbitrary"), vmem_limit_bytes=V7X_VMEM_LIMIT),
        name="pack_w_in",
    )(w_in_t)


def kernel(x_prompt, x_sample, state_mlstm_C, state_mlstm_n, state_mlstm_m, state_ffn_conv, w_in, b_igate, b_fgate, g_norm1, g_vnorm, w_spatial, b_spatial, w_branch_a, w_branch_b, w_out, g_norm2, w_up, conv_w, conv_b, w_down, g_final):
    depth = w_in.shape[0]
    n_dec = x_sample.shape[0]
    xp = x_prompt
    xs = x_sample.reshape(n_dec, D_MODEL)
    gf = g_final.reshape(1, D_MODEL)

    wm, wgt = _pack_w_in(jnp.swapaxes(w_in, 1, 2))
    head_pad = ((0, 0), (0, GATE_ROWS - N_HEADS))
    gbias_col = jnp.concatenate([jnp.pad(b_igate, head_pad), jnp.pad(b_fgate, head_pad)],
                                axis=1).reshape(depth, 2 * GATE_ROWS, 1)
    head_lanes = ((0, 0), (0, V7X_LANES - N_HEADS))
    gbias_row = jnp.concatenate([jnp.pad(b_igate, head_lanes), jnp.pad(b_fgate, head_lanes)],
                                axis=1).reshape(depth, 1, 2 * V7X_LANES)
    g1 = g_norm1.reshape(depth, 1, D_MODEL)
    gv = g_vnorm.reshape(depth, 1, D_B)
    g2 = g_norm2.reshape(depth, 1, D_MODEL)
    bst = jnp.swapaxes(b_spatial, 1, 2)
    ws0 = jnp.repeat(w_spatial[:, :, 0, 0], DG, axis=1).reshape(depth, 1, D_B)
    bs0 = jnp.repeat(b_spatial[:, :, 0], DG, axis=1).reshape(depth, 1, D_B)
    wa = w_branch_a.astype(BF16)
    wb = w_branch_b.astype(BF16)
    wo = w_out.astype(BF16)
    wup = w_up.astype(BF16)
    wdn = w_down.astype(BF16)
    cb = conv_b.reshape(depth, 1, 2 * D_FF)
    n_state = state_mlstm_n.reshape(depth, n_dec, N_HEADS * DK)
    m_pad = jnp.pad(state_mlstm_m, ((0, 0), (0, 0), (0, V7X_LANES - N_HEADS)))

    small = [[] for _ in range(4)]
    cp_stack = None
    convp_stack = None
    c_stack = None
    vn_stack = None
    conv_stack = None
    for l in range(depth):
        final = l == depth - 1

        xp, cp_stack, n_p, m_p = _prompt_mixer(xp, l, wm, wgt, gbias_col, g1, gv, w_spatial, bst, wa, wb, wo,
                                               cp_stack)
        xp, convp_stack = _prompt_ffn(xp, l, g2, wup, conv_w, cb, wdn, gf, final, convp_stack)

        z, gates = _sample_proj(xs, l, g1, wm, wgt)
        h, c_stack, n_s, m_s = _sample_state(
            z, gates, gbias_row[l], m_pad[l], n_state[l], state_mlstm_C, l, c_stack)
        xs, vn_stack = _sample_mixer(xs, z, h, l, gv, ws0, bs0, wa, wb, wo, vn_stack)
        xs, conv_stack = _sample_ffn(xs, gf, l, state_ffn_conv, g2, wup, conv_w, cb, wdn, final, conv_stack)

        for lst, val in zip(small, (n_p, m_p[:, :N_HEADS, 0],
                                    n_s.reshape(n_dec, N_HEADS, DK), m_s[:, :N_HEADS])):
            lst.append(val)
    st = [jnp.stack(o) for o in small]
    return (xp, xs, cp_stack, st[0], st[1], convp_stack, c_stack, st[2], st[3], conv_stack, vn_stack)
```

```python
import functools

import jax
import jax.numpy as jnp
from jax import lax
from jax.experimental import pallas as pl
from jax.experimental.pallas import tpu as pltpu

D_MODEL = 1024
N_HEADS = 4
DK = 128
DV = 256
CHUNK = 128
D_B = 1024
N_GROUPS = 4
DG = D_B // N_GROUPS
D_FF = 2816
CONV_W = 3
EPS = 1e-6
K_SCALE = DK ** -0.5

OFF_Q = 0
OFF_K = OFF_Q + N_HEADS * DK
OFF_V = OFF_K + N_HEADS * DK
OFF_O = OFF_V + N_HEADS * DV
OFF_U = OFF_O + N_HEADS * DV
OFF_VB = OFF_U + D_B
OFF_GA = OFF_VB + D_B
OFF_GB = OFF_GA + D_MODEL
P_MAIN = OFF_GB + D_MODEL
GATE_LO = 2 * N_HEADS * DK + 2 * N_HEADS * DV
GATE_HI = GATE_LO + 2 * N_HEADS

V7X_LANES = 128
V7X_SUBLANES = 8
GATE_ROWS = V7X_SUBLANES
V7X_VMEM_LIMIT = 56 * 1024 * 1024
FFN_COL_TILE = 256
PROJ_TILE = 256
MIXER_BLOCK = 512
FFN_BLOCK = 512
SAMPLE_BLOCK = 16
PACK_TILE = 256
PACK_DIMS = 256
SAMPLE_PROJ_TILE = P_MAIN // 4
SAMPLE_FFN_STEPS = 2

F32 = jnp.float32
BF16 = jnp.bfloat16


def _dot(a, b):
    return jnp.dot(a, b, preferred_element_type=F32)


def _dot_nt(a, b):
    return lax.dot_general(a, b, (((1,), (1,)), ((), ())), preferred_element_type=F32)


def _rmsnorm(x, g):
    r = lax.rsqrt(jnp.mean(x * x, axis=-1, keepdims=True) + EPS)
    return x * r * g


def _log_sigmoid(x):
    return jnp.minimum(x, 0.0) - jnp.log1p(jnp.exp(-jnp.abs(x)))


def _scan_lanes(x, op, fill):
    lane = lax.broadcasted_iota(jnp.int32, x.shape, 1)
    k = 1
    while k < x.shape[1]:
        shifted = pltpu.roll(x, k, 1)
        x = op(x, jnp.where(lane >= k, shifted, fill))
        k *= 2
    return x


def _anchor_zero(x):
    sub, lanes = V7X_SUBLANES, V7X_LANES
    acc = jnp.zeros((sub, lanes), jnp.uint32)
    for r in range(x.shape[0] // sub):
        for c in range(x.shape[1] // lanes):
            piece = pltpu.bitcast(x[r * sub:(r + 1) * sub, c * lanes:(c + 1) * lanes], jnp.uint32)
            acc = acc | ((piece >> 16) >> 16)
    return pltpu.bitcast(acc, F32)


def _next_block_map(nb, steps):
    def index_map(b, s):
        nxt = jnp.minimum(b * steps + s + 1, nb * steps - 1)
        return (nxt // steps, nxt % steps, 0)
    return index_map


def _mixer_kernel(x_ref, wm_ref, wgt_ref, gbias_ref, g1_ref, gv_ref, ws_ref, bst_ref,
                  wa_ref, wb_ref, wo_ref, *rest, block):
    (xo_ref, c_ref, n_ref, m_ref,
     ct_s, n_s, m_s, q_s, k_s, v_s, h_s, so_s, u_s, vb_s, sg_s, um_s) = rest[-16:]
    s = pl.program_id(1)
    n_chunks = block // CHUNK

    @pl.when(s == 0)
    def _():
        ct_s[...] = jnp.zeros_like(ct_s)
        n_s[...] = jnp.zeros_like(n_s)
        m_s[...] = jnp.zeros_like(m_s)

    xn = _rmsnorm(x_ref[...], g1_ref[...]).astype(BF16)

    gates = _dot_nt(wgt_ref[...], xn) + gbias_ref[...]

    def proj(off, t):
        return _dot(xn, wm_ref[:, off + t * PROJ_TILE:off + (t + 1) * PROJ_TILE])

    def tile(t):
        return slice(t * PROJ_TILE, (t + 1) * PROJ_TILE)

    for t in range(N_HEADS * DK // PROJ_TILE):
        q_s[:, tile(t)] = proj(OFF_Q, t)
        k_s[:, tile(t)] = proj(OFF_K, t) * K_SCALE
    for t in range(N_HEADS * DV // PROJ_TILE):
        v_s[:, tile(t)] = proj(OFF_V, t)

    sumsq = [jnp.zeros((block, 1), F32)]

    def vb_tile(t):
        g = jax.nn.gelu(proj(OFF_VB, t))
        vb_s[:, tile(t)] = g
        sumsq[0] = sumsq[0] + jnp.sum(g * g, axis=-1, keepdims=True)

    def u_tile(t):
        u_s[:, tile(t)] = jax.nn.gelu(proj(OFF_U, t))

    def o_tile(t):
        so_s[:, tile(t)] = jax.nn.sigmoid(proj(OFF_O, t))

    def ga_tile(t):
        sg_s[:, tile(t)] = jax.nn.sigmoid(proj(OFF_GA, t))

    def gb_tile(t):
        sg_s[:, D_MODEL + t * PROJ_TILE:D_MODEL + (t + 1) * PROJ_TILE] = jax.nn.sigmoid(proj(OFF_GB, t))

    jobs = [(f, t) for f in (vb_tile, u_tile, o_tile, ga_tile, gb_tile) for t in range(D_MODEL // PROJ_TILE)]

    def run_jobs(count):
        for _ in range(min(count, len(jobs))):
            f, t = jobs.pop(0)
            f(t)

    row_i = lax.broadcasted_iota(jnp.int32, (CHUNK, CHUNK), 0)
    col_i = lax.broadcasted_iota(jnp.int32, (CHUNK, CHUNK), 1)
    causal = row_i >= col_i
    heads = range(N_HEADS)

    for c in range(n_chunks):
        r0 = c * CHUNK
        ig = gates[0:GATE_ROWS, r0:r0 + CHUNK]
        lf = _log_sigmoid(gates[GATE_ROWS:2 * GATE_ROWS, r0:r0 + CHUNK])
        b = _scan_lanes(lf, jnp.add, 0.0)
        a = ig - b
        m_prev = m_s[...]
        gmax = jnp.maximum(m_prev, _scan_lanes(a, jnp.maximum, -jnp.inf))
        m_t = b + gmax
        w_inter = jnp.exp(m_prev - gmax)
        g_last = gmax[:, CHUNK - 1:CHUNK]
        w_last = jnp.exp(a - g_last)
        floor = jnp.exp(-m_t)
        decay = w_inter[:, CHUNK - 1:CHUNK]
        m_s[...] = jnp.broadcast_to(m_t[:, CHUNK - 1:CHUNK], m_s.shape)

        rows = jnp.concatenate(
            [gmax, w_inter, w_last, floor,
             jnp.zeros((CHUNK - 4 * GATE_ROWS, CHUNK), F32)], axis=0)
        cols = rows.T

        def col(kind, h):
            return cols[:, kind * GATE_ROWS + h:kind * GATE_ROWS + h + 1]

        qf = [q_s[r0:r0 + CHUNK, h * DK:(h + 1) * DK] for h in heads]
        kf = [k_s[r0:r0 + CHUNK, h * DK:(h + 1) * DK] for h in heads]
        vf = [v_s[r0:r0 + CHUNK, h * DV:(h + 1) * DV] for h in heads]
        qb = [x.astype(BF16) for x in qf]
        kt = [x.T.astype(BF16) for x in kf]
        run_jobs(2)
        zero = jnp.zeros((DK, CHUNK), BF16)
        sc = []
        for h in range(0, N_HEADS, 2):
            kk = jnp.concatenate([jnp.concatenate([kt[h], zero], axis=1),
                                  jnp.concatenate([zero, kt[h + 1]], axis=1)], axis=0)
            pair = _dot(jnp.concatenate([qb[h], qb[h + 1]], axis=1), kk)
            sc += [pair[:, :CHUNK], pair[:, CHUNK:]]
        dmat = [jnp.where(causal, jnp.exp(a[h:h + 1, :] - col(0, h)), 0.0) for h in heads]
        run_jobs(1)
        sd = [sc[h] * dmat[h] for h in heads]
        ct = [ct_s[h] for h in heads]
        num = [_dot(jnp.concatenate([sd[h], col(1, h) * qf[h]], axis=1).astype(BF16),
                    jnp.concatenate([vf[h], ct[h]], axis=0).astype(BF16)) for h in heads]
        run_jobs(2)
        for h in heads:
            nh = n_s[h:h + 1, :]
            den = (jnp.sum(sd[h], axis=-1, keepdims=True)
                   + col(1, h) * jnp.sum(qf[h] * nh, axis=-1, keepdims=True))
            h_s[r0:r0 + CHUNK, h * DV:(h + 1) * DV] = num[h] * (1.0 / jnp.maximum(jnp.abs(den), col(3, h)))
            dec = decay[h:h + 1, :]
            n_s[h:h + 1, :] = dec * nh + jnp.sum(col(2, h) * kf[h], axis=0, keepdims=True)
        run_jobs(1)
        for h in heads:
            ct_s[h] = decay[h:h + 1, :] * ct[h] + _dot(kt[h], (col(2, h) * vf[h]).astype(BF16))
    run_jobs(len(jobs))

    y_a = _dot((so_s[...] * h_s[...]).astype(BF16), wa_ref[...])

    rinv = lax.rsqrt(sumsq[0] * (1.0 / D_B) + EPS)
    for g in range(N_GROUPS):
        gcols = slice(g * DG, (g + 1) * DG)
        w_tri = jnp.where(causal, ws_ref[g], 0.0).astype(BF16)
        bias_c = bst_ref[:, g:g + 1]
        for c in range(n_chunks):
            rws = slice(c * CHUNK, (c + 1) * CHUNK)
            vn = vb_s[rws, gcols] * rinv[rws] * gv_ref[:, gcols]
            um_s[rws, gcols] = u_s[rws, gcols] * (_dot(w_tri, vn.astype(BF16)) + bias_c)
    y_b = _dot(um_s[...].astype(BF16), wb_ref[...])

    merged = sg_s[:, 0:D_MODEL] * y_a + sg_s[:, D_MODEL:2 * D_MODEL] * y_b
    xo_ref[...] = x_ref[...] + _dot(merged.astype(BF16), wo_ref[...])

    @pl.when(s == pl.num_programs(1) - 1)
    def _():
        for h in range(N_HEADS):
            c_ref[h] = ct_s[h].T
        n_ref[...] = n_s[...]
        m_ref[...] = m_s[...]


def _resident(arr, layer=None):
    if layer is None:
        nd = arr.ndim
        return pl.BlockSpec(arr.shape, lambda *_: (0,) * nd, pipeline_mode=pl.Buffered(1))
    nd = arr.ndim - 1
    return pl.BlockSpec((None,) + arr.shape[1:], lambda *_: (layer,) + (0,) * nd,
                        pipeline_mode=pl.Buffered(1))


def _prompt_mixer(x, layer, wm, wgt, gbias, g1, gv, ws, bst, wa, wb, wo, c_stack):
    nb, seq, _ = x.shape
    depth = wm.shape[0]
    block = MIXER_BLOCK
    grid = (nb, seq // block)
    xspec = pl.BlockSpec((None, block, D_MODEL), lambda b, s: (b, s, 0))
    out_shape = (
        jax.ShapeDtypeStruct((nb, seq, D_MODEL), F32),
        jax.ShapeDtypeStruct((depth, nb, N_HEADS, DV, DK), F32),
        jax.ShapeDtypeStruct((nb, N_HEADS, DK), F32),
        jax.ShapeDtypeStruct((nb, GATE_ROWS, V7X_LANES), F32),
    )
    out_specs = (
        xspec,
        pl.BlockSpec((None, None, N_HEADS, DV, DK), lambda b, s: (layer, b, 0, 0, 0)),
        pl.BlockSpec((None, N_HEADS, DK), lambda b, s: (b, 0, 0)),
        pl.BlockSpec((None, GATE_ROWS, V7X_LANES), lambda b, s: (b, 0, 0)),
    )
    in_specs = [xspec] + [_resident(a, layer) for a in (wm, wgt, gbias, g1, gv, ws, bst, wa, wb, wo)]
    args = [x, wm, wgt, gbias, g1, gv, ws, bst, wa, wb, wo]
    aliases = {}
    if c_stack is not None:
        in_specs.append(pl.BlockSpec(memory_space=pl.ANY))
        aliases = {len(args): 1}
        args.append(c_stack)
    scratch = [
        pltpu.VMEM((N_HEADS, DK, DV), F32),
        pltpu.VMEM((N_HEADS, DK), F32),
        pltpu.VMEM((GATE_ROWS, V7X_LANES), F32),
        pltpu.VMEM((block, N_HEADS * DK), F32),
        pltpu.VMEM((block, N_HEADS * DK), F32),
        pltpu.VMEM((block, N_HEADS * DV), F32),
        pltpu.VMEM((block, N_HEADS * DV), F32),
        pltpu.VMEM((block, N_HEADS * DV), F32),
        pltpu.VMEM((block, D_B), F32),
        pltpu.VMEM((block, D_B), F32),
        pltpu.VMEM((block, 2 * D_MODEL), F32),
        pltpu.VMEM((block, D_B), F32),
    ]
    return pl.pallas_call(
        functools.partial(_mixer_kernel, block=block),
        grid=grid, in_specs=in_specs, out_specs=out_specs, out_shape=out_shape,
        scratch_shapes=scratch, input_output_aliases=aliases,
        compiler_params=pltpu.CompilerParams(
            dimension_semantics=("arbitrary", "arbitrary"), vmem_limit_bytes=V7X_VMEM_LIMIT),
        name="prompt_mixer",
    )(*args)


def _conv_taps(up, carry_s, cw_ref, cb_ref, cols):
    sub = V7X_SUBLANES
    rows = up.shape[0]
    last1 = up[rows - sub:rows]
    last2 = up[rows - 2 * sub:rows - sub]
    first = lax.broadcasted_iota(jnp.int32, last1.shape, 0) == 0
    back1 = jnp.where(first, carry_s[sub - 1:sub, cols], pltpu.roll(last1, 1, 0))
    back2 = jnp.where(first, carry_s[sub - 2:sub - 1, cols], pltpu.roll(last2, 1, 0))
    carry_s[sub - 2:sub - 1, cols] = last2[sub - 1:sub]
    carry_s[sub - 1:sub, cols] = last1[sub - 1:sub]
    m1 = jnp.concatenate([back1, up[0:rows - sub]], axis=0)
    m2 = jnp.concatenate([back2, back1, up[0:rows - 2 * sub]], axis=0)
    return (cb_ref[:, cols] + cw_ref[0:1, cols] * m2 + cw_ref[1:2, cols] * m1
            + cw_ref[2:3, cols] * up)


def _perm_pitch(block):
    return block // V7X_SUBLANES + V7X_SUBLANES


def _ffn_kernel(x_ref, xnext_ref, g2_ref, wup_ref, cw_ref, cb_ref, wdn_ref, gf_ref, *rest, block, final):
    xo_ref, conv_ref, carry_s, act_s, perm_s, unperm_s, xn_s = rest[-7:]
    s = pl.program_id(1)
    sub, lanes = V7X_SUBLANES, V7X_LANES
    groups = block // sub
    chunks = D_MODEL // lanes
    pitch = _perm_pitch(block)

    @pl.when(s == 0)
    def _():
        carry_s[...] = jnp.zeros_like(carry_s)

    def stage(src_ref):
        for c in range(chunks):
            for i in range(sub):
                perm_s[c, i * pitch:i * pitch + groups] = src_ref[i * groups:(i + 1) * groups,
                                                                  c * lanes:(c + 1) * lanes]
        x = jnp.concatenate(
            [jnp.concatenate([perm_s[c, pl.ds(r, sub, stride=pitch), :] for c in range(chunks)], axis=1)
             for r in range(groups)], axis=0)
        xn = _rmsnorm(x, g2_ref[...])
        xn_s[...] = xn.astype(BF16)
        return xn

    @pl.when((pl.program_id(0) == 0) & (s == 0))
    def _():
        stage(x_ref)

    xn = xn_s[...]
    for j in range(D_FF // FFN_COL_TILE):
        halves = []
        for half in range(2):
            cols = slice(half * D_FF + j * FFN_COL_TILE, half * D_FF + (j + 1) * FFN_COL_TILE)
            up = _dot(xn, wup_ref[:, cols])
            halves.append(_conv_taps(up, carry_s, cw_ref, cb_ref, cols))
        act_s[:, j * FFN_COL_TILE:(j + 1) * FFN_COL_TILE] = (jax.nn.silu(halves[0]) * halves[1]).astype(BF16)
    staged = stage(xnext_ref)
    down = _dot(act_s[...], wdn_ref[...])
    anchor = _anchor_zero(staged)
    for r in range(groups):
        for c in range(chunks):
            tile_rc = down[r * sub:(r + 1) * sub, c * lanes:(c + 1) * lanes]
            if r == 0 and c == 0:
                tile_rc = tile_rc + anchor
            unperm_s[c, pl.ds(r, sub, stride=pitch), :] = tile_rc
    for i in range(sub):
        rows = slice(i * groups, (i + 1) * groups)
        y = x_ref[rows, :] + jnp.concatenate(
            [unperm_s[c, i * pitch:i * pitch + groups] for c in range(chunks)], axis=1)
        if final:
            y = _rmsnorm(y, gf_ref[...])
        xo_ref[rows, :] = y

    @pl.when(s == pl.num_programs(1) - 1)
    def _():
        conv_ref[...] = carry_s[V7X_SUBLANES - (CONV_W - 1):V7X_SUBLANES, :]


def _prompt_ffn(x, layer, g2, wup, cw, cb, wdn, gf, final, conv_stack):
    nb, seq, _ = x.shape
    depth = wup.shape[0]
    block = FFN_BLOCK
    steps = seq // block
    grid = (nb, steps)
    xspec = pl.BlockSpec((None, block, D_MODEL), lambda b, s: (b, s, 0))
    out_shape = (
        jax.ShapeDtypeStruct((nb, seq, D_MODEL), F32),
        jax.ShapeDtypeStruct((depth, nb, CONV_W - 1, 2 * D_FF), F32),
    )
    out_specs = (xspec, pl.BlockSpec((None, None, CONV_W - 1, 2 * D_FF), lambda b, s: (layer, b, 0, 0)))
    in_specs = ([xspec, pl.BlockSpec((None, block, D_MODEL), _next_block_map(nb, steps))]
                + [_resident(a, layer) for a in (g2, wup, cw, cb, wdn)] + [_resident(gf)])
    args = [x, x, g2, wup, cw, cb, wdn, gf]
    aliases = {}
    if conv_stack is not None:
        in_specs.append(pl.BlockSpec(memory_space=pl.ANY))
        aliases = {len(args): 1}
        args.append(conv_stack)
    reorder = pltpu.VMEM((D_MODEL // V7X_LANES, V7X_SUBLANES * _perm_pitch(block), V7X_LANES), F32)
    return pl.pallas_call(
        functools.partial(_ffn_kernel, block=block, final=final),
        grid=grid, in_specs=in_specs, out_specs=out_specs, out_shape=out_shape,
        scratch_shapes=[pltpu.VMEM((V7X_SUBLANES, 2 * D_FF), F32),
                        pltpu.VMEM((block, D_FF), BF16),
                        reorder, reorder,
                        pltpu.VMEM((block, D_MODEL), BF16)],
        input_output_aliases=aliases,
        compiler_params=pltpu.CompilerParams(
            dimension_semantics=("arbitrary", "arbitrary"), vmem_limit_bytes=V7X_VMEM_LIMIT),
        name="prompt_ffn",
    )(*args)


def _sample_proj_kernel(x_ref, g1_ref, wgt_ref, wm_ref, z_ref, gates_ref):
    xn = _rmsnorm(x_ref[...], g1_ref[...]).astype(BF16)
    z_ref[...] = _dot(xn, wm_ref[...])

    @pl.when(pl.program_id(0) == 0)
    def _():
        gt = _dot_nt(wgt_ref[...], xn)
        gt = jnp.concatenate([gt, jnp.zeros((V7X_LANES - gt.shape[0], gt.shape[1]), F32)], axis=0)
        g = gt.T
        gates_ref[...] = jnp.concatenate([g, pltpu.roll(g, V7X_LANES - GATE_ROWS, 1)], axis=1)


def _single_step(kernel_fn, name, out_shape, whole, layered, layer, stacked=None):
    in_specs = [_resident(a) for a in whole] + [_resident(a, layer) for a in layered]
    args = list(whole) + list(layered)
    out_specs = [pl.BlockSpec(o.shape, lambda i, nd=len(o.shape): (0,) * nd) for o in out_shape]
    aliases = {}
    if stacked is not None:
        k, prev = stacked
        nd = len(out_shape[k].shape) - 1
        out_specs[k] = pl.BlockSpec((None,) + out_shape[k].shape[1:], lambda i: (layer,) + (0,) * nd)
        if prev is not None:
            in_specs.append(pl.BlockSpec(memory_space=pl.ANY))
            aliases = {len(args): k}
            args.append(prev)
    return pl.pallas_call(
        kernel_fn,
        grid=(1,),
        in_specs=in_specs, out_specs=tuple(out_specs), out_shape=out_shape,
        input_output_aliases=aliases,
        compiler_params=pltpu.CompilerParams(
            dimension_semantics=("arbitrary",), vmem_limit_bytes=V7X_VMEM_LIMIT),
        name=name,
    )(*args)


def _sample_proj(x, layer, g1, wm, wgt):
    n = x.shape[0]
    assert n == V7X_LANES, "the gate transpose assumes one lane tile of sample rows"
    tile = SAMPLE_PROJ_TILE
    return pl.pallas_call(
        _sample_proj_kernel,
        grid=(P_MAIN // tile,),
        in_specs=[_resident(x), _resident(g1, layer), _resident(wgt, layer),
                  pl.BlockSpec((None, D_MODEL, tile), lambda j: (layer, 0, j))],
        out_specs=(pl.BlockSpec((n, tile), lambda j: (0, j)),
                   pl.BlockSpec((n, 2 * V7X_LANES), lambda j: (0, 0))),
        out_shape=(jax.ShapeDtypeStruct((n, P_MAIN), F32), jax.ShapeDtypeStruct((n, 2 * V7X_LANES), F32)),
        compiler_params=pltpu.CompilerParams(
            dimension_semantics=("arbitrary",), vmem_limit_bytes=V7X_VMEM_LIMIT),
        name="sample_proj",
    )(x, g1, wgt, wm)


def _sample_state_kernel(q_ref, k_ref, v_ref, gates_ref, gbias_ref, m_ref, n_ref, c_ref, *rest):
    h_ref, co_ref, no_ref, mo_ref = rest[-4:]
    tb = SAMPLE_BLOCK
    ig = gates_ref[:, 0:V7X_LANES] + gbias_ref[:, 0:V7X_LANES]
    lf = _log_sigmoid(gates_ref[:, V7X_LANES:] + gbias_ref[:, V7X_LANES:])
    inter = lf + m_ref[...]
    m_t = jnp.maximum(inter, ig)
    d_in = jnp.exp(ig - m_t)
    w_inter = jnp.exp(inter - m_t)
    floor = jnp.exp(-m_t)
    mo_ref[...] = m_t

    row8 = lax.broadcasted_iota(jnp.int32, (CHUNK, DK), 0)
    for h in range(N_HEADS):
        q8 = q_ref[:, h * DK:(h + 1) * DK]
        k8 = k_ref[:, h * DK:(h + 1) * DK] * K_SCALE
        v8 = v_ref[:, h * DV:(h + 1) * DV]
        d_h = d_in[:, h:h + 1]
        w_h = w_inter[:, h:h + 1]
        n8 = n_ref[:, h * DK:(h + 1) * DK]
        s = jnp.sum(q8 * k8, axis=-1, keepdims=True) * d_h
        den = s + w_h * jnp.sum(q8 * n8, axis=-1, keepdims=True)
        qb = q8.astype(BF16)
        inter_rows = [_dot_nt(qb, c_ref[j, h].astype(BF16))[j:j + 1] for j in range(tb)]
        num = s * v8 + w_h * jnp.concatenate(inter_rows, axis=0)
        h_ref[:, h * DV:(h + 1) * DV] = num / jnp.maximum(jnp.abs(den), floor[:, h:h + 1])
        no_ref[:, h * DK:(h + 1) * DK] = w_h * n8 + d_h * k8

        vt = jnp.concatenate([d_h * v8, jnp.zeros((CHUNK - tb, DV), F32)], axis=0).T.astype(BF16)
        kpad = jnp.concatenate([k8, jnp.zeros((CHUNK - tb, DK), F32)], axis=0)
        for j in range(tb):
            kj = jnp.where(row8 == j, kpad, 0.0).astype(BF16)
            co_ref[j, h] = w_inter[j:j + 1, h:h + 1] * c_ref[j, h] + _dot(vt, kj)


def _sample_state(z, gates, gbias, m_pad, n_state, c_state, layer, c_stack):
    n = z.shape[0]
    tb = SAMPLE_BLOCK
    c_block = pl.BlockSpec((None, tb, N_HEADS, DV, DK), lambda i: (layer, i, 0, 0, 0))
    in_specs = [
        pl.BlockSpec((tb, N_HEADS * DK), lambda i: (i, OFF_Q // (N_HEADS * DK))),
        pl.BlockSpec((tb, N_HEADS * DK), lambda i: (i, OFF_K // (N_HEADS * DK))),
        pl.BlockSpec((tb, N_HEADS * DV), lambda i: (i, OFF_V // (N_HEADS * DV))),
        pl.BlockSpec((tb, 2 * V7X_LANES), lambda i: (i, 0)),
        pl.BlockSpec((1, 2 * V7X_LANES), lambda i: (0, 0)),
        pl.BlockSpec((tb, V7X_LANES), lambda i: (i, 0)),
        pl.BlockSpec((tb, N_HEADS * DK), lambda i: (i, 0)),
        c_block,
    ]
    args = [z, z, z, gates, gbias, m_pad, n_state, c_state]
    aliases = {}
    if c_stack is not None:
        in_specs.append(pl.BlockSpec(memory_space=pl.ANY))
        aliases = {len(args): 1}
        args.append(c_stack)
    out_shape = (
        jax.ShapeDtypeStruct((n, N_HEADS * DV), F32),
        jax.ShapeDtypeStruct(c_state.shape, F32),
        jax.ShapeDtypeStruct((n, N_HEADS * DK), F32),
        jax.ShapeDtypeStruct((n, V7X_LANES), F32),
    )
    out_specs = (
        pl.BlockSpec((tb, N_HEADS * DV), lambda i: (i, 0)),
        c_block,
        pl.BlockSpec((tb, N_HEADS * DK), lambda i: (i, 0)),
        pl.BlockSpec((tb, V7X_LANES), lambda i: (i, 0)),
    )
    return pl.pallas_call(
        _sample_state_kernel,
        grid=(n // tb,), in_specs=in_specs, out_specs=out_specs, out_shape=out_shape,
        input_output_aliases=aliases,
        compiler_params=pltpu.CompilerParams(
            dimension_semantics=("arbitrary",), vmem_limit_bytes=V7X_VMEM_LIMIT),
        name="sample_state",
    )(*args)


def _sample_mixer_kernel(x_ref, z_ref, h_ref, gv_ref, ws0_ref, bs0_ref, wa_ref, wb_ref, wo_ref, *rest):
    xo_ref, vn_ref = rest[-2:]
    y_a = _dot((jax.nn.sigmoid(z_ref[:, OFF_O:OFF_U]) * h_ref[...]).astype(BF16), wa_ref[...])
    u = jax.nn.gelu(z_ref[:, OFF_U:OFF_VB])
    vn = _rmsnorm(jax.nn.gelu(z_ref[:, OFF_VB:OFF_GA]), gv_ref[...])
    vn_ref[:, 0, :] = vn
    mixed = ws0_ref[...] * vn + bs0_ref[...]
    y_b = _dot((u * mixed).astype(BF16), wb_ref[...])
    merged = (jax.nn.sigmoid(z_ref[:, OFF_GA:OFF_GB]) * y_a
              + jax.nn.sigmoid(z_ref[:, OFF_GB:P_MAIN]) * y_b)
    xo_ref[...] = x_ref[...] + _dot(merged.astype(BF16), wo_ref[...])


def _sample_mixer(x, z, h, layer, gv, ws0, bs0, wa, wb, wo, vn_stack):
    n = x.shape[0]
    out_shape = (jax.ShapeDtypeStruct((n, D_MODEL), F32),
                 jax.ShapeDtypeStruct((wa.shape[0], n, 1, D_B), F32))
    return _single_step(_sample_mixer_kernel, "sample_mixer", out_shape,
                        (x, z, h), (gv, ws0, bs0, wa, wb, wo), layer, stacked=(1, vn_stack))


def _sample_ffn_kernel(x_ref, gf_ref, g2_ref, bufa_ref, bufg_ref, wupa_ref, wupg_ref, cwa_ref, cwg_ref,
                       cba_ref, cbg_ref, wdn_ref, *rest, final):
    xo_ref, nbuf_ref, acc_s = rest[-3:]
    j = pl.program_id(0)
    last = SAMPLE_FFN_STEPS - 1
    width = D_FF // SAMPLE_FFN_STEPS
    x = x_ref[...]
    xn = _rmsnorm(x, g2_ref[...]).astype(BF16)

    def conv(up, buf_ref, cw_ref, cb_ref):
        b0 = buf_ref[:, 0, :]
        b1 = buf_ref[:, 1, :]
        return b1, cb_ref[...] + cw_ref[0:1, :] * b0 + cw_ref[1:2, :] * b1 + cw_ref[2:3, :] * up

    up_a = _dot(xn, wupa_ref[...])
    up_g = _dot(xn, wupg_ref[...])
    prev_a, conv_a = conv(up_a, bufa_ref, cwa_ref, cba_ref)
    prev_g, conv_g = conv(up_g, bufg_ref, cwg_ref, cbg_ref)
    part = _dot((jax.nn.silu(conv_a) * conv_g).astype(BF16), wdn_ref[...])

    for jj in range(SAMPLE_FFN_STEPS):
        @pl.when(j == jj)
        def _():
            for col0, prev, up in ((jj * width, prev_a, up_a), (D_FF + jj * width, prev_g, up_g)):
                nbuf_ref[:, 0, col0:col0 + width] = prev
                nbuf_ref[:, 1, col0:col0 + width] = up

    @pl.when(j == 0)
    def _():
        acc_s[...] = part

    @pl.when((j > 0) & (j < last))
    def _():
        acc_s[...] += part

    @pl.when(j == last)
    def _():
        y = x + (acc_s[...] + part)
        if final:
            xo_ref[:, 0, :] = _rmsnorm(y, gf_ref[...])
        else:
            xo_ref[...] = y


def _sample_ffn(x, gf, layer, buf, g2, wup, cw, cb, wdn, final, conv_stack):
    n = x.shape[0]
    steps = SAMPLE_FFN_STEPS
    width = D_FF // steps

    def cols(shape, gate):
        nd = len(shape)
        return pl.BlockSpec((None,) + shape[:-1] + (width,),
                            lambda j: (layer,) + (0,) * (nd - 1) + (gate * steps + j,))

    in_specs = [_resident(x), _resident(gf), _resident(g2, layer),
                cols(buf.shape[1:], 0), cols(buf.shape[1:], 1),
                cols(wup.shape[1:], 0), cols(wup.shape[1:], 1),
                cols(cw.shape[1:], 0), cols(cw.shape[1:], 1),
                cols(cb.shape[1:], 0), cols(cb.shape[1:], 1),
                pl.BlockSpec((None, width, D_MODEL), lambda j: (layer, j, 0))]
    args = [x, gf, g2, buf, buf, wup, wup, cw, cw, cb, cb, wdn]
    aliases = {}
    if conv_stack is not None:
        in_specs.append(pl.BlockSpec(memory_space=pl.ANY))
        aliases = {len(args): 1}
        args.append(conv_stack)
    x_out = (n, 1, D_MODEL) if final else (n, D_MODEL)
    nd_out = len(x_out)
    return pl.pallas_call(
        functools.partial(_sample_ffn_kernel, final=final),
        grid=(steps,),
        in_specs=in_specs,
        out_specs=(pl.BlockSpec(x_out, lambda j: (0,) * nd_out),
                   pl.BlockSpec((None,) + buf.shape[1:], lambda j: (layer, 0, 0, 0))),
        out_shape=(jax.ShapeDtypeStruct(x_out, F32), jax.ShapeDtypeStruct(buf.shape, F32)),
        scratch_shapes=[pltpu.VMEM((n, D_MODEL), F32)],
        input_output_aliases=aliases,
        compiler_params=pltpu.CompilerParams(
            dimension_semantics=("arbitrary",), vmem_limit_bytes=V7X_VMEM_LIMIT),
        name="sample_ffn",
    )(*args)


def _pack_kernel(wt_ref, o_ref, wgt_ref):
    g8 = wt_ref[GATE_LO:GATE_HI, :]
    head = lax.broadcasted_iota(jnp.int32, g8.shape, 0) < N_HEADS
    wgt_ref[...] = jnp.concatenate(
        [jnp.where(head, g8, 0.0), jnp.where(head, pltpu.roll(g8, N_HEADS, 0), 0.0)], axis=0).astype(BF16)
    for j in range(P_MAIN // PACK_TILE):
        dst = j * PACK_TILE
        src = dst if dst < GATE_LO else dst + (GATE_HI - GATE_LO)
        o_ref[:, dst:dst + PACK_TILE] = wt_ref[src:src + PACK_TILE, :].T.astype(BF16)


def _pack_w_in(w_in_t):
    depth, p_in, d = w_in_t.shape
    return pl.pallas_call(
        _pack_kernel,
        grid=(depth, d // PACK_DIMS),
        in_specs=[pl.BlockSpec((None, p_in, PACK_DIMS), lambda l, r: (l, 0, r))],
        out_specs=(pl.BlockSpec((None, PACK_DIMS, P_MAIN), lambda l, r: (l, r, 0)),
                   pl.BlockSpec((None, 2 * GATE_ROWS, PACK_DIMS), lambda l, r: (l, 0, r))),
        out_shape=(jax.ShapeDtypeStruct((depth, d, P_MAIN), BF16),
                   jax.ShapeDtypeStruct((depth, 2 * GATE_ROWS, d), BF16)),
        compiler_params=pltpu.CompilerParams(
            dimension_semantics=("arbitrary", "arbitrary"), vmem_limit_bytes=V7X_VMEM_LIMIT),
        name="pack_w_in",
    )(w_in_t)


def kernel(x_prompt, x_sample, state_mlstm_C, state_mlstm_n, state_mlstm_m, state_ffn_conv, w_in, b_igate, b_fgate, g_norm1, g_vnorm, w_spatial, b_spatial, w_branch_a, w_branch_b, w_out, g_norm2, w_up, conv_w, conv_b, w_down, g_final):
    depth = w_in.shape[0]
    n_dec = x_sample.shape[0]
    xp = x_prompt
    xs = x_sample.reshape(n_dec, D_MODEL)
    gf = g_final.reshape(1, D_MODEL)

    wm, wgt = _pack_w_in(jnp.swapaxes(w_in, 1, 2))
    head_pad = ((0, 0), (0, GATE_ROWS - N_HEADS))
    gbias_col = jnp.concatenate([jnp.pad(b_igate, head_pad), jnp.pad(b_fgate, head_pad)],
                                axis=1).reshape(depth, 2 * GATE_ROWS, 1)
    head_lanes = ((0, 0), (0, V7X_LANES - N_HEADS))
    gbias_row = jnp.concatenate([jnp.pad(b_igate, head_lanes), jnp.pad(b_fgate, head_lanes)],
                                axis=1).reshape(depth, 1, 2 * V7X_LANES)
    g1 = g_norm1.reshape(depth, 1, D_MODEL)
    gv = g_vnorm.reshape(depth, 1, D_B)
    g2 = g_norm2.reshape(depth, 1, D_MODEL)
    bst = jnp.swapaxes(b_spatial, 1, 2)
    ws0 = jnp.repeat(w_spatial[:, :, 0, 0], DG, axis=1).reshape(depth, 1, D_B)
    bs0 = jnp.repeat(b_spatial[:, :, 0], DG, axis=1).reshape(depth, 1, D_B)
    wa = w_branch_a.astype(BF16)
    wb = w_branch_b.astype(BF16)
    wo = w_out.astype(BF16)
    wup = w_up.astype(BF16)
    wdn = w_down.astype(BF16)
    cb = conv_b.reshape(depth, 1, 2 * D_FF)
    n_state = state_mlstm_n.reshape(depth, n_dec, N_HEADS * DK)
    m_pad = jnp.pad(state_mlstm_m, ((0, 0), (0, 0), (0, V7X_LANES - N_HEADS)))

    small = [[] for _ in range(4)]
    cp_stack = None
    convp_stack = None
    c_stack = None
    vn_stack = None
    conv_stack = None
    for l in range(depth):
        final = l == depth - 1

        xp, cp_stack, n_p, m_p = _prompt_mixer(xp, l, wm, wgt, gbias_col, g1, gv, w_spatial, bst, wa, wb, wo,
                                               cp_stack)
        xp, convp_stack = _prompt_ffn(xp, l, g2, wup, conv_w, cb, wdn, gf, final, convp_stack)

        z, gates = _sample_proj(xs, l, g1, wm, wgt)
        h, c_stack, n_s, m_s = _sample_state(
            z, gates, gbias_row[l], m_pad[l], n_state[l], state_mlstm_C, l, c_stack)
        xs, vn_stack = _sample_mixer(xs, z, h, l, gv, ws0, bs0, wa, wb, wo, vn_stack)
        xs, conv_stack = _sample_ffn(xs, gf, l, state_ffn_conv, g2, wup, conv_w, cb, wdn, final, conv_stack)

        for lst, val in zip(small, (n_p, m_p[:, :N_HEADS, 0],
                                    n_s.reshape(n_dec, N_HEADS, DK), m_s[:, :N_HEADS])):
            lst.append(val)
    st = [jnp.stack(o) for o in small]
    return (xp, xs, cp_stack, st[0], st[1], convp_stack, c_stack, st[2], st[3], conv_stack, vn_stack)
```

```python
import functools

import jax
import jax.numpy as jnp
from jax import lax
from jax.experimental import pallas as pl
from jax.experimental.pallas import tpu as pltpu

D_MODEL = 1024
N_HEADS = 4
DK = 128
DV = 256
CHUNK = 128
D_B = 1024
N_GROUPS = 4
DG = D_B // N_GROUPS
D_FF = 2816
CONV_W = 3
EPS = 1e-6
K_SCALE = DK ** -0.5

OFF_Q = 0
OFF_K = OFF_Q + N_HEADS * DK
OFF_V = OFF_K + N_HEADS * DK
OFF_O = OFF_V + N_HEADS * DV
OFF_U = OFF_O + N_HEADS * DV
OFF_VB = OFF_U + D_B
OFF_GA = OFF_VB + D_B
OFF_GB = OFF_GA + D_MODEL
P_MAIN = OFF_GB + D_MODEL
GATE_LO = 2 * N_HEADS * DK + 2 * N_HEADS * DV
GATE_HI = GATE_LO + 2 * N_HEADS

V7X_LANES = 128
V7X_SUBLANES = 8
GATE_ROWS = V7X_SUBLANES
V7X_VMEM_LIMIT = 56 * 1024 * 1024
FFN_COL_TILE = 256
PROJ_TILE = 256
MIXER_BLOCK = 512
FFN_BLOCK = 512
SAMPLE_BLOCK = 16
PACK_TILE = 256
PACK_DIMS = 256
SAMPLE_PROJ_TILE = P_MAIN // 2
SAMPLE_FFN_STEPS = 2

F32 = jnp.float32
BF16 = jnp.bfloat16


def _dot(a, b):
    return jnp.dot(a, b, preferred_element_type=F32)


def _dot_nt(a, b):
    return lax.dot_general(a, b, (((1,), (1,)), ((), ())), preferred_element_type=F32)


def _rmsnorm(x, g):
    r = lax.rsqrt(jnp.mean(x * x, axis=-1, keepdims=True) + EPS)
    return x * r * g


def _log_sigmoid(x):
    return jnp.minimum(x, 0.0) - jnp.log1p(jnp.exp(-jnp.abs(x)))


def _scan_lanes(x, op, fill):
    lane = lax.broadcasted_iota(jnp.int32, x.shape, 1)
    k = 1
    while k < x.shape[1]:
        shifted = pltpu.roll(x, k, 1)
        x = op(x, jnp.where(lane >= k, shifted, fill))
        k *= 2
    return x


def _anchor_zero(x):
    sub, lanes = V7X_SUBLANES, V7X_LANES
    acc = jnp.zeros((sub, lanes), jnp.uint32)
    for r in range(x.shape[0] // sub):
        for c in range(x.shape[1] // lanes):
            piece = pltpu.bitcast(x[r * sub:(r + 1) * sub, c * lanes:(c + 1) * lanes], jnp.uint32)
            acc = acc | ((piece >> 16) >> 16)
    return pltpu.bitcast(acc, F32)


def _next_block_map(nb, steps):
    def index_map(b, s):
        nxt = jnp.minimum(b * steps + s + 1, nb * steps - 1)
        return (nxt // steps, nxt % steps, 0)
    return index_map


def _mixer_kernel(x_ref, wm_ref, wgt_ref, gbias_ref, g1_ref, gv_ref, ws_ref, bst_ref,
                  wa_ref, wb_ref, wo_ref, *rest, block):
    (xo_ref, c_ref, n_ref, m_ref,
     ct_s, n_s, m_s, q_s, k_s, v_s, h_s, so_s, u_s, vb_s, sg_s, um_s) = rest[-16:]
    s = pl.program_id(1)
    n_chunks = block // CHUNK

    @pl.when(s == 0)
    def _():
        ct_s[...] = jnp.zeros_like(ct_s)
        n_s[...] = jnp.zeros_like(n_s)
        m_s[...] = jnp.zeros_like(m_s)

    xn = _rmsnorm(x_ref[...], g1_ref[...]).astype(BF16)

    gates = _dot_nt(wgt_ref[...], xn) + gbias_ref[...]

    def proj(off, t):
        return _dot(xn, wm_ref[:, off + t * PROJ_TILE:off + (t + 1) * PROJ_TILE])

    def tile(t):
        return slice(t * PROJ_TILE, (t + 1) * PROJ_TILE)

    for t in range(N_HEADS * DK // PROJ_TILE):
        q_s[:, tile(t)] = proj(OFF_Q, t)
        k_s[:, tile(t)] = proj(OFF_K, t) * K_SCALE
    for t in range(N_HEADS * DV // PROJ_TILE):
        v_s[:, tile(t)] = proj(OFF_V, t)

    sumsq = [jnp.zeros((block, 1), F32)]

    def vb_tile(t):
        g = jax.nn.gelu(proj(OFF_VB, t))
        vb_s[:, tile(t)] = g
        sumsq[0] = sumsq[0] + jnp.sum(g * g, axis=-1, keepdims=True)

    def u_tile(t):
        u_s[:, tile(t)] = jax.nn.gelu(proj(OFF_U, t))

    def o_tile(t):
        so_s[:, tile(t)] = jax.nn.sigmoid(proj(OFF_O, t))

    def ga_tile(t):
        sg_s[:, tile(t)] = jax.nn.sigmoid(proj(OFF_GA, t))

    def gb_tile(t):
        sg_s[:, D_MODEL + t * PROJ_TILE:D_MODEL + (t + 1) * PROJ_TILE] = jax.nn.sigmoid(proj(OFF_GB, t))

    jobs = [(f, t) for f in (vb_tile, u_tile, o_tile, ga_tile, gb_tile) for t in range(D_MODEL // PROJ_TILE)]

    def run_jobs(count):
        for _ in range(min(count, len(jobs))):
            f, t = jobs.pop(0)
            f(t)

    row_i = lax.broadcasted_iota(jnp.int32, (CHUNK, CHUNK), 0)
    col_i = lax.broadcasted_iota(jnp.int32, (CHUNK, CHUNK), 1)
    causal = row_i >= col_i
    heads = range(N_HEADS)

    for c in range(n_chunks):
        r0 = c * CHUNK
        ig = gates[0:GATE_ROWS, r0:r0 + CHUNK]
        lf = _log_sigmoid(gates[GATE_ROWS:2 * GATE_ROWS, r0:r0 + CHUNK])
        b = _scan_lanes(lf, jnp.add, 0.0)
        a = ig - b
        m_prev = m_s[...]
        gmax = jnp.maximum(m_prev, _scan_lanes(a, jnp.maximum, -jnp.inf))
        m_t = b + gmax
        w_inter = jnp.exp(m_prev - gmax)
        g_last = gmax[:, CHUNK - 1:CHUNK]
        w_last = jnp.exp(a - g_last)
        floor = jnp.exp(-m_t)
        decay = w_inter[:, CHUNK - 1:CHUNK]
        m_s[...] = jnp.broadcast_to(m_t[:, CHUNK - 1:CHUNK], m_s.shape)

        rows = jnp.concatenate(
            [gmax, w_inter, w_last, floor,
             jnp.zeros((CHUNK - 4 * GATE_ROWS, CHUNK), F32)], axis=0)
        cols = rows.T

        def col(kind, h):
            return cols[:, kind * GATE_ROWS + h:kind * GATE_ROWS + h + 1]

        qf = [q_s[r0:r0 + CHUNK, h * DK:(h + 1) * DK] for h in heads]
        kf = [k_s[r0:r0 + CHUNK, h * DK:(h + 1) * DK] for h in heads]
        vf = [v_s[r0:r0 + CHUNK, h * DV:(h + 1) * DV] for h in heads]
        qb = [x.astype(BF16) for x in qf]
        kt = [x.T.astype(BF16) for x in kf]
        run_jobs(2)
        zero = jnp.zeros((DK, CHUNK), BF16)
        sc = []
        for h in range(0, N_HEADS, 2):
            kk = jnp.concatenate([jnp.concatenate([kt[h], zero], axis=1),
                                  jnp.concatenate([zero, kt[h + 1]], axis=1)], axis=0)
            pair = _dot(jnp.concatenate([qb[h], qb[h + 1]], axis=1), kk)
            sc += [pair[:, :CHUNK], pair[:, CHUNK:]]
        dmat = [jnp.where(causal, jnp.exp(a[h:h + 1, :] - col(0, h)), 0.0) for h in heads]
        run_jobs(1)
        sd = [sc[h] * dmat[h] for h in heads]
        ct = [ct_s[h] for h in heads]
        num = [_dot(jnp.concatenate([sd[h], col(1, h) * qf[h]], axis=1).astype(BF16),
                    jnp.concatenate([vf[h], ct[h]], axis=0).astype(BF16)) for h in heads]
        run_jobs(2)
        for h in heads:
            nh = n_s[h:h + 1, :]
            den = (jnp.sum(sd[h], axis=-1, keepdims=True)
                   + col(1, h) * jnp.sum(qf[h] * nh, axis=-1, keepdims=True))
            h_s[r0:r0 + CHUNK, h * DV:(h + 1) * DV] = num[h] * (1.0 / jnp.maximum(jnp.abs(den), col(3, h)))
            dec = decay[h:h + 1, :]
            n_s[h:h + 1, :] = dec * nh + jnp.sum(col(2, h) * kf[h], axis=0, keepdims=True)
        run_jobs(1)
        for h in heads:
            ct_s[h] = decay[h:h + 1, :] * ct[h] + _dot(kt[h], (col(2, h) * vf[h]).astype(BF16))
    run_jobs(len(jobs))

    y_a = _dot((so_s[...] * h_s[...]).astype(BF16), wa_ref[...])

    rinv = lax.rsqrt(sumsq[0] * (1.0 / D_B) + EPS)
    for g in range(N_GROUPS):
        gcols = slice(g * DG, (g + 1) * DG)
        w_tri = jnp.where(causal, ws_ref[g], 0.0).astype(BF16)
        bias_c = bst_ref[:, g:g + 1]
        for c in range(n_chunks):
            rws = slice(c * CHUNK, (c + 1) * CHUNK)
            vn = vb_s[rws, gcols] * rinv[rws] * gv_ref[:, gcols]
            um_s[rws, gcols] = u_s[rws, gcols] * (_dot(w_tri, vn.astype(BF16)) + bias_c)
    y_b = _dot(um_s[...].astype(BF16), wb_ref[...])

    merged = sg_s[:, 0:D_MODEL] * y_a + sg_s[:, D_MODEL:2 * D_MODEL] * y_b
    xo_ref[...] = x_ref[...] + _dot(merged.astype(BF16), wo_ref[...])

    @pl.when(s == pl.num_programs(1) - 1)
    def _():
        for h in range(N_HEADS):
            c_ref[h] = ct_s[h].T
        n_ref[...] = n_s[...]
        m_ref[...] = m_s[...]


def _resident(arr, layer=None):
    if layer is None:
        nd = arr.ndim
        return pl.BlockSpec(arr.shape, lambda *_: (0,) * nd, pipeline_mode=pl.Buffered(1))
    nd = arr.ndim - 1
    return pl.BlockSpec((None,) + arr.shape[1:], lambda *_: (layer,) + (0,) * nd,
                        pipeline_mode=pl.Buffered(1))


def _prompt_mixer(x, layer, wm, wgt, gbias, g1, gv, ws, bst, wa, wb, wo, c_stack):
    nb, seq, _ = x.shape
    depth = wm.shape[0]
    block = MIXER_BLOCK
    grid = (nb, seq // block)
    xspec = pl.BlockSpec((None, block, D_MODEL), lambda b, s: (b, s, 0))
    out_shape = (
        jax.ShapeDtypeStruct((nb, seq, D_MODEL), F32),
        jax.ShapeDtypeStruct((depth, nb, N_HEADS, DV, DK), F32),
        jax.ShapeDtypeStruct((nb, N_HEADS, DK), F32),
        jax.ShapeDtypeStruct((nb, GATE_ROWS, V7X_LANES), F32),
    )
    out_specs = (
        xspec,
        pl.BlockSpec((None, None, N_HEADS, DV, DK), lambda b, s: (layer, b, 0, 0, 0)),
        pl.BlockSpec((None, N_HEADS, DK), lambda b, s: (b, 0, 0)),
        pl.BlockSpec((None, GATE_ROWS, V7X_LANES), lambda b, s: (b, 0, 0)),
    )
    in_specs = [xspec] + [_resident(a, layer) for a in (wm, wgt, gbias, g1, gv, ws, bst, wa, wb, wo)]
    args = [x, wm, wgt, gbias, g1, gv, ws, bst, wa, wb, wo]
    aliases = {}
    if c_stack is not None:
        in_specs.append(pl.BlockSpec(memory_space=pl.ANY))
        aliases = {len(args): 1}
        args.append(c_stack)
    scratch = [
        pltpu.VMEM((N_HEADS, DK, DV), F32),
        pltpu.VMEM((N_HEADS, DK), F32),
        pltpu.VMEM((GATE_ROWS, V7X_LANES), F32),
        pltpu.VMEM((block, N_HEADS * DK), F32),
        pltpu.VMEM((block, N_HEADS * DK), F32),
        pltpu.VMEM((block, N_HEADS * DV), F32),
        pltpu.VMEM((block, N_HEADS * DV), F32),
        pltpu.VMEM((block, N_HEADS * DV), F32),
        pltpu.VMEM((block, D_B), F32),
        pltpu.VMEM((block, D_B), F32),
        pltpu.VMEM((block, 2 * D_MODEL), F32),
        pltpu.VMEM((block, D_B), F32),
    ]
    return pl.pallas_call(
        functools.partial(_mixer_kernel, block=block),
        grid=grid, in_specs=in_specs, out_specs=out_specs, out_shape=out_shape,
        scratch_shapes=scratch, input_output_aliases=aliases,
        compiler_params=pltpu.CompilerParams(
            dimension_semantics=("arbitrary", "arbitrary"), vmem_limit_bytes=V7X_VMEM_LIMIT),
        name="prompt_mixer",
    )(*args)


def _conv_taps(up, carry_s, cw_ref, cb_ref, cols):
    sub = V7X_SUBLANES
    rows = up.shape[0]
    last1 = up[rows - sub:rows]
    last2 = up[rows - 2 * sub:rows - sub]
    first = lax.broadcasted_iota(jnp.int32, last1.shape, 0) == 0
    back1 = jnp.where(first, carry_s[sub - 1:sub, cols], pltpu.roll(last1, 1, 0))
    back2 = jnp.where(first, carry_s[sub - 2:sub - 1, cols], pltpu.roll(last2, 1, 0))
    carry_s[sub - 2:sub - 1, cols] = last2[sub - 1:sub]
    carry_s[sub - 1:sub, cols] = last1[sub - 1:sub]
    m1 = jnp.concatenate([back1, up[0:rows - sub]], axis=0)
    m2 = jnp.concatenate([back2, back1, up[0:rows - 2 * sub]], axis=0)
    return (cb_ref[:, cols] + cw_ref[0:1, cols] * m2 + cw_ref[1:2, cols] * m1
            + cw_ref[2:3, cols] * up)


def _perm_pitch(block):
    return block // V7X_SUBLANES + V7X_SUBLANES


def _ffn_kernel(x_ref, xnext_ref, g2_ref, wup_ref, cw_ref, cb_ref, wdn_ref, gf_ref, *rest, block, final):
    xo_ref, conv_ref, carry_s, act_s, perm_s, unperm_s, xn_s = rest[-7:]
    s = pl.program_id(1)
    sub, lanes = V7X_SUBLANES, V7X_LANES
    groups = block // sub
    chunks = D_MODEL // lanes
    pitch = _perm_pitch(block)

    @pl.when(s == 0)
    def _():
        carry_s[...] = jnp.zeros_like(carry_s)

    def stage(src_ref):
        for c in range(chunks):
            for i in range(sub):
                perm_s[c, i * pitch:i * pitch + groups] = src_ref[i * groups:(i + 1) * groups,
                                                                  c * lanes:(c + 1) * lanes]
        x = jnp.concatenate(
            [jnp.concatenate([perm_s[c, pl.ds(r, sub, stride=pitch), :] for c in range(chunks)], axis=1)
             for r in range(groups)], axis=0)
        xn = _rmsnorm(x, g2_ref[...])
        xn_s[...] = xn.astype(BF16)
        return xn

    @pl.when((pl.program_id(0) == 0) & (s == 0))
    def _():
        stage(x_ref)

    xn = xn_s[...]
    for j in range(D_FF // FFN_COL_TILE):
        halves = []
        for half in range(2):
            cols = slice(half * D_FF + j * FFN_COL_TILE, half * D_FF + (j + 1) * FFN_COL_TILE)
            up = _dot(xn, wup_ref[:, cols])
            halves.append(_conv_taps(up, carry_s, cw_ref, cb_ref, cols))
        act_s[:, j * FFN_COL_TILE:(j + 1) * FFN_COL_TILE] = (jax.nn.silu(halves[0]) * halves[1]).astype(BF16)
    staged = stage(xnext_ref)
    down = _dot(act_s[...], wdn_ref[...])
    anchor = _anchor_zero(staged)
    for r in range(groups):
        for c in range(chunks):
            tile_rc = down[r * sub:(r + 1) * sub, c * lanes:(c + 1) * lanes]
            if r == 0 and c == 0:
                tile_rc = tile_rc + anchor
            unperm_s[c, pl.ds(r, sub, stride=pitch), :] = tile_rc
    for i in range(sub):
        rows = slice(i * groups, (i + 1) * groups)
        y = x_ref[rows, :] + jnp.concatenate(
            [unperm_s[c, i * pitch:i * pitch + groups] for c in range(chunks)], axis=1)
        if final:
            y = _rmsnorm(y, gf_ref[...])
        xo_ref[rows, :] = y

    @pl.when(s == pl.num_programs(1) - 1)
    def _():
        conv_ref[...] = carry_s[V7X_SUBLANES - (CONV_W - 1):V7X_SUBLANES, :]


def _prompt_ffn(x, layer, g2, wup, cw, cb, wdn, gf, final, conv_stack):
    nb, seq, _ = x.shape
    depth = wup.shape[0]
    block = FFN_BLOCK
    steps = seq // block
    grid = (nb, steps)
    xspec = pl.BlockSpec((None, block, D_MODEL), lambda b, s: (b, s, 0))
    out_shape = (
        jax.ShapeDtypeStruct((nb, seq, D_MODEL), F32),
        jax.ShapeDtypeStruct((depth, nb, CONV_W - 1, 2 * D_FF), F32),
    )
    out_specs = (xspec, pl.BlockSpec((None, None, CONV_W - 1, 2 * D_FF), lambda b, s: (layer, b, 0, 0)))
    in_specs = ([xspec, pl.BlockSpec((None, block, D_MODEL), _next_block_map(nb, steps))]
                + [_resident(a, layer) for a in (g2, wup, cw, cb, wdn)] + [_resident(gf)])
    args = [x, x, g2, wup, cw, cb, wdn, gf]
    aliases = {}
    if conv_stack is not None:
        in_specs.append(pl.BlockSpec(memory_space=pl.ANY))
        aliases = {len(args): 1}
        args.append(conv_stack)
    reorder = pltpu.VMEM((D_MODEL // V7X_LANES, V7X_SUBLANES * _perm_pitch(block), V7X_LANES), F32)
    return pl.pallas_call(
        functools.partial(_ffn_kernel, block=block, final=final),
        grid=grid, in_specs=in_specs, out_specs=out_specs, out_shape=out_shape,
        scratch_shapes=[pltpu.VMEM((V7X_SUBLANES, 2 * D_FF), F32),
                        pltpu.VMEM((block, D_FF), BF16),
                        reorder, reorder,
                        pltpu.VMEM((block, D_MODEL), BF16)],
        input_output_aliases=aliases,
        compiler_params=pltpu.CompilerParams(
            dimension_semantics=("arbitrary", "arbitrary"), vmem_limit_bytes=V7X_VMEM_LIMIT),
        name="prompt_ffn",
    )(*args)


def _sample_proj_kernel(x_ref, g1_ref, wgt_ref, wm_ref, z_ref, gates_ref):
    xn = _rmsnorm(x_ref[...], g1_ref[...]).astype(BF16)
    z_ref[...] = _dot(xn, wm_ref[...])

    @pl.when(pl.program_id(0) == 0)
    def _():
        gt = _dot_nt(wgt_ref[...], xn)
        gt = jnp.concatenate([gt, jnp.zeros((V7X_LANES - gt.shape[0], gt.shape[1]), F32)], axis=0)
        g = gt.T
        gates_ref[...] = jnp.concatenate([g, pltpu.roll(g, V7X_LANES - GATE_ROWS, 1)], axis=1)


def _single_step(kernel_fn, name, out_shape, whole, layered, layer, stacked=None):
    in_specs = [_resident(a) for a in whole] + [_resident(a, layer) for a in layered]
    args = list(whole) + list(layered)
    out_specs = [pl.BlockSpec(o.shape, lambda i, nd=len(o.shape): (0,) * nd) for o in out_shape]
    aliases = {}
    if stacked is not None:
        k, prev = stacked
        nd = len(out_shape[k].shape) - 1
        out_specs[k] = pl.BlockSpec((None,) + out_shape[k].shape[1:], lambda i: (layer,) + (0,) * nd)
        if prev is not None:
            in_specs.append(pl.BlockSpec(memory_space=pl.ANY))
            aliases = {len(args): k}
            args.append(prev)
    return pl.pallas_call(
        kernel_fn,
        grid=(1,),
        in_specs=in_specs, out_specs=tuple(out_specs), out_shape=out_shape,
        input_output_aliases=aliases,
        compiler_params=pltpu.CompilerParams(
            dimension_semantics=("arbitrary",), vmem_limit_bytes=V7X_VMEM_LIMIT),
        name=name,
    )(*args)


def _sample_proj(x, layer, g1, wm, wgt):
    n = x.shape[0]
    assert n == V7X_LANES, "the gate transpose assumes one lane tile of sample rows"
    tile = SAMPLE_PROJ_TILE
    return pl.pallas_call(
        _sample_proj_kernel,
        grid=(P_MAIN // tile,),
        in_specs=[_resident(x), _resident(g1, layer), _resident(wgt, layer),
                  pl.BlockSpec((None, D_MODEL, tile), lambda j: (layer, 0, j))],
        out_specs=(pl.BlockSpec((n, tile), lambda j: (0, j)),
                   pl.BlockSpec((n, 2 * V7X_LANES), lambda j: (0, 0))),
        out_shape=(jax.ShapeDtypeStruct((n, P_MAIN), F32), jax.ShapeDtypeStruct((n, 2 * V7X_LANES), F32)),
        compiler_params=pltpu.CompilerParams(
            dimension_semantics=("arbitrary",), vmem_limit_bytes=V7X_VMEM_LIMIT),
        name="sample_proj",
    )(x, g1, wgt, wm)


def _sample_state_kernel(q_ref, k_ref, v_ref, gates_ref, gbias_ref, m_ref, n_ref, c_ref, *rest):
    h_ref, co_ref, no_ref, mo_ref = rest[-4:]
    tb = SAMPLE_BLOCK
    ig = gates_ref[:, 0:V7X_LANES] + gbias_ref[:, 0:V7X_LANES]
    lf = _log_sigmoid(gates_ref[:, V7X_LANES:] + gbias_ref[:, V7X_LANES:])
    inter = lf + m_ref[...]
    m_t = jnp.maximum(inter, ig)
    d_in = jnp.exp(ig - m_t)
    w_inter = jnp.exp(inter - m_t)
    floor = jnp.exp(-m_t)
    mo_ref[...] = m_t

    row8 = lax.broadcasted_iota(jnp.int32, (CHUNK, DK), 0)
    for h in range(N_HEADS):
        q8 = q_ref[:, h * DK:(h + 1) * DK]
        k8 = k_ref[:, h * DK:(h + 1) * DK] * K_SCALE
        v8 = v_ref[:, h * DV:(h + 1) * DV]
        d_h = d_in[:, h:h + 1]
        w_h = w_inter[:, h:h + 1]
        n8 = n_ref[:, h * DK:(h + 1) * DK]
        s = jnp.sum(q8 * k8, axis=-1, keepdims=True) * d_h
        den = s + w_h * jnp.sum(q8 * n8, axis=-1, keepdims=True)
        qb = q8.astype(BF16)
        inter_rows = [_dot_nt(qb, c_ref[j, h].astype(BF16))[j:j + 1] for j in range(tb)]
        num = s * v8 + w_h * jnp.concatenate(inter_rows, axis=0)
        h_ref[:, h * DV:(h + 1) * DV] = num / jnp.maximum(jnp.abs(den), floor[:, h:h + 1])
        no_ref[:, h * DK:(h + 1) * DK] = w_h * n8 + d_h * k8

        vt = jnp.concatenate([d_h * v8, jnp.zeros((CHUNK - tb, DV), F32)], axis=0).T.astype(BF16)
        kpad = jnp.concatenate([k8, jnp.zeros((CHUNK - tb, DK), F32)], axis=0)
        for j in range(tb):
            kj = jnp.where(row8 == j, kpad, 0.0).astype(BF16)
            co_ref[j, h] = w_inter[j:j + 1, h:h + 1] * c_ref[j, h] + _dot(vt, kj)


def _sample_state(z, gates, gbias, m_pad, n_state, c_state, layer, c_stack):
    n = z.shape[0]
    tb = SAMPLE_BLOCK
    c_block = pl.BlockSpec((None, tb, N_HEADS, DV, DK), lambda i: (layer, i, 0, 0, 0))
    in_specs = [
        pl.BlockSpec((tb, N_HEADS * DK), lambda i: (i, OFF_Q // (N_HEADS * DK))),
        pl.BlockSpec((tb, N_HEADS * DK), lambda i: (i, OFF_K // (N_HEADS * DK))),
        pl.BlockSpec((tb, N_HEADS * DV), lambda i: (i, OFF_V // (N_HEADS * DV))),
        pl.BlockSpec((tb, 2 * V7X_LANES), lambda i: (i, 0)),
        pl.BlockSpec((1, 2 * V7X_LANES), lambda i: (0, 0)),
        pl.BlockSpec((tb, V7X_LANES), lambda i: (i, 0)),
        pl.BlockSpec((tb, N_HEADS * DK), lambda i: (i, 0)),
        c_block,
    ]
    args = [z, z, z, gates, gbias, m_pad, n_state, c_state]
    aliases = {}
    if c_stack is not None:
        in_specs.append(pl.BlockSpec(memory_space=pl.ANY))
        aliases = {len(args): 1}
        args.append(c_stack)
    out_shape = (
        jax.ShapeDtypeStruct((n, N_HEADS * DV), F32),
        jax.ShapeDtypeStruct(c_state.shape, F32),
        jax.ShapeDtypeStruct((n, N_HEADS * DK), F32),
        jax.ShapeDtypeStruct((n, V7X_LANES), F32),
    )
    out_specs = (
        pl.BlockSpec((tb, N_HEADS * DV), lambda i: (i, 0)),
        c_block,
        pl.BlockSpec((tb, N_HEADS * DK), lambda i: (i, 0)),
        pl.BlockSpec((tb, V7X_LANES), lambda i: (i, 0)),
    )
    return pl.pallas_call(
        _sample_state_kernel,
        grid=(n // tb,), in_specs=in_specs, out_specs=out_specs, out_shape=out_shape,
        input_output_aliases=aliases,
        compiler_params=pltpu.CompilerParams(
            dimension_semantics=("arbitrary",), vmem_limit_bytes=V7X_VMEM_LIMIT),
        name="sample_state",
    )(*args)


def _sample_mixer_kernel(x_ref, z_ref, h_ref, gv_ref, ws0_ref, bs0_ref, wa_ref, wb_ref, wo_ref, *rest):
    xo_ref, vn_ref = rest[-2:]
    y_a = _dot((jax.nn.sigmoid(z_ref[:, OFF_O:OFF_U]) * h_ref[...]).astype(BF16), wa_ref[...])
    u = jax.nn.gelu(z_ref[:, OFF_U:OFF_VB])
    vn = _rmsnorm(jax.nn.gelu(z_ref[:, OFF_VB:OFF_GA]), gv_ref[...])
    vn_ref[:, 0, :] = vn
    mixed = ws0_ref[...] * vn + bs0_ref[...]
    y_b = _dot((u * mixed).astype(BF16), wb_ref[...])
    merged = (jax.nn.sigmoid(z_ref[:, OFF_GA:OFF_GB]) * y_a
              + jax.nn.sigmoid(z_ref[:, OFF_GB:P_MAIN]) * y_b)
    xo_ref[...] = x_ref[...] + _dot(merged.astype(BF16), wo_ref[...])


def _sample_mixer(x, z, h, layer, gv, ws0, bs0, wa, wb, wo, vn_stack):
    n = x.shape[0]
    out_shape = (jax.ShapeDtypeStruct((n, D_MODEL), F32),
                 jax.ShapeDtypeStruct((wa.shape[0], n, 1, D_B), F32))
    return _single_step(_sample_mixer_kernel, "sample_mixer", out_shape,
                        (x, z, h), (gv, ws0, bs0, wa, wb, wo), layer, stacked=(1, vn_stack))


def _sample_ffn_kernel(x_ref, gf_ref, g2_ref, bufa_ref, bufg_ref, wupa_ref, wupg_ref, cwa_ref, cwg_ref,
                       cba_ref, cbg_ref, wdn_ref, *rest, final):
    xo_ref, nbuf_ref, acc_s = rest[-3:]
    j = pl.program_id(0)
    last = SAMPLE_FFN_STEPS - 1
    width = D_FF // SAMPLE_FFN_STEPS
    x = x_ref[...]
    xn = _rmsnorm(x, g2_ref[...]).astype(BF16)

    def conv(up, buf_ref, cw_ref, cb_ref):
        b0 = buf_ref[:, 0, :]
        b1 = buf_ref[:, 1, :]
        return b1, cb_ref[...] + cw_ref[0:1, :] * b0 + cw_ref[1:2, :] * b1 + cw_ref[2:3, :] * up

    up_a = _dot(xn, wupa_ref[...])
    up_g = _dot(xn, wupg_ref[...])
    prev_a, conv_a = conv(up_a, bufa_ref, cwa_ref, cba_ref)
    prev_g, conv_g = conv(up_g, bufg_ref, cwg_ref, cbg_ref)
    part = _dot((jax.nn.silu(conv_a) * conv_g).astype(BF16), wdn_ref[...])

    for jj in range(SAMPLE_FFN_STEPS):
        @pl.when(j == jj)
        def _():
            for col0, prev, up in ((jj * width, prev_a, up_a), (D_FF + jj * width, prev_g, up_g)):
                nbuf_ref[:, 0, col0:col0 + width] = prev
                nbuf_ref[:, 1, col0:col0 + width] = up

    @pl.when(j == 0)
    def _():
        acc_s[...] = part

    @pl.when((j > 0) & (j < last))
    def _():
        acc_s[...] += part

    @pl.when(j == last)
    def _():
        y = x + (acc_s[...] + part)
        if final:
            xo_ref[:, 0, :] = _rmsnorm(y, gf_ref[...])
        else:
            xo_ref[...] = y


def _sample_ffn(x, gf, layer, buf, g2, wup, cw, cb, wdn, final, conv_stack):
    n = x.shape[0]
    steps = SAMPLE_FFN_STEPS
    width = D_FF // steps

    def cols(shape, gate):
        nd = len(shape)
        return pl.BlockSpec((None,) + shape[:-1] + (width,),
                            lambda j: (layer,) + (0,) * (nd - 1) + (gate * steps + j,))

    in_specs = [_resident(x), _resident(gf), _resident(g2, layer),
                cols(buf.shape[1:], 0), cols(buf.shape[1:], 1),
                cols(wup.shape[1:], 0), cols(wup.shape[1:], 1),
                cols(cw.shape[1:], 0), cols(cw.shape[1:], 1),
                cols(cb.shape[1:], 0), cols(cb.shape[1:], 1),
                pl.BlockSpec((None, width, D_MODEL), lambda j: (layer, j, 0))]
    args = [x, gf, g2, buf, buf, wup, wup, cw, cw, cb, cb, wdn]
    aliases = {}
    if conv_stack is not None:
        in_specs.append(pl.BlockSpec(memory_space=pl.ANY))
        aliases = {len(args): 1}
        args.append(conv_stack)
    x_out = (n, 1, D_MODEL) if final else (n, D_MODEL)
    nd_out = len(x_out)
    return pl.pallas_call(
        functools.partial(_sample_ffn_kernel, final=final),
        grid=(steps,),
        in_specs=in_specs,
        out_specs=(pl.BlockSpec(x_out, lambda j: (0,) * nd_out),
                   pl.BlockSpec((None,) + buf.shape[1:], lambda j: (layer, 0, 0, 0))),
        out_shape=(jax.ShapeDtypeStruct(x_out, F32), jax.ShapeDtypeStruct(buf.shape, F32)),
        scratch_shapes=[pltpu.VMEM((n, D_MODEL), F32)],
        input_output_aliases=aliases,
        compiler_params=pltpu.CompilerParams(
            dimension_semantics=("arbitrary",), vmem_limit_bytes=V7X_VMEM_LIMIT),
        name="sample_ffn",
    )(*args)


def _pack_kernel(wt_ref, *refs):
    n_cast = (len(refs) - 2) // 2
    casts_in, (o_ref, wgt_ref), casts_out = refs[:n_cast], refs[n_cast:n_cast + 2], refs[n_cast + 2:]
    g8 = wt_ref[GATE_LO:GATE_HI, :]
    head = lax.broadcasted_iota(jnp.int32, g8.shape, 0) < N_HEADS
    wgt_ref[...] = jnp.concatenate(
        [jnp.where(head, g8, 0.0), jnp.where(head, pltpu.roll(g8, N_HEADS, 0), 0.0)], axis=0).astype(BF16)
    for j in range(P_MAIN // PACK_TILE):
        dst = j * PACK_TILE
        src = dst if dst < GATE_LO else dst + (GATE_HI - GATE_LO)
        o_ref[:, dst:dst + PACK_TILE] = wt_ref[src:src + PACK_TILE, :].T.astype(BF16)
    for src_ref, dst_ref in zip(casts_in, casts_out):
        dst_ref[...] = src_ref[...].astype(BF16)


def _pack_weights(w_in_t, casts):
    depth, p_in, d = w_in_t.shape
    steps = d // PACK_DIMS

    def slab(w):
        return pl.BlockSpec((None, w.shape[1] // steps, w.shape[2]), lambda l, r: (l, r, 0))

    return pl.pallas_call(
        _pack_kernel,
        grid=(depth, steps),
        in_specs=[pl.BlockSpec((None, p_in, PACK_DIMS), lambda l, r: (l, 0, r))] + [slab(w) for w in casts],
        out_specs=(pl.BlockSpec((None, PACK_DIMS, P_MAIN), lambda l, r: (l, r, 0)),
                   pl.BlockSpec((None, 2 * GATE_ROWS, PACK_DIMS), lambda l, r: (l, 0, r)))
        + tuple(slab(w) for w in casts),
        out_shape=(jax.ShapeDtypeStruct((depth, d, P_MAIN), BF16),
                   jax.ShapeDtypeStruct((depth, 2 * GATE_ROWS, d), BF16))
        + tuple(jax.ShapeDtypeStruct(w.shape, BF16) for w in casts),
        compiler_params=pltpu.CompilerParams(
            dimension_semantics=("arbitrary", "arbitrary"), vmem_limit_bytes=V7X_VMEM_LIMIT),
        name="pack_weights",
    )(w_in_t, *casts)


def kernel(x_prompt, x_sample, state_mlstm_C, state_mlstm_n, state_mlstm_m, state_ffn_conv, w_in, b_igate, b_fgate, g_norm1, g_vnorm, w_spatial, b_spatial, w_branch_a, w_branch_b, w_out, g_norm2, w_up, conv_w, conv_b, w_down, g_final):
    depth = w_in.shape[0]
    n_dec = x_sample.shape[0]
    xp = x_prompt
    xs = x_sample.reshape(n_dec, D_MODEL)
    gf = g_final.reshape(1, D_MODEL)

    wm, wgt, wb, wdn = _pack_weights(jnp.swapaxes(w_in, 1, 2), (w_branch_b, w_down))
    head_pad = ((0, 0), (0, GATE_ROWS - N_HEADS))
    gbias_col = jnp.concatenate([jnp.pad(b_igate, head_pad), jnp.pad(b_fgate, head_pad)],
                                axis=1).reshape(depth, 2 * GATE_ROWS, 1)
    head_lanes = ((0, 0), (0, V7X_LANES - N_HEADS))
    gbias_row = jnp.concatenate([jnp.pad(b_igate, head_lanes), jnp.pad(b_fgate, head_lanes)],
                                axis=1).reshape(depth, 1, 2 * V7X_LANES)
    g1 = g_norm1.reshape(depth, 1, D_MODEL)
    gv = g_vnorm.reshape(depth, 1, D_B)
    g2 = g_norm2.reshape(depth, 1, D_MODEL)
    bst = jnp.swapaxes(b_spatial, 1, 2)
    ws0 = jnp.repeat(w_spatial[:, :, 0, 0], DG, axis=1).reshape(depth, 1, D_B)
    bs0 = jnp.repeat(b_spatial[:, :, 0], DG, axis=1).reshape(depth, 1, D_B)
    wa = w_branch_a.astype(BF16)
    wo = w_out.astype(BF16)
    wup = w_up.astype(BF16)
    cb = conv_b.reshape(depth, 1, 2 * D_FF)
    n_state = state_mlstm_n.reshape(depth, n_dec, N_HEADS * DK)
    m_pad = jnp.pad(state_mlstm_m, ((0, 0), (0, 0), (0, V7X_LANES - N_HEADS)))

    small = [[] for _ in range(4)]
    cp_stack = None
    convp_stack = None
    c_stack = None
    vn_stack = None
    conv_stack = None
    for l in range(depth):
        final = l == depth - 1

        xp, cp_stack, n_p, m_p = _prompt_mixer(xp, l, wm, wgt, gbias_col, g1, gv, w_spatial, bst, wa, wb, wo,
                                               cp_stack)
        xp, convp_stack = _prompt_ffn(xp, l, g2, wup, conv_w, cb, wdn, gf, final, convp_stack)

        z, gates = _sample_proj(xs, l, g1, wm, wgt)
        h, c_stack, n_s, m_s = _sample_state(
            z, gates, gbias_row[l], m_pad[l], n_state[l], state_mlstm_C, l, c_stack)
        xs, vn_stack = _sample_mixer(xs, z, h, l, gv, ws0, bs0, wa, wb, wo, vn_stack)
        xs, conv_stack = _sample_ffn(xs, gf, l, state_ffn_conv, g2, wup, conv_w, cb, wdn, final, conv_stack)

        for lst, val in zip(small, (n_p, m_p[:, :N_HEADS, 0],
                                    n_s.reshape(n_dec, N_HEADS, DK), m_s[:, :N_HEADS])):
            lst.append(val)
    st = [jnp.stack(o) for o in small]
    return (xp, xs, cp_stack, st[0], st[1], convp_stack, c_stack, st[2], st[3], conv_stack, vn_stack)
```

```python
import functools

import jax
import jax.numpy as jnp
from jax import lax
from jax.experimental import pallas as pl
from jax.experimental.pallas import tpu as pltpu

D_MODEL = 1024
N_HEADS = 4
DK = 128
DV = 256
CHUNK = 128
D_B = 1024
N_GROUPS = 4
DG = D_B // N_GROUPS
D_FF = 2816
CONV_W = 3
EPS = 1e-6
K_SCALE = DK ** -0.5

OFF_Q = 0
OFF_K = OFF_Q + N_HEADS * DK
OFF_V = OFF_K + N_HEADS * DK
OFF_O = OFF_V + N_HEADS * DV
OFF_U = OFF_O + N_HEADS * DV
OFF_VB = OFF_U + D_B
OFF_GA = OFF_VB + D_B
OFF_GB = OFF_GA + D_MODEL
P_MAIN = OFF_GB + D_MODEL
GATE_LO = 2 * N_HEADS * DK + 2 * N_HEADS * DV
GATE_HI = GATE_LO + 2 * N_HEADS

V7X_LANES = 128
V7X_SUBLANES = 8
GATE_ROWS = V7X_SUBLANES
V7X_VMEM_LIMIT = 56 * 1024 * 1024
FFN_COL_TILE = 256
PROJ_TILE = 256
MIXER_BLOCK = 512
FFN_BLOCK = 512
SAMPLE_BLOCK = 16
PACK_TILE = 256
PACK_DIMS = 256
SAMPLE_PROJ_TILE = P_MAIN // 2
SAMPLE_FFN_STEPS = 2

F32 = jnp.float32
BF16 = jnp.bfloat16


def _dot(a, b):
    return jnp.dot(a, b, preferred_element_type=F32)


def _dot_nt(a, b):
    return lax.dot_general(a, b, (((1,), (1,)), ((), ())), preferred_element_type=F32)


def _rmsnorm(x, g):
    r = lax.rsqrt(jnp.mean(x * x, axis=-1, keepdims=True) + EPS)
    return x * r * g


def _log_sigmoid(x):
    return jnp.minimum(x, 0.0) - jnp.log1p(jnp.exp(-jnp.abs(x)))


def _scan_lanes(x, op, fill):
    lane = lax.broadcasted_iota(jnp.int32, x.shape, 1)
    k = 1
    while k < x.shape[1]:
        shifted = pltpu.roll(x, k, 1)
        x = op(x, jnp.where(lane >= k, shifted, fill))
        k *= 2
    return x


def _anchor_zero(x):
    sub, lanes = V7X_SUBLANES, V7X_LANES
    acc = jnp.zeros((sub, lanes), jnp.uint32)
    for r in range(x.shape[0] // sub):
        for c in range(x.shape[1] // lanes):
            piece = pltpu.bitcast(x[r * sub:(r + 1) * sub, c * lanes:(c + 1) * lanes], jnp.uint32)
            acc = acc | ((piece >> 16) >> 16)
    return pltpu.bitcast(acc, F32)


def _next_block_map(nb, steps):
    def index_map(b, s):
        nxt = jnp.minimum(b * steps + s + 1, nb * steps - 1)
        return (nxt // steps, nxt % steps, 0)
    return index_map


def _mixer_kernel(x_ref, wm_ref, wgt_ref, gbias_ref, g1_ref, gv_ref, ws_ref, bst_ref,
                  wa_ref, wb_ref, wo_ref, *rest, block):
    (xo_ref, c_ref, n_ref, m_ref,
     ct_s, n_s, m_s, q_s, k_s, v_s, h_s, so_s, u_s, vb_s, sg_s, um_s) = rest[-16:]
    s = pl.program_id(1)
    n_chunks = block // CHUNK

    @pl.when(s == 0)
    def _():
        ct_s[...] = jnp.zeros_like(ct_s)
        n_s[...] = jnp.zeros_like(n_s)
        m_s[...] = jnp.zeros_like(m_s)

    xn = _rmsnorm(x_ref[...], g1_ref[...]).astype(BF16)

    gates = _dot_nt(wgt_ref[...], xn) + gbias_ref[...]

    def proj(off, t):
        return _dot(xn, wm_ref[:, off + t * PROJ_TILE:off + (t + 1) * PROJ_TILE])

    def tile(t):
        return slice(t * PROJ_TILE, (t + 1) * PROJ_TILE)

    for t in range(N_HEADS * DK // PROJ_TILE):
        q_s[:, tile(t)] = proj(OFF_Q, t)
        k_s[:, tile(t)] = proj(OFF_K, t) * K_SCALE
    for t in range(N_HEADS * DV // PROJ_TILE):
        v_s[:, tile(t)] = proj(OFF_V, t)

    sumsq = [jnp.zeros((block, 1), F32)]

    def vb_tile(t):
        g = jax.nn.gelu(proj(OFF_VB, t))
        vb_s[:, tile(t)] = g
        sumsq[0] = sumsq[0] + jnp.sum(g * g, axis=-1, keepdims=True)

    def u_tile(t):
        u_s[:, tile(t)] = jax.nn.gelu(proj(OFF_U, t))

    def o_tile(t):
        so_s[:, tile(t)] = jax.nn.sigmoid(proj(OFF_O, t))

    def ga_tile(t):
        sg_s[:, tile(t)] = jax.nn.sigmoid(proj(OFF_GA, t))

    def gb_tile(t):
        sg_s[:, D_MODEL + t * PROJ_TILE:D_MODEL + (t + 1) * PROJ_TILE] = jax.nn.sigmoid(proj(OFF_GB, t))

    jobs = [(f, t) for f in (vb_tile, u_tile, o_tile, ga_tile, gb_tile) for t in range(D_MODEL // PROJ_TILE)]

    def run_jobs(count):
        for _ in range(min(count, len(jobs))):
            f, t = jobs.pop(0)
            f(t)

    row_i = lax.broadcasted_iota(jnp.int32, (CHUNK, CHUNK), 0)
    col_i = lax.broadcasted_iota(jnp.int32, (CHUNK, CHUNK), 1)
    causal = row_i >= col_i
    heads = range(N_HEADS)

    for c in range(n_chunks):
        r0 = c * CHUNK
        ig = gates[0:GATE_ROWS, r0:r0 + CHUNK]
        lf = _log_sigmoid(gates[GATE_ROWS:2 * GATE_ROWS, r0:r0 + CHUNK])
        b = _scan_lanes(lf, jnp.add, 0.0)
        a = ig - b
        m_prev = m_s[...]
        gmax = jnp.maximum(m_prev, _scan_lanes(a, jnp.maximum, -jnp.inf))
        m_t = b + gmax
        w_inter = jnp.exp(m_prev - gmax)
        g_last = gmax[:, CHUNK - 1:CHUNK]
        w_last = jnp.exp(a - g_last)
        floor = jnp.exp(-m_t)
        decay = w_inter[:, CHUNK - 1:CHUNK]
        m_s[...] = jnp.broadcast_to(m_t[:, CHUNK - 1:CHUNK], m_s.shape)

        rows = jnp.concatenate(
            [gmax, w_inter, w_last, floor,
             jnp.zeros((CHUNK - 4 * GATE_ROWS, CHUNK), F32)], axis=0)
        cols = rows.T

        def col(kind, h):
            return cols[:, kind * GATE_ROWS + h:kind * GATE_ROWS + h + 1]

        qf = [q_s[r0:r0 + CHUNK, h * DK:(h + 1) * DK] for h in heads]
        kf = [k_s[r0:r0 + CHUNK, h * DK:(h + 1) * DK] for h in heads]
        vf = [v_s[r0:r0 + CHUNK, h * DV:(h + 1) * DV] for h in heads]
        qb = [x.astype(BF16) for x in qf]
        kt = [x.T.astype(BF16) for x in kf]
        run_jobs(2)
        zero = jnp.zeros((DK, CHUNK), BF16)
        sc = []
        for h in range(0, N_HEADS, 2):
            kk = jnp.concatenate([jnp.concatenate([kt[h], zero], axis=1),
                                  jnp.concatenate([zero, kt[h + 1]], axis=1)], axis=0)
            pair = _dot(jnp.concatenate([qb[h], qb[h + 1]], axis=1), kk)
            sc += [pair[:, :CHUNK], pair[:, CHUNK:]]
        dmat = [jnp.where(causal, jnp.exp(a[h:h + 1, :] - col(0, h)), 0.0) for h in heads]
        run_jobs(1)
        sd = [sc[h] * dmat[h] for h in heads]
        ct = [ct_s[h] for h in heads]
        num = [_dot(jnp.concatenate([sd[h], col(1, h) * qf[h]], axis=1).astype(BF16),
                    jnp.concatenate([vf[h], ct[h]], axis=0).astype(BF16)) for h in heads]
        run_jobs(2)
        for h in heads:
            nh = n_s[h:h + 1, :]
            den = (jnp.sum(sd[h], axis=-1, keepdims=True)
                   + col(1, h) * jnp.sum(qf[h] * nh, axis=-1, keepdims=True))
            h_s[r0:r0 + CHUNK, h * DV:(h + 1) * DV] = num[h] * (1.0 / jnp.maximum(jnp.abs(den), col(3, h)))
            dec = decay[h:h + 1, :]
            n_s[h:h + 1, :] = dec * nh + jnp.sum(col(2, h) * kf[h], axis=0, keepdims=True)
        run_jobs(1)
        for h in heads:
            ct_s[h] = decay[h:h + 1, :] * ct[h] + _dot(kt[h], (col(2, h) * vf[h]).astype(BF16))
    run_jobs(len(jobs))

    y_a = _dot((so_s[...] * h_s[...]).astype(BF16), wa_ref[...])

    rinv = lax.rsqrt(sumsq[0] * (1.0 / D_B) + EPS)
    for g in range(N_GROUPS):
        gcols = slice(g * DG, (g + 1) * DG)
        w_tri = jnp.where(causal, ws_ref[g], 0.0).astype(BF16)
        bias_c = bst_ref[:, g:g + 1]
        for c in range(n_chunks):
            rws = slice(c * CHUNK, (c + 1) * CHUNK)
            vn = vb_s[rws, gcols] * rinv[rws] * gv_ref[:, gcols]
            um_s[rws, gcols] = u_s[rws, gcols] * (_dot(w_tri, vn.astype(BF16)) + bias_c)
    y_b = _dot(um_s[...].astype(BF16), wb_ref[...])

    merged = sg_s[:, 0:D_MODEL] * y_a + sg_s[:, D_MODEL:2 * D_MODEL] * y_b
    xo_ref[...] = x_ref[...] + _dot(merged.astype(BF16), wo_ref[...])

    @pl.when(s == pl.num_programs(1) - 1)
    def _():
        for h in range(N_HEADS):
            c_ref[h] = ct_s[h].T
        n_ref[...] = n_s[...]
        m_ref[...] = m_s[...]


def _resident(arr, layer=None):
    if layer is None:
        nd = arr.ndim
        return pl.BlockSpec(arr.shape, lambda *_: (0,) * nd, pipeline_mode=pl.Buffered(1))
    nd = arr.ndim - 1
    return pl.BlockSpec((None,) + arr.shape[1:], lambda *_: (layer,) + (0,) * nd,
                        pipeline_mode=pl.Buffered(1))


def _prompt_mixer(x, layer, wm, wgt, gbias, g1, gv, ws, bst, wa, wb, wo, c_stack):
    nb, seq, _ = x.shape
    depth = wm.shape[0]
    block = MIXER_BLOCK
    grid = (nb, seq // block)
    xspec = pl.BlockSpec((None, block, D_MODEL), lambda b, s: (b, s, 0))
    out_shape = (
        jax.ShapeDtypeStruct((nb, seq, D_MODEL), F32),
        jax.ShapeDtypeStruct((depth, nb, N_HEADS, DV, DK), F32),
        jax.ShapeDtypeStruct((nb, N_HEADS, DK), F32),
        jax.ShapeDtypeStruct((nb, GATE_ROWS, V7X_LANES), F32),
    )
    out_specs = (
        xspec,
        pl.BlockSpec((None, None, N_HEADS, DV, DK), lambda b, s: (layer, b, 0, 0, 0)),
        pl.BlockSpec((None, N_HEADS, DK), lambda b, s: (b, 0, 0)),
        pl.BlockSpec((None, GATE_ROWS, V7X_LANES), lambda b, s: (b, 0, 0)),
    )
    in_specs = [xspec] + [_resident(a, layer) for a in (wm, wgt, gbias, g1, gv, ws, bst, wa, wb, wo)]
    args = [x, wm, wgt, gbias, g1, gv, ws, bst, wa, wb, wo]
    aliases = {}
    if c_stack is not None:
        in_specs.append(pl.BlockSpec(memory_space=pl.ANY))
        aliases = {len(args): 1}
        args.append(c_stack)
    scratch = [
        pltpu.VMEM((N_HEADS, DK, DV), F32),
        pltpu.VMEM((N_HEADS, DK), F32),
        pltpu.VMEM((GATE_ROWS, V7X_LANES), F32),
        pltpu.VMEM((block, N_HEADS * DK), F32),
        pltpu.VMEM((block, N_HEADS * DK), F32),
        pltpu.VMEM((block, N_HEADS * DV), F32),
        pltpu.VMEM((block, N_HEADS * DV), F32),
        pltpu.VMEM((block, N_HEADS * DV), F32),
        pltpu.VMEM((block, D_B), F32),
        pltpu.VMEM((block, D_B), F32),
        pltpu.VMEM((block, 2 * D_MODEL), F32),
        pltpu.VMEM((block, D_B), F32),
    ]
    return pl.pallas_call(
        functools.partial(_mixer_kernel, block=block),
        grid=grid, in_specs=in_specs, out_specs=out_specs, out_shape=out_shape,
        scratch_shapes=scratch, input_output_aliases=aliases,
        compiler_params=pltpu.CompilerParams(
            dimension_semantics=("arbitrary", "arbitrary"), vmem_limit_bytes=V7X_VMEM_LIMIT),
        name="prompt_mixer",
    )(*args)


def _conv_taps(up, carry_s, cw_ref, cb_ref, cols):
    sub = V7X_SUBLANES
    rows = up.shape[0]
    last1 = up[rows - sub:rows]
    last2 = up[rows - 2 * sub:rows - sub]
    first = lax.broadcasted_iota(jnp.int32, last1.shape, 0) == 0
    back1 = jnp.where(first, carry_s[sub - 1:sub, cols], pltpu.roll(last1, 1, 0))
    back2 = jnp.where(first, carry_s[sub - 2:sub - 1, cols], pltpu.roll(last2, 1, 0))
    carry_s[sub - 2:sub - 1, cols] = last2[sub - 1:sub]
    carry_s[sub - 1:sub, cols] = last1[sub - 1:sub]
    m1 = jnp.concatenate([back1, up[0:rows - sub]], axis=0)
    m2 = jnp.concatenate([back2, back1, up[0:rows - 2 * sub]], axis=0)
    return (cb_ref[:, cols] + cw_ref[0:1, cols] * m2 + cw_ref[1:2, cols] * m1
            + cw_ref[2:3, cols] * up)


def _perm_pitch(block):
    return block // V7X_SUBLANES + V7X_SUBLANES


def _ffn_kernel(x_ref, xnext_ref, g2_ref, wup_ref, cw_ref, cb_ref, wdn_ref, gf_ref, *rest, block, final):
    xo_ref, conv_ref, carry_s, act_s, perm_s, unperm_s, xn_s = rest[-7:]
    s = pl.program_id(1)
    sub, lanes = V7X_SUBLANES, V7X_LANES
    groups = block // sub
    chunks = D_MODEL // lanes
    pitch = _perm_pitch(block)

    @pl.when(s == 0)
    def _():
        carry_s[...] = jnp.zeros_like(carry_s)

    def stage(src_ref):
        for c in range(chunks):
            for i in range(sub):
                perm_s[c, i * pitch:i * pitch + groups] = src_ref[i * groups:(i + 1) * groups,
                                                                  c * lanes:(c + 1) * lanes]
        x = jnp.concatenate(
            [jnp.concatenate([perm_s[c, pl.ds(r, sub, stride=pitch), :] for c in range(chunks)], axis=1)
             for r in range(groups)], axis=0)
        xn = _rmsnorm(x, g2_ref[...])
        xn_s[...] = xn.astype(BF16)
        return xn

    @pl.when((pl.program_id(0) == 0) & (s == 0))
    def _():
        stage(x_ref)

    xn = xn_s[...]
    for j in range(D_FF // FFN_COL_TILE):
        halves = []
        for half in range(2):
            cols = slice(half * D_FF + j * FFN_COL_TILE, half * D_FF + (j + 1) * FFN_COL_TILE)
            up = _dot(xn, wup_ref[:, cols])
            halves.append(_conv_taps(up, carry_s, cw_ref, cb_ref, cols))
        act_s[:, j * FFN_COL_TILE:(j + 1) * FFN_COL_TILE] = (jax.nn.silu(halves[0]) * halves[1]).astype(BF16)
    staged = stage(xnext_ref)
    down = _dot(act_s[...], wdn_ref[...])
    anchor = _anchor_zero(staged)
    for r in range(groups):
        for c in range(chunks):
            tile_rc = down[r * sub:(r + 1) * sub, c * lanes:(c + 1) * lanes]
            if r == 0 and c == 0:
                tile_rc = tile_rc + anchor
            unperm_s[c, pl.ds(r, sub, stride=pitch), :] = tile_rc
    for i in range(sub):
        rows = slice(i * groups, (i + 1) * groups)
        y = x_ref[rows, :] + jnp.concatenate(
            [unperm_s[c, i * pitch:i * pitch + groups] for c in range(chunks)], axis=1)
        if final:
            y = _rmsnorm(y, gf_ref[...])
        xo_ref[rows, :] = y

    @pl.when(s == pl.num_programs(1) - 1)
    def _():
        conv_ref[...] = carry_s[V7X_SUBLANES - (CONV_W - 1):V7X_SUBLANES, :]


def _prompt_ffn(x, layer, g2, wup, cw, cb, wdn, gf, final, conv_stack):
    nb, seq, _ = x.shape
    depth = wup.shape[0]
    block = FFN_BLOCK
    steps = seq // block
    grid = (nb, steps)
    xspec = pl.BlockSpec((None, block, D_MODEL), lambda b, s: (b, s, 0))
    out_shape = (
        jax.ShapeDtypeStruct((nb, seq, D_MODEL), F32),
        jax.ShapeDtypeStruct((depth, nb, CONV_W - 1, 2 * D_FF), F32),
    )
    out_specs = (xspec, pl.BlockSpec((None, None, CONV_W - 1, 2 * D_FF), lambda b, s: (layer, b, 0, 0)))
    in_specs = ([xspec, pl.BlockSpec((None, block, D_MODEL), _next_block_map(nb, steps))]
                + [_resident(a, layer) for a in (g2, wup, cw, cb, wdn)] + [_resident(gf)])
    args = [x, x, g2, wup, cw, cb, wdn, gf]
    aliases = {}
    if conv_stack is not None:
        in_specs.append(pl.BlockSpec(memory_space=pl.ANY))
        aliases = {len(args): 1}
        args.append(conv_stack)
    reorder = pltpu.VMEM((D_MODEL // V7X_LANES, V7X_SUBLANES * _perm_pitch(block), V7X_LANES), F32)
    return pl.pallas_call(
        functools.partial(_ffn_kernel, block=block, final=final),
        grid=grid, in_specs=in_specs, out_specs=out_specs, out_shape=out_shape,
        scratch_shapes=[pltpu.VMEM((V7X_SUBLANES, 2 * D_FF), F32),
                        pltpu.VMEM((block, D_FF), BF16),
                        reorder, reorder,
                        pltpu.VMEM((block, D_MODEL), BF16)],
        input_output_aliases=aliases,
        compiler_params=pltpu.CompilerParams(
            dimension_semantics=("arbitrary", "arbitrary"), vmem_limit_bytes=V7X_VMEM_LIMIT),
        name="prompt_ffn",
    )(*args)


def _sample_proj_kernel(x_ref, g1_ref, wgt_ref, wm_ref, z_ref, gates_ref):
    xn = _rmsnorm(x_ref[...], g1_ref[...]).astype(BF16)
    z_ref[...] = _dot(xn, wm_ref[...])

    @pl.when(pl.program_id(0) == 0)
    def _():
        gt = _dot_nt(wgt_ref[...], xn)
        gt = jnp.concatenate([gt, jnp.zeros((V7X_LANES - gt.shape[0], gt.shape[1]), F32)], axis=0)
        g = gt.T
        gates_ref[...] = jnp.concatenate([g, pltpu.roll(g, V7X_LANES - GATE_ROWS, 1)], axis=1)


def _single_step(kernel_fn, name, out_shape, whole, layered, layer, stacked=None):
    in_specs = [_resident(a) for a in whole] + [_resident(a, layer) for a in layered]
    args = list(whole) + list(layered)
    out_specs = [pl.BlockSpec(o.shape, lambda i, nd=len(o.shape): (0,) * nd) for o in out_shape]
    aliases = {}
    if stacked is not None:
        k, prev = stacked
        nd = len(out_shape[k].shape) - 1
        out_specs[k] = pl.BlockSpec((None,) + out_shape[k].shape[1:], lambda i: (layer,) + (0,) * nd)
        if prev is not None:
            in_specs.append(pl.BlockSpec(memory_space=pl.ANY))
            aliases = {len(args): k}
            args.append(prev)
    return pl.pallas_call(
        kernel_fn,
        grid=(1,),
        in_specs=in_specs, out_specs=tuple(out_specs), out_shape=out_shape,
        input_output_aliases=aliases,
        compiler_params=pltpu.CompilerParams(
            dimension_semantics=("arbitrary",), vmem_limit_bytes=V7X_VMEM_LIMIT),
        name=name,
    )(*args)


def _sample_proj(x, layer, g1, wm, wgt):
    n = x.shape[0]
    assert n == V7X_LANES, "the gate transpose assumes one lane tile of sample rows"
    tile = SAMPLE_PROJ_TILE
    return pl.pallas_call(
        _sample_proj_kernel,
        grid=(P_MAIN // tile,),
        in_specs=[_resident(x), _resident(g1, layer), _resident(wgt, layer),
                  pl.BlockSpec((None, D_MODEL, tile), lambda j: (layer, 0, j))],
        out_specs=(pl.BlockSpec((n, tile), lambda j: (0, j)),
                   pl.BlockSpec((n, 2 * V7X_LANES), lambda j: (0, 0))),
        out_shape=(jax.ShapeDtypeStruct((n, P_MAIN), F32), jax.ShapeDtypeStruct((n, 2 * V7X_LANES), F32)),
        compiler_params=pltpu.CompilerParams(
            dimension_semantics=("arbitrary",), vmem_limit_bytes=V7X_VMEM_LIMIT),
        name="sample_proj",
    )(x, g1, wgt, wm)


def _sample_state_kernel(q_ref, k_ref, v_ref, gates_ref, gbias_ref, m_ref, n_ref, c_ref, *rest):
    h_ref, co_ref, no_ref, mo_ref = rest[-4:]
    tb = SAMPLE_BLOCK
    ig = gates_ref[:, 0:V7X_LANES] + gbias_ref[:, 0:V7X_LANES]
    lf = _log_sigmoid(gates_ref[:, V7X_LANES:] + gbias_ref[:, V7X_LANES:])
    inter = lf + m_ref[...]
    m_t = jnp.maximum(inter, ig)
    d_in = jnp.exp(ig - m_t)
    w_inter = jnp.exp(inter - m_t)
    floor = jnp.exp(-m_t)
    mo_ref[...] = m_t

    row8 = lax.broadcasted_iota(jnp.int32, (CHUNK, DK), 0)
    for h in range(N_HEADS):
        q8 = q_ref[:, h * DK:(h + 1) * DK]
        k8 = k_ref[:, h * DK:(h + 1) * DK] * K_SCALE
        v8 = v_ref[:, h * DV:(h + 1) * DV]
        d_h = d_in[:, h:h + 1]
        w_h = w_inter[:, h:h + 1]
        n8 = n_ref[:, h * DK:(h + 1) * DK]
        s = jnp.sum(q8 * k8, axis=-1, keepdims=True) * d_h
        den = s + w_h * jnp.sum(q8 * n8, axis=-1, keepdims=True)
        qb = q8.astype(BF16)
        inter_rows = [_dot_nt(qb, c_ref[j, h].astype(BF16))[j:j + 1] for j in range(tb)]
        num = s * v8 + w_h * jnp.concatenate(inter_rows, axis=0)
        h_ref[:, h * DV:(h + 1) * DV] = num / jnp.maximum(jnp.abs(den), floor[:, h:h + 1])
        no_ref[:, h * DK:(h + 1) * DK] = w_h * n8 + d_h * k8

        vt = jnp.concatenate([d_h * v8, jnp.zeros((CHUNK - tb, DV), F32)], axis=0).T.astype(BF16)
        kpad = jnp.concatenate([k8, jnp.zeros((CHUNK - tb, DK), F32)], axis=0)
        for j in range(tb):
            kj = jnp.where(row8 == j, kpad, 0.0).astype(BF16)
            co_ref[j, h] = w_inter[j:j + 1, h:h + 1] * c_ref[j, h] + _dot(vt, kj)


def _sample_state(z, gates, gbias, m_pad, n_state, c_state, layer, c_stack):
    n = z.shape[0]
    tb = SAMPLE_BLOCK
    c_block = pl.BlockSpec((None, tb, N_HEADS, DV, DK), lambda i: (layer, i, 0, 0, 0))
    in_specs = [
        pl.BlockSpec((tb, N_HEADS * DK), lambda i: (i, OFF_Q // (N_HEADS * DK))),
        pl.BlockSpec((tb, N_HEADS * DK), lambda i: (i, OFF_K // (N_HEADS * DK))),
        pl.BlockSpec((tb, N_HEADS * DV), lambda i: (i, OFF_V // (N_HEADS * DV))),
        pl.BlockSpec((tb, 2 * V7X_LANES), lambda i: (i, 0)),
        pl.BlockSpec((1, 2 * V7X_LANES), lambda i: (0, 0)),
        pl.BlockSpec((tb, V7X_LANES), lambda i: (i, 0)),
        pl.BlockSpec((tb, N_HEADS * DK), lambda i: (i, 0)),
        c_block,
    ]
    args = [z, z, z, gates, gbias, m_pad, n_state, c_state]
    aliases = {}
    if c_stack is not None:
        in_specs.append(pl.BlockSpec(memory_space=pl.ANY))
        aliases = {len(args): 1}
        args.append(c_stack)
    out_shape = (
        jax.ShapeDtypeStruct((n, N_HEADS * DV), F32),
        jax.ShapeDtypeStruct(c_state.shape, F32),
        jax.ShapeDtypeStruct((n, N_HEADS * DK), F32),
        jax.ShapeDtypeStruct((n, V7X_LANES), F32),
    )
    out_specs = (
        pl.BlockSpec((tb, N_HEADS * DV), lambda i: (i, 0)),
        c_block,
        pl.BlockSpec((tb, N_HEADS * DK), lambda i: (i, 0)),
        pl.BlockSpec((tb, V7X_LANES), lambda i: (i, 0)),
    )
    return pl.pallas_call(
        _sample_state_kernel,
        grid=(n // tb,), in_specs=in_specs, out_specs=out_specs, out_shape=out_shape,
        input_output_aliases=aliases,
        compiler_params=pltpu.CompilerParams(
            dimension_semantics=("arbitrary",), vmem_limit_bytes=V7X_VMEM_LIMIT),
        name="sample_state",
    )(*args)


def _sample_mixer_kernel(x_ref, z_ref, h_ref, gv_ref, ws0_ref, bs0_ref, wa_ref, wb_ref, wo_ref, *rest):
    xo_ref, vn_ref = rest[-2:]
    y_a = _dot((jax.nn.sigmoid(z_ref[:, OFF_O:OFF_U]) * h_ref[...]).astype(BF16), wa_ref[...])
    u = jax.nn.gelu(z_ref[:, OFF_U:OFF_VB])
    vn = _rmsnorm(jax.nn.gelu(z_ref[:, OFF_VB:OFF_GA]), gv_ref[...])
    vn_ref[:, 0, :] = vn
    mixed = ws0_ref[...] * vn + bs0_ref[...]
    y_b = _dot((u * mixed).astype(BF16), wb_ref[...])
    merged = (jax.nn.sigmoid(z_ref[:, OFF_GA:OFF_GB]) * y_a
              + jax.nn.sigmoid(z_ref[:, OFF_GB:P_MAIN]) * y_b)
    xo_ref[...] = x_ref[...] + _dot(merged.astype(BF16), wo_ref[...])


def _sample_mixer(x, z, h, layer, gv, ws0, bs0, wa, wb, wo, vn_stack):
    n = x.shape[0]
    out_shape = (jax.ShapeDtypeStruct((n, D_MODEL), F32),
                 jax.ShapeDtypeStruct((wa.shape[0], n, 1, D_B), F32))
    return _single_step(_sample_mixer_kernel, "sample_mixer", out_shape,
                        (x, z, h), (gv, ws0, bs0, wa, wb, wo), layer, stacked=(1, vn_stack))


def _sample_ffn_kernel(x_ref, gf_ref, g2_ref, bufa_ref, bufg_ref, wupa_ref, wupg_ref, cwa_ref, cwg_ref,
                       cba_ref, cbg_ref, wdn_ref, *rest, final):
    xo_ref, nbuf_ref, acc_s = rest[-3:]
    j = pl.program_id(0)
    last = SAMPLE_FFN_STEPS - 1
    width = D_FF // SAMPLE_FFN_STEPS
    x = x_ref[...]
    xn = _rmsnorm(x, g2_ref[...]).astype(BF16)

    def conv(up, buf_ref, cw_ref, cb_ref):
        b0 = buf_ref[:, 0, :]
        b1 = buf_ref[:, 1, :]
        return b1, cb_ref[...] + cw_ref[0:1, :] * b0 + cw_ref[1:2, :] * b1 + cw_ref[2:3, :] * up

    up_a = _dot(xn, wupa_ref[...])
    up_g = _dot(xn, wupg_ref[...])
    prev_a, conv_a = conv(up_a, bufa_ref, cwa_ref, cba_ref)
    prev_g, conv_g = conv(up_g, bufg_ref, cwg_ref, cbg_ref)
    part = _dot((jax.nn.silu(conv_a) * conv_g).astype(BF16), wdn_ref[...])

    for jj in range(SAMPLE_FFN_STEPS):
        @pl.when(j == jj)
        def _():
            for col0, prev, up in ((jj * width, prev_a, up_a), (D_FF + jj * width, prev_g, up_g)):
                nbuf_ref[:, 0, col0:col0 + width] = prev
                nbuf_ref[:, 1, col0:col0 + width] = up

    @pl.when(j == 0)
    def _():
        acc_s[...] = part

    @pl.when((j > 0) & (j < last))
    def _():
        acc_s[...] += part

    @pl.when(j == last)
    def _():
        y = x + (acc_s[...] + part)
        if final:
            xo_ref[:, 0, :] = _rmsnorm(y, gf_ref[...])
        else:
            xo_ref[...] = y


def _sample_ffn(x, gf, layer, buf, g2, wup, cw, cb, wdn, final, conv_stack):
    n = x.shape[0]
    steps = SAMPLE_FFN_STEPS
    width = D_FF // steps

    def cols(shape, gate):
        nd = len(shape)
        return pl.BlockSpec((None,) + shape[:-1] + (width,),
                            lambda j: (layer,) + (0,) * (nd - 1) + (gate * steps + j,))

    in_specs = [_resident(x), _resident(gf), _resident(g2, layer),
                cols(buf.shape[1:], 0), cols(buf.shape[1:], 1),
                cols(wup.shape[1:], 0), cols(wup.shape[1:], 1),
                cols(cw.shape[1:], 0), cols(cw.shape[1:], 1),
                cols(cb.shape[1:], 0), cols(cb.shape[1:], 1),
                pl.BlockSpec((None, width, D_MODEL), lambda j: (layer, j, 0))]
    args = [x, gf, g2, buf, buf, wup, wup, cw, cw, cb, cb, wdn]
    aliases = {}
    if conv_stack is not None:
        in_specs.append(pl.BlockSpec(memory_space=pl.ANY))
        aliases = {len(args): 1}
        args.append(conv_stack)
    x_out = (n, 1, D_MODEL) if final else (n, D_MODEL)
    nd_out = len(x_out)
    return pl.pallas_call(
        functools.partial(_sample_ffn_kernel, final=final),
        grid=(steps,),
        in_specs=in_specs,
        out_specs=(pl.BlockSpec(x_out, lambda j: (0,) * nd_out),
                   pl.BlockSpec((None,) + buf.shape[1:], lambda j: (layer, 0, 0, 0))),
        out_shape=(jax.ShapeDtypeStruct(x_out, F32), jax.ShapeDtypeStruct(buf.shape, F32)),
        scratch_shapes=[pltpu.VMEM((n, D_MODEL), F32)],
        input_output_aliases=aliases,
        compiler_params=pltpu.CompilerParams(
            dimension_semantics=("arbitrary",), vmem_limit_bytes=V7X_VMEM_LIMIT),
        name="sample_ffn",
    )(*args)


def _pack_kernel(wt_ref, *refs):
    n_cast = (len(refs) - 2) // 2
    casts_in, (o_ref, wgt_ref), casts_out = refs[:n_cast], refs[n_cast:n_cast + 2], refs[n_cast + 2:]
    g8 = wt_ref[GATE_LO:GATE_HI, :]
    head = lax.broadcasted_iota(jnp.int32, g8.shape, 0) < N_HEADS
    wgt_ref[...] = jnp.concatenate(
        [jnp.where(head, g8, 0.0), jnp.where(head, pltpu.roll(g8, N_HEADS, 0), 0.0)], axis=0).astype(BF16)
    for j in range(P_MAIN // PACK_TILE):
        dst = j * PACK_TILE
        src = dst if dst < GATE_LO else dst + (GATE_HI - GATE_LO)
        o_ref[:, dst:dst + PACK_TILE] = wt_ref[src:src + PACK_TILE, :].T.astype(BF16)
    for src_ref, dst_ref in zip(casts_in, casts_out):
        dst_ref[...] = src_ref[...].astype(BF16)


def _pack_weights(w_in_t, casts):
    depth, p_in, d = w_in_t.shape
    steps = d // PACK_DIMS

    def slab(w):
        return pl.BlockSpec((None, w.shape[1] // steps, w.shape[2]), lambda l, r: (l, r, 0))

    return pl.pallas_call(
        _pack_kernel,
        grid=(depth, steps),
        in_specs=[pl.BlockSpec((None, p_in, PACK_DIMS), lambda l, r: (l, 0, r))] + [slab(w) for w in casts],
        out_specs=(pl.BlockSpec((None, PACK_DIMS, P_MAIN), lambda l, r: (l, r, 0)),
                   pl.BlockSpec((None, 2 * GATE_ROWS, PACK_DIMS), lambda l, r: (l, 0, r)))
        + tuple(slab(w) for w in casts),
        out_shape=(jax.ShapeDtypeStruct((depth, d, P_MAIN), BF16),
                   jax.ShapeDtypeStruct((depth, 2 * GATE_ROWS, d), BF16))
        + tuple(jax.ShapeDtypeStruct(w.shape, BF16) for w in casts),
        compiler_params=pltpu.CompilerParams(
            dimension_semantics=("arbitrary", "arbitrary"), vmem_limit_bytes=V7X_VMEM_LIMIT),
        name="pack_weights",
    )(w_in_t, *casts)


def kernel(x_prompt, x_sample, state_mlstm_C, state_mlstm_n, state_mlstm_m, state_ffn_conv, w_in, b_igate, b_fgate, g_norm1, g_vnorm, w_spatial, b_spatial, w_branch_a, w_branch_b, w_out, g_norm2, w_up, conv_w, conv_b, w_down, g_final):
    depth = w_in.shape[0]
    n_dec = x_sample.shape[0]
    xp = x_prompt
    xs = x_sample.reshape(n_dec, D_MODEL)
    gf = g_final.reshape(1, D_MODEL)

    wm, wgt, wa, wb, wdn = _pack_weights(jnp.swapaxes(w_in, 1, 2), (w_branch_a, w_branch_b, w_down))
    head_pad = ((0, 0), (0, GATE_ROWS - N_HEADS))
    gbias_col = jnp.concatenate([jnp.pad(b_igate, head_pad), jnp.pad(b_fgate, head_pad)],
                                axis=1).reshape(depth, 2 * GATE_ROWS, 1)
    head_lanes = ((0, 0), (0, V7X_LANES - N_HEADS))
    gbias_row = jnp.concatenate([jnp.pad(b_igate, head_lanes), jnp.pad(b_fgate, head_lanes)],
                                axis=1).reshape(depth, 1, 2 * V7X_LANES)
    g1 = g_norm1.reshape(depth, 1, D_MODEL)
    gv = g_vnorm.reshape(depth, 1, D_B)
    g2 = g_norm2.reshape(depth, 1, D_MODEL)
    bst = jnp.swapaxes(b_spatial, 1, 2)
    ws0 = jnp.repeat(w_spatial[:, :, 0, 0], DG, axis=1).reshape(depth, 1, D_B)
    bs0 = jnp.repeat(b_spatial[:, :, 0], DG, axis=1).reshape(depth, 1, D_B)
    wo = w_out.astype(BF16)
    wup = w_up.astype(BF16)
    cb = conv_b.reshape(depth, 1, 2 * D_FF)
    n_state = state_mlstm_n.reshape(depth, n_dec, N_HEADS * DK)
    m_pad = jnp.pad(state_mlstm_m, ((0, 0), (0, 0), (0, V7X_LANES - N_HEADS)))

    small = [[] for _ in range(4)]
    cp_stack = None
    convp_stack = None
    c_stack = None
    vn_stack = None
    conv_stack = None
    for l in range(depth):
        final = l == depth - 1

        xp, cp_stack, n_p, m_p = _prompt_mixer(xp, l, wm, wgt, gbias_col, g1, gv, w_spatial, bst, wa, wb, wo,
                                               cp_stack)
        xp, convp_stack = _prompt_ffn(xp, l, g2, wup, conv_w, cb, wdn, gf, final, convp_stack)

        z, gates = _sample_proj(xs, l, g1, wm, wgt)
        h, c_stack, n_s, m_s = _sample_state(
            z, gates, gbias_row[l], m_pad[l], n_state[l], state_mlstm_C, l, c_stack)
        xs, vn_stack = _sample_mixer(xs, z, h, l, gv, ws0, bs0, wa, wb, wo, vn_stack)
        xs, conv_stack = _sample_ffn(xs, gf, l, state_ffn_conv, g2, wup, conv_w, cb, wdn, final, conv_stack)

        for lst, val in zip(small, (n_p, m_p[:, :N_HEADS, 0],
                                    n_s.reshape(n_dec, N_HEADS, DK), m_s[:, :N_HEADS])):
            lst.append(val)
    st = [jnp.stack(o) for o in small]
    return (xp, xs, cp_stack, st[0], st[1], convp_stack, c_stack, st[2], st[3], conv_stack, vn_stack)
```

```python
import functools

import jax
import jax.numpy as jnp
from jax import lax
from jax.experimental import pallas as pl
from jax.experimental.pallas import tpu as pltpu

D_MODEL = 1024
N_HEADS = 4
DK = 128
DV = 256
CHUNK = 128
D_B = 1024
N_GROUPS = 4
DG = D_B // N_GROUPS
D_FF = 2816
CONV_W = 3
EPS = 1e-6
K_SCALE = DK ** -0.5

OFF_Q = 0
OFF_K = OFF_Q + N_HEADS * DK
OFF_V = OFF_K + N_HEADS * DK
OFF_O = OFF_V + N_HEADS * DV
OFF_U = OFF_O + N_HEADS * DV
OFF_VB = OFF_U + D_B
OFF_GA = OFF_VB + D_B
OFF_GB = OFF_GA + D_MODEL
P_MAIN = OFF_GB + D_MODEL
GATE_LO = 2 * N_HEADS * DK + 2 * N_HEADS * DV
GATE_HI = GATE_LO + 2 * N_HEADS

V7X_LANES = 128
V7X_SUBLANES = 8
GATE_ROWS = V7X_SUBLANES
V7X_VMEM_LIMIT = 56 * 1024 * 1024
FFN_COL_TILE = 256
PROJ_TILE = 256
MIXER_BLOCK = 512
FFN_BLOCK = 512
SAMPLE_BLOCK = 16
PACK_TILE = 256
PACK_DIMS = 256
SAMPLE_PROJ_TILE = P_MAIN // 2
SAMPLE_FFN_STEPS = 2

F32 = jnp.float32
BF16 = jnp.bfloat16


def _dot(a, b):
    return jnp.dot(a, b, preferred_element_type=F32)


def _dot_nt(a, b):
    return lax.dot_general(a, b, (((1,), (1,)), ((), ())), preferred_element_type=F32)


def _rmsnorm(x, g):
    r = lax.rsqrt(jnp.mean(x * x, axis=-1, keepdims=True) + EPS)
    return x * r * g


def _log_sigmoid(x):
    return jnp.minimum(x, 0.0) - jnp.log1p(jnp.exp(-jnp.abs(x)))


def _scan_lanes(x, op, fill):
    lane = lax.broadcasted_iota(jnp.int32, x.shape, 1)
    k = 1
    while k < x.shape[1]:
        shifted = pltpu.roll(x, k, 1)
        x = op(x, jnp.where(lane >= k, shifted, fill))
        k *= 2
    return x


def _anchor_zero(x):
    sub, lanes = V7X_SUBLANES, V7X_LANES
    acc = jnp.zeros((sub, lanes), jnp.uint32)
    for r in range(x.shape[0] // sub):
        for c in range(x.shape[1] // lanes):
            piece = pltpu.bitcast(x[r * sub:(r + 1) * sub, c * lanes:(c + 1) * lanes], jnp.uint32)
            acc = acc | ((piece >> 16) >> 16)
    return pltpu.bitcast(acc, F32)


def _next_block_map(nb, steps):
    def index_map(b, s):
        nxt = jnp.minimum(b * steps + s + 1, nb * steps - 1)
        return (nxt // steps, nxt % steps, 0)
    return index_map


def _mixer_kernel(x_ref, wm_ref, wgt_ref, gbias_ref, g1_ref, gv_ref, ws_ref, bst_ref,
                  wa_ref, wb_ref, wo_ref, *rest, block):
    (xo_ref, c_ref, n_ref, m_ref,
     ct_s, n_s, m_s, q_s, k_s, v_s, h_s, so_s, u_s, vb_s, sg_s, um_s) = rest[-16:]
    s = pl.program_id(1)
    n_chunks = block // CHUNK

    @pl.when(s == 0)
    def _():
        ct_s[...] = jnp.zeros_like(ct_s)
        n_s[...] = jnp.zeros_like(n_s)
        m_s[...] = jnp.zeros_like(m_s)

    xn = _rmsnorm(x_ref[...], g1_ref[...]).astype(BF16)

    gates = _dot_nt(wgt_ref[...], xn) + gbias_ref[...]

    def proj(off, t):
        return _dot(xn, wm_ref[:, off + t * PROJ_TILE:off + (t + 1) * PROJ_TILE])

    def tile(t):
        return slice(t * PROJ_TILE, (t + 1) * PROJ_TILE)

    for t in range(N_HEADS * DK // PROJ_TILE):
        q_s[:, tile(t)] = proj(OFF_Q, t)
        k_s[:, tile(t)] = proj(OFF_K, t) * K_SCALE
    for t in range(N_HEADS * DV // PROJ_TILE):
        v_s[:, tile(t)] = proj(OFF_V, t)

    sumsq = [jnp.zeros((block, 1), F32)]

    def vb_tile(t):
        g = jax.nn.gelu(proj(OFF_VB, t))
        vb_s[:, tile(t)] = g
        sumsq[0] = sumsq[0] + jnp.sum(g * g, axis=-1, keepdims=True)

    def u_tile(t):
        u_s[:, tile(t)] = jax.nn.gelu(proj(OFF_U, t))

    def o_tile(t):
        so_s[:, tile(t)] = jax.nn.sigmoid(proj(OFF_O, t))

    def ga_tile(t):
        sg_s[:, tile(t)] = jax.nn.sigmoid(proj(OFF_GA, t))

    def gb_tile(t):
        sg_s[:, D_MODEL + t * PROJ_TILE:D_MODEL + (t + 1) * PROJ_TILE] = jax.nn.sigmoid(proj(OFF_GB, t))

    jobs = [(f, t) for f in (vb_tile, u_tile, o_tile, ga_tile, gb_tile) for t in range(D_MODEL // PROJ_TILE)]

    def run_jobs(count):
        for _ in range(min(count, len(jobs))):
            f, t = jobs.pop(0)
            f(t)

    row_i = lax.broadcasted_iota(jnp.int32, (CHUNK, CHUNK), 0)
    col_i = lax.broadcasted_iota(jnp.int32, (CHUNK, CHUNK), 1)
    causal = row_i >= col_i
    heads = range(N_HEADS)

    for c in range(n_chunks):
        r0 = c * CHUNK
        ig = gates[0:GATE_ROWS, r0:r0 + CHUNK]
        lf = _log_sigmoid(gates[GATE_ROWS:2 * GATE_ROWS, r0:r0 + CHUNK])
        b = _scan_lanes(lf, jnp.add, 0.0)
        a = ig - b
        m_prev = m_s[...]
        gmax = jnp.maximum(m_prev, _scan_lanes(a, jnp.maximum, -jnp.inf))
        m_t = b + gmax
        w_inter = jnp.exp(m_prev - gmax)
        g_last = gmax[:, CHUNK - 1:CHUNK]
        w_last = jnp.exp(a - g_last)
        floor = jnp.exp(-m_t)
        decay = w_inter[:, CHUNK - 1:CHUNK]
        m_s[...] = jnp.broadcast_to(m_t[:, CHUNK - 1:CHUNK], m_s.shape)

        rows = jnp.concatenate(
            [gmax, w_inter, w_last, floor,
             jnp.zeros((CHUNK - 4 * GATE_ROWS, CHUNK), F32)], axis=0)
        cols = rows.T

        def col(kind, h):
            return cols[:, kind * GATE_ROWS + h:kind * GATE_ROWS + h + 1]

        qf = [q_s[r0:r0 + CHUNK, h * DK:(h + 1) * DK] for h in heads]
        kf = [k_s[r0:r0 + CHUNK, h * DK:(h + 1) * DK] for h in heads]
        vf = [v_s[r0:r0 + CHUNK, h * DV:(h + 1) * DV] for h in heads]
        qb = [x.astype(BF16) for x in qf]
        kt = [x.T.astype(BF16) for x in kf]
        run_jobs(2)
        zero = jnp.zeros((DK, CHUNK), BF16)
        sc = []
        for h in range(0, N_HEADS, 2):
            kk = jnp.concatenate([jnp.concatenate([kt[h], zero], axis=1),
                                  jnp.concatenate([zero, kt[h + 1]], axis=1)], axis=0)
            pair = _dot(jnp.concatenate([qb[h], qb[h + 1]], axis=1), kk)
            sc += [pair[:, :CHUNK], pair[:, CHUNK:]]
        dmat = [jnp.where(causal, jnp.exp(a[h:h + 1, :] - col(0, h)), 0.0) for h in heads]
        run_jobs(1)
        sd = [sc[h] * dmat[h] for h in heads]
        ct = [ct_s[h] for h in heads]
        num = [_dot(jnp.concatenate([sd[h], col(1, h) * qf[h]], axis=1).astype(BF16),
                    jnp.concatenate([vf[h], ct[h]], axis=0).astype(BF16)) for h in heads]
        run_jobs(2)
        for h in heads:
            nh = n_s[h:h + 1, :]
            den = (jnp.sum(sd[h], axis=-1, keepdims=True)
                   + col(1, h) * jnp.sum(qf[h] * nh, axis=-1, keepdims=True))
            h_s[r0:r0 + CHUNK, h * DV:(h + 1) * DV] = num[h] * (1.0 / jnp.maximum(jnp.abs(den), col(3, h)))
            dec = decay[h:h + 1, :]
            n_s[h:h + 1, :] = dec * nh + jnp.sum(col(2, h) * kf[h], axis=0, keepdims=True)
        run_jobs(1)
        for h in heads:
            ct_s[h] = decay[h:h + 1, :] * ct[h] + _dot(kt[h], (col(2, h) * vf[h]).astype(BF16))
    run_jobs(len(jobs))

    y_a = _dot((so_s[...] * h_s[...]).astype(BF16), wa_ref[...])

    rinv = lax.rsqrt(sumsq[0] * (1.0 / D_B) + EPS)
    for g in range(N_GROUPS):
        gcols = slice(g * DG, (g + 1) * DG)
        w_tri = jnp.where(causal, ws_ref[g], 0.0).astype(BF16)
        bias_c = bst_ref[:, g:g + 1]
        for c in range(n_chunks):
            rws = slice(c * CHUNK, (c + 1) * CHUNK)
            vn = vb_s[rws, gcols] * rinv[rws] * gv_ref[:, gcols]
            um_s[rws, gcols] = u_s[rws, gcols] * (_dot(w_tri, vn.astype(BF16)) + bias_c)
    y_b = _dot(um_s[...].astype(BF16), wb_ref[...])

    merged = sg_s[:, 0:D_MODEL] * y_a + sg_s[:, D_MODEL:2 * D_MODEL] * y_b
    xo_ref[...] = x_ref[...] + _dot(merged.astype(BF16), wo_ref[...])

    @pl.when(s == pl.num_programs(1) - 1)
    def _():
        for h in range(N_HEADS):
            c_ref[h] = ct_s[h].T
        n_ref[...] = n_s[...]
        m_ref[...] = m_s[...]


def _resident(arr, layer=None):
    if layer is None:
        nd = arr.ndim
        return pl.BlockSpec(arr.shape, lambda *_: (0,) * nd, pipeline_mode=pl.Buffered(1))
    nd = arr.ndim - 1
    return pl.BlockSpec((None,) + arr.shape[1:], lambda *_: (layer,) + (0,) * nd,
                        pipeline_mode=pl.Buffered(1))


def _prompt_mixer(x, layer, wm, wgt, gbias, g1, gv, ws, bst, wa, wb, wo, c_stack):
    nb, seq, _ = x.shape
    depth = wm.shape[0]
    block = MIXER_BLOCK
    grid = (nb, seq // block)
    xspec = pl.BlockSpec((None, block, D_MODEL), lambda b, s: (b, s, 0))
    out_shape = (
        jax.ShapeDtypeStruct((nb, seq, D_MODEL), F32),
        jax.ShapeDtypeStruct((depth, nb, N_HEADS, DV, DK), F32),
        jax.ShapeDtypeStruct((nb, N_HEADS, DK), F32),
        jax.ShapeDtypeStruct((nb, GATE_ROWS, V7X_LANES), F32),
    )
    out_specs = (
        xspec,
        pl.BlockSpec((None, None, N_HEADS, DV, DK), lambda b, s: (layer, b, 0, 0, 0)),
        pl.BlockSpec((None, N_HEADS, DK), lambda b, s: (b, 0, 0)),
        pl.BlockSpec((None, GATE_ROWS, V7X_LANES), lambda b, s: (b, 0, 0)),
    )
    in_specs = [xspec] + [_resident(a, layer) for a in (wm, wgt, gbias, g1, gv, ws, bst, wa, wb, wo)]
    args = [x, wm, wgt, gbias, g1, gv, ws, bst, wa, wb, wo]
    aliases = {}
    if c_stack is not None:
        in_specs.append(pl.BlockSpec(memory_space=pl.ANY))
        aliases = {len(args): 1}
        args.append(c_stack)
    scratch = [
        pltpu.VMEM((N_HEADS, DK, DV), F32),
        pltpu.VMEM((N_HEADS, DK), F32),
        pltpu.VMEM((GATE_ROWS, V7X_LANES), F32),
        pltpu.VMEM((block, N_HEADS * DK), F32),
        pltpu.VMEM((block, N_HEADS * DK), F32),
        pltpu.VMEM((block, N_HEADS * DV), F32),
        pltpu.VMEM((block, N_HEADS * DV), F32),
        pltpu.VMEM((block, N_HEADS * DV), F32),
        pltpu.VMEM((block, D_B), F32),
        pltpu.VMEM((block, D_B), F32),
        pltpu.VMEM((block, 2 * D_MODEL), F32),
        pltpu.VMEM((block, D_B), F32),
    ]
    return pl.pallas_call(
        functools.partial(_mixer_kernel, block=block),
        grid=grid, in_specs=in_specs, out_specs=out_specs, out_shape=out_shape,
        scratch_shapes=scratch, input_output_aliases=aliases,
        compiler_params=pltpu.CompilerParams(
            dimension_semantics=("arbitrary", "arbitrary"), vmem_limit_bytes=V7X_VMEM_LIMIT),
        name="prompt_mixer",
    )(*args)


def _conv_taps(up, carry_s, cw_ref, cb_ref, cols):
    sub = V7X_SUBLANES
    rows = up.shape[0]
    last1 = up[rows - sub:rows]
    last2 = up[rows - 2 * sub:rows - sub]
    first = lax.broadcasted_iota(jnp.int32, last1.shape, 0) == 0
    back1 = jnp.where(first, carry_s[sub - 1:sub, cols], pltpu.roll(last1, 1, 0))
    back2 = jnp.where(first, carry_s[sub - 2:sub - 1, cols], pltpu.roll(last2, 1, 0))
    carry_s[sub - 2:sub - 1, cols] = last2[sub - 1:sub]
    carry_s[sub - 1:sub, cols] = last1[sub - 1:sub]
    m1 = jnp.concatenate([back1, up[0:rows - sub]], axis=0)
    m2 = jnp.concatenate([back2, back1, up[0:rows - 2 * sub]], axis=0)
    return (cb_ref[:, cols] + cw_ref[0:1, cols] * m2 + cw_ref[1:2, cols] * m1
            + cw_ref[2:3, cols] * up)


def _perm_pitch(block):
    return block // V7X_SUBLANES + V7X_SUBLANES


def _ffn_kernel(x_ref, xnext_ref, g2_ref, wup_ref, cw_ref, cb_ref, wdn_ref, gf_ref, *rest, block, final):
    xo_ref, conv_ref, carry_s, act_s, perm_s, unperm_s, xn_s = rest[-7:]
    s = pl.program_id(1)
    sub, lanes = V7X_SUBLANES, V7X_LANES
    groups = block // sub
    chunks = D_MODEL // lanes
    pitch = _perm_pitch(block)

    @pl.when(s == 0)
    def _():
        carry_s[...] = jnp.zeros_like(carry_s)

    def stage(src_ref):
        for c in range(chunks):
            for i in range(sub):
                perm_s[c, i * pitch:i * pitch + groups] = src_ref[i * groups:(i + 1) * groups,
                                                                  c * lanes:(c + 1) * lanes]
        x = jnp.concatenate(
            [jnp.concatenate([perm_s[c, pl.ds(r, sub, stride=pitch), :] for c in range(chunks)], axis=1)
             for r in range(groups)], axis=0)
        xn = _rmsnorm(x, g2_ref[...])
        xn_s[...] = xn.astype(BF16)
        return xn

    @pl.when((pl.program_id(0) == 0) & (s == 0))
    def _():
        stage(x_ref)

    xn = xn_s[...]
    for j in range(D_FF // FFN_COL_TILE):
        halves = []
        for half in range(2):
            cols = slice(half * D_FF + j * FFN_COL_TILE, half * D_FF + (j + 1) * FFN_COL_TILE)
            up = _dot(xn, wup_ref[:, cols])
            halves.append(_conv_taps(up, carry_s, cw_ref, cb_ref, cols))
        act_s[:, j * FFN_COL_TILE:(j + 1) * FFN_COL_TILE] = (jax.nn.silu(halves[0]) * halves[1]).astype(BF16)
    staged = stage(xnext_ref)
    down = _dot(act_s[...], wdn_ref[...])
    anchor = _anchor_zero(staged)
    for r in range(groups):
        for c in range(chunks):
            tile_rc = down[r * sub:(r + 1) * sub, c * lanes:(c + 1) * lanes]
            if r == 0 and c == 0:
                tile_rc = tile_rc + anchor
            unperm_s[c, pl.ds(r, sub, stride=pitch), :] = tile_rc
    for i in range(sub):
        rows = slice(i * groups, (i + 1) * groups)
        y = x_ref[rows, :] + jnp.concatenate(
            [unperm_s[c, i * pitch:i * pitch + groups] for c in range(chunks)], axis=1)
        if final:
            y = _rmsnorm(y, gf_ref[...])
        xo_ref[rows, :] = y

    @pl.when(s == pl.num_programs(1) - 1)
    def _():
        conv_ref[...] = carry_s[V7X_SUBLANES - (CONV_W - 1):V7X_SUBLANES, :]


def _prompt_ffn(x, layer, g2, wup, cw, cb, wdn, gf, final, conv_stack):
    nb, seq, _ = x.shape
    depth = wup.shape[0]
    block = FFN_BLOCK
    steps = seq // block
    grid = (nb, steps)
    xspec = pl.BlockSpec((None, block, D_MODEL), lambda b, s: (b, s, 0))
    out_shape = (
        jax.ShapeDtypeStruct((nb, seq, D_MODEL), F32),
        jax.ShapeDtypeStruct((depth, nb, CONV_W - 1, 2 * D_FF), F32),
    )
    out_specs = (xspec, pl.BlockSpec((None, None, CONV_W - 1, 2 * D_FF), lambda b, s: (layer, b, 0, 0)))
    in_specs = ([xspec, pl.BlockSpec((None, block, D_MODEL), _next_block_map(nb, steps))]
                + [_resident(a, layer) for a in (g2, wup, cw, cb, wdn)] + [_resident(gf)])
    args = [x, x, g2, wup, cw, cb, wdn, gf]
    aliases = {}
    if conv_stack is not None:
        in_specs.append(pl.BlockSpec(memory_space=pl.ANY))
        aliases = {len(args): 1}
        args.append(conv_stack)
    reorder = pltpu.VMEM((D_MODEL // V7X_LANES, V7X_SUBLANES * _perm_pitch(block), V7X_LANES), F32)
    return pl.pallas_call(
        functools.partial(_ffn_kernel, block=block, final=final),
        grid=grid, in_specs=in_specs, out_specs=out_specs, out_shape=out_shape,
        scratch_shapes=[pltpu.VMEM((V7X_SUBLANES, 2 * D_FF), F32),
                        pltpu.VMEM((block, D_FF), BF16),
                        reorder, reorder,
                        pltpu.VMEM((block, D_MODEL), BF16)],
        input_output_aliases=aliases,
        compiler_params=pltpu.CompilerParams(
            dimension_semantics=("arbitrary", "arbitrary"), vmem_limit_bytes=V7X_VMEM_LIMIT),
        name="prompt_ffn",
    )(*args)


def _sample_proj_kernel(x_ref, g1_ref, wgt_ref, wm_ref, z_ref, gates_ref):
    xn = _rmsnorm(x_ref[...], g1_ref[...]).astype(BF16)
    z_ref[...] = _dot(xn, wm_ref[...])

    @pl.when(pl.program_id(0) == 0)
    def _():
        gt = _dot_nt(wgt_ref[...], xn)
        gt = jnp.concatenate([gt, jnp.zeros((V7X_LANES - gt.shape[0], gt.shape[1]), F32)], axis=0)
        g = gt.T
        gates_ref[...] = jnp.concatenate([g, pltpu.roll(g, V7X_LANES - GATE_ROWS, 1)], axis=1)


def _single_step(kernel_fn, name, out_shape, whole, layered, layer, stacked=None):
    in_specs = [_resident(a) for a in whole] + [_resident(a, layer) for a in layered]
    args = list(whole) + list(layered)
    out_specs = [pl.BlockSpec(o.shape, lambda i, nd=len(o.shape): (0,) * nd) for o in out_shape]
    aliases = {}
    if stacked is not None:
        k, prev = stacked
        nd = len(out_shape[k].shape) - 1
        out_specs[k] = pl.BlockSpec((None,) + out_shape[k].shape[1:], lambda i: (layer,) + (0,) * nd)
        if prev is not None:
            in_specs.append(pl.BlockSpec(memory_space=pl.ANY))
            aliases = {len(args): k}
            args.append(prev)
    return pl.pallas_call(
        kernel_fn,
        grid=(1,),
        in_specs=in_specs, out_specs=tuple(out_specs), out_shape=out_shape,
        input_output_aliases=aliases,
        compiler_params=pltpu.CompilerParams(
            dimension_semantics=("arbitrary",), vmem_limit_bytes=V7X_VMEM_LIMIT),
        name=name,
    )(*args)


def _sample_proj(x, layer, g1, wm, wgt):
    n = x.shape[0]
    assert n == V7X_LANES, "the gate transpose assumes one lane tile of sample rows"
    tile = SAMPLE_PROJ_TILE
    return pl.pallas_call(
        _sample_proj_kernel,
        grid=(P_MAIN // tile,),
        in_specs=[_resident(x), _resident(g1, layer), _resident(wgt, layer),
                  pl.BlockSpec((None, D_MODEL, tile), lambda j: (layer, 0, j))],
        out_specs=(pl.BlockSpec((n, tile), lambda j: (0, j)),
                   pl.BlockSpec((n, 2 * V7X_LANES), lambda j: (0, 0))),
        out_shape=(jax.ShapeDtypeStruct((n, P_MAIN), F32), jax.ShapeDtypeStruct((n, 2 * V7X_LANES), F32)),
        compiler_params=pltpu.CompilerParams(
            dimension_semantics=("arbitrary",), vmem_limit_bytes=V7X_VMEM_LIMIT),
        name="sample_proj",
    )(x, g1, wgt, wm)


def _sample_state_kernel(q_ref, k_ref, v_ref, gates_ref, gbias_ref, m_ref, n_ref, c_ref, *rest):
    h_ref, co_ref, no_ref, mo_ref = rest[-4:]
    tb = SAMPLE_BLOCK
    ig = gates_ref[:, 0:V7X_LANES] + gbias_ref[:, 0:V7X_LANES]
    lf = _log_sigmoid(gates_ref[:, V7X_LANES:] + gbias_ref[:, V7X_LANES:])
    inter = lf + m_ref[...]
    m_t = jnp.maximum(inter, ig)
    d_in = jnp.exp(ig - m_t)
    w_inter = jnp.exp(inter - m_t)
    floor = jnp.exp(-m_t)
    mo_ref[...] = m_t

    row8 = lax.broadcasted_iota(jnp.int32, (CHUNK, DK), 0)
    for h in range(N_HEADS):
        q8 = q_ref[:, h * DK:(h + 1) * DK]
        k8 = k_ref[:, h * DK:(h + 1) * DK] * K_SCALE
        v8 = v_ref[:, h * DV:(h + 1) * DV]
        d_h = d_in[:, h:h + 1]
        w_h = w_inter[:, h:h + 1]
        n8 = n_ref[:, h * DK:(h + 1) * DK]
        s = jnp.sum(q8 * k8, axis=-1, keepdims=True) * d_h
        den = s + w_h * jnp.sum(q8 * n8, axis=-1, keepdims=True)
        qb = q8.astype(BF16)
        inter_rows = [_dot_nt(qb, c_ref[j, h].astype(BF16))[j:j + 1] for j in range(tb)]
        num = s * v8 + w_h * jnp.concatenate(inter_rows, axis=0)
        h_ref[:, h * DV:(h + 1) * DV] = num / jnp.maximum(jnp.abs(den), floor[:, h:h + 1])
        no_ref[:, h * DK:(h + 1) * DK] = w_h * n8 + d_h * k8

        vt = jnp.concatenate([d_h * v8, jnp.zeros((CHUNK - tb, DV), F32)], axis=0).T.astype(BF16)
        kpad = jnp.concatenate([k8, jnp.zeros((CHUNK - tb, DK), F32)], axis=0)
        for j in range(tb):
            kj = jnp.where(row8 == j, kpad, 0.0).astype(BF16)
            co_ref[j, h] = w_inter[j:j + 1, h:h + 1] * c_ref[j, h] + _dot(vt, kj)


def _sample_state(z, gates, gbias, m_pad, n_state, c_state, layer, c_stack):
    n = z.shape[0]
    tb = SAMPLE_BLOCK
    c_block = pl.BlockSpec((None, tb, N_HEADS, DV, DK), lambda i: (layer, i, 0, 0, 0))
    in_specs = [
        pl.BlockSpec((tb, N_HEADS * DK), lambda i: (i, OFF_Q // (N_HEADS * DK))),
        pl.BlockSpec((tb, N_HEADS * DK), lambda i: (i, OFF_K // (N_HEADS * DK))),
        pl.BlockSpec((tb, N_HEADS * DV), lambda i: (i, OFF_V // (N_HEADS * DV))),
        pl.BlockSpec((tb, 2 * V7X_LANES), lambda i: (i, 0)),
        pl.BlockSpec((1, 2 * V7X_LANES), lambda i: (0, 0)),
        pl.BlockSpec((tb, V7X_LANES), lambda i: (i, 0)),
        pl.BlockSpec((tb, N_HEADS * DK), lambda i: (i, 0)),
        c_block,
    ]
    args = [z, z, z, gates, gbias, m_pad, n_state, c_state]
    aliases = {}
    if c_stack is not None:
        in_specs.append(pl.BlockSpec(memory_space=pl.ANY))
        aliases = {len(args): 1}
        args.append(c_stack)
    out_shape = (
        jax.ShapeDtypeStruct((n, N_HEADS * DV), F32),
        jax.ShapeDtypeStruct(c_state.shape, F32),
        jax.ShapeDtypeStruct((n, N_HEADS * DK), F32),
        jax.ShapeDtypeStruct((n, V7X_LANES), F32),
    )
    out_specs = (
        pl.BlockSpec((tb, N_HEADS * DV), lambda i: (i, 0)),
        c_block,
        pl.BlockSpec((tb, N_HEADS * DK), lambda i: (i, 0)),
        pl.BlockSpec((tb, V7X_LANES), lambda i: (i, 0)),
    )
    return pl.pallas_call(
        _sample_state_kernel,
        grid=(n // tb,), in_specs=in_specs, out_specs=out_specs, out_shape=out_shape,
        input_output_aliases=aliases,
        compiler_params=pltpu.CompilerParams(
            dimension_semantics=("arbitrary",), vmem_limit_bytes=V7X_VMEM_LIMIT),
        name="sample_state",
    )(*args)


def _sample_mixer_kernel(x_ref, z_ref, h_ref, gv_ref, ws0_ref, bs0_ref, wa_ref, wb_ref, wo_ref, *rest):
    xo_ref, vn_ref = rest[-2:]
    y_a = _dot((jax.nn.sigmoid(z_ref[:, OFF_O:OFF_U]) * h_ref[...]).astype(BF16), wa_ref[...])
    u = jax.nn.gelu(z_ref[:, OFF_U:OFF_VB])
    vn = _rmsnorm(jax.nn.gelu(z_ref[:, OFF_VB:OFF_GA]), gv_ref[...])
    vn_ref[:, 0, :] = vn
    mixed = ws0_ref[...] * vn + bs0_ref[...]
    y_b = _dot((u * mixed).astype(BF16), wb_ref[...])
    merged = (jax.nn.sigmoid(z_ref[:, OFF_GA:OFF_GB]) * y_a
              + jax.nn.sigmoid(z_ref[:, OFF_GB:P_MAIN]) * y_b)
    xo_ref[...] = x_ref[...] + _dot(merged.astype(BF16), wo_ref[...])


def _sample_mixer(x, z, h, layer, gv, ws0, bs0, wa, wb, wo, vn_stack):
    n = x.shape[0]
    out_shape = (jax.ShapeDtypeStruct((n, D_MODEL), F32),
                 jax.ShapeDtypeStruct((wa.shape[0], n, 1, D_B), F32))
    return _single_step(_sample_mixer_kernel, "sample_mixer", out_shape,
                        (x, z, h), (gv, ws0, bs0, wa, wb, wo), layer, stacked=(1, vn_stack))


def _sample_ffn_kernel(x_ref, gf_ref, g2_ref, bufa_ref, bufg_ref, wupa_ref, wupg_ref, cwa_ref, cwg_ref,
                       cba_ref, cbg_ref, wdn_ref, *rest, final):
    xo_ref, nbuf_ref, acc_s = rest[-3:]
    j = pl.program_id(0)
    last = SAMPLE_FFN_STEPS - 1
    width = D_FF // SAMPLE_FFN_STEPS
    x = x_ref[...]
    xn = _rmsnorm(x, g2_ref[...]).astype(BF16)

    def conv(up, buf_ref, cw_ref, cb_ref):
        b0 = buf_ref[:, 0, :]
        b1 = buf_ref[:, 1, :]
        return b1, cb_ref[...] + cw_ref[0:1, :] * b0 + cw_ref[1:2, :] * b1 + cw_ref[2:3, :] * up

    up_a = _dot(xn, wupa_ref[...])
    up_g = _dot(xn, wupg_ref[...])
    prev_a, conv_a = conv(up_a, bufa_ref, cwa_ref, cba_ref)
    prev_g, conv_g = conv(up_g, bufg_ref, cwg_ref, cbg_ref)
    part = _dot((jax.nn.silu(conv_a) * conv_g).astype(BF16), wdn_ref[...])

    for jj in range(SAMPLE_FFN_STEPS):
        @pl.when(j == jj)
        def _():
            for col0, prev, up in ((jj * width, prev_a, up_a), (D_FF + jj * width, prev_g, up_g)):
                nbuf_ref[:, 0, col0:col0 + width] = prev
                nbuf_ref[:, 1, col0:col0 + width] = up

    @pl.when(j == 0)
    def _():
        acc_s[...] = part

    @pl.when((j > 0) & (j < last))
    def _():
        acc_s[...] += part

    @pl.when(j == last)
    def _():
        y = x + (acc_s[...] + part)
        if final:
            xo_ref[:, 0, :] = _rmsnorm(y, gf_ref[...])
        else:
            xo_ref[...] = y


def _sample_ffn(x, gf, layer, buf, g2, wup, cw, cb, wdn, final, conv_stack):
    n = x.shape[0]
    steps = SAMPLE_FFN_STEPS
    width = D_FF // steps

    def cols(shape, gate):
        nd = len(shape)
        return pl.BlockSpec((None,) + shape[:-1] + (width,),
                            lambda j: (layer,) + (0,) * (nd - 1) + (gate * steps + j,))

    in_specs = [_resident(x), _resident(gf), _resident(g2, layer),
                cols(buf.shape[1:], 0), cols(buf.shape[1:], 1),
                cols(wup.shape[1:], 0), cols(wup.shape[1:], 1),
                cols(cw.shape[1:], 0), cols(cw.shape[1:], 1),
                cols(cb.shape[1:], 0), cols(cb.shape[1:], 1),
                pl.BlockSpec((None, width, D_MODEL), lambda j: (layer, j, 0))]
    args = [x, gf, g2, buf, buf, wup, wup, cw, cw, cb, cb, wdn]
    aliases = {}
    if conv_stack is not None:
        in_specs.append(pl.BlockSpec(memory_space=pl.ANY))
        aliases = {len(args): 1}
        args.append(conv_stack)
    x_out = (n, 1, D_MODEL) if final else (n, D_MODEL)
    nd_out = len(x_out)
    return pl.pallas_call(
        functools.partial(_sample_ffn_kernel, final=final),
        grid=(steps,),
        in_specs=in_specs,
        out_specs=(pl.BlockSpec(x_out, lambda j: (0,) * nd_out),
                   pl.BlockSpec((None,) + buf.shape[1:], lambda j: (layer, 0, 0, 0))),
        out_shape=(jax.ShapeDtypeStruct(x_out, F32), jax.ShapeDtypeStruct(buf.shape, F32)),
        scratch_shapes=[pltpu.VMEM((n, D_MODEL), F32)],
        input_output_aliases=aliases,
        compiler_params=pltpu.CompilerParams(
            dimension_semantics=("arbitrary",), vmem_limit_bytes=V7X_VMEM_LIMIT),
        name="sample_ffn",
    )(*args)


def _pack_kernel(wt_ref, *refs):
    n_cast = (len(refs) - 2) // 2
    casts_in, (o_ref, wgt_ref), casts_out = refs[:n_cast], refs[n_cast:n_cast + 2], refs[n_cast + 2:]
    g8 = wt_ref[GATE_LO:GATE_HI, :]
    head = lax.broadcasted_iota(jnp.int32, g8.shape, 0) < N_HEADS
    wgt_ref[...] = jnp.concatenate(
        [jnp.where(head, g8, 0.0), jnp.where(head, pltpu.roll(g8, N_HEADS, 0), 0.0)], axis=0).astype(BF16)
    for j in range(P_MAIN // PACK_TILE):
        dst = j * PACK_TILE
        src = dst if dst < GATE_LO else dst + (GATE_HI - GATE_LO)
        o_ref[:, dst:dst + PACK_TILE] = wt_ref[src:src + PACK_TILE, :].T.astype(BF16)
    for src_ref, dst_ref in zip(casts_in, casts_out):
        dst_ref[...] = src_ref[...].astype(BF16)


def _pack_weights(w_in_t, casts):
    depth, p_in, d = w_in_t.shape
    steps = d // PACK_DIMS

    def slab(w):
        return pl.BlockSpec((None, w.shape[1] // steps, w.shape[2]), lambda l, r: (l, r, 0))

    return pl.pallas_call(
        _pack_kernel,
        grid=(depth, steps),
        in_specs=[pl.BlockSpec((None, p_in, PACK_DIMS), lambda l, r: (l, 0, r))] + [slab(w) for w in casts],
        out_specs=(pl.BlockSpec((None, PACK_DIMS, P_MAIN), lambda l, r: (l, r, 0)),
                   pl.BlockSpec((None, 2 * GATE_ROWS, PACK_DIMS), lambda l, r: (l, 0, r)))
        + tuple(slab(w) for w in casts),
        out_shape=(jax.ShapeDtypeStruct((depth, d, P_MAIN), BF16),
                   jax.ShapeDtypeStruct((depth, 2 * GATE_ROWS, d), BF16))
        + tuple(jax.ShapeDtypeStruct(w.shape, BF16) for w in casts),
        compiler_params=pltpu.CompilerParams(
            dimension_semantics=("arbitrary", "arbitrary"), vmem_limit_bytes=V7X_VMEM_LIMIT),
        name="pack_weights",
    )(w_in_t, *casts)


def kernel(x_prompt, x_sample, state_mlstm_C, state_mlstm_n, state_mlstm_m, state_ffn_conv, w_in, b_igate, b_fgate, g_norm1, g_vnorm, w_spatial, b_spatial, w_branch_a, w_branch_b, w_out, g_norm2, w_up, conv_w, conv_b, w_down, g_final):
    depth = w_in.shape[0]
    n_dec = x_sample.shape[0]
    xp = x_prompt
    xs = x_sample.reshape(n_dec, D_MODEL)
    gf = g_final.reshape(1, D_MODEL)

    wm, wgt, wa, wb, wdn, wup = _pack_weights(jnp.swapaxes(w_in, 1, 2),
                                              (w_branch_a, w_branch_b, w_down, w_up))
    head_pad = ((0, 0), (0, GATE_ROWS - N_HEADS))
    gbias_col = jnp.concatenate([jnp.pad(b_igate, head_pad), jnp.pad(b_fgate, head_pad)],
                                axis=1).reshape(depth, 2 * GATE_ROWS, 1)
    head_lanes = ((0, 0), (0, V7X_LANES - N_HEADS))
    gbias_row = jnp.concatenate([jnp.pad(b_igate, head_lanes), jnp.pad(b_fgate, head_lanes)],
                                axis=1).reshape(depth, 1, 2 * V7X_LANES)
    g1 = g_norm1.reshape(depth, 1, D_MODEL)
    gv = g_vnorm.reshape(depth, 1, D_B)
    g2 = g_norm2.reshape(depth, 1, D_MODEL)
    bst = jnp.swapaxes(b_spatial, 1, 2)
    ws0 = jnp.repeat(w_spatial[:, :, 0, 0], DG, axis=1).reshape(depth, 1, D_B)
    bs0 = jnp.repeat(b_spatial[:, :, 0], DG, axis=1).reshape(depth, 1, D_B)
    wo = w_out.astype(BF16)
    cb = conv_b.reshape(depth, 1, 2 * D_FF)
    n_state = state_mlstm_n.reshape(depth, n_dec, N_HEADS * DK)
    m_pad = jnp.pad(state_mlstm_m, ((0, 0), (0, 0), (0, V7X_LANES - N_HEADS)))

    small = [[] for _ in range(4)]
    cp_stack = None
    convp_stack = None
    c_stack = None
    vn_stack = None
    conv_stack = None
    for l in range(depth):
        final = l == depth - 1

        xp, cp_stack, n_p, m_p = _prompt_mixer(xp, l, wm, wgt, gbias_col, g1, gv, w_spatial, bst, wa, wb, wo,
                                               cp_stack)
        xp, convp_stack = _prompt_ffn(xp, l, g2, wup, conv_w, cb, wdn, gf, final, convp_stack)

        z, gates = _sample_proj(xs, l, g1, wm, wgt)
        h, c_stack, n_s, m_s = _sample_state(
            z, gates, gbias_row[l], m_pad[l], n_state[l], state_mlstm_C, l, c_stack)
        xs, vn_stack = _sample_mixer(xs, z, h, l, gv, ws0, bs0, wa, wb, wo, vn_stack)
        xs, conv_stack = _sample_ffn(xs, gf, l, state_ffn_conv, g2, wup, conv_w, cb, wdn, final, conv_stack)

        for lst, val in zip(small, (n_p, m_p[:, :N_HEADS, 0],
                                    n_s.reshape(n_dec, N_HEADS, DK), m_s[:, :N_HEADS])):
            lst.append(val)
    st = [jnp.stack(o) for o in small]
    return (xp, xs, cp_stack, st[0], st[1], convp_stack, c_stack, st[2], st[3], conv_stack, vn_stack)
```

```python
import functools

import jax
import jax.numpy as jnp
from jax import lax
from jax.experimental import pallas as pl
from jax.experimental.pallas import tpu as pltpu

D_MODEL = 1024
N_HEADS = 4
DK = 128
DV = 256
CHUNK = 128
D_B = 1024
N_GROUPS = 4
DG = D_B // N_GROUPS
D_FF = 2816
CONV_W = 3
EPS = 1e-6
K_SCALE = DK ** -0.5

OFF_Q = 0
OFF_K = OFF_Q + N_HEADS * DK
OFF_V = OFF_K + N_HEADS * DK
OFF_O = OFF_V + N_HEADS * DV
OFF_U = OFF_O + N_HEADS * DV
OFF_VB = OFF_U + D_B
OFF_GA = OFF_VB + D_B
OFF_GB = OFF_GA + D_MODEL
P_MAIN = OFF_GB + D_MODEL
GATE_LO = 2 * N_HEADS * DK + 2 * N_HEADS * DV
GATE_HI = GATE_LO + 2 * N_HEADS

V7X_LANES = 128
V7X_SUBLANES = 8
GATE_ROWS = V7X_SUBLANES
V7X_VMEM_LIMIT = 56 * 1024 * 1024
FFN_COL_TILE = 256
PROJ_TILE = 256
MIXER_BLOCK = 512
FFN_BLOCK = 512
SAMPLE_BLOCK = 16
PACK_TILE = 256
PACK_DIMS = 256
SAMPLE_PROJ_TILE = P_MAIN // 2
SAMPLE_FFN_STEPS = 2

F32 = jnp.float32
BF16 = jnp.bfloat16


def _dot(a, b):
    return jnp.dot(a, b, preferred_element_type=F32)


def _dot_nt(a, b):
    return lax.dot_general(a, b, (((1,), (1,)), ((), ())), preferred_element_type=F32)


def _rmsnorm(x, g):
    r = lax.rsqrt(jnp.mean(x * x, axis=-1, keepdims=True) + EPS)
    return x * r * g


def _log_sigmoid(x):
    return jnp.minimum(x, 0.0) - jnp.log1p(jnp.exp(-jnp.abs(x)))


def _scan_lanes(x, op, fill):
    lane = lax.broadcasted_iota(jnp.int32, x.shape, 1)
    k = 1
    while k < x.shape[1]:
        shifted = pltpu.roll(x, k, 1)
        x = op(x, jnp.where(lane >= k, shifted, fill))
        k *= 2
    return x


def _anchor_zero(x):
    sub, lanes = V7X_SUBLANES, V7X_LANES
    acc = jnp.zeros((sub, lanes), jnp.uint32)
    for r in range(x.shape[0] // sub):
        for c in range(x.shape[1] // lanes):
            piece = pltpu.bitcast(x[r * sub:(r + 1) * sub, c * lanes:(c + 1) * lanes], jnp.uint32)
            acc = acc | ((piece >> 16) >> 16)
    return pltpu.bitcast(acc, F32)


def _next_block_map(nb, steps):
    def index_map(b, s):
        nxt = jnp.minimum(b * steps + s + 1, nb * steps - 1)
        return (nxt // steps, nxt % steps, 0)
    return index_map


def _mixer_kernel(x_ref, wm_ref, wgt_ref, gbias_ref, g1_ref, gv_ref, ws_ref, bst_ref,
                  wa_ref, wb_ref, wo_ref, *rest, block):
    (xo_ref, c_ref, n_ref, m_ref,
     ct_s, n_s, m_s, q_s, k_s, v_s, h_s, so_s, u_s, vb_s, sg_s, um_s, xn_s) = rest[-17:]
    s = pl.program_id(1)
    n_chunks = block // CHUNK

    @pl.when(s == 0)
    def _():
        ct_s[...] = jnp.zeros_like(ct_s)
        n_s[...] = jnp.zeros_like(n_s)
        m_s[...] = jnp.zeros_like(m_s)

    xn_s[...] = _rmsnorm(x_ref[...], g1_ref[...]).astype(BF16)

    gates = _dot_nt(wgt_ref[...], xn_s[...]) + gbias_ref[...]

    def proj(off, t):
        return _dot(xn_s[...], wm_ref[:, off + t * PROJ_TILE:off + (t + 1) * PROJ_TILE])

    def tile(t):
        return slice(t * PROJ_TILE, (t + 1) * PROJ_TILE)

    for t in range(N_HEADS * DK // PROJ_TILE):
        q_s[:, tile(t)] = proj(OFF_Q, t)
        k_s[:, tile(t)] = proj(OFF_K, t) * K_SCALE
    for t in range(N_HEADS * DV // PROJ_TILE):
        v_s[:, tile(t)] = proj(OFF_V, t)

    sumsq = [jnp.zeros((block, 1), F32)]

    def vb_tile(t):
        g = jax.nn.gelu(proj(OFF_VB, t))
        vb_s[:, tile(t)] = g
        sumsq[0] = sumsq[0] + jnp.sum(g * g, axis=-1, keepdims=True)

    def u_tile(t):
        u_s[:, tile(t)] = jax.nn.gelu(proj(OFF_U, t))

    def o_tile(t):
        so_s[:, tile(t)] = jax.nn.sigmoid(proj(OFF_O, t))

    def ga_tile(t):
        sg_s[:, tile(t)] = jax.nn.sigmoid(proj(OFF_GA, t))

    def gb_tile(t):
        sg_s[:, D_MODEL + t * PROJ_TILE:D_MODEL + (t + 1) * PROJ_TILE] = jax.nn.sigmoid(proj(OFF_GB, t))

    jobs = [(f, t) for f in (vb_tile, u_tile, o_tile, ga_tile, gb_tile) for t in range(D_MODEL // PROJ_TILE)]

    def run_jobs(count):
        for _ in range(min(count, len(jobs))):
            f, t = jobs.pop(0)
            f(t)

    row_i = lax.broadcasted_iota(jnp.int32, (CHUNK, CHUNK), 0)
    col_i = lax.broadcasted_iota(jnp.int32, (CHUNK, CHUNK), 1)
    causal = row_i >= col_i
    heads = range(N_HEADS)

    for c in range(n_chunks):
        r0 = c * CHUNK
        ig = gates[0:GATE_ROWS, r0:r0 + CHUNK]
        lf = _log_sigmoid(gates[GATE_ROWS:2 * GATE_ROWS, r0:r0 + CHUNK])
        b = _scan_lanes(lf, jnp.add, 0.0)
        a = ig - b
        m_prev = m_s[...]
        gmax = jnp.maximum(m_prev, _scan_lanes(a, jnp.maximum, -jnp.inf))
        m_t = b + gmax
        w_inter = jnp.exp(m_prev - gmax)
        g_last = gmax[:, CHUNK - 1:CHUNK]
        w_last = jnp.exp(a - g_last)
        floor = jnp.exp(-m_t)
        decay = w_inter[:, CHUNK - 1:CHUNK]
        m_s[...] = jnp.broadcast_to(m_t[:, CHUNK - 1:CHUNK], m_s.shape)

        rows = jnp.concatenate(
            [gmax, w_inter, w_last, floor,
             jnp.zeros((CHUNK - 4 * GATE_ROWS, CHUNK), F32)], axis=0)
        cols = rows.T

        def col(kind, h):
            return cols[:, kind * GATE_ROWS + h:kind * GATE_ROWS + h + 1]

        qf = [q_s[r0:r0 + CHUNK, h * DK:(h + 1) * DK] for h in heads]
        kf = [k_s[r0:r0 + CHUNK, h * DK:(h + 1) * DK] for h in heads]
        vf = [v_s[r0:r0 + CHUNK, h * DV:(h + 1) * DV] for h in heads]
        qb = [x.astype(BF16) for x in qf]
        kt = [x.T.astype(BF16) for x in kf]
        run_jobs(2)
        zero = jnp.zeros((DK, CHUNK), BF16)
        sc = []
        for h in range(0, N_HEADS, 2):
            kk = jnp.concatenate([jnp.concatenate([kt[h], zero], axis=1),
                                  jnp.concatenate([zero, kt[h + 1]], axis=1)], axis=0)
            pair = _dot(jnp.concatenate([qb[h], qb[h + 1]], axis=1), kk)
            sc += [pair[:, :CHUNK], pair[:, CHUNK:]]
        dmat = [jnp.where(causal, jnp.exp(a[h:h + 1, :] - col(0, h)), 0.0) for h in heads]
        run_jobs(1)
        sd = [sc[h] * dmat[h] for h in heads]
        ct = [ct_s[h] for h in heads]
        num = [_dot(jnp.concatenate([sd[h], col(1, h) * qf[h]], axis=1).astype(BF16),
                    jnp.concatenate([vf[h], ct[h]], axis=0).astype(BF16)) for h in heads]
        run_jobs(2)
        for h in heads:
            nh = n_s[h:h + 1, :]
            den = (jnp.sum(sd[h], axis=-1, keepdims=True)
                   + col(1, h) * jnp.sum(qf[h] * nh, axis=-1, keepdims=True))
            h_s[r0:r0 + CHUNK, h * DV:(h + 1) * DV] = num[h] * (1.0 / jnp.maximum(jnp.abs(den), col(3, h)))
            dec = decay[h:h + 1, :]
            n_s[h:h + 1, :] = dec * nh + jnp.sum(col(2, h) * kf[h], axis=0, keepdims=True)
        run_jobs(1)
        for h in heads:
            ct_s[h] = decay[h:h + 1, :] * ct[h] + _dot(kt[h], (col(2, h) * vf[h]).astype(BF16))
    run_jobs(len(jobs))

    y_a = _dot((so_s[...] * h_s[...]).astype(BF16), wa_ref[...])

    rinv = lax.rsqrt(sumsq[0] * (1.0 / D_B) + EPS)
    for g in range(N_GROUPS):
        gcols = slice(g * DG, (g + 1) * DG)
        w_tri = jnp.where(causal, ws_ref[g], 0.0).astype(BF16)
        bias_c = bst_ref[:, g:g + 1]
        for c in range(n_chunks):
            rws = slice(c * CHUNK, (c + 1) * CHUNK)
            vn = vb_s[rws, gcols] * rinv[rws] * gv_ref[:, gcols]
            um_s[rws, gcols] = u_s[rws, gcols] * (_dot(w_tri, vn.astype(BF16)) + bias_c)
    y_b = _dot(um_s[...].astype(BF16), wb_ref[...])

    merged = sg_s[:, 0:D_MODEL] * y_a + sg_s[:, D_MODEL:2 * D_MODEL] * y_b
    xo_ref[...] = x_ref[...] + _dot(merged.astype(BF16), wo_ref[...])

    @pl.when(s == pl.num_programs(1) - 1)
    def _():
        for h in range(N_HEADS):
            c_ref[h] = ct_s[h].T
        n_ref[...] = n_s[...]
        m_ref[...] = m_s[...]


def _resident(arr, layer=None):
    if layer is None:
        nd = arr.ndim
        return pl.BlockSpec(arr.shape, lambda *_: (0,) * nd, pipeline_mode=pl.Buffered(1))
    nd = arr.ndim - 1
    return pl.BlockSpec((None,) + arr.shape[1:], lambda *_: (layer,) + (0,) * nd,
                        pipeline_mode=pl.Buffered(1))


def _prompt_mixer(x, layer, wm, wgt, gbias, g1, gv, ws, bst, wa, wb, wo, c_stack):
    nb, seq, _ = x.shape
    depth = wm.shape[0]
    block = MIXER_BLOCK
    grid = (nb, seq // block)
    xspec = pl.BlockSpec((None, block, D_MODEL), lambda b, s: (b, s, 0))
    out_shape = (
        jax.ShapeDtypeStruct((nb, seq, D_MODEL), F32),
        jax.ShapeDtypeStruct((depth, nb, N_HEADS, DV, DK), F32),
        jax.ShapeDtypeStruct((nb, N_HEADS, DK), F32),
        jax.ShapeDtypeStruct((nb, GATE_ROWS, V7X_LANES), F32),
    )
    out_specs = (
        xspec,
        pl.BlockSpec((None, None, N_HEADS, DV, DK), lambda b, s: (layer, b, 0, 0, 0)),
        pl.BlockSpec((None, N_HEADS, DK), lambda b, s: (b, 0, 0)),
        pl.BlockSpec((None, GATE_ROWS, V7X_LANES), lambda b, s: (b, 0, 0)),
    )
    in_specs = [xspec] + [_resident(a, layer) for a in (wm, wgt, gbias, g1, gv, ws, bst, wa, wb, wo)]
    args = [x, wm, wgt, gbias, g1, gv, ws, bst, wa, wb, wo]
    aliases = {}
    if c_stack is not None:
        in_specs.append(pl.BlockSpec(memory_space=pl.ANY))
        aliases = {len(args): 1}
        args.append(c_stack)
    scratch = [
        pltpu.VMEM((N_HEADS, DK, DV), F32),
        pltpu.VMEM((N_HEADS, DK), F32),
        pltpu.VMEM((GATE_ROWS, V7X_LANES), F32),
        pltpu.VMEM((block, N_HEADS * DK), F32),
        pltpu.VMEM((block, N_HEADS * DK), F32),
        pltpu.VMEM((block, N_HEADS * DV), F32),
        pltpu.VMEM((block, N_HEADS * DV), F32),
        pltpu.VMEM((block, N_HEADS * DV), F32),
        pltpu.VMEM((block, D_B), F32),
        pltpu.VMEM((block, D_B), F32),
        pltpu.VMEM((block, 2 * D_MODEL), F32),
        pltpu.VMEM((block, D_B), F32),
        pltpu.VMEM((block, D_MODEL), BF16),
    ]
    return pl.pallas_call(
        functools.partial(_mixer_kernel, block=block),
        grid=grid, in_specs=in_specs, out_specs=out_specs, out_shape=out_shape,
        scratch_shapes=scratch, input_output_aliases=aliases,
        compiler_params=pltpu.CompilerParams(
            dimension_semantics=("arbitrary", "arbitrary"), vmem_limit_bytes=V7X_VMEM_LIMIT),
        name="prompt_mixer",
    )(*args)


def _conv_taps(up, carry_s, cw_ref, cb_ref, cols):
    sub = V7X_SUBLANES
    rows = up.shape[0]
    last1 = up[rows - sub:rows]
    last2 = up[rows - 2 * sub:rows - sub]
    first = lax.broadcasted_iota(jnp.int32, last1.shape, 0) == 0
    back1 = jnp.where(first, carry_s[sub - 1:sub, cols], pltpu.roll(last1, 1, 0))
    back2 = jnp.where(first, carry_s[sub - 2:sub - 1, cols], pltpu.roll(last2, 1, 0))
    carry_s[sub - 2:sub - 1, cols] = last2[sub - 1:sub]
    carry_s[sub - 1:sub, cols] = last1[sub - 1:sub]
    m1 = jnp.concatenate([back1, up[0:rows - sub]], axis=0)
    m2 = jnp.concatenate([back2, back1, up[0:rows - 2 * sub]], axis=0)
    return (cb_ref[:, cols] + cw_ref[0:1, cols] * m2 + cw_ref[1:2, cols] * m1
            + cw_ref[2:3, cols] * up)


def _perm_pitch(block):
    return block // V7X_SUBLANES + V7X_SUBLANES


def _ffn_kernel(x_ref, xnext_ref, g2_ref, wup_ref, cw_ref, cb_ref, wdn_ref, gf_ref, *rest, block, final):
    xo_ref, conv_ref, carry_s, act_s, perm_s, unperm_s, xn_s = rest[-7:]
    s = pl.program_id(1)
    sub, lanes = V7X_SUBLANES, V7X_LANES
    groups = block // sub
    chunks = D_MODEL // lanes
    pitch = _perm_pitch(block)

    @pl.when(s == 0)
    def _():
        carry_s[...] = jnp.zeros_like(carry_s)

    def stage(src_ref):
        for c in range(chunks):
            for i in range(sub):
                perm_s[c, i * pitch:i * pitch + groups] = src_ref[i * groups:(i + 1) * groups,
                                                                  c * lanes:(c + 1) * lanes]
        x = jnp.concatenate(
            [jnp.concatenate([perm_s[c, pl.ds(r, sub, stride=pitch), :] for c in range(chunks)], axis=1)
             for r in range(groups)], axis=0)
        xn = _rmsnorm(x, g2_ref[...])
        xn_s[...] = xn.astype(BF16)
        return xn

    @pl.when((pl.program_id(0) == 0) & (s == 0))
    def _():
        stage(x_ref)

    xn = xn_s[...]
    for j in range(D_FF // FFN_COL_TILE):
        halves = []
        for half in range(2):
            cols = slice(half * D_FF + j * FFN_COL_TILE, half * D_FF + (j + 1) * FFN_COL_TILE)
            up = _dot(xn, wup_ref[:, cols])
            halves.append(_conv_taps(up, carry_s, cw_ref, cb_ref, cols))
        act_s[:, j * FFN_COL_TILE:(j + 1) * FFN_COL_TILE] = (jax.nn.silu(halves[0]) * halves[1]).astype(BF16)
    staged = stage(xnext_ref)
    down = _dot(act_s[...], wdn_ref[...])
    anchor = _anchor_zero(staged)
    for r in range(groups):
        for c in range(chunks):
            tile_rc = down[r * sub:(r + 1) * sub, c * lanes:(c + 1) * lanes]
            if r == 0 and c == 0:
                tile_rc = tile_rc + anchor
            unperm_s[c, pl.ds(r, sub, stride=pitch), :] = tile_rc
    for i in range(sub):
        rows = slice(i * groups, (i + 1) * groups)
        y = x_ref[rows, :] + jnp.concatenate(
            [unperm_s[c, i * pitch:i * pitch + groups] for c in range(chunks)], axis=1)
        if final:
            y = _rmsnorm(y, gf_ref[...])
        xo_ref[rows, :] = y

    @pl.when(s == pl.num_programs(1) - 1)
    def _():
        conv_ref[...] = carry_s[V7X_SUBLANES - (CONV_W - 1):V7X_SUBLANES, :]


def _prompt_ffn(x, layer, g2, wup, cw, cb, wdn, gf, final, conv_stack):
    nb, seq, _ = x.shape
    depth = wup.shape[0]
    block = FFN_BLOCK
    steps = seq // block
    grid = (nb, steps)
    xspec = pl.BlockSpec((None, block, D_MODEL), lambda b, s: (b, s, 0))
    out_shape = (
        jax.ShapeDtypeStruct((nb, seq, D_MODEL), F32),
        jax.ShapeDtypeStruct((depth, nb, CONV_W - 1, 2 * D_FF), F32),
    )
    out_specs = (xspec, pl.BlockSpec((None, None, CONV_W - 1, 2 * D_FF), lambda b, s: (layer, b, 0, 0)))
    in_specs = ([xspec, pl.BlockSpec((None, block, D_MODEL), _next_block_map(nb, steps))]
                + [_resident(a, layer) for a in (g2, wup, cw, cb, wdn)] + [_resident(gf)])
    args = [x, x, g2, wup, cw, cb, wdn, gf]
    aliases = {}
    if conv_stack is not None:
        in_specs.append(pl.BlockSpec(memory_space=pl.ANY))
        aliases = {len(args): 1}
        args.append(conv_stack)
    reorder = pltpu.VMEM((D_MODEL // V7X_LANES, V7X_SUBLANES * _perm_pitch(block), V7X_LANES), F32)
    return pl.pallas_call(
        functools.partial(_ffn_kernel, block=block, final=final),
        grid=grid, in_specs=in_specs, out_specs=out_specs, out_shape=out_shape,
        scratch_shapes=[pltpu.VMEM((V7X_SUBLANES, 2 * D_FF), F32),
                        pltpu.VMEM((block, D_FF), BF16),
                        reorder, reorder,
                        pltpu.VMEM((block, D_MODEL), BF16)],
        input_output_aliases=aliases,
        compiler_params=pltpu.CompilerParams(
            dimension_semantics=("arbitrary", "arbitrary"), vmem_limit_bytes=V7X_VMEM_LIMIT),
        name="prompt_ffn",
    )(*args)


def _sample_proj_kernel(x_ref, g1_ref, wgt_ref, wm_ref, z_ref, gates_ref):
    xn = _rmsnorm(x_ref[...], g1_ref[...]).astype(BF16)
    z_ref[...] = _dot(xn, wm_ref[...])

    @pl.when(pl.program_id(0) == 0)
    def _():
        gt = _dot_nt(wgt_ref[...], xn)
        gt = jnp.concatenate([gt, jnp.zeros((V7X_LANES - gt.shape[0], gt.shape[1]), F32)], axis=0)
        g = gt.T
        gates_ref[...] = jnp.concatenate([g, pltpu.roll(g, V7X_LANES - GATE_ROWS, 1)], axis=1)


def _single_step(kernel_fn, name, out_shape, whole, layered, layer, stacked=None):
    in_specs = [_resident(a) for a in whole] + [_resident(a, layer) for a in layered]
    args = list(whole) + list(layered)
    out_specs = [pl.BlockSpec(o.shape, lambda i, nd=len(o.shape): (0,) * nd) for o in out_shape]
    aliases = {}
    if stacked is not None:
        k, prev = stacked
        nd = len(out_shape[k].shape) - 1
        out_specs[k] = pl.BlockSpec((None,) + out_shape[k].shape[1:], lambda i: (layer,) + (0,) * nd)
        if prev is not None:
            in_specs.append(pl.BlockSpec(memory_space=pl.ANY))
            aliases = {len(args): k}
            args.append(prev)
    return pl.pallas_call(
        kernel_fn,
        grid=(1,),
        in_specs=in_specs, out_specs=tuple(out_specs), out_shape=out_shape,
        input_output_aliases=aliases,
        compiler_params=pltpu.CompilerParams(
            dimension_semantics=("arbitrary",), vmem_limit_bytes=V7X_VMEM_LIMIT),
        name=name,
    )(*args)


def _sample_proj(x, layer, g1, wm, wgt):
    n = x.shape[0]
    assert n == V7X_LANES, "the gate transpose assumes one lane tile of sample rows"
    tile = SAMPLE_PROJ_TILE
    return pl.pallas_call(
        _sample_proj_kernel,
        grid=(P_MAIN // tile,),
        in_specs=[_resident(x), _resident(g1, layer), _resident(wgt, layer),
                  pl.BlockSpec((None, D_MODEL, tile), lambda j: (layer, 0, j))],
        out_specs=(pl.BlockSpec((n, tile), lambda j: (0, j)),
                   pl.BlockSpec((n, 2 * V7X_LANES), lambda j: (0, 0))),
        out_shape=(jax.ShapeDtypeStruct((n, P_MAIN), F32), jax.ShapeDtypeStruct((n, 2 * V7X_LANES), F32)),
        compiler_params=pltpu.CompilerParams(
            dimension_semantics=("arbitrary",), vmem_limit_bytes=V7X_VMEM_LIMIT),
        name="sample_proj",
    )(x, g1, wgt, wm)


def _sample_state_kernel(q_ref, k_ref, v_ref, gates_ref, gbias_ref, m_ref, n_ref, c_ref, *rest):
    h_ref, co_ref, no_ref, mo_ref = rest[-4:]
    tb = SAMPLE_BLOCK
    ig = gates_ref[:, 0:V7X_LANES] + gbias_ref[:, 0:V7X_LANES]
    lf = _log_sigmoid(gates_ref[:, V7X_LANES:] + gbias_ref[:, V7X_LANES:])
    inter = lf + m_ref[...]
    m_t = jnp.maximum(inter, ig)
    d_in = jnp.exp(ig - m_t)
    w_inter = jnp.exp(inter - m_t)
    floor = jnp.exp(-m_t)
    mo_ref[...] = m_t

    row8 = lax.broadcasted_iota(jnp.int32, (CHUNK, DK), 0)
    for h in range(N_HEADS):
        q8 = q_ref[:, h * DK:(h + 1) * DK]
        k8 = k_ref[:, h * DK:(h + 1) * DK] * K_SCALE
        v8 = v_ref[:, h * DV:(h + 1) * DV]
        d_h = d_in[:, h:h + 1]
        w_h = w_inter[:, h:h + 1]
        n8 = n_ref[:, h * DK:(h + 1) * DK]
        s = jnp.sum(q8 * k8, axis=-1, keepdims=True) * d_h
        den = s + w_h * jnp.sum(q8 * n8, axis=-1, keepdims=True)
        qb = q8.astype(BF16)
        inter_rows = [_dot_nt(qb, c_ref[j, h].astype(BF16))[j:j + 1] for j in range(tb)]
        num = s * v8 + w_h * jnp.concatenate(inter_rows, axis=0)
        h_ref[:, h * DV:(h + 1) * DV] = num / jnp.maximum(jnp.abs(den), floor[:, h:h + 1])
        no_ref[:, h * DK:(h + 1) * DK] = w_h * n8 + d_h * k8

        vt = jnp.concatenate([d_h * v8, jnp.zeros((CHUNK - tb, DV), F32)], axis=0).T.astype(BF16)
        kpad = jnp.concatenate([k8, jnp.zeros((CHUNK - tb, DK), F32)], axis=0)
        for j in range(tb):
            kj = jnp.where(row8 == j, kpad, 0.0).astype(BF16)
            co_ref[j, h] = w_inter[j:j + 1, h:h + 1] * c_ref[j, h] + _dot(vt, kj)


def _sample_state(z, gates, gbias, m_pad, n_state, c_state, layer, c_stack):
    n = z.shape[0]
    tb = SAMPLE_BLOCK
    c_block = pl.BlockSpec((None, tb, N_HEADS, DV, DK), lambda i: (layer, i, 0, 0, 0))
    in_specs = [
        pl.BlockSpec((tb, N_HEADS * DK), lambda i: (i, OFF_Q // (N_HEADS * DK))),
        pl.BlockSpec((tb, N_HEADS * DK), lambda i: (i, OFF_K // (N_HEADS * DK))),
        pl.BlockSpec((tb, N_HEADS * DV), lambda i: (i, OFF_V // (N_HEADS * DV))),
        pl.BlockSpec((tb, 2 * V7X_LANES), lambda i: (i, 0)),
        pl.BlockSpec((1, 2 * V7X_LANES), lambda i: (0, 0)),
        pl.BlockSpec((tb, V7X_LANES), lambda i: (i, 0)),
        pl.BlockSpec((tb, N_HEADS * DK), lambda i: (i, 0)),
        c_block,
    ]
    args = [z, z, z, gates, gbias, m_pad, n_state, c_state]
    aliases = {}
    if c_stack is not None:
        in_specs.append(pl.BlockSpec(memory_space=pl.ANY))
        aliases = {len(args): 1}
        args.append(c_stack)
    out_shape = (
        jax.ShapeDtypeStruct((n, N_HEADS * DV), F32),
        jax.ShapeDtypeStruct(c_state.shape, F32),
        jax.ShapeDtypeStruct((n, N_HEADS * DK), F32),
        jax.ShapeDtypeStruct((n, V7X_LANES), F32),
    )
    out_specs = (
        pl.BlockSpec((tb, N_HEADS * DV), lambda i: (i, 0)),
        c_block,
        pl.BlockSpec((tb, N_HEADS * DK), lambda i: (i, 0)),
        pl.BlockSpec((tb, V7X_LANES), lambda i: (i, 0)),
    )
    return pl.pallas_call(
        _sample_state_kernel,
        grid=(n // tb,), in_specs=in_specs, out_specs=out_specs, out_shape=out_shape,
        input_output_aliases=aliases,
        compiler_params=pltpu.CompilerParams(
            dimension_semantics=("arbitrary",), vmem_limit_bytes=V7X_VMEM_LIMIT),
        name="sample_state",
    )(*args)


def _sample_mixer_kernel(x_ref, z_ref, h_ref, gv_ref, ws0_ref, bs0_ref, wa_ref, wb_ref, wo_ref, *rest):
    xo_ref, vn_ref = rest[-2:]
    y_a = _dot((jax.nn.sigmoid(z_ref[:, OFF_O:OFF_U]) * h_ref[...]).astype(BF16), wa_ref[...])
    u = jax.nn.gelu(z_ref[:, OFF_U:OFF_VB])
    vn = _rmsnorm(jax.nn.gelu(z_ref[:, OFF_VB:OFF_GA]), gv_ref[...])
    vn_ref[:, 0, :] = vn
    mixed = ws0_ref[...] * vn + bs0_ref[...]
    y_b = _dot((u * mixed).astype(BF16), wb_ref[...])
    merged = (jax.nn.sigmoid(z_ref[:, OFF_GA:OFF_GB]) * y_a
              + jax.nn.sigmoid(z_ref[:, OFF_GB:P_MAIN]) * y_b)
    xo_ref[...] = x_ref[...] + _dot(merged.astype(BF16), wo_ref[...])


def _sample_mixer(x, z, h, layer, gv, ws0, bs0, wa, wb, wo, vn_stack):
    n = x.shape[0]
    out_shape = (jax.ShapeDtypeStruct((n, D_MODEL), F32),
                 jax.ShapeDtypeStruct((wa.shape[0], n, 1, D_B), F32))
    return _single_step(_sample_mixer_kernel, "sample_mixer", out_shape,
                        (x, z, h), (gv, ws0, bs0, wa, wb, wo), layer, stacked=(1, vn_stack))


def _sample_ffn_kernel(x_ref, gf_ref, g2_ref, bufa_ref, bufg_ref, wupa_ref, wupg_ref, cwa_ref, cwg_ref,
                       cba_ref, cbg_ref, wdn_ref, *rest, final):
    xo_ref, nbuf_ref, acc_s = rest[-3:]
    j = pl.program_id(0)
    last = SAMPLE_FFN_STEPS - 1
    width = D_FF // SAMPLE_FFN_STEPS
    x = x_ref[...]
    xn = _rmsnorm(x, g2_ref[...]).astype(BF16)

    def conv(up, buf_ref, cw_ref, cb_ref):
        b0 = buf_ref[:, 0, :]
        b1 = buf_ref[:, 1, :]
        return b1, cb_ref[...] + cw_ref[0:1, :] * b0 + cw_ref[1:2, :] * b1 + cw_ref[2:3, :] * up

    up_a = _dot(xn, wupa_ref[...])
    up_g = _dot(xn, wupg_ref[...])
    prev_a, conv_a = conv(up_a, bufa_ref, cwa_ref, cba_ref)
    prev_g, conv_g = conv(up_g, bufg_ref, cwg_ref, cbg_ref)
    part = _dot((jax.nn.silu(conv_a) * conv_g).astype(BF16), wdn_ref[...])

    for jj in range(SAMPLE_FFN_STEPS):
        @pl.when(j == jj)
        def _():
            for col0, prev, up in ((jj * width, prev_a, up_a), (D_FF + jj * width, prev_g, up_g)):
                nbuf_ref[:, 0, col0:col0 + width] = prev
                nbuf_ref[:, 1, col0:col0 + width] = up

    @pl.when(j == 0)
    def _():
        acc_s[...] = part

    @pl.when((j > 0) & (j < last))
    def _():
        acc_s[...] += part

    @pl.when(j == last)
    def _():
        y = x + (acc_s[...] + part)
        if final:
            xo_ref[:, 0, :] = _rmsnorm(y, gf_ref[...])
        else:
            xo_ref[...] = y


def _sample_ffn(x, gf, layer, buf, g2, wup, cw, cb, wdn, final, conv_stack):
    n = x.shape[0]
    steps = SAMPLE_FFN_STEPS
    width = D_FF // steps

    def cols(shape, gate):
        nd = len(shape)
        return pl.BlockSpec((None,) + shape[:-1] + (width,),
                            lambda j: (layer,) + (0,) * (nd - 1) + (gate * steps + j,))

    in_specs = [_resident(x), _resident(gf), _resident(g2, layer),
                cols(buf.shape[1:], 0), cols(buf.shape[1:], 1),
                cols(wup.shape[1:], 0), cols(wup.shape[1:], 1),
                cols(cw.shape[1:], 0), cols(cw.shape[1:], 1),
                cols(cb.shape[1:], 0), cols(cb.shape[1:], 1),
                pl.BlockSpec((None, width, D_MODEL), lambda j: (layer, j, 0))]
    args = [x, gf, g2, buf, buf, wup, wup, cw, cw, cb, cb, wdn]
    aliases = {}
    if conv_stack is not None:
        in_specs.append(pl.BlockSpec(memory_space=pl.ANY))
        aliases = {len(args): 1}
        args.append(conv_stack)
    x_out = (n, 1, D_MODEL) if final else (n, D_MODEL)
    nd_out = len(x_out)
    return pl.pallas_call(
        functools.partial(_sample_ffn_kernel, final=final),
        grid=(steps,),
        in_specs=in_specs,
        out_specs=(pl.BlockSpec(x_out, lambda j: (0,) * nd_out),
                   pl.BlockSpec((None,) + buf.shape[1:], lambda j: (layer, 0, 0, 0))),
        out_shape=(jax.ShapeDtypeStruct(x_out, F32), jax.ShapeDtypeStruct(buf.shape, F32)),
        scratch_shapes=[pltpu.VMEM((n, D_MODEL), F32)],
        input_output_aliases=aliases,
        compiler_params=pltpu.CompilerParams(
            dimension_semantics=("arbitrary",), vmem_limit_bytes=V7X_VMEM_LIMIT),
        name="sample_ffn",
    )(*args)


def _pack_kernel(wt_ref, *refs):
    n_cast = (len(refs) - 2) // 2
    casts_in, (o_ref, wgt_ref), casts_out = refs[:n_cast], refs[n_cast:n_cast + 2], refs[n_cast + 2:]
    g8 = wt_ref[GATE_LO:GATE_HI, :]
    head = lax.broadcasted_iota(jnp.int32, g8.shape, 0) < N_HEADS
    wgt_ref[...] = jnp.concatenate(
        [jnp.where(head, g8, 0.0), jnp.where(head, pltpu.roll(g8, N_HEADS, 0), 0.0)], axis=0).astype(BF16)
    for j in range(P_MAIN // PACK_TILE):
        dst = j * PACK_TILE
        src = dst if dst < GATE_LO else dst + (GATE_HI - GATE_LO)
        o_ref[:, dst:dst + PACK_TILE] = wt_ref[src:src + PACK_TILE, :].T.astype(BF16)
    for src_ref, dst_ref in zip(casts_in, casts_out):
        dst_ref[...] = src_ref[...].astype(BF16)


def _pack_weights(w_in_t, casts):
    depth, p_in, d = w_in_t.shape
    steps = d // PACK_DIMS

    def slab(w):
        return pl.BlockSpec((None, w.shape[1] // steps, w.shape[2]), lambda l, r: (l, r, 0))

    return pl.pallas_call(
        _pack_kernel,
        grid=(depth, steps),
        in_specs=[pl.BlockSpec((None, p_in, PACK_DIMS), lambda l, r: (l, 0, r))] + [slab(w) for w in casts],
        out_specs=(pl.BlockSpec((None, PACK_DIMS, P_MAIN), lambda l, r: (l, r, 0)),
                   pl.BlockSpec((None, 2 * GATE_ROWS, PACK_DIMS), lambda l, r: (l, 0, r)))
        + tuple(slab(w) for w in casts),
        out_shape=(jax.ShapeDtypeStruct((depth, d, P_MAIN), BF16),
                   jax.ShapeDtypeStruct((depth, 2 * GATE_ROWS, d), BF16))
        + tuple(jax.ShapeDtypeStruct(w.shape, BF16) for w in casts),
        compiler_params=pltpu.CompilerParams(
            dimension_semantics=("arbitrary", "arbitrary"), vmem_limit_bytes=V7X_VMEM_LIMIT),
        name="pack_weights",
    )(w_in_t, *casts)


def kernel(x_prompt, x_sample, state_mlstm_C, state_mlstm_n, state_mlstm_m, state_ffn_conv, w_in, b_igate, b_fgate, g_norm1, g_vnorm, w_spatial, b_spatial, w_branch_a, w_branch_b, w_out, g_norm2, w_up, conv_w, conv_b, w_down, g_final):
    depth = w_in.shape[0]
    n_dec = x_sample.shape[0]
    xp = x_prompt
    xs = x_sample.reshape(n_dec, D_MODEL)
    gf = g_final.reshape(1, D_MODEL)

    wm, wgt, wa, wb, wdn = _pack_weights(jnp.swapaxes(w_in, 1, 2), (w_branch_a, w_branch_b, w_down))
    head_pad = ((0, 0), (0, GATE_ROWS - N_HEADS))
    gbias_col = jnp.concatenate([jnp.pad(b_igate, head_pad), jnp.pad(b_fgate, head_pad)],
                                axis=1).reshape(depth, 2 * GATE_ROWS, 1)
    head_lanes = ((0, 0), (0, V7X_LANES - N_HEADS))
    gbias_row = jnp.concatenate([jnp.pad(b_igate, head_lanes), jnp.pad(b_fgate, head_lanes)],
                                axis=1).reshape(depth, 1, 2 * V7X_LANES)
    g1 = g_norm1.reshape(depth, 1, D_MODEL)
    gv = g_vnorm.reshape(depth, 1, D_B)
    g2 = g_norm2.reshape(depth, 1, D_MODEL)
    bst = jnp.swapaxes(b_spatial, 1, 2)
    ws0 = jnp.repeat(w_spatial[:, :, 0, 0], DG, axis=1).reshape(depth, 1, D_B)
    bs0 = jnp.repeat(b_spatial[:, :, 0], DG, axis=1).reshape(depth, 1, D_B)
    wo = w_out.astype(BF16)
    wup = w_up.astype(BF16)
    cb = conv_b.reshape(depth, 1, 2 * D_FF)
    n_state = state_mlstm_n.reshape(depth, n_dec, N_HEADS * DK)
    m_pad = jnp.pad(state_mlstm_m, ((0, 0), (0, 0), (0, V7X_LANES - N_HEADS)))

    small = [[] for _ in range(4)]
    cp_stack = None
    convp_stack = None
    c_stack = None
    vn_stack = None
    conv_stack = None
    for l in range(depth):
        final = l == depth - 1

        xp, cp_stack, n_p, m_p = _prompt_mixer(xp, l, wm, wgt, gbias_col, g1, gv, w_spatial, bst, wa, wb, wo,
                                               cp_stack)
        xp, convp_stack = _prompt_ffn(xp, l, g2, wup, conv_w, cb, wdn, gf, final, convp_stack)

        z, gates = _sample_proj(xs, l, g1, wm, wgt)
        h, c_stack, n_s, m_s = _sample_state(
            z, gates, gbias_row[l], m_pad[l], n_state[l], state_mlstm_C, l, c_stack)
        xs, vn_stack = _sample_mixer(xs, z, h, l, gv, ws0, bs0, wa, wb, wo, vn_stack)
        xs, conv_stack = _sample_ffn(xs, gf, l, state_ffn_conv, g2, wup, conv_w, cb, wdn, final, conv_stack)

        for lst, val in zip(small, (n_p, m_p[:, :N_HEADS, 0],
                                    n_s.reshape(n_dec, N_HEADS, DK), m_s[:, :N_HEADS])):
            lst.append(val)
    st = [jnp.stack(o) for o in small]
    return (xp, xs, cp_stack, st[0], st[1], convp_stack, c_stack, st[2], st[3], conv_stack, vn_stack)
```
